```python
import math
import jax, jax.numpy as jnp
from jax import lax
import numpy as np

D_MODEL = 1024
BATCH = 4
SEQ = 8192
DEPTH = 2

HEAD_DIM = 64
N_HEADS = D_MODEL // HEAD_DIM
N_KV_GROUPS = 2
HEADS_PER_GROUP = N_HEADS // N_KV_GROUPS
KV_DIM = N_KV_GROUPS * HEAD_DIM
CMP_BLOCK = 32
CMP_STRIDE = 16
CMP_HIDDEN = 256
SEL_BLOCK = 64
N_SELECT = 16
WINDOW = 512
Q_BLOCK = 128
SEL_FORCE = 100.0
ROPE_THETA = 10000.0
SCALE = HEAD_DIM ** -0.5
LRU_WIDTH = D_MODEL
LRU_HEADS = 16
LRU_BLOCK = LRU_WIDTH // LRU_HEADS
LRU_C = 8.0
CONV_WIDTH = 4
SSM_INNER = D_MODEL
SSM_HEAD_DIM = 64
SSM_HEADS = SSM_INNER // SSM_HEAD_DIM
SSM_GROUPS = 2
SSM_STATE = 128
SSM_CHUNK = 128
SSM_CONV_DIM = SSM_INNER + 2 * SSM_GROUPS * SSM_STATE
PEER_HEADS = 8
PEER_KEYS = 128
PEER_EXPERTS = PEER_KEYS * PEER_KEYS
PEER_QUERY_DIM = 256
PEER_HALF = PEER_QUERY_DIM // 2
PEER_TOPK = 16
PEER_CHUNK = 128
N_BRANCH = 3
IN_SIZES = (N_HEADS * HEAD_DIM, 6 * KV_DIM, 3 * N_HEADS,
            LRU_WIDTH, LRU_WIDTH,
            SSM_INNER, SSM_CONV_DIM, SSM_HEADS,
            N_BRANCH * D_MODEL)
IN_COLS = sum(IN_SIZES)
EPS = 1e-6
NEG = -1e30

kernel_name = 'hybrid_nsa_rglru_ssd_peer'


def rms_norm(x, g):
    x32 = x.astype(jnp.float32)
    y = x32 * lax.rsqrt(jnp.mean(x32 * x32, axis=-1, keepdims=True) + EPS)
    return (y * g.astype(jnp.float32)).astype(x.dtype)


def rope(x, positions):
    half = x.shape[-1] // 2
    inv = ROPE_THETA ** (-jnp.arange(half, dtype=jnp.float32) / half)
    ang = positions.astype(jnp.float32)[..., None] * inv
    cos = jnp.cos(ang)[:, :, None, :]
    sin = jnp.sin(ang)[:, :, None, :]
    x32 = x.astype(jnp.float32)
    x1, x2 = x32[..., :half], x32[..., half:]
    return jnp.concatenate([x1 * cos - x2 * sin, x2 * cos + x1 * sin], axis=-1).astype(x.dtype)


def masked_softmax(s, mask):
    s32 = jnp.where(mask, s.astype(jnp.float32), NEG)
    return jax.nn.softmax(s32, axis=-1) * mask


def causal_conv(x, w, b):
    k_w, s = w.shape[0], x.shape[1]
    xp = jnp.pad(x, ((0, 0), (k_w - 1, 0), (0, 0)))
    out = b
    for k in range(k_w):
        out = out + xp[:, k:k + s] * w[k]
    return out


def segsum(a):
    t = a.shape[-1]
    rep = jnp.broadcast_to(a[..., None], a.shape + (t,))
    strict = jnp.tril(jnp.ones((t, t), dtype=bool), -1)
    cs = jnp.cumsum(jnp.where(strict, rep, 0.0), axis=-2)
    return jnp.where(jnp.tril(jnp.ones((t, t), dtype=bool)), cs, -jnp.inf)


def linear_recurrence_combine(left, right):
    a1, b1 = left
    a2, b2 = right
    return a1 * a2, a2 * b1 + b2


def nsa_branch(q, kv, gate_logits, positions, q_norm_g, k_norm_g, cmp_pe_k, cmp_pe_v,
               cmp_k_w1, cmp_k_w2, cmp_v_w1, cmp_v_w2):
    B, S = q.shape[:2]
    G, HG, dh = N_KV_GROUPS, HEADS_PER_GROUP, HEAD_DIM
    q = rope(rms_norm(q.reshape(B, S, N_HEADS, dh), q_norm_g), positions)
    k_c, v_c, k_s, v_s, k_w, v_w = [a.reshape(B, S, G, dh) for a in jnp.split(kv, 6, axis=-1)]
    k_c = rope(rms_norm(k_c, k_norm_g), positions)
    k_s = rope(rms_norm(k_s, k_norm_g), positions)
    k_w = rope(rms_norm(k_w, k_norm_g), positions)
    gates = jax.nn.sigmoid(gate_logits).reshape(B, S, G, HG, 3)

    n_cmp = (S - CMP_BLOCK) // CMP_STRIDE + 1
    cmp_start = CMP_STRIDE * jnp.arange(n_cmp)
    cmp_idx = cmp_start[:, None] + jnp.arange(CMP_BLOCK)[None, :]

    def compress(t, pe, w1, w2):
        blk = t[:, cmp_idx] + pe[None, None, :, None, :]
        blk = blk.transpose(0, 1, 3, 2, 4).reshape(B, n_cmp, G, CMP_BLOCK * dh)
        return jax.nn.gelu(blk @ w1) @ w2

    k_cmp = rms_norm(compress(k_c, cmp_pe_k, cmp_k_w1, cmp_k_w2), k_norm_g)
    v_cmp = compress(v_c, cmp_pe_v, cmp_v_w1, cmp_v_w2)
    cmp_end = cmp_start + CMP_BLOCK - 1

    n_sel = S // SEL_BLOCK
    n_pick = min(N_SELECT, n_sel)
    sel_start = SEL_BLOCK * jnp.arange(n_sel)
    overlap = jnp.clip(jnp.minimum(cmp_start[:, None] + CMP_BLOCK, sel_start[None, :] + SEL_BLOCK)
                       - jnp.maximum(cmp_start[:, None], sel_start[None, :]), 0).astype(jnp.float32) / CMP_BLOCK
    ks_blk = k_s.reshape(B, n_sel, SEL_BLOCK, G, dh).transpose(0, 3, 1, 2, 4)
    vs_blk = v_s.reshape(B, n_sel, SEL_BLOCK, G, dh).transpose(0, 3, 1, 2, 4)
    b_ix = jnp.arange(B)[:, None, None, None]
    g_ix = jnp.arange(G)[None, :, None, None]
    blk_ids = jnp.arange(n_sel)

    kw_pad = jnp.pad(k_w, ((0, 0), (WINDOW, 0), (0, 0), (0, 0)))
    vw_pad = jnp.pad(v_w, ((0, 0), (WINDOW, 0), (0, 0), (0, 0)))

    n_qb = S // Q_BLOCK
    q_blocks = q.reshape(B, n_qb, Q_BLOCK, G, HG, dh).transpose(1, 0, 2, 3, 4, 5)
    g_blocks = gates.reshape(B, n_qb, Q_BLOCK, G, HG, 3).transpose(1, 0, 2, 3, 4, 5)

    def query_block(args):
        qb, gb, ib = args
        t = ib * Q_BLOCK + jnp.arange(Q_BLOCK)
        s_c = jnp.einsum('bqghd,bcgd->bghqc', qb, k_cmp) * SCALE
        p_c = masked_softmax(s_c, cmp_end[None, :] <= t[:, None])
        o_c = jnp.einsum('bghqc,bcgd->bqghd', p_c.astype(qb.dtype), v_cmp)
        imp = jnp.einsum('bghqc,cj->bgqj', p_c, overlap)
        cur = t // SEL_BLOCK
        causal_blk = sel_start[None, :] <= t[:, None]
        forced = (blk_ids[None, :] == 0) | (blk_ids[None, :] == cur[:, None]) | (blk_ids[None, :] == cur[:, None] - 1)
        score = jnp.where(forced, SEL_FORCE, jnp.where(causal_blk, imp, -1.0))
        _, sel = lax.top_k(score, n_pick)
        kg = ks_blk[b_ix, g_ix, sel]
        vg = vs_blk[b_ix, g_ix, sel]
        key_pos = sel[..., None] * SEL_BLOCK + jnp.arange(SEL_BLOCK)
        mask_s = (key_pos <= t[None, None, :, None, None]).reshape(B, G, 1, Q_BLOCK, n_pick * SEL_BLOCK)
        s_s = jnp.einsum('bqghd,bgqnld->bghqnl', qb, kg) * SCALE
        p_s = masked_softmax(s_s.reshape(B, G, HG, Q_BLOCK, n_pick * SEL_BLOCK), mask_s)
        p_s = p_s.reshape(B, G, HG, Q_BLOCK, n_pick, SEL_BLOCK).astype(qb.dtype)
        o_s = jnp.einsum('bghqnl,bgqnld->bqghd', p_s, vg)
        kw = lax.dynamic_slice_in_dim(kw_pad, ib * Q_BLOCK, Q_BLOCK + WINDOW, axis=1)
        vw = lax.dynamic_slice_in_dim(vw_pad, ib * Q_BLOCK, Q_BLOCK + WINDOW, axis=1)
        s_pos = ib * Q_BLOCK - WINDOW + jnp.arange(Q_BLOCK + WINDOW)
        mask_w = (s_pos[None, :] <= t[:, None]) & (s_pos[None, :] > t[:, None] - WINDOW) & (s_pos[None, :] >= 0)
        s_w = jnp.einsum('bqghd,bkgd->bghqk', qb, kw) * SCALE
        p_w = masked_softmax(s_w, mask_w)
        o_w = jnp.einsum('bghqk,bkgd->bqghd', p_w.astype(qb.dtype), vw)
        return gb[..., 0:1] * o_c + gb[..., 1:2] * o_s + gb[..., 2:3] * o_w

    out = lax.map(query_block, (q_blocks, g_blocks, jnp.arange(n_qb)))
    return out.transpose(1, 0, 2, 3, 4, 5).reshape(B, S, N_HEADS * dh)


def rglru_branch(x_in, gate_in, positions, conv_w, conv_b, w_r, b_r, w_i, b_i, lam):
    B, S, _ = x_in.shape
    xc = causal_conv(x_in, conv_w, conv_b)
    xh = xc.reshape(B, S, LRU_HEADS, LRU_BLOCK)
    r = jax.nn.sigmoid(jnp.einsum('bshi,hij->bshj', xh, w_r).reshape(B, S, LRU_WIDTH) + b_r)
    i = jax.nn.sigmoid(jnp.einsum('bshi,hij->bshj', xh, w_i).reshape(B, S, LRU_WIDTH) + b_i)
    log_a = -LRU_C * r.astype(jnp.float32) * jax.nn.softplus(-lam.astype(jnp.float32))
    reset = (positions == 0)[..., None]
    a = jnp.where(reset, 0.0, jnp.exp(log_a))
    mult = jnp.where(reset, 1.0, jnp.sqrt(-jnp.expm1(2.0 * log_a)))
    b = mult * (i * xc).astype(jnp.float32)
    _, h = lax.associative_scan(linear_recurrence_combine, (a, b), axis=1)
    return h.astype(x_in.dtype) * jax.nn.gelu(gate_in)


def ssd_chunked(x, dt, A, bm, cm):
    B, S = x.shape[:2]
    c, L = S // SSM_CHUNK, SSM_CHUNK
    G, HG, P, N = SSM_GROUPS, SSM_HEADS // SSM_GROUPS, SSM_HEAD_DIM, SSM_STATE
    X = (x.astype(jnp.float32) * dt[..., None]).reshape(B, c, L, G, HG, P)
    adt = (dt * A).reshape(B, c, L, G, HG).transpose(0, 3, 4, 1, 2)
    Bc = bm.astype(jnp.float32).reshape(B, c, L, G, N)
    Cc = cm.astype(jnp.float32).reshape(B, c, L, G, N)
    a_cs = jnp.cumsum(adt, axis=-1)
    decay = jnp.exp(segsum(adt))
    cb = jnp.einsum('bclgn,bcsgn->bgcls', Cc, Bc)
    y_diag = jnp.einsum('bghcls,bcsghp->bclghp', cb[:, :, None] * decay, X)
    decay_states = jnp.exp(a_cs[..., -1:] - a_cs).transpose(0, 3, 4, 1, 2)[..., None]
    states = jnp.einsum('bclgn,bclghp->bcghpn', Bc, X * decay_states)
    chunk_tot = jnp.pad(a_cs[..., -1], ((0, 0), (0, 0), (0, 0), (1, 0)))
    decay_chunk = jnp.exp(segsum(chunk_tot))
    states = jnp.concatenate([jnp.zeros_like(states[:, :1]), states], axis=1)
    states = jnp.einsum('bghzc,bcghpn->bzghpn', decay_chunk, states)[:, :-1]
    y_off = jnp.einsum('bclgn,bcghpn->bclghp', Cc, states) * jnp.exp(a_cs).transpose(0, 3, 4, 1, 2)[..., None]
    return (y_diag + y_off).reshape(B, S, SSM_HEADS, P).astype(x.dtype)


def ssd_branch(z, xbc, dt_raw, conv_w, conv_b, dt_bias, a_log, d_skip, norm_g):
    B, S, _ = z.shape
    xbc = jax.nn.silu(causal_conv(xbc, conv_w, conv_b))
    xs, bm, cm = jnp.split(xbc, [SSM_INNER, SSM_INNER + SSM_GROUPS * SSM_STATE], axis=-1)
    xs = xs.reshape(B, S, SSM_HEADS, SSM_HEAD_DIM)
    bm = bm.reshape(B, S, SSM_GROUPS, SSM_STATE)
    cm = cm.reshape(B, S, SSM_GROUPS, SSM_STATE)
    dt = jax.nn.softplus(dt_raw.astype(jnp.float32) + dt_bias.astype(jnp.float32))
    A = -jnp.exp(a_log.astype(jnp.float32))
    y = ssd_chunked(xs, dt, A, bm, cm) + d_skip[:, None] * xs
    y = y.reshape(B, S, SSM_INNER) * jax.nn.silu(z)
    y = rms_norm(y.reshape(B, S, SSM_GROUPS, SSM_INNER // SSM_GROUPS),
                 norm_g.reshape(SSM_GROUPS, SSM_INNER // SSM_GROUPS))
    return y.reshape(B, S, SSM_INNER)


def peer_ffn(h, w_q, sub_keys, u, v):
    B, S, D = h.shape
    n_ch = S // PEER_CHUNK
    hc = h.reshape(B, n_ch, PEER_CHUNK, D).transpose(1, 0, 2, 3)

    def chunk(hb):
        q = (hb @ w_q).reshape(B, PEER_CHUNK, PEER_HEADS, 2, PEER_HALF)
        s = jnp.einsum('bchxd,hxkd->bchxk', q, sub_keys).astype(jnp.float32)
        s1, i1 = lax.top_k(s[..., 0, :], PEER_TOPK)
        s2, i2 = lax.top_k(s[..., 1, :], PEER_TOPK)
        n_cand = PEER_TOPK * PEER_TOPK
        cand = (s1[..., :, None] + s2[..., None, :]).reshape(B, PEER_CHUNK, PEER_HEADS, n_cand)
        cidx = (i1[..., :, None] * PEER_KEYS + i2[..., None, :]).reshape(B, PEER_CHUNK, PEER_HEADS, n_cand)
        sc, pos = lax.top_k(cand, PEER_TOPK)
        eidx = jnp.take_along_axis(cidx, pos, axis=-1)
        g = jax.nn.softmax(sc, axis=-1)
        act = jax.nn.gelu(jnp.einsum('bchkd,bcd->bchk', u[eidx], hb).astype(jnp.float32))
        return jnp.einsum('bchk,bchkd->bcd', (g * act).astype(hb.dtype), v[eidx])

    out = lax.map(chunk, hc)
    return out.transpose(1, 0, 2, 3).reshape(B, S, D)


def hybrid_mixer(xn, positions, w_in, q_norm_g, k_norm_g, cmp_pe_k, cmp_pe_v, cmp_k_w1, cmp_k_w2,
                 cmp_v_w1, cmp_v_w2, lru_conv_w, lru_conv_b, lru_w_r, lru_b_r, lru_w_i, lru_b_i,
                 lru_lambda, ssm_conv_w, ssm_conv_b, ssm_dt_bias, ssm_a_log, ssm_d, ssm_norm_g,
                 w_branch, w_out):
    B, S, _ = xn.shape
    splits = np.cumsum(IN_SIZES)[:-1].tolist()
    a_q, a_kv, a_g, l_x, l_g, s_z, s_xbc, s_dt, m_g = [xn @ w for w in jnp.split(w_in, splits, axis=-1)]
    y_a = nsa_branch(a_q, a_kv, a_g, positions, q_norm_g, k_norm_g, cmp_pe_k, cmp_pe_v,
                     cmp_k_w1, cmp_k_w2, cmp_v_w1, cmp_v_w2)
    y_b = rglru_branch(l_x, l_g, positions, lru_conv_w, lru_conv_b, lru_w_r, lru_b_r,
                       lru_w_i, lru_b_i, lru_lambda)
    y_c = ssd_branch(s_z, s_xbc, s_dt, ssm_conv_w, ssm_conv_b, ssm_dt_bias, ssm_a_log, ssm_d, ssm_norm_g)
    merge_gates = jax.nn.sigmoid(m_g).reshape(B, S, N_BRANCH, D_MODEL)
    merged = (merge_gates[:, :, 0] * (y_a @ w_branch[0])
              + merge_gates[:, :, 1] * (y_b @ w_branch[1])
              + merge_gates[:, :, 2] * (y_c @ w_branch[2]))
    return merged @ w_out


def setup_inputs(seed: int = 0) -> dict:
    key = jax.random.key(seed)
    ks = jax.random.split(key, 32)
    f32 = jnp.float32
    L = DEPTH

    def nrm(k, shape, scale):
        return jax.random.normal(k, shape, f32) * scale

    def gain(k, shape):
        return 1.0 + 0.02 * jax.random.normal(k, shape, f32)

    u_lru = jax.random.uniform(ks[14], (L, LRU_WIDTH), f32, 0.9, 0.999)
    s_lru = u_lru ** (1.0 / LRU_C)
    dt0 = jnp.exp(jax.random.uniform(ks[17], (L, SSM_HEADS), f32, math.log(1e-3), math.log(1e-1)))
    return {
        'x': jax.random.normal(ks[0], (BATCH, SEQ, D_MODEL), f32),
        'positions': jnp.broadcast_to(jnp.arange(SEQ, dtype=jnp.int32)[None, :], (BATCH, SEQ)),
        'mix_norm_g': gain(ks[1], (L, D_MODEL)),
        'w_in': nrm(ks[2], (L, D_MODEL, IN_COLS), D_MODEL ** -0.5),
        'q_norm_g': gain(ks[3], (L, HEAD_DIM)),
        'k_norm_g': gain(ks[4], (L, HEAD_DIM)),
        'cmp_pe_k': nrm(ks[5], (L, CMP_BLOCK, HEAD_DIM), 0.1),
        'cmp_pe_v': nrm(ks[6], (L, CMP_BLOCK, HEAD_DIM), 0.1),
        'cmp_k_w1': nrm(ks[7], (L, CMP_BLOCK * HEAD_DIM, CMP_HIDDEN), (CMP_BLOCK * HEAD_DIM) ** -0.5),
        'cmp_k_w2': nrm(ks[8], (L, CMP_HIDDEN, HEAD_DIM), CMP_HIDDEN ** -0.5),
        'cmp_v_w1': nrm(ks[9], (L, CMP_BLOCK * HEAD_DIM, CMP_HIDDEN), (CMP_BLOCK * HEAD_DIM) ** -0.5),
        'cmp_v_w2': nrm(ks[10], (L, CMP_HIDDEN, HEAD_DIM), CMP_HIDDEN ** -0.5),
        'lru_conv_w': nrm(ks[11], (L, CONV_WIDTH, LRU_WIDTH), CONV_WIDTH ** -0.5),
        'lru_conv_b': nrm(ks[12], (L, LRU_WIDTH), 0.01),
        'lru_w_r': nrm(ks[13], (L, LRU_HEADS, LRU_BLOCK, LRU_BLOCK), LRU_BLOCK ** -0.5),
        'lru_b_r': nrm(ks[15], (L, LRU_WIDTH), 0.01),
        'lru_w_i': nrm(ks[16], (L, LRU_HEADS, LRU_BLOCK, LRU_BLOCK), LRU_BLOCK ** -0.5),
        'lru_b_i': nrm(ks[18], (L, LRU_WIDTH), 0.01),
        'lru_lambda': jnp.log(s_lru) - jnp.log1p(-s_lru),
        'ssm_conv_w': nrm(ks[19], (L, CONV_WIDTH, SSM_CONV_DIM), CONV_WIDTH ** -0.5),
        'ssm_conv_b': nrm(ks[20], (L, SSM_CONV_DIM), 0.01),
        'ssm_dt_bias': dt0 + jnp.log(-jnp.expm1(-dt0)),
        'ssm_a_log': jnp.log(jax.random.uniform(ks[21], (L, SSM_HEADS), f32, 1.0, 16.0)),
        'ssm_d': 1.0 + 0.1 * jax.random.normal(ks[22], (L, SSM_HEADS), f32),
        'ssm_norm_g': gain(ks[23], (L, SSM_INNER)),
        'w_branch': nrm(ks[24], (L, N_BRANCH, D_MODEL, D_MODEL), D_MODEL ** -0.5),
        'w_out': nrm(ks[25], (L, D_MODEL, D_MODEL), D_MODEL ** -0.5),
        'ffn_norm_g': gain(ks[26], (L, D_MODEL)),
        'peer_w_q': nrm(ks[27], (L, D_MODEL, PEER_HEADS * PEER_QUERY_DIM), D_MODEL ** -0.5),
        'peer_sub_keys': nrm(ks[28], (L, PEER_HEADS, 2, PEER_KEYS, PEER_HALF), PEER_HALF ** -0.5),
        'peer_u': nrm(ks[29], (L, PEER_EXPERTS, D_MODEL), D_MODEL ** -0.5),
        'peer_v': nrm(ks[30], (L, PEER_EXPERTS, D_MODEL), PEER_HEADS ** -0.5),
    }


def reference(x, positions, mix_norm_g, w_in, q_norm_g, k_norm_g, cmp_pe_k, cmp_pe_v, cmp_k_w1,
              cmp_k_w2, cmp_v_w1, cmp_v_w2, lru_conv_w, lru_conv_b, lru_w_r, lru_b_r, lru_w_i,
              lru_b_i, lru_lambda, ssm_conv_w, ssm_conv_b, ssm_dt_bias, ssm_a_log, ssm_d,
              ssm_norm_g, w_branch, w_out, ffn_norm_g, peer_w_q, peer_sub_keys, peer_u, peer_v):
    for l in range(DEPTH):
        xn = rms_norm(x, mix_norm_g[l])
        x = x + hybrid_mixer(xn, positions, w_in[l], q_norm_g[l], k_norm_g[l], cmp_pe_k[l], cmp_pe_v[l],
                             cmp_k_w1[l], cmp_k_w2[l], cmp_v_w1[l], cmp_v_w2[l], lru_conv_w[l],
                             lru_conv_b[l], lru_w_r[l], lru_b_r[l], lru_w_i[l], lru_b_i[l],
                             lru_lambda[l], ssm_conv_w[l], ssm_conv_b[l], ssm_dt_bias[l],
                             ssm_a_log[l], ssm_d[l], ssm_norm_g[l], w_branch[l], w_out[l])
        x = x + peer_ffn(rms_norm(x, ffn_norm_g[l]), peer_w_q[l], peer_sub_keys[l], peer_u[l], peer_v[l])
    return x
```

```python
import functools
import math

import jax
import jax.numpy as jnp
from jax import lax
from jax.experimental import pallas as pl
from jax.experimental.pallas import tpu as pltpu

F32 = jnp.float32
BF16 = jnp.bfloat16
I32 = jnp.int32

D_MODEL = 1024
HEAD_DIM = 64
N_HEADS = 16
N_KV_GROUPS = 2
HEADS_PER_GROUP = 8
CMP_BLOCK = 32
CMP_STRIDE = 16
CMP_HIDDEN = 256
SEL_BLOCK = 64
N_SELECT = 16
WINDOW = 512
SEL_FORCE = 100.0
ROPE_THETA = 10000.0
SCALE = HEAD_DIM ** -0.5
LRU_HEADS = 16
LRU_BLOCK = 64
LRU_C = 8.0
CONV_WIDTH = 4
SSM_HEADS = 16
SSM_HEAD_DIM = 64
SSM_GROUPS = 2
SSM_STATE = 128
SSM_CHUNK = 128
SSM_INNER = 1024
PEER_HEADS = 8
PEER_KEYS = 128
PEER_HALF = 128
PEER_TOPK = 16
EPS = 1e-6
NEG = -1e30
LANES = 128

VMEM_LIMIT = 56 * 1024 * 1024


def _params(sem):
    return pltpu.CompilerParams(dimension_semantics=sem, vmem_limit_bytes=VMEM_LIMIT)


def _gelu(x):
    return 0.5 * x * (1.0 + jnp.tanh(math.sqrt(2.0 / math.pi) * (x + 0.044715 * x * x * x)))


def _sigmoid(x):
    return 1.0 / (1.0 + jnp.exp(-x))


def _softplus(x):
    return jnp.maximum(x, 0.0) + jnp.log(1.0 + jnp.exp(-jnp.abs(x)))


def _rmsnorm_kernel(x_ref, g_ref, o_ref):
    x = x_ref[...].astype(F32)
    ms = jnp.mean(x * x, axis=-1, keepdims=True)
    o_ref[...] = (x * lax.rsqrt(ms + EPS) * g_ref[...]).astype(o_ref.dtype)


def _rmsnorm(x, g, tm=512):
    t, d = x.shape
    return pl.pallas_call(
        _rmsnorm_kernel, grid=(t // tm,),
        in_specs=[pl.BlockSpec((tm, d), lambda i: (i, 0)), pl.BlockSpec((1, d), lambda i: (0, 0))],
        out_specs=pl.BlockSpec((tm, d), lambda i: (i, 0)),
        out_shape=jax.ShapeDtypeStruct((t, d), BF16),
        compiler_params=_params(("parallel",)), name="rmsnorm")(x, g.reshape(1, d).astype(F32))


def _mm_kernel(x_ref, w_ref, o_ref):
    o_ref[...] = jnp.dot(x_ref[...], w_ref[...], preferred_element_type=F32).astype(o_ref.dtype)


def _matmul(x, w, out_dtype, name, tm=512, tn=None):
    t, k = x.shape
    n = w.shape[1]
    tn = n if tn is None else tn
    return pl.pallas_call(
        _mm_kernel, grid=(n // tn, t // tm),
        in_specs=[pl.BlockSpec((tm, k), lambda j, i: (i, 0)), pl.BlockSpec((k, tn), lambda j, i: (0, j))],
        out_specs=pl.BlockSpec((tm, tn), lambda j, i: (i, j)),
        out_shape=jax.ShapeDtypeStruct((t, n), out_dtype),
        compiler_params=_params(("parallel", "parallel")), name=name)(x, w)


def _norm_rope(x, g, cos, sin_signed, bd):
    ms = jnp.dot((x * x).astype(BF16), bd, preferred_element_type=F32)
    y = x * lax.rsqrt(ms + EPS) * g
    lane = lax.broadcasted_iota(I32, y.shape, 1)
    first_half = (lane % HEAD_DIM) < (HEAD_DIM // 2)
    partner = jnp.where(first_half, pltpu.roll(y, LANES - HEAD_DIM // 2, 1), pltpu.roll(y, HEAD_DIM // 2, 1))
    return y * cos + partner * sin_signed


def _nsa_prep_kernel(aq_ref, akv_ref, pos_ref, inv_ref, sgn_ref, gq_ref, gk_ref, bd_ref,
                     q_ref, kc_ref, vc_ref, ks_ref, vs_ref, kw_ref, vw_ref):
    ang = pos_ref[...].astype(F32) * inv_ref[...]
    cos = jnp.cos(ang)
    sin_signed = jnp.sin(ang) * sgn_ref[...]
    bd = bd_ref[...]
    for c in range(D_MODEL // LANES):
        x = aq_ref[:, c * LANES:(c + 1) * LANES].astype(F32)
        y = (_norm_rope(x, gq_ref[...], cos, sin_signed, bd) * SCALE).astype(q_ref.dtype)
        q_ref[0, 2 * c] = y[:, :HEAD_DIM]
        q_ref[0, 2 * c + 1] = y[:, HEAD_DIM:]
    outs = (kc_ref, vc_ref, ks_ref, vs_ref, kw_ref, vw_ref)
    for c, o_ref in enumerate(outs):
        x = akv_ref[:, c * LANES:(c + 1) * LANES]
        if c % 2 == 0:
            y = _norm_rope(x.astype(F32), gk_ref[...], cos, sin_signed, bd).astype(o_ref.dtype)
        else:
            y = x.astype(o_ref.dtype)
        o_ref[0, 0] = y[:, :HEAD_DIM]
        o_ref[0, 1] = y[:, HEAD_DIM:]


def _nsa_prep(a_q, a_kv, positions, q_norm_g, k_norm_g, b, s, tm=512):
    half = HEAD_DIM // 2
    lane = jnp.arange(LANES)
    inv = (ROPE_THETA ** (-((lane % half).astype(F32)) / half)).reshape(1, LANES)
    sgn = jnp.where((lane % HEAD_DIM) < half, -1.0, 1.0).astype(F32).reshape(1, LANES)
    bd = jnp.where((lane[:, None] // HEAD_DIM) == (lane[None, :] // HEAD_DIM), 1.0 / HEAD_DIM, 0.0).astype(BF16)
    gq = jnp.tile(q_norm_g.astype(F32), 2).reshape(1, LANES)
    gk = jnp.tile(k_norm_g.astype(F32), 2).reshape(1, LANES)
    nt = s // tm
    row = lambda i: (i, 0)
    const = lambda i: (0, 0)
    kv_shape = jax.ShapeDtypeStruct((b, N_KV_GROUPS, s, HEAD_DIM), BF16)
    kv_spec = pl.BlockSpec((1, N_KV_GROUPS, tm, HEAD_DIM), lambda i: (i // nt, 0, i % nt, 0))
    return pl.pallas_call(
        _nsa_prep_kernel, grid=(b * nt,),
        in_specs=[pl.BlockSpec((tm, D_MODEL), row), pl.BlockSpec((tm, 6 * LANES), row),
                  pl.BlockSpec((tm, 1), row), pl.BlockSpec((1, LANES), const), pl.BlockSpec((1, LANES), const),
                  pl.BlockSpec((1, LANES), const), pl.BlockSpec((1, LANES), const),
                  pl.BlockSpec((LANES, LANES), const)],
        out_specs=[pl.BlockSpec((1, N_HEADS, tm, HEAD_DIM), lambda i: (i // nt, 0, i % nt, 0))] + [kv_spec] * 6,
        out_shape=[jax.ShapeDtypeStruct((b, N_HEADS, s, HEAD_DIM), BF16)] + [kv_shape] * 6,
        compiler_params=_params(("parallel",)), name="nsa_prep",
    )(a_q, a_kv, positions.reshape(b * s, 1).astype(I32), inv, sgn, gq, gk, bd)


def _compress_kernel(uk_ref, uv_ref, pek_ref, pev_ref, kw1_ref, kw2_ref, vw1_ref, vw2_ref, gk_ref,
                     kc_ref, vc_ref):
    half = CMP_STRIDE * HEAD_DIM

    def mlp(u, pe, w1_ref, w2_ref):
        n = u.shape[0]
        ha = jnp.dot(u, w1_ref[:half, :], preferred_element_type=F32)
        hb = jnp.dot(u, w1_ref[half:, :], preferred_element_type=F32)
        bias = jnp.dot(pe, w1_ref[...], preferred_element_type=F32)[0:1, :]
        pre = ha + pltpu.roll(hb, n - 1, 0) + bias
        return jnp.dot(_gelu(pre).astype(BF16), w2_ref[...], preferred_element_type=F32)

    k = mlp(uk_ref[0, 0], pek_ref[...], kw1_ref, kw2_ref)
    ms = jnp.mean(k * k, axis=-1, keepdims=True)
    kc_ref[0, 0] = (k * lax.rsqrt(ms + EPS) * gk_ref[...]).astype(kc_ref.dtype)
    vc_ref[0, 0] = mlp(uv_ref[0, 0], pev_ref[...], vw1_ref, vw2_ref).astype(vc_ref.dtype)


def _compress(kc, vc, pe_k, pe_v, kw1, kw2, vw1, vw2, k_norm_g):
    b, g, s, dh = kc.shape
    ng = s // CMP_STRIDE
    wide = CMP_BLOCK * dh
    uk = kc.reshape(b, g, ng, CMP_STRIDE * dh)
    uv = vc.reshape(b, g, ng, CMP_STRIDE * dh)
    pek = jnp.zeros((8, wide), BF16).at[0].set(pe_k.reshape(wide).astype(BF16))
    pev = jnp.zeros((8, wide), BF16).at[0].set(pe_v.reshape(wide).astype(BF16))
    u_spec = pl.BlockSpec((1, 1, ng, CMP_STRIDE * dh), lambda i, j: (i, j, 0, 0))
    c2 = lambda i, j: (0, 0)
    o_spec = pl.BlockSpec((1, 1, ng, dh), lambda i, j: (i, j, 0, 0))
    o_shape = jax.ShapeDtypeStruct((b, g, ng, dh), BF16)
    return pl.pallas_call(
        _compress_kernel, grid=(b, g),
        in_specs=[u_spec, u_spec, pl.BlockSpec((8, wide), c2), pl.BlockSpec((8, wide), c2),
                  pl.BlockSpec((wide, CMP_HIDDEN), c2), pl.BlockSpec((CMP_HIDDEN, dh), c2),
                  pl.BlockSpec((wide, CMP_HIDDEN), c2), pl.BlockSpec((CMP_HIDDEN, dh), c2),
                  pl.BlockSpec((1, dh), c2)],
        out_specs=[o_spec, o_spec], out_shape=[o_shape, o_shape],
        compiler_params=_params(("parallel", "parallel")), name="nsa_compress",
    )(uk, uv, pek, pev, kw1.astype(BF16), kw2.astype(BF16), vw1.astype(BF16), vw2.astype(BF16),
      k_norm_g.reshape(1, dh).astype(F32))


def _cmp_select_kernel(q_ref, kc_ref, vc_ref, ov_ref, o_ref, bias_ref, *, tq, n_cmp, n_pick):
    i = pl.program_id(2)
    hg = q_ref.shape[1]
    nc = kc_ref.shape[2]
    nsel = ov_ref.shape[1]
    q = q_ref[0].reshape(hg * tq, HEAD_DIM)
    s = lax.dot_general(q, kc_ref[0, 0], (((1,), (1,)), ((), ())), preferred_element_type=F32)
    s = s.reshape(hg, tq, nc)
    t = i * tq + lax.broadcasted_iota(I32, (tq, nc), 0)
    c = lax.broadcasted_iota(I32, (tq, nc), 1)
    mask = ((CMP_STRIDE * c + CMP_BLOCK - 1) <= t) & (c < n_cmp)
    s = jnp.where(mask[None], s, NEG)
    m = jnp.max(s, axis=-1, keepdims=True)
    p = jnp.where(mask[None], jnp.exp(s - m), 0.0)
    l = jnp.sum(p, axis=-1, keepdims=True)
    p = p * jnp.where(l > 0.0, 1.0 / l, 0.0)
    o = jnp.dot(p.astype(BF16).reshape(hg * tq, nc), vc_ref[0, 0], preferred_element_type=F32)
    o_ref[0] = o.reshape(hg, tq, HEAD_DIM).astype(o_ref.dtype)

    psum = jnp.sum(p, axis=0)
    hi = psum.astype(BF16)
    lo = (psum - hi.astype(F32)).astype(BF16)
    imp = (jnp.dot(hi, ov_ref[...], preferred_element_type=F32)
           + jnp.dot(lo, ov_ref[...], preferred_element_type=F32))
    tt = i * tq + lax.broadcasted_iota(I32, (tq, nsel), 0)
    j = lax.broadcasted_iota(I32, (tq, nsel), 1)
    cur = tt // SEL_BLOCK
    forced = (j == 0) | (j == cur) | (j == cur - 1)
    score = jnp.where(forced, SEL_FORCE, jnp.where(j * SEL_BLOCK <= tt, imp, -1.0))
    picked = jnp.zeros((tq, nsel), jnp.bool_)
    for _ in range(n_pick):
        mx = jnp.max(score, axis=-1, keepdims=True)
        idx = jnp.min(jnp.where(score == mx, j, nsel), axis=-1, keepdims=True)
        hit = j == idx
        picked = picked | hit
        score = jnp.where(hit, -jnp.inf, score)
    bias_ref[0, 0] = jnp.where(picked & (j <= cur), 0.0, NEG).astype(bias_ref.dtype)


def _cmp_select(q, k_cmp, v_cmp, s, tq=128):
    b, nh, _, dh = q.shape
    g, hg = N_KV_GROUPS, HEADS_PER_GROUP
    nc = k_cmp.shape[2]
    n_cmp = (s - CMP_BLOCK) // CMP_STRIDE + 1
    nsel = s // SEL_BLOCK
    n_pick = min(N_SELECT, nsel)
    cs = CMP_STRIDE * jnp.arange(nc)
    ss = SEL_BLOCK * jnp.arange(nsel)
    ov = jnp.clip(jnp.minimum(cs[:, None] + CMP_BLOCK, ss[None, :] + SEL_BLOCK)
                  - jnp.maximum(cs[:, None], ss[None, :]), 0).astype(F32) / CMP_BLOCK
    ov = jnp.where(jnp.arange(nc)[:, None] < n_cmp, ov, 0.0).astype(BF16)
    kern = functools.partial(_cmp_select_kernel, tq=tq, n_cmp=n_cmp, n_pick=n_pick)
    return pl.pallas_call(
        kern, grid=(b, g, s // tq),
        in_specs=[pl.BlockSpec((1, hg, tq, dh), lambda bi, gi, i: (bi, gi, i, 0)),
                  pl.BlockSpec((1, 1, nc, dh), lambda bi, gi, i: (bi, gi, 0, 0)),
                  pl.BlockSpec((1, 1, nc, dh), lambda bi, gi, i: (bi, gi, 0, 0)),
                  pl.BlockSpec((nc, nsel), lambda bi, gi, i: (0, 0))],
        out_specs=[pl.BlockSpec((1, hg, tq, dh), lambda bi, gi, i: (bi, gi, i, 0)),
                   pl.BlockSpec((1, 1, tq, nsel), lambda bi, gi, i: (bi, gi, i, 0))],
        out_shape=[jax.ShapeDtypeStruct((b, nh, s, dh), BF16),
                   jax.ShapeDtypeStruct((b, g, s, nsel), BF16)],
        compiler_params=_params(("parallel", "parallel", "parallel")), name="nsa_cmp_select",
    )(q, k_cmp, v_cmp, ov)


def _sel_attn_kernel(q_ref, bias_ref, k_ref, v_ref, o_ref, qa_ref, m_ref, l_ref, acc_ref, *, tq, tk):
    i = pl.program_id(2)
    j = pl.program_id(3)
    hg = q_ref.shape[1]
    nsel = bias_ref.shape[3]
    rows = hg * tq
    last_j = ((i + 1) * tq - 1) // tk

    @pl.when(j == 0)
    def _():
        qa_ref[:, :nsel] = jnp.broadcast_to(bias_ref[0, 0][None], (hg, tq, nsel)).reshape(rows, nsel)
        qa_ref[:, nsel:] = q_ref[0].reshape(rows, HEAD_DIM)
        m_ref[...] = jnp.full(m_ref.shape, -jnp.inf, F32)
        l_ref[...] = jnp.zeros(l_ref.shape, F32)
        acc_ref[...] = jnp.zeros(acc_ref.shape, F32)

    @pl.when(j <= last_j)
    def _():
        key = j * tk + lax.broadcasted_iota(I32, (tk, nsel), 0)
        blk = lax.broadcasted_iota(I32, (tk, nsel), 1)
        onehot = jnp.where(key // SEL_BLOCK == blk, 1.0, 0.0).astype(BF16)
        ka = jnp.concatenate([onehot, k_ref[0, 0]], axis=1)
        s = lax.dot_general(qa_ref[...], ka, (((1,), (1,)), ((), ())), preferred_element_type=F32)
        t = i * tq + lax.broadcasted_iota(I32, (tq, tk), 0)
        kp = j * tk + lax.broadcasted_iota(I32, (tq, tk), 1)
        s = jnp.where((kp <= t)[None], s.reshape(hg, tq, tk), NEG).reshape(rows, tk)
        m_old = m_ref[...]
        m_new = jnp.maximum(m_old, jnp.max(s, axis=-1, keepdims=True))
        alpha = jnp.exp(m_old - m_new)
        p = jnp.exp(s - m_new)
        l_ref[...] = alpha * l_ref[...] + jnp.sum(p, axis=-1, keepdims=True)
        acc_ref[...] = alpha * acc_ref[...] + jnp.dot(p.astype(BF16), v_ref[0, 0], preferred_element_type=F32)
        m_ref[...] = m_new

    @pl.when(j == last_j)
    def _():
        o_ref[0] = (acc_ref[...] / l_ref[...]).reshape(hg, tq, HEAD_DIM).astype(o_ref.dtype)


def _sel_attn(q, bias, k_s, v_s, s, tq=256, tk=512):
    b, nh, _, dh = q.shape
    g, hg = N_KV_GROUPS, HEADS_PER_GROUP
    nsel = bias.shape[3]
    tk = min(tk, s)
    rows = hg * tq
    kv_map = lambda bi, gi, i, j: (bi, gi, jnp.minimum(j, ((i + 1) * tq - 1) // tk), 0)
    kern = functools.partial(_sel_attn_kernel, tq=tq, tk=tk)
    return pl.pallas_call(
        kern, grid=(b, g, s // tq, s // tk),
        in_specs=[pl.BlockSpec((1, hg, tq, dh), lambda bi, gi, i, j: (bi, gi, i, 0)),
                  pl.BlockSpec((1, 1, tq, nsel), lambda bi, gi, i, j: (bi, gi, i, 0)),
                  pl.BlockSpec((1, 1, tk, dh), kv_map), pl.BlockSpec((1, 1, tk, dh), kv_map)],
        out_specs=pl.BlockSpec((1, hg, tq, dh), lambda bi, gi, i, j: (bi, gi, i, 0)),
        out_shape=jax.ShapeDtypeStruct((b, nh, s, dh), BF16),
        scratch_shapes=[pltpu.VMEM((rows, nsel + dh), BF16), pltpu.VMEM((rows, 1), F32),
                        pltpu.VMEM((rows, 1), F32), pltpu.VMEM((rows, dh), F32)],
        compiler_params=_params(("parallel", "parallel", "parallel", "arbitrary")), name="nsa_sel_attn",
    )(q, bias, k_s, v_s)


def _win_attn_kernel(q_ref, *refs, tq, nwin):
    k_refs, v_refs, o_ref = refs[:nwin], refs[nwin:2 * nwin], refs[2 * nwin]
    i = pl.program_id(2)
    hg = q_ref.shape[1]
    rows = hg * tq
    nk = nwin * tq
    k = jnp.concatenate([r[0, 0] for r in k_refs], axis=0)
    v = jnp.concatenate([r[0, 0] for r in v_refs], axis=0)
    s = lax.dot_general(q_ref[0].reshape(rows, HEAD_DIM), k, (((1,), (1,)), ((), ())),
                        preferred_element_type=F32)
    t = i * tq + lax.broadcasted_iota(I32, (tq, nk), 0)
    sp = i * tq - WINDOW + lax.broadcasted_iota(I32, (tq, nk), 1)
    mask = (sp <= t) & (sp > t - WINDOW) & (sp >= 0)
    s = jnp.where(mask[None], s.reshape(hg, tq, nk), NEG)
    m = jnp.max(s, axis=-1, keepdims=True)
    p = jnp.where(mask[None], jnp.exp(s - m), 0.0)
    l = jnp.sum(p, axis=-1, keepdims=True)
    o = jnp.dot(p.astype(BF16).reshape(rows, nk), v, preferred_element_type=F32)
    o_ref[0] = (o.reshape(hg, tq, HEAD_DIM) / l).astype(o_ref.dtype)


def _win_attn(q, k_w, v_w, s, tq=256):
    b, nh, _, dh = q.shape
    g, hg = N_KV_GROUPS, HEADS_PER_GROUP
    nwin = WINDOW // tq + 1
    pad = ((0, 0), (0, 0), (WINDOW, 0), (0, 0))
    kp, vp = jnp.pad(k_w, pad), jnp.pad(v_w, pad)
    kv_specs = [pl.BlockSpec((1, 1, tq, dh), functools.partial(lambda bi, gi, i, w: (bi, gi, i + w, 0), w=w))
                for w in range(nwin)]
    kern = functools.partial(_win_attn_kernel, tq=tq, nwin=nwin)
    return pl.pallas_call(
        kern, grid=(b, g, s // tq),
        in_specs=[pl.BlockSpec((1, hg, tq, dh), lambda bi, gi, i: (bi, gi, i, 0))] + kv_specs + kv_specs,
        out_specs=pl.BlockSpec((1, hg, tq, dh), lambda bi, gi, i: (bi, gi, i, 0)),
        out_shape=jax.ShapeDtypeStruct((b, nh, s, dh), BF16),
        compiler_params=_params(("parallel", "parallel", "parallel")), name="nsa_win_attn",
    )(q, *([kp] * nwin), *([vp] * nwin))


def _causal_conv(x, tail, w_ref, b_ref):
    n = x.shape[0]
    xx = jnp.concatenate([tail, x], axis=0)
    out = b_ref[...] + w_ref[CONV_WIDTH - 1:CONV_WIDTH, :] * x
    for k in range(CONV_WIDTH - 1):
        off = 8 - (CONV_WIDTH - 1) + k
        out = out + w_ref[k:k + 1, :] * xx[off:off + n, :]
    return out


def _rglru_kernel(x_ref, gate_ref, pos_ref, cw_ref, cb_ref, wr_ref, br_ref, wi_ref, bi_ref, lam_ref,
                  o_ref, tail_ref, h_ref):
    @pl.when(pl.program_id(1) == 0)
    def _():
        tail_ref[...] = jnp.zeros(tail_ref.shape, F32)
        h_ref[...] = jnp.zeros(h_ref.shape, F32)

    n = x_ref.shape[0]
    x = x_ref[...].astype(F32)
    xc = _causal_conv(x, tail_ref[...], cw_ref, cb_ref)
    tail_ref[...] = x[n - 8:, :]
    xcb = xc.astype(BF16)
    r = _sigmoid(jnp.dot(xcb, wr_ref[...], preferred_element_type=F32) + br_ref[...])
    gi = _sigmoid(jnp.dot(xcb, wi_ref[...], preferred_element_type=F32) + bi_ref[...])
    log_a = -LRU_C * r * _softplus(-lam_ref[...])
    reset = pos_ref[...] == 0
    a = jnp.where(reset, 0.0, jnp.exp(log_a))
    mult = jnp.where(reset, 1.0, jnp.sqrt(jnp.maximum(1.0 - jnp.exp(2.0 * log_a), 0.0)))
    bb = mult * (gi * xc)
    row = lax.broadcasted_iota(I32, a.shape, 0)
    d = 1
    while d < n:
        a_sh = pltpu.roll(a, d, 0)
        b_sh = pltpu.roll(bb, d, 0)
        live = row >= d
        bb = jnp.where(live, a * b_sh + bb, bb)
        a = jnp.where(live, a * a_sh, a)
        d *= 2
    h = bb + a * h_ref[...]
    h_ref[...] = h[n - 1:n, :]
    o_ref[...] = (h * _gelu(gate_ref[...].astype(F32))).astype(o_ref.dtype)


def _block_diag(w):
    nb, bs, _ = w.shape
    eye = jnp.eye(nb, dtype=w.dtype)
    return (w[:, :, None, :] * eye[:, None, :, None]).reshape(nb * bs, nb * bs)


def _rglru(l_x, l_g, positions, conv_w, conv_b, w_r, b_r, w_i, b_i, lam, b, s, ts=256):
    d = l_x.shape[1]
    nt = s // ts
    row = lambda bi, i: (bi * nt + i, 0)
    c2 = lambda bi, i: (0, 0)
    vec = lambda v: v.reshape(1, d).astype(F32)
    return pl.pallas_call(
        _rglru_kernel, grid=(b, nt),
        in_specs=[pl.BlockSpec((ts, d), row), pl.BlockSpec((ts, d), row), pl.BlockSpec((ts, 1), row),
                  pl.BlockSpec((CONV_WIDTH, d), c2), pl.BlockSpec((1, d), c2),
                  pl.BlockSpec((d, d), c2), pl.BlockSpec((1, d), c2),
                  pl.BlockSpec((d, d), c2), pl.BlockSpec((1, d), c2), pl.BlockSpec((1, d), c2)],
        out_specs=pl.BlockSpec((ts, d), row),
        out_shape=jax.ShapeDtypeStruct((b * s, d), BF16),
        scratch_shapes=[pltpu.VMEM((8, d), F32), pltpu.VMEM((1, d), F32)],
        compiler_params=_params(("parallel", "arbitrary")), name="rglru",
    )(l_x, l_g, positions.reshape(b * s, 1).astype(I32), conv_w.astype(F32), vec(conv_b),
      _block_diag(w_r).astype(BF16), vec(b_r), _block_diag(w_i).astype(BF16), vec(b_i), vec(lam))


def _ssd_kernel(z_ref, xbc_ref, dt_ref, cw_ref, cb_ref, dtb_ref, alog_ref, dfull_ref, ng_ref,
                o_ref, tail_ref, state_ref, y_ref):
    @pl.when(pl.program_id(1) == 0)
    def _():
        tail_ref[...] = jnp.zeros(tail_ref.shape, F32)
        state_ref[...] = jnp.zeros(state_ref.shape, F32)

    n = xbc_ref.shape[0]
    hg = SSM_HEADS // SSM_GROUPS
    x = xbc_ref[...].astype(F32)
    xc = _causal_conv(x, tail_ref[...], cw_ref, cb_ref)
    tail_ref[...] = x[n - 8:, :]
    xc = xc * _sigmoid(xc)
    xs = xc[:, :SSM_INNER]
    dt = _softplus(dt_ref[...] + dtb_ref[...])
    adt = dt * (-jnp.exp(alog_ref[...]))
    row = lax.broadcasted_iota(I32, adt.shape, 0)
    acs = adt
    d = 1
    while d < n:
        acs = acs + jnp.where(row >= d, pltpu.roll(acs, d, 0), 0.0)
        d *= 2
    acs_t = acs.T
    li = lax.broadcasted_iota(I32, (n, n), 0)
    si = lax.broadcasted_iota(I32, (n, n), 1)
    tri = li >= si
    for g in range(SSM_GROUPS):
        bm = xc[:, SSM_INNER + g * SSM_STATE:SSM_INNER + (g + 1) * SSM_STATE].astype(BF16)
        cm = xc[:, SSM_INNER + (SSM_GROUPS + g) * SSM_STATE:SSM_INNER + (SSM_GROUPS + g + 1) * SSM_STATE].astype(BF16)
        cb = lax.dot_general(cm, bm, (((1,), (1,)), ((), ())), preferred_element_type=F32)
        bm_t = bm.T
        for hh in range(hg):
            h = g * hg + hh
            acol = acs[:, h:h + 1]
            arow = acs_t[h:h + 1, :]
            decay = jnp.exp(jnp.where(tri, acol - arow, NEG))
            xh = xs[:, h * SSM_HEAD_DIM:(h + 1) * SSM_HEAD_DIM] * dt[:, h:h + 1]
            a_last = acs[n - 1:n, h:h + 1]
            y = jnp.dot((cb * decay).astype(BF16), xh.astype(BF16), preferred_element_type=F32)
            st = state_ref[h]
            y = y + jnp.dot(cm, st.astype(BF16), preferred_element_type=F32) * jnp.exp(acol)
            upd = jnp.dot(bm_t, (xh * jnp.exp(a_last - acol)).astype(BF16), preferred_element_type=F32)
            state_ref[h] = jnp.exp(a_last) * st + upd
            y_ref[:, h * SSM_HEAD_DIM:(h + 1) * SSM_HEAD_DIM] = y
    z = z_ref[...].astype(F32)
    y = (y_ref[...] + dfull_ref[...] * xs) * (z * _sigmoid(z))
    gw = SSM_INNER // SSM_GROUPS
    for g in range(SSM_GROUPS):
        yg = y[:, g * gw:(g + 1) * gw]
        ms = jnp.mean(yg * yg, axis=-1, keepdims=True)
        o_ref[:, g * gw:(g + 1) * gw] = (yg * lax.rsqrt(ms + EPS) * ng_ref[:, g * gw:(g + 1) * gw]).astype(o_ref.dtype)


def _ssd(s_z, s_xbc, s_dt, conv_w, conv_b, dt_bias, a_log, d_skip, norm_g, b, s):
    n = SSM_CHUNK
    nt = s // n
    c = s_xbc.shape[1]
    row = lambda bi, i: (bi * nt + i, 0)
    c2 = lambda bi, i: (0, 0)
    pad_h = lambda v: jnp.zeros((1, LANES), F32).at[0, :SSM_HEADS].set(v.astype(F32))
    dfull = jnp.repeat(d_skip.astype(F32), SSM_HEAD_DIM).reshape(1, SSM_INNER)
    return pl.pallas_call(
        _ssd_kernel, grid=(b, nt),
        in_specs=[pl.BlockSpec((n, SSM_INNER), row), pl.BlockSpec((n, c), row), pl.BlockSpec((n, LANES), row),
                  pl.BlockSpec((CONV_WIDTH, c), c2), pl.BlockSpec((1, c), c2),
                  pl.BlockSpec((1, LANES), c2), pl.BlockSpec((1, LANES), c2),
                  pl.BlockSpec((1, SSM_INNER), c2), pl.BlockSpec((1, SSM_INNER), c2)],
        out_specs=pl.BlockSpec((n, SSM_INNER), row),
        out_shape=jax.ShapeDtypeStruct((b * s, SSM_INNER), BF16),
        scratch_shapes=[pltpu.VMEM((8, c), F32), pltpu.VMEM((SSM_HEADS, SSM_STATE, SSM_HEAD_DIM), F32),
                        pltpu.VMEM((n, SSM_INNER), F32)],
        compiler_params=_params(("parallel", "arbitrary")), name="ssd",
    )(s_z, s_xbc, s_dt, conv_w.astype(F32), conv_b.reshape(1, c).astype(F32), pad_h(dt_bias), pad_h(a_log),
      dfull, norm_g.reshape(1, SSM_INNER).astype(F32))


def _merge_kernel(oc_ref, os_ref, ow_ref, ag_ref, yb_ref, yc_ref, mg_ref, x_ref, wb_ref, wo_ref,
                  o_ref, ya_ref):
    gates = _sigmoid(ag_ref[...])
    for h in range(N_HEADS):
        ya = (gates[:, 3 * h:3 * h + 1] * oc_ref[0, h].astype(F32)
              + gates[:, 3 * h + 1:3 * h + 2] * os_ref[0, h].astype(F32)
              + gates[:, 3 * h + 2:3 * h + 3] * ow_ref[0, h].astype(F32))
        ya_ref[:, h * HEAD_DIM:(h + 1) * HEAD_DIM] = ya.astype(ya_ref.dtype)
    d = D_MODEL
    merged = _sigmoid(mg_ref[:, 0:d].astype(F32)) * jnp.dot(ya_ref[...], wb_ref[0], preferred_element_type=F32)
    merged += _sigmoid(mg_ref[:, d:2 * d].astype(F32)) * jnp.dot(yb_ref[...], wb_ref[1], preferred_element_type=F32)
    merged += _sigmoid(mg_ref[:, 2 * d:3 * d].astype(F32)) * jnp.dot(yc_ref[...], wb_ref[2], preferred_element_type=F32)
    o_ref[...] = x_ref[...] + jnp.dot(merged.astype(BF16), wo_ref[...], preferred_element_type=F32)


def _merge(o_c, o_s, o_w, a_g, y_b, y_c, m_g, x, w_branch, w_out, b, s, tm=256):
    d = D_MODEL
    nt = s // tm
    row = lambda i: (i, 0)
    o_spec = pl.BlockSpec((1, N_HEADS, tm, HEAD_DIM), lambda i: (i // nt, 0, i % nt, 0))
    return pl.pallas_call(
        _merge_kernel, grid=(b * nt,),
        in_specs=[o_spec, o_spec, o_spec, pl.BlockSpec((tm, LANES), row), pl.BlockSpec((tm, d), row),
                  pl.BlockSpec((tm, d), row), pl.BlockSpec((tm, 3 * d), row), pl.BlockSpec((tm, d), row),
                  pl.BlockSpec((3, d, d), lambda i: (0, 0, 0)), pl.BlockSpec((d, d), lambda i: (0, 0))],
        out_specs=pl.BlockSpec((tm, d), row),
        out_shape=jax.ShapeDtypeStruct((b * s, d), F32),
        scratch_shapes=[pltpu.VMEM((tm, d), BF16)],
        compiler_params=_params(("parallel",)), name="merge",
    )(o_c, o_s, o_w, a_g, y_b, y_c, m_g, x, w_branch.astype(BF16), w_out.astype(BF16))


def _topk_rows(s, k):
    n = s.shape[0]
    row = lax.broadcasted_iota(I32, s.shape, 0)
    vals, idxs = [], []
    for _ in range(k):
        m = jnp.max(s, axis=0, keepdims=True)
        idx = jnp.min(jnp.where(s == m, row, n), axis=0, keepdims=True)
        vals.append(m)
        idxs.append(idx)
        s = jnp.where(row == idx, -jnp.inf, s)
    return jnp.concatenate(vals, axis=0), jnp.concatenate(idxs, axis=0)


def _pick_rows(table, sel, k):
    out = jnp.zeros(sel.shape, table.dtype)
    for a in range(k):
        out = jnp.where(sel == a, table[a:a + 1, :], out)
    return out


def _peer_route_kernel(h_ref, wq_ref, keys_ref, i1_ref, i2_ref, g_ref):
    k = PEER_TOPK
    qt = lax.dot_general(wq_ref[...], h_ref[...], (((1,), (1,)), ((), ())), preferred_element_type=F32)
    qt = qt.astype(BF16)
    i1s, i2s, gs = [], [], []
    for hd in range(PEER_HEADS):
        tops = []
        for half in range(2):
            c = hd * 2 + half
            sc = jnp.dot(keys_ref[c], qt[c * PEER_HALF:(c + 1) * PEER_HALF, :], preferred_element_type=F32)
            tops.append(_topk_rows(sc, k))
        (s1, i1), (s2, i2) = tops
        cand = jnp.concatenate([s1[a:a + 1, :] + s2 for a in range(k)], axis=0)
        sc, pos = _topk_rows(cand, k)
        e = jnp.exp(sc - sc[0:1, :])
        gs.append(e / jnp.sum(e, axis=0, keepdims=True))
        i1s.append(_pick_rows(i1, pos // k, k))
        i2s.append(_pick_rows(i2, pos % k, k))
    i1_ref[...] = jnp.concatenate(i1s, axis=0).T
    i2_ref[...] = jnp.concatenate(i2s, axis=0).T
    g_ref[...] = jnp.concatenate(gs, axis=0).T


def _peer_route(h, w_q, sub_keys, tm=256):
    t, d = h.shape
    nq = w_q.shape[1]
    wq_t = w_q.T.astype(BF16)
    keys = sub_keys.reshape(PEER_HEADS * 2, PEER_KEYS, PEER_HALF).astype(BF16)
    slots = PEER_HEADS * PEER_TOPK
    row = lambda i: (i, 0)
    return pl.pallas_call(
        _peer_route_kernel, grid=(t // tm,),
        in_specs=[pl.BlockSpec((tm, d), row), pl.BlockSpec((nq, d), lambda i: (0, 0)),
                  pl.BlockSpec((PEER_HEADS * 2, PEER_KEYS, PEER_HALF), lambda i: (0, 0, 0))],
        out_specs=[pl.BlockSpec((tm, slots), row)] * 3,
        out_shape=[jax.ShapeDtypeStruct((t, slots), I32), jax.ShapeDtypeStruct((t, slots), I32),
                   jax.ShapeDtypeStruct((t, slots), F32)],
        compiler_params=_params(("parallel",)), name="peer_route",
    )(h, wq_t, keys)


def _peer_u_kernel(h_ref, u_ref, i1_ref, i2_ref, o_ref, *, blocks):
    c = pl.program_id(0)
    a = lax.dot_general(h_ref[...], u_ref[...], (((1,), (1,)), ((), ())), preferred_element_type=F32)
    i1 = i1_ref[...]
    i2 = i2_ref[...]
    acc = jnp.zeros(i1.shape, F32)
    for bk in range(blocks):
        got = jnp.take_along_axis(a[:, bk * PEER_KEYS:(bk + 1) * PEER_KEYS], i2, axis=1)
        acc = jnp.where(i1 == c * blocks + bk, got, acc)
    o_ref[0] = acc


def _peer_u(h, u, i1, i2, tm=512, blocks=16):
    t, d = h.shape
    ne = u.shape[0]
    ec = blocks * PEER_KEYS
    nchunk = ne // ec
    slots = i1.shape[1]
    kern = functools.partial(_peer_u_kernel, blocks=blocks)
    return pl.pallas_call(
        kern, grid=(nchunk, t // tm),
        in_specs=[pl.BlockSpec((tm, d), lambda c, i: (i, 0)), pl.BlockSpec((ec, d), lambda c, i: (c, 0)),
                  pl.BlockSpec((tm, slots), lambda c, i: (i, 0)), pl.BlockSpec((tm, slots), lambda c, i: (i, 0))],
        out_specs=pl.BlockSpec((1, tm, slots), lambda c, i: (c, i, 0)),
        out_shape=jax.ShapeDtypeStruct((nchunk, t, slots), F32),
        compiler_params=_params(("parallel", "parallel")), name="peer_u",
    )(h, u, i1, i2)


def _peer_v_kernel(parts_ref, g_ref, i1_ref, i2_ref, v_ref, x_ref, o_ref, w_ref, wg_ref, *, tm, blocks):
    c = pl.program_id(1)
    nk = PEER_KEYS

    @pl.when(c == 0)
    def _():
        w_ref[...] = g_ref[...] * _gelu(jnp.sum(parts_ref[...], axis=0))
        o_ref[...] = x_ref[...]
        sub = lax.broadcasted_iota(I32, (nk, w_ref.shape[1]), 0)

        def per_token(t, carry):
            wrow = w_ref[pl.ds(t, 1), :]
            lhs = jnp.where(i1_ref[pl.ds(t, 1), :] == sub, wrow, 0.0).astype(BF16)
            rhs = jnp.where(i2_ref[pl.ds(t, 1), :] == sub, 1.0, 0.0).astype(BF16)
            grid = lax.dot_general(lhs, rhs, (((1,), (1,)), ((), ())), preferred_element_type=F32)
            wg_ref[pl.ds(pl.multiple_of(t * nk, nk), nk), :] = grid
            return carry

        lax.fori_loop(0, tm, per_token, 0)

    acc = jnp.zeros(o_ref.shape, F32)
    for bk in range(blocks):
        i1 = c * blocks + bk
        lhs = wg_ref[pl.ds(i1, tm, stride=nk), :].astype(BF16)
        acc += jnp.dot(lhs, v_ref[bk], preferred_element_type=F32)
    o_ref[...] += acc


def _peer_v(parts, g, i1, i2, v, x, tm=256, blocks=32):
    t, d = x.shape
    nchunk_u = parts.shape[0]
    slots = g.shape[1]
    nk = PEER_KEYS
    v3 = v.reshape(nk, nk, d)
    kern = functools.partial(_peer_v_kernel, tm=tm, blocks=blocks)
    row = lambda i, c: (i, 0)
    return pl.pallas_call(
        kern, grid=(t // tm, nk // blocks),
        in_specs=[pl.BlockSpec((nchunk_u, tm, slots), lambda i, c: (0, i, 0)),
                  pl.BlockSpec((tm, slots), row), pl.BlockSpec((tm, slots), row), pl.BlockSpec((tm, slots), row),
                  pl.BlockSpec((blocks, nk, d), lambda i, c: (c, 0, 0)), pl.BlockSpec((tm, d), row)],
        out_specs=pl.BlockSpec((tm, d), row),
        out_shape=jax.ShapeDtypeStruct((t, d), F32),
        scratch_shapes=[pltpu.VMEM((tm, slots), F32), pltpu.VMEM((tm * nk, nk), F32)],
        compiler_params=_params(("parallel", "arbitrary")), name="peer_v",
    )(parts, g, i1, i2, v3, x)


IN_SIZES = (1024, 768, 48, 1024, 1024, 1024, 1536, 16, 3072)


def _pad_cols(w, n):
    return jnp.pad(w, ((0, 0), (0, n - w.shape[1])))


def _mixer(x, positions, b, s, p):
    xn = _rmsnorm(x, p["mix_norm_g"])
    offs = [0]
    for n in IN_SIZES:
        offs.append(offs[-1] + n)
    w = [p["w_in"][:, offs[i]:offs[i + 1]] for i in range(len(IN_SIZES))]
    a_q = _matmul(xn, w[0].astype(BF16), BF16, "proj_q")
    a_kv = _matmul(xn, w[1].astype(BF16), BF16, "proj_kv")
    a_g = _matmul(xn, _pad_cols(w[2], LANES).astype(BF16), F32, "proj_ag")
    l_x = _matmul(xn, w[3].astype(BF16), BF16, "proj_lx")
    l_g = _matmul(xn, w[4].astype(BF16), BF16, "proj_lg")
    s_z = _matmul(xn, w[5].astype(BF16), BF16, "proj_sz")
    s_xbc = _matmul(xn, w[6].astype(BF16), BF16, "proj_sxbc")
    s_dt = _matmul(xn, _pad_cols(w[7], LANES).astype(BF16), F32, "proj_sdt")
    m_g = _matmul(xn, w[8].astype(BF16), BF16, "proj_mg", tn=1024)

    q, kc, vc, ks, vs, kw, vw = _nsa_prep(a_q, a_kv, positions, p["q_norm_g"], p["k_norm_g"], b, s)
    k_cmp, v_cmp = _compress(kc, vc, p["cmp_pe_k"], p["cmp_pe_v"], p["cmp_k_w1"], p["cmp_k_w2"],
                             p["cmp_v_w1"], p["cmp_v_w2"], p["k_norm_g"])
    o_c, bias = _cmp_select(q, k_cmp, v_cmp, s)
    o_s = _sel_attn(q, bias, ks, vs, s)
    o_w = _win_attn(q, kw, vw, s)
    y_b = _rglru(l_x, l_g, positions, p["lru_conv_w"], p["lru_conv_b"], p["lru_w_r"], p["lru_b_r"],
                 p["lru_w_i"], p["lru_b_i"], p["lru_lambda"], b, s)
    y_c = _ssd(s_z, s_xbc, s_dt, p["ssm_conv_w"], p["ssm_conv_b"], p["ssm_dt_bias"], p["ssm_a_log"],
               p["ssm_d"], p["ssm_norm_g"], b, s)
    return _merge(o_c, o_s, o_w, a_g, y_b, y_c, m_g, x, p["w_branch"], p["w_out"], b, s)


def _peer(x, p):
    h = _rmsnorm(x, p["ffn_norm_g"])
    i1, i2, g = _peer_route(h, p["peer_w_q"], p["peer_sub_keys"])
    parts = _peer_u(h, p["peer_u"].astype(BF16), i1, i2)
    return _peer_v(parts, g, i1, i2, p["peer_v"].astype(BF16), x)


_LAYER_PARAMS = ("mix_norm_g", "w_in", "q_norm_g", "k_norm_g", "cmp_pe_k", "cmp_pe_v", "cmp_k_w1", "cmp_k_w2",
                 "cmp_v_w1", "cmp_v_w2", "lru_conv_w", "lru_conv_b", "lru_w_r", "lru_b_r", "lru_w_i", "lru_b_i",
                 "lru_lambda", "ssm_conv_w", "ssm_conv_b", "ssm_dt_bias", "ssm_a_log", "ssm_d", "ssm_norm_g",
                 "w_branch", "w_out", "ffn_norm_g", "peer_w_q", "peer_sub_keys", "peer_u", "peer_v")


def kernel(x, positions, mix_norm_g, w_in, q_norm_g, k_norm_g, cmp_pe_k, cmp_pe_v, cmp_k_w1, cmp_k_w2, cmp_v_w1, cmp_v_w2, lru_conv_w, lru_conv_b, lru_w_r, lru_b_r, lru_w_i, lru_b_i, lru_lambda, ssm_conv_w, ssm_conv_b, ssm_dt_bias, ssm_a_log, ssm_d, ssm_norm_g, w_branch, w_out, ffn_norm_g, peer_w_q, peer_sub_keys, peer_u, peer_v):
    stacked = dict(zip(_LAYER_PARAMS, (mix_norm_g, w_in, q_norm_g, k_norm_g, cmp_pe_k, cmp_pe_v, cmp_k_w1,
                                       cmp_k_w2, cmp_v_w1, cmp_v_w2, lru_conv_w, lru_conv_b, lru_w_r, lru_b_r,
                                       lru_w_i, lru_b_i, lru_lambda, ssm_conv_w, ssm_conv_b, ssm_dt_bias,
                                       ssm_a_log, ssm_d, ssm_norm_g, w_branch, w_out, ffn_norm_g, peer_w_q,
                                       peer_sub_keys, peer_u, peer_v)))
    b, s, d = x.shape
    xf = x.reshape(b * s, d).astype(F32)
    for layer in range(mix_norm_g.shape[0]):
        p = {name: arr[layer] for name, arr in stacked.items()}
        xf = _mixer(xf, positions, b, s, p)
        xf = _peer(xf, p)
    return xf.reshape(b, s, d).astype(x.dtype)
```

```python
import functools
import math

import jax
import jax.numpy as jnp
from jax import lax
from jax.experimental import pallas as pl
from jax.experimental.pallas import tpu as pltpu

F32 = jnp.float32
BF16 = jnp.bfloat16
I32 = jnp.int32

D_MODEL = 1024
HEAD_DIM = 64
N_HEADS = 16
N_KV_GROUPS = 2
HEADS_PER_GROUP = 8
CMP_BLOCK = 32
CMP_STRIDE = 16
CMP_HIDDEN = 256
SEL_BLOCK = 64
N_SELECT = 16
WINDOW = 512
SEL_FORCE = 100.0
ROPE_THETA = 10000.0
SCALE = HEAD_DIM ** -0.5
LOG2E = math.log2(math.e)
LRU_HEADS = 16
LRU_BLOCK = 64
LRU_C = 8.0
CONV_WIDTH = 4
SSM_HEADS = 16
SSM_HEAD_DIM = 64
SSM_GROUPS = 2
SSM_STATE = 128
SSM_CHUNK = 128
SSM_INNER = 1024
PEER_HEADS = 8
PEER_KEYS = 128
PEER_HALF = 128
PEER_TOPK = 16
EPS = 1e-6
NEG = -1e30
LANES = 128

VMEM_LIMIT = 56 * 1024 * 1024


def _params(sem):
    return pltpu.CompilerParams(dimension_semantics=sem, vmem_limit_bytes=VMEM_LIMIT)


def _gelu(x):
    return 0.5 * x * (1.0 + jnp.tanh(math.sqrt(2.0 / math.pi) * (x + 0.044715 * x * x * x)))


def _sigmoid(x):
    return 1.0 / (1.0 + jnp.exp(-x))


def _softplus(x):
    return jnp.maximum(x, 0.0) + jnp.log(1.0 + jnp.exp(-jnp.abs(x)))


def _rmsnorm_kernel(x_ref, g_ref, o_ref):
    x = x_ref[...].astype(F32)
    ms = jnp.mean(x * x, axis=-1, keepdims=True)
    o_ref[...] = (x * lax.rsqrt(ms + EPS) * g_ref[...]).astype(o_ref.dtype)


def _rmsnorm(x, g, tm=512):
    t, d = x.shape
    return pl.pallas_call(
        _rmsnorm_kernel, grid=(t // tm,),
        in_specs=[pl.BlockSpec((tm, d), lambda i: (i, 0)), pl.BlockSpec((1, d), lambda i: (0, 0))],
        out_specs=pl.BlockSpec((tm, d), lambda i: (i, 0)),
        out_shape=jax.ShapeDtypeStruct((t, d), BF16),
        compiler_params=_params(("parallel",)), name="rmsnorm")(x, g.reshape(1, d).astype(F32))


def _mm_kernel(x_ref, w_ref, o_ref):
    o_ref[...] = jnp.dot(x_ref[...], w_ref[...], preferred_element_type=F32).astype(o_ref.dtype)


def _matmul(x, w, out_dtype, name, tm=512, tn=None):
    t, k = x.shape
    n = w.shape[1]
    tn = n if tn is None else tn
    return pl.pallas_call(
        _mm_kernel, grid=(n // tn, t // tm),
        in_specs=[pl.BlockSpec((tm, k), lambda j, i: (i, 0)), pl.BlockSpec((k, tn), lambda j, i: (0, j))],
        out_specs=pl.BlockSpec((tm, tn), lambda j, i: (i, j)),
        out_shape=jax.ShapeDtypeStruct((t, n), out_dtype),
        compiler_params=_params(("parallel", "parallel")), name=name)(x, w)


def _norm_rope(x, g, cos, sin_signed, bd):
    ms = jnp.dot((x * x).astype(BF16), bd, preferred_element_type=F32)
    y = x * lax.rsqrt(ms + EPS) * g
    lane = lax.broadcasted_iota(I32, y.shape, 1)
    first_half = (lane % HEAD_DIM) < (HEAD_DIM // 2)
    partner = jnp.where(first_half, pltpu.roll(y, LANES - HEAD_DIM // 2, 1), pltpu.roll(y, HEAD_DIM // 2, 1))
    return y * cos + partner * sin_signed


def _nsa_prep_kernel(aq_ref, akv_ref, pos_ref, inv_ref, sgn_ref, gq_ref, gk_ref, bd_ref,
                     q_ref, kc_ref, vc_ref, ks_ref, vs_ref, kw_ref, vw_ref):
    ang = pos_ref[...].astype(F32) * inv_ref[...]
    cos = jnp.cos(ang)
    sin_signed = jnp.sin(ang) * sgn_ref[...]
    bd = bd_ref[...]
    for c in range(D_MODEL // LANES):
        x = aq_ref[:, c * LANES:(c + 1) * LANES].astype(F32)
        y = (_norm_rope(x, gq_ref[...], cos, sin_signed, bd) * (SCALE * LOG2E)).astype(q_ref.dtype)
        q_ref[0, 2 * c] = y[:, :HEAD_DIM]
        q_ref[0, 2 * c + 1] = y[:, HEAD_DIM:]
    outs = (kc_ref, vc_ref, ks_ref, vs_ref, kw_ref, vw_ref)
    for c, o_ref in enumerate(outs):
        x = akv_ref[:, c * LANES:(c + 1) * LANES]
        if c % 2 == 0:
            y = _norm_rope(x.astype(F32), gk_ref[...], cos, sin_signed, bd).astype(o_ref.dtype)
        else:
            y = x.astype(o_ref.dtype)
        o_ref[0, 0] = y[:, :HEAD_DIM]
        o_ref[0, 1] = y[:, HEAD_DIM:]


def _nsa_prep(a_q, a_kv, positions, q_norm_g, k_norm_g, b, s, tm=512):
    half = HEAD_DIM // 2
    lane = jnp.arange(LANES)
    inv = (ROPE_THETA ** (-((lane % half).astype(F32)) / half)).reshape(1, LANES)
    sgn = jnp.where((lane % HEAD_DIM) < half, -1.0, 1.0).astype(F32).reshape(1, LANES)
    bd = jnp.where((lane[:, None] // HEAD_DIM) == (lane[None, :] // HEAD_DIM), 1.0 / HEAD_DIM, 0.0).astype(BF16)
    gq = jnp.tile(q_norm_g.astype(F32), 2).reshape(1, LANES)
    gk = jnp.tile(k_norm_g.astype(F32), 2).reshape(1, LANES)
    nt = s // tm
    row = lambda i: (i, 0)
    const = lambda i: (0, 0)
    kv_shape = jax.ShapeDtypeStruct((b, N_KV_GROUPS, s, HEAD_DIM), BF16)
    kv_spec = pl.BlockSpec((1, N_KV_GROUPS, tm, HEAD_DIM), lambda i: (i // nt, 0, i % nt, 0))
    return pl.pallas_call(
        _nsa_prep_kernel, grid=(b * nt,),
        in_specs=[pl.BlockSpec((tm, D_MODEL), row), pl.BlockSpec((tm, 6 * LANES), row),
                  pl.BlockSpec((tm, 1), row), pl.BlockSpec((1, LANES), const), pl.BlockSpec((1, LANES), const),
                  pl.BlockSpec((1, LANES), const), pl.BlockSpec((1, LANES), const),
                  pl.BlockSpec((LANES, LANES), const)],
        out_specs=[pl.BlockSpec((1, N_HEADS, tm, HEAD_DIM), lambda i: (i // nt, 0, i % nt, 0))] + [kv_spec] * 6,
        out_shape=[jax.ShapeDtypeStruct((b, N_HEADS, s, HEAD_DIM), BF16)] + [kv_shape] * 6,
        compiler_params=_params(("parallel",)), name="nsa_prep",
    )(a_q, a_kv, positions.reshape(b * s, 1).astype(I32), inv, sgn, gq, gk, bd)


def _compress_kernel(uk_ref, uv_ref, pek_ref, pev_ref, kw1_ref, kw2_ref, vw1_ref, vw2_ref, gk_ref,
                     kc_ref, vc_ref):
    half = CMP_STRIDE * HEAD_DIM

    def mlp(u, pe, w1_ref, w2_ref):
        n = u.shape[0]
        ha = jnp.dot(u, w1_ref[:half, :], preferred_element_type=F32)
        hb = jnp.dot(u, w1_ref[half:, :], preferred_element_type=F32)
        bias = jnp.dot(pe, w1_ref[...], preferred_element_type=F32)[0:1, :]
        pre = ha + pltpu.roll(hb, n - 1, 0) + bias
        return jnp.dot(_gelu(pre).astype(BF16), w2_ref[...], preferred_element_type=F32)

    k = mlp(uk_ref[0, 0], pek_ref[...], kw1_ref, kw2_ref)
    ms = jnp.mean(k * k, axis=-1, keepdims=True)
    kc_ref[0, 0] = (k * lax.rsqrt(ms + EPS) * gk_ref[...]).astype(kc_ref.dtype)
    vc_ref[0, 0] = mlp(uv_ref[0, 0], pev_ref[...], vw1_ref, vw2_ref).astype(vc_ref.dtype)


def _compress(kc, vc, pe_k, pe_v, kw1, kw2, vw1, vw2, k_norm_g):
    b, g, s, dh = kc.shape
    ng = s // CMP_STRIDE
    wide = CMP_BLOCK * dh
    uk = kc.reshape(b, g, ng, CMP_STRIDE * dh)
    uv = vc.reshape(b, g, ng, CMP_STRIDE * dh)
    pek = jnp.zeros((8, wide), BF16).at[0].set(pe_k.reshape(wide).astype(BF16))
    pev = jnp.zeros((8, wide), BF16).at[0].set(pe_v.reshape(wide).astype(BF16))
    u_spec = pl.BlockSpec((1, 1, ng, CMP_STRIDE * dh), lambda i, j: (i, j, 0, 0))
    c2 = lambda i, j: (0, 0)
    o_spec = pl.BlockSpec((1, 1, ng, dh), lambda i, j: (i, j, 0, 0))
    o_shape = jax.ShapeDtypeStruct((b, g, ng, dh), BF16)
    return pl.pallas_call(
        _compress_kernel, grid=(b, g),
        in_specs=[u_spec, u_spec, pl.BlockSpec((8, wide), c2), pl.BlockSpec((8, wide), c2),
                  pl.BlockSpec((wide, CMP_HIDDEN), c2), pl.BlockSpec((CMP_HIDDEN, dh), c2),
                  pl.BlockSpec((wide, CMP_HIDDEN), c2), pl.BlockSpec((CMP_HIDDEN, dh), c2),
                  pl.BlockSpec((1, dh), c2)],
        out_specs=[o_spec, o_spec], out_shape=[o_shape, o_shape],
        compiler_params=_params(("parallel", "parallel")), name="nsa_compress",
    )(uk, uv, pek, pev, kw1.astype(BF16), kw2.astype(BF16), vw1.astype(BF16), vw2.astype(BF16),
      k_norm_g.reshape(1, dh).astype(F32))


def _cmp_select_kernel(q_ref, kc_ref, vc_ref, ov_ref, o_ref, bias_ref, *, tq, n_cmp, n_pick):
    i = pl.program_id(2)
    hg = q_ref.shape[1]
    nc = kc_ref.shape[2]
    nsel = ov_ref.shape[1]
    q = q_ref[0].reshape(hg * tq, HEAD_DIM)
    s = lax.dot_general(q, kc_ref[0, 0], (((1,), (1,)), ((), ())), preferred_element_type=F32)
    s = s.reshape(hg, tq, nc)
    t = i * tq + lax.broadcasted_iota(I32, (tq, nc), 0)
    c = lax.broadcasted_iota(I32, (tq, nc), 1)
    mask = ((CMP_STRIDE * c + CMP_BLOCK - 1) <= t) & (c < n_cmp)
    s = jnp.where(mask[None], s, NEG)
    m = jnp.max(s, axis=-1, keepdims=True)
    p = jnp.where(mask[None], jnp.exp2(s - m), 0.0)
    l = jnp.sum(p, axis=-1, keepdims=True)
    p = p * jnp.where(l > 0.0, 1.0 / l, 0.0)
    o = jnp.dot(p.astype(BF16).reshape(hg * tq, nc), vc_ref[0, 0], preferred_element_type=F32)
    o_ref[0] = o.reshape(hg, tq, HEAD_DIM).astype(o_ref.dtype)

    psum = jnp.sum(p, axis=0)
    hi = psum.astype(BF16)
    lo = (psum - hi.astype(F32)).astype(BF16)
    imp = (jnp.dot(hi, ov_ref[...], preferred_element_type=F32)
           + jnp.dot(lo, ov_ref[...], preferred_element_type=F32))
    tt = i * tq + lax.broadcasted_iota(I32, (tq, nsel), 0)
    j = lax.broadcasted_iota(I32, (tq, nsel), 1)
    cur = tt // SEL_BLOCK
    forced = (j == 0) | (j == cur) | (j == cur - 1)
    score = jnp.where(forced, SEL_FORCE, jnp.where(j * SEL_BLOCK <= tt, imp, -1.0))
    picked = jnp.zeros((tq, nsel), jnp.bool_)
    for _ in range(n_pick):
        mx = jnp.max(score, axis=-1, keepdims=True)
        idx = jnp.min(jnp.where(score == mx, j, nsel), axis=-1, keepdims=True)
        hit = j == idx
        picked = picked | hit
        score = jnp.where(hit, -jnp.inf, score)
    bias_ref[0, 0] = jnp.where(picked & (j <= cur), 0.0, NEG).astype(bias_ref.dtype)


def _cmp_select(q, k_cmp, v_cmp, s, tq=128):
    b, nh, _, dh = q.shape
    g, hg = N_KV_GROUPS, HEADS_PER_GROUP
    nc = k_cmp.shape[2]
    n_cmp = (s - CMP_BLOCK) // CMP_STRIDE + 1
    nsel = s // SEL_BLOCK
    n_pick = min(N_SELECT, nsel)
    cs = CMP_STRIDE * jnp.arange(nc)
    ss = SEL_BLOCK * jnp.arange(nsel)
    ov = jnp.clip(jnp.minimum(cs[:, None] + CMP_BLOCK, ss[None, :] + SEL_BLOCK)
                  - jnp.maximum(cs[:, None], ss[None, :]), 0).astype(F32) / CMP_BLOCK
    ov = jnp.where(jnp.arange(nc)[:, None] < n_cmp, ov, 0.0).astype(BF16)
    kern = functools.partial(_cmp_select_kernel, tq=tq, n_cmp=n_cmp, n_pick=n_pick)
    return pl.pallas_call(
        kern, grid=(b, g, s // tq),
        in_specs=[pl.BlockSpec((1, hg, tq, dh), lambda bi, gi, i: (bi, gi, i, 0)),
                  pl.BlockSpec((1, 1, nc, dh), lambda bi, gi, i: (bi, gi, 0, 0)),
                  pl.BlockSpec((1, 1, nc, dh), lambda bi, gi, i: (bi, gi, 0, 0)),
                  pl.BlockSpec((nc, nsel), lambda bi, gi, i: (0, 0))],
        out_specs=[pl.BlockSpec((1, hg, tq, dh), lambda bi, gi, i: (bi, gi, i, 0)),
                   pl.BlockSpec((1, 1, tq, nsel), lambda bi, gi, i: (bi, gi, i, 0))],
        out_shape=[jax.ShapeDtypeStruct((b, nh, s, dh), BF16),
                   jax.ShapeDtypeStruct((b, g, s, nsel), BF16)],
        compiler_params=_params(("parallel", "parallel", "parallel")), name="nsa_cmp_select",
    )(q, k_cmp, v_cmp, ov)


def _sel_attn_kernel(it_ref, jt_ref, q_ref, bias_ref, k_ref, v_ref, o_ref, qa_ref, m_ref, acc_ref, *, tq, tk):
    pid = pl.program_id(2)
    i = it_ref[pid]
    j = jt_ref[pid]
    hg = q_ref.shape[1]
    nsel = bias_ref.shape[3]
    last_j = ((i + 1) * tq - 1) // tk

    @pl.when(j == 0)
    def _():
        for h in range(hg):
            qa_ref[h, :, :nsel] = bias_ref[0, 0]
            qa_ref[h, :, nsel:] = q_ref[0, h]
        m_ref[...] = jnp.full(m_ref.shape, -jnp.inf, F32)
        acc_ref[...] = jnp.zeros(acc_ref.shape, F32)

    def accumulate(diagonal):
        key = j * tk + lax.broadcasted_iota(I32, (tk, nsel), 0)
        blk = lax.broadcasted_iota(I32, (tk, nsel), 1)
        onehot = jnp.where(key // SEL_BLOCK == blk, 1.0, 0.0).astype(BF16)
        ka = jnp.concatenate([onehot, k_ref[0, 0]], axis=1)
        va = jnp.concatenate([v_ref[0, 0].astype(F32), jnp.ones((tk, LANES - HEAD_DIM), F32)], axis=1)
        vat = va.T.astype(BF16)
        if diagonal:
            kp = j * tk + lax.broadcasted_iota(I32, (tk, tq), 0)
            t = i * tq + lax.broadcasted_iota(I32, (tk, tq), 1)
            causal = kp <= t
        score = lambda h: lax.dot_general(ka, qa_ref[h], (((1,), (1,)), ((), ())), preferred_element_type=F32)
        st_next = score(0)
        for h in range(hg):
            st = st_next
            if h + 1 < hg:
                st_next = score(h + 1)
            if diagonal:
                st = jnp.where(causal, st, NEG)
            m_old = m_ref[h]
            m_new = jnp.maximum(m_old, jnp.max(st, axis=0, keepdims=True))
            alpha = jnp.exp2(m_old - m_new)
            p = jnp.exp2(st - m_new)
            acc_ref[h] = alpha * acc_ref[h] + jnp.dot(vat, p.astype(BF16), preferred_element_type=F32)
            m_ref[h] = m_new

    @pl.when(j < last_j)
    def _():
        accumulate(False)

    @pl.when(j == last_j)
    def _():
        accumulate(True)
        for h in range(hg):
            acc = acc_ref[h]
            o = acc[:HEAD_DIM, :] / acc[HEAD_DIM:HEAD_DIM + 1, :]
            o_ref[0, h] = o.T.astype(o_ref.dtype)


def _sel_attn(q, bias, k_s, v_s, s, tq=512, tk=512):
    assert tq <= tk
    b, nh, _, dh = q.shape
    g, hg = N_KV_GROUPS, HEADS_PER_GROUP
    nsel = bias.shape[3]
    tk = min(tk, s)
    pairs = [(i, j) for i in range(s // tq) for j in range(((i + 1) * tq - 1) // tk + 1)]
    it = jnp.asarray([pr[0] for pr in pairs], I32)
    jt = jnp.asarray([pr[1] for pr in pairs], I32)
    q_map = lambda bi, gi, pid, it_ref, jt_ref: (bi, gi, it_ref[pid], 0)
    kv_map = lambda bi, gi, pid, it_ref, jt_ref: (bi, gi, jt_ref[pid], 0)
    kern = functools.partial(_sel_attn_kernel, tq=tq, tk=tk)
    grid_spec = pltpu.PrefetchScalarGridSpec(
        num_scalar_prefetch=2, grid=(b, g, len(pairs)),
        in_specs=[pl.BlockSpec((1, hg, tq, dh), q_map), pl.BlockSpec((1, 1, tq, nsel), q_map),
                  pl.BlockSpec((1, 1, tk, dh), kv_map), pl.BlockSpec((1, 1, tk, dh), kv_map)],
        out_specs=pl.BlockSpec((1, hg, tq, dh), q_map),
        scratch_shapes=[pltpu.VMEM((hg, tq, nsel + dh), BF16), pltpu.VMEM((hg, 1, tq), F32),
                        pltpu.VMEM((hg, LANES, tq), F32)])
    return pl.pallas_call(
        kern, grid_spec=grid_spec, out_shape=jax.ShapeDtypeStruct((b, nh, s, dh), BF16),
        compiler_params=_params(("parallel", "parallel", "arbitrary")), name="nsa_sel_attn",
    )(it, jt, q, bias, k_s, v_s)


def _win_attn_kernel(q_ref, *refs, tq, nwin):
    k_refs, v_refs, o_ref = refs[:nwin], refs[nwin:2 * nwin], refs[2 * nwin]
    i = pl.program_id(2)
    hg = q_ref.shape[1]
    rows = hg * tq
    nk = nwin * tq
    k = jnp.concatenate([r[0, 0] for r in k_refs], axis=0)
    v = jnp.concatenate([r[0, 0] for r in v_refs], axis=0)
    s = lax.dot_general(q_ref[0].reshape(rows, HEAD_DIM), k, (((1,), (1,)), ((), ())),
                        preferred_element_type=F32)
    t = i * tq + lax.broadcasted_iota(I32, (tq, nk), 0)
    sp = i * tq - WINDOW + lax.broadcasted_iota(I32, (tq, nk), 1)
    mask = (sp <= t) & (sp > t - WINDOW) & (sp >= 0)
    s = jnp.where(mask[None], s.reshape(hg, tq, nk), NEG)
    m = jnp.max(s, axis=-1, keepdims=True)
    p = jnp.where(mask[None], jnp.exp2(s - m), 0.0)
    l = jnp.sum(p, axis=-1, keepdims=True)
    o = jnp.dot(p.astype(BF16).reshape(rows, nk), v, preferred_element_type=F32)
    o_ref[0] = (o.reshape(hg, tq, HEAD_DIM) / l).astype(o_ref.dtype)


def _win_attn(q, k_w, v_w, s, tq=256):
    b, nh, _, dh = q.shape
    g, hg = N_KV_GROUPS, HEADS_PER_GROUP
    nwin = WINDOW // tq + 1
    pad = ((0, 0), (0, 0), (WINDOW, 0), (0, 0))
    kp, vp = jnp.pad(k_w, pad), jnp.pad(v_w, pad)
    kv_specs = [pl.BlockSpec((1, 1, tq, dh), functools.partial(lambda bi, gi, i, w: (bi, gi, i + w, 0), w=w))
                for w in range(nwin)]
    kern = functools.partial(_win_attn_kernel, tq=tq, nwin=nwin)
    return pl.pallas_call(
        kern, grid=(b, g, s // tq),
        in_specs=[pl.BlockSpec((1, hg, tq, dh), lambda bi, gi, i: (bi, gi, i, 0))] + kv_specs + kv_specs,
        out_specs=pl.BlockSpec((1, hg, tq, dh), lambda bi, gi, i: (bi, gi, i, 0)),
        out_shape=jax.ShapeDtypeStruct((b, nh, s, dh), BF16),
        compiler_params=_params(("parallel", "parallel", "parallel")), name="nsa_win_attn",
    )(q, *([kp] * nwin), *([vp] * nwin))


def _causal_conv(x, tail, w_ref, b_ref):
    n = x.shape[0]
    xx = jnp.concatenate([tail, x], axis=0)
    out = b_ref[...] + w_ref[CONV_WIDTH - 1:CONV_WIDTH, :] * x
    for k in range(CONV_WIDTH - 1):
        off = 8 - (CONV_WIDTH - 1) + k
        out = out + w_ref[k:k + 1, :] * xx[off:off + n, :]
    return out


def _rglru_kernel(x_ref, gate_ref, pos_ref, cw_ref, cb_ref, wr_ref, br_ref, wi_ref, bi_ref, lam_ref,
                  o_ref, tail_ref, h_ref):
    @pl.when(pl.program_id(1) == 0)
    def _():
        tail_ref[...] = jnp.zeros(tail_ref.shape, F32)
        h_ref[...] = jnp.zeros(h_ref.shape, F32)

    n = x_ref.shape[0]
    x = x_ref[...].astype(F32)
    xc = _causal_conv(x, tail_ref[...], cw_ref, cb_ref)
    tail_ref[...] = x[n - 8:, :]
    xcb = xc.astype(BF16)
    r = _sigmoid(jnp.dot(xcb, wr_ref[...], preferred_element_type=F32) + br_ref[...])
    gi = _sigmoid(jnp.dot(xcb, wi_ref[...], preferred_element_type=F32) + bi_ref[...])
    log_a = -LRU_C * r * _softplus(-lam_ref[...])
    reset = pos_ref[...] == 0
    a = jnp.where(reset, 0.0, jnp.exp(log_a))
    mult = jnp.where(reset, 1.0, jnp.sqrt(jnp.maximum(1.0 - jnp.exp(2.0 * log_a), 0.0)))
    bb = mult * (gi * xc)
    row = lax.broadcasted_iota(I32, a.shape, 0)
    d = 1
    while d < n:
        a_sh = pltpu.roll(a, d, 0)
        b_sh = pltpu.roll(bb, d, 0)
        live = row >= d
        bb = jnp.where(live, a * b_sh + bb, bb)
        a = jnp.where(live, a * a_sh, a)
        d *= 2
    h = bb + a * h_ref[...]
    h_ref[...] = h[n - 1:n, :]
    o_ref[...] = (h * _gelu(gate_ref[...].astype(F32))).astype(o_ref.dtype)


def _block_diag(w):
    nb, bs, _ = w.shape
    eye = jnp.eye(nb, dtype=w.dtype)
    return (w[:, :, None, :] * eye[:, None, :, None]).reshape(nb * bs, nb * bs)


def _rglru(l_x, l_g, positions, conv_w, conv_b, w_r, b_r, w_i, b_i, lam, b, s, ts=256):
    d = l_x.shape[1]
    nt = s // ts
    row = lambda bi, i: (bi * nt + i, 0)
    c2 = lambda bi, i: (0, 0)
    vec = lambda v: v.reshape(1, d).astype(F32)
    return pl.pallas_call(
        _rglru_kernel, grid=(b, nt),
        in_specs=[pl.BlockSpec((ts, d), row), pl.BlockSpec((ts, d), row), pl.BlockSpec((ts, 1), row),
                  pl.BlockSpec((CONV_WIDTH, d), c2), pl.BlockSpec((1, d), c2),
                  pl.BlockSpec((d, d), c2), pl.BlockSpec((1, d), c2),
                  pl.BlockSpec((d, d), c2), pl.BlockSpec((1, d), c2), pl.BlockSpec((1, d), c2)],
        out_specs=pl.BlockSpec((ts, d), row),
        out_shape=jax.ShapeDtypeStruct((b * s, d), BF16),
        scratch_shapes=[pltpu.VMEM((8, d), F32), pltpu.VMEM((1, d), F32)],
        compiler_params=_params(("parallel", "arbitrary")), name="rglru",
    )(l_x, l_g, positions.reshape(b * s, 1).astype(I32), conv_w.astype(F32), vec(conv_b),
      _block_diag(w_r).astype(BF16), vec(b_r), _block_diag(w_i).astype(BF16), vec(b_i), vec(lam))


def _ssd_kernel(z_ref, xbc_ref, dt_ref, cw_ref, cb_ref, dtb_ref, alog_ref, dfull_ref, ng_ref,
                o_ref, tail_ref, state_ref, y_ref):
    @pl.when(pl.program_id(1) == 0)
    def _():
        tail_ref[...] = jnp.zeros(tail_ref.shape, F32)
        state_ref[...] = jnp.zeros(state_ref.shape, F32)

    n = xbc_ref.shape[0]
    hg = SSM_HEADS // SSM_GROUPS
    x = xbc_ref[...].astype(F32)
    xc = _causal_conv(x, tail_ref[...], cw_ref, cb_ref)
    tail_ref[...] = x[n - 8:, :]
    xc = xc * _sigmoid(xc)
    xs = xc[:, :SSM_INNER]
    dt = _softplus(dt_ref[...] + dtb_ref[...])
    adt = dt * (-jnp.exp(alog_ref[...]))
    row = lax.broadcasted_iota(I32, adt.shape, 0)
    acs = adt
    d = 1
    while d < n:
        acs = acs + jnp.where(row >= d, pltpu.roll(acs, d, 0), 0.0)
        d *= 2
    acs_t = acs.T
    li = lax.broadcasted_iota(I32, (n, n), 0)
    si = lax.broadcasted_iota(I32, (n, n), 1)
    tri = li >= si
    for g in range(SSM_GROUPS):
        bm = xc[:, SSM_INNER + g * SSM_STATE:SSM_INNER + (g + 1) * SSM_STATE].astype(BF16)
        cm = xc[:, SSM_INNER + (SSM_GROUPS + g) * SSM_STATE:SSM_INNER + (SSM_GROUPS + g + 1) * SSM_STATE].astype(BF16)
        cb = lax.dot_general(cm, bm, (((1,), (1,)), ((), ())), preferred_element_type=F32)
        bm_t = bm.T
        for hh in range(hg):
            h = g * hg + hh
            acol = acs[:, h:h + 1]
            arow = acs_t[h:h + 1, :]
            decay = jnp.exp(jnp.where(tri, acol - arow, NEG))
            xh = xs[:, h * SSM_HEAD_DIM:(h + 1) * SSM_HEAD_DIM] * dt[:, h:h + 1]
            a_last = acs[n - 1:n, h:h + 1]
            y = jnp.dot((cb * decay).astype(BF16), xh.astype(BF16), preferred_element_type=F32)
            st = state_ref[h]
            y = y + jnp.dot(cm, st.astype(BF16), preferred_element_type=F32) * jnp.exp(acol)
            upd = jnp.dot(bm_t, (xh * jnp.exp(a_last - acol)).astype(BF16), preferred_element_type=F32)
            state_ref[h] = jnp.exp(a_last) * st + upd
            y_ref[:, h * SSM_HEAD_DIM:(h + 1) * SSM_HEAD_DIM] = y
    z = z_ref[...].astype(F32)
    y = (y_ref[...] + dfull_ref[...] * xs) * (z * _sigmoid(z))
    gw = SSM_INNER // SSM_GROUPS
    for g in range(SSM_GROUPS):
        yg = y[:, g * gw:(g + 1) * gw]
        ms = jnp.mean(yg * yg, axis=-1, keepdims=True)
        o_ref[:, g * gw:(g + 1) * gw] = (yg * lax.rsqrt(ms + EPS) * ng_ref[:, g * gw:(g + 1) * gw]).astype(o_ref.dtype)


def _ssd(s_z, s_xbc, s_dt, conv_w, conv_b, dt_bias, a_log, d_skip, norm_g, b, s):
    n = SSM_CHUNK
    nt = s // n
    c = s_xbc.shape[1]
    row = lambda bi, i: (bi * nt + i, 0)
    c2 = lambda bi, i: (0, 0)
    pad_h = lambda v: jnp.zeros((1, LANES), F32).at[0, :SSM_HEADS].set(v.astype(F32))
    dfull = jnp.repeat(d_skip.astype(F32), SSM_HEAD_DIM).reshape(1, SSM_INNER)
    return pl.pallas_call(
        _ssd_kernel, grid=(b, nt),
        in_specs=[pl.BlockSpec((n, SSM_INNER), row), pl.BlockSpec((n, c), row), pl.BlockSpec((n, LANES), row),
                  pl.BlockSpec((CONV_WIDTH, c), c2), pl.BlockSpec((1, c), c2),
                  pl.BlockSpec((1, LANES), c2), pl.BlockSpec((1, LANES), c2),
                  pl.BlockSpec((1, SSM_INNER), c2), pl.BlockSpec((1, SSM_INNER), c2)],
        out_specs=pl.BlockSpec((n, SSM_INNER), row),
        out_shape=jax.ShapeDtypeStruct((b * s, SSM_INNER), BF16),
        scratch_shapes=[pltpu.VMEM((8, c), F32), pltpu.VMEM((SSM_HEADS, SSM_STATE, SSM_HEAD_DIM), F32),
                        pltpu.VMEM((n, SSM_INNER), F32)],
        compiler_params=_params(("parallel", "arbitrary")), name="ssd",
    )(s_z, s_xbc, s_dt, conv_w.astype(F32), conv_b.reshape(1, c).astype(F32), pad_h(dt_bias), pad_h(a_log),
      dfull, norm_g.reshape(1, SSM_INNER).astype(F32))


def _merge_kernel(oc_ref, os_ref, ow_ref, ag_ref, yb_ref, yc_ref, mg_ref, x_ref, wb_ref, wo_ref,
                  o_ref, ya_ref):
    gates = _sigmoid(ag_ref[...])
    for h in range(N_HEADS):
        ya = (gates[:, 3 * h:3 * h + 1] * oc_ref[0, h].astype(F32)
              + gates[:, 3 * h + 1:3 * h + 2] * os_ref[0, h].astype(F32)
              + gates[:, 3 * h + 2:3 * h + 3] * ow_ref[0, h].astype(F32))
        ya_ref[:, h * HEAD_DIM:(h + 1) * HEAD_DIM] = ya.astype(ya_ref.dtype)
    d = D_MODEL
    merged = _sigmoid(mg_ref[:, 0:d].astype(F32)) * jnp.dot(ya_ref[...], wb_ref[0], preferred_element_type=F32)
    merged += _sigmoid(mg_ref[:, d:2 * d].astype(F32)) * jnp.dot(yb_ref[...], wb_ref[1], preferred_element_type=F32)
    merged += _sigmoid(mg_ref[:, 2 * d:3 * d].astype(F32)) * jnp.dot(yc_ref[...], wb_ref[2], preferred_element_type=F32)
    o_ref[...] = x_ref[...] + jnp.dot(merged.astype(BF16), wo_ref[...], preferred_element_type=F32)


def _merge(o_c, o_s, o_w, a_g, y_b, y_c, m_g, x, w_branch, w_out, b, s, tm=256):
    d = D_MODEL
    nt = s // tm
    row = lambda i: (i, 0)
    o_spec = pl.BlockSpec((1, N_HEADS, tm, HEAD_DIM), lambda i: (i // nt, 0, i % nt, 0))
    return pl.pallas_call(
        _merge_kernel, grid=(b * nt,),
        in_specs=[o_spec, o_spec, o_spec, pl.BlockSpec((tm, LANES), row), pl.BlockSpec((tm, d), row),
                  pl.BlockSpec((tm, d), row), pl.BlockSpec((tm, 3 * d), row), pl.BlockSpec((tm, d), row),
                  pl.BlockSpec((3, d, d), lambda i: (0, 0, 0)), pl.BlockSpec((d, d), lambda i: (0, 0))],
        out_specs=pl.BlockSpec((tm, d), row),
        out_shape=jax.ShapeDtypeStruct((b * s, d), F32),
        scratch_shapes=[pltpu.VMEM((tm, d), BF16)],
        compiler_params=_params(("parallel",)), name="merge",
    )(o_c, o_s, o_w, a_g, y_b, y_c, m_g, x, w_branch.astype(BF16), w_out.astype(BF16))


def _topk_rows(s, k):
    n = s.shape[0]
    row = lax.broadcasted_iota(I32, s.shape, 0)
    vals, idxs = [], []
    for _ in range(k):
        m = jnp.max(s, axis=0, keepdims=True)
        idx = jnp.min(jnp.where(s == m, row, n), axis=0, keepdims=True)
        vals.append(m)
        idxs.append(idx)
        s = jnp.where(row == idx, -jnp.inf, s)
    return jnp.concatenate(vals, axis=0), jnp.concatenate(idxs, axis=0)


def _pick_rows(table, sel, k):
    out = jnp.zeros(sel.shape, table.dtype)
    for a in range(k):
        out = jnp.where(sel == a, table[a:a + 1, :], out)
    return out


def _peer_route_kernel(h_ref, wq_ref, keys_ref, i1_ref, i2_ref, g_ref):
    k = PEER_TOPK
    qt = lax.dot_general(wq_ref[...], h_ref[...], (((1,), (1,)), ((), ())), preferred_element_type=F32)
    qt = qt.astype(BF16)
    i1s, i2s, gs = [], [], []
    for hd in range(PEER_HEADS):
        tops = []
        for half in range(2):
            c = hd * 2 + half
            sc = jnp.dot(keys_ref[c], qt[c * PEER_HALF:(c + 1) * PEER_HALF, :], preferred_element_type=F32)
            tops.append(_topk_rows(sc, k))
        (s1, i1), (s2, i2) = tops
        cand = jnp.concatenate([s1[a:a + 1, :] + s2 for a in range(k)], axis=0)
        sc, pos = _topk_rows(cand, k)
        e = jnp.exp(sc - sc[0:1, :])
        gs.append(e / jnp.sum(e, axis=0, keepdims=True))
        i1s.append(_pick_rows(i1, pos // k, k))
        i2s.append(_pick_rows(i2, pos % k, k))
    i1_ref[...] = jnp.concatenate(i1s, axis=0).T
    i2_ref[...] = jnp.concatenate(i2s, axis=0).T
    g_ref[...] = jnp.concatenate(gs, axis=0).T


def _peer_route(h, w_q, sub_keys, tm=256):
    t, d = h.shape
    nq = w_q.shape[1]
    wq_t = w_q.T.astype(BF16)
    keys = sub_keys.reshape(PEER_HEADS * 2, PEER_KEYS, PEER_HALF).astype(BF16)
    slots = PEER_HEADS * PEER_TOPK
    row = lambda i: (i, 0)
    return pl.pallas_call(
        _peer_route_kernel, grid=(t // tm,),
        in_specs=[pl.BlockSpec((tm, d), row), pl.BlockSpec((nq, d), lambda i: (0, 0)),
                  pl.BlockSpec((PEER_HEADS * 2, PEER_KEYS, PEER_HALF), lambda i: (0, 0, 0))],
        out_specs=[pl.BlockSpec((tm, slots), row)] * 3,
        out_shape=[jax.ShapeDtypeStruct((t, slots), I32), jax.ShapeDtypeStruct((t, slots), I32),
                   jax.ShapeDtypeStruct((t, slots), F32)],
        compiler_params=_params(("parallel",)), name="peer_route",
    )(h, wq_t, keys)


def _peer_u_kernel(h_ref, u_ref, i1_ref, i2_ref, o_ref, *, blocks):
    c = pl.program_id(0)
    a = lax.dot_general(h_ref[...], u_ref[...], (((1,), (1,)), ((), ())), preferred_element_type=F32)
    i1 = i1_ref[...]
    i2 = i2_ref[...]
    acc = jnp.zeros(i1.shape, F32)
    for bk in range(blocks):
        got = jnp.take_along_axis(a[:, bk * PEER_KEYS:(bk + 1) * PEER_KEYS], i2, axis=1)
        acc = jnp.where(i1 == c * blocks + bk, got, acc)
    o_ref[0] = acc


def _peer_u(h, u, i1, i2, tm=512, blocks=16):
    t, d = h.shape
    ne = u.shape[0]
    ec = blocks * PEER_KEYS
    nchunk = ne // ec
    slots = i1.shape[1]
    kern = functools.partial(_peer_u_kernel, blocks=blocks)
    return pl.pallas_call(
        kern, grid=(nchunk, t // tm),
        in_specs=[pl.BlockSpec((tm, d), lambda c, i: (i, 0)), pl.BlockSpec((ec, d), lambda c, i: (c, 0)),
                  pl.BlockSpec((tm, slots), lambda c, i: (i, 0)), pl.BlockSpec((tm, slots), lambda c, i: (i, 0))],
        out_specs=pl.BlockSpec((1, tm, slots), lambda c, i: (c, i, 0)),
        out_shape=jax.ShapeDtypeStruct((nchunk, t, slots), F32),
        compiler_params=_params(("parallel", "parallel")), name="peer_u",
    )(h, u, i1, i2)


def _peer_v_kernel(parts_ref, g_ref, i1_ref, i2_ref, v_ref, x_ref, o_ref, w_ref, wg_ref, *, tm, blocks):
    c = pl.program_id(1)
    nk = PEER_KEYS

    @pl.when(c == 0)
    def _():
        w_ref[...] = g_ref[...] * _gelu(jnp.sum(parts_ref[...], axis=0))
        o_ref[...] = x_ref[...]
        sub = lax.broadcasted_iota(I32, (nk, w_ref.shape[1]), 0)

        def per_token(t, carry):
            wrow = w_ref[pl.ds(t, 1), :]
            lhs = jnp.where(i1_ref[pl.ds(t, 1), :] == sub, wrow, 0.0).astype(BF16)
            rhs = jnp.where(i2_ref[pl.ds(t, 1), :] == sub, 1.0, 0.0).astype(BF16)
            grid = lax.dot_general(lhs, rhs, (((1,), (1,)), ((), ())), preferred_element_type=F32)
            wg_ref[pl.ds(pl.multiple_of(t * nk, nk), nk), :] = grid
            return carry

        lax.fori_loop(0, tm, per_token, 0, unroll=16)

    acc = jnp.zeros(o_ref.shape, F32)
    for bk in range(0, blocks, 2):
        i1 = c * blocks + bk
        lhs = jnp.concatenate([wg_ref[pl.ds(i1, tm, stride=nk), :], wg_ref[pl.ds(i1 + 1, tm, stride=nk), :]],
                              axis=1).astype(BF16)
        rhs = v_ref[bk:bk + 2].reshape(2 * nk, v_ref.shape[2])
        acc += jnp.dot(lhs, rhs, preferred_element_type=F32)
    o_ref[...] += acc


def _peer_v(parts, g, i1, i2, v, x, tm=256, blocks=32):
    t, d = x.shape
    nchunk_u = parts.shape[0]
    slots = g.shape[1]
    nk = PEER_KEYS
    v3 = v.reshape(nk, nk, d)
    kern = functools.partial(_peer_v_kernel, tm=tm, blocks=blocks)
    row = lambda i, c: (i, 0)
    return pl.pallas_call(
        kern, grid=(t // tm, nk // blocks),
        in_specs=[pl.BlockSpec((nchunk_u, tm, slots), lambda i, c: (0, i, 0)),
                  pl.BlockSpec((tm, slots), row), pl.BlockSpec((tm, slots), row), pl.BlockSpec((tm, slots), row),
                  pl.BlockSpec((blocks, nk, d), lambda i, c: (c, 0, 0)), pl.BlockSpec((tm, d), row)],
        out_specs=pl.BlockSpec((tm, d), row),
        out_shape=jax.ShapeDtypeStruct((t, d), F32),
        scratch_shapes=[pltpu.VMEM((tm, slots), F32), pltpu.VMEM((tm * nk, nk), F32)],
        compiler_params=_params(("parallel", "arbitrary")), name="peer_v",
    )(parts, g, i1, i2, v3, x)


IN_SIZES = (1024, 768, 48, 1024, 1024, 1024, 1536, 16, 3072)


def _pad_cols(w, n):
    return jnp.pad(w, ((0, 0), (0, n - w.shape[1])))


def _mixer(x, positions, b, s, p):
    xn = _rmsnorm(x, p["mix_norm_g"])
    offs = [0]
    for n in IN_SIZES:
        offs.append(offs[-1] + n)
    w = [p["w_in"][:, offs[i]:offs[i + 1]] for i in range(len(IN_SIZES))]
    a_q = _matmul(xn, w[0].astype(BF16), BF16, "proj_q")
    a_kv = _matmul(xn, w[1].astype(BF16), BF16, "proj_kv")
    a_g = _matmul(xn, _pad_cols(w[2], LANES).astype(BF16), F32, "proj_ag")
    l_x = _matmul(xn, w[3].astype(BF16), BF16, "proj_lx")
    l_g = _matmul(xn, w[4].astype(BF16), BF16, "proj_lg")
    s_z = _matmul(xn, w[5].astype(BF16), BF16, "proj_sz")
    s_xbc = _matmul(xn, w[6].astype(BF16), BF16, "proj_sxbc")
    s_dt = _matmul(xn, _pad_cols(w[7], LANES).astype(BF16), F32, "proj_sdt")
    m_g = _matmul(xn, w[8].astype(BF16), BF16, "proj_mg", tn=1024)

    q, kc, vc, ks, vs, kw, vw = _nsa_prep(a_q, a_kv, positions, p["q_norm_g"], p["k_norm_g"], b, s)
    k_cmp, v_cmp = _compress(kc, vc, p["cmp_pe_k"], p["cmp_pe_v"], p["cmp_k_w1"], p["cmp_k_w2"],
                             p["cmp_v_w1"], p["cmp_v_w2"], p["k_norm_g"])
    o_c, bias = _cmp_select(q, k_cmp, v_cmp, s)
    o_s = _sel_attn(q, bias, ks, vs, s)
    o_w = _win_attn(q, kw, vw, s)
    y_b = _rglru(l_x, l_g, positions, p["lru_conv_w"], p["lru_conv_b"], p["lru_w_r"], p["lru_b_r"],
                 p["lru_w_i"], p["lru_b_i"], p["lru_lambda"], b, s)
    y_c = _ssd(s_z, s_xbc, s_dt, p["ssm_conv_w"], p["ssm_conv_b"], p["ssm_dt_bias"], p["ssm_a_log"],
               p["ssm_d"], p["ssm_norm_g"], b, s)
    return _merge(o_c, o_s, o_w, a_g, y_b, y_c, m_g, x, p["w_branch"], p["w_out"], b, s)


def _peer(x, p):
    h = _rmsnorm(x, p["ffn_norm_g"])
    i1, i2, g = _peer_route(h, p["peer_w_q"], p["peer_sub_keys"])
    parts = _peer_u(h, p["peer_u"].astype(BF16), i1, i2)
    return _peer_v(parts, g, i1, i2, p["peer_v"].astype(BF16), x)


_LAYER_PARAMS = ("mix_norm_g", "w_in", "q_norm_g", "k_norm_g", "cmp_pe_k", "cmp_pe_v", "cmp_k_w1", "cmp_k_w2",
                 "cmp_v_w1", "cmp_v_w2", "lru_conv_w", "lru_conv_b", "lru_w_r", "lru_b_r", "lru_w_i", "lru_b_i",
                 "lru_lambda", "ssm_conv_w", "ssm_conv_b", "ssm_dt_bias", "ssm_a_log", "ssm_d", "ssm_norm_g",
                 "w_branch", "w_out", "ffn_norm_g", "peer_w_q", "peer_sub_keys", "peer_u", "peer_v")


def kernel(x, positions, mix_norm_g, w_in, q_norm_g, k_norm_g, cmp_pe_k, cmp_pe_v, cmp_k_w1, cmp_k_w2, cmp_v_w1, cmp_v_w2, lru_conv_w, lru_conv_b, lru_w_r, lru_b_r, lru_w_i, lru_b_i, lru_lambda, ssm_conv_w, ssm_conv_b, ssm_dt_bias, ssm_a_log, ssm_d, ssm_norm_g, w_branch, w_out, ffn_norm_g, peer_w_q, peer_sub_keys, peer_u, peer_v):
    stacked = dict(zip(_LAYER_PARAMS, (mix_norm_g, w_in, q_norm_g, k_norm_g, cmp_pe_k, cmp_pe_v, cmp_k_w1,
                                       cmp_k_w2, cmp_v_w1, cmp_v_w2, lru_conv_w, lru_conv_b, lru_w_r, lru_b_r,
                                       lru_w_i, lru_b_i, lru_lambda, ssm_conv_w, ssm_conv_b, ssm_dt_bias,
                                       ssm_a_log, ssm_d, ssm_norm_g, w_branch, w_out, ffn_norm_g, peer_w_q,
                                       peer_sub_keys, peer_u, peer_v)))
    b, s, d = x.shape
    xf = x.reshape(b * s, d).astype(F32)
    for layer in range(mix_norm_g.shape[0]):
        p = {name: arr[layer] for name, arr in stacked.items()}
        xf = _mixer(xf, positions, b, s, p)
        xf = _peer(xf, p)
    return xf.reshape(b, s, d).astype(x.dtype)
```

```python
import functools
import math

import jax
import jax.numpy as jnp
from jax import lax
from jax.experimental import pallas as pl
from jax.experimental.pallas import tpu as pltpu

F32 = jnp.float32
BF16 = jnp.bfloat16
I32 = jnp.int32

D_MODEL = 1024
HEAD_DIM = 64
N_HEADS = 16
N_KV_GROUPS = 2
HEADS_PER_GROUP = 8
CMP_BLOCK = 32
CMP_STRIDE = 16
CMP_HIDDEN = 256
SEL_BLOCK = 64
N_SELECT = 16
WINDOW = 512
SEL_FORCE = 100.0
ROPE_THETA = 10000.0
SCALE = HEAD_DIM ** -0.5
LOG2E = math.log2(math.e)
LRU_HEADS = 16
LRU_BLOCK = 64
LRU_C = 8.0
CONV_WIDTH = 4
SSM_HEADS = 16
SSM_HEAD_DIM = 64
SSM_GROUPS = 2
SSM_STATE = 128
SSM_CHUNK = 128
SSM_INNER = 1024
PEER_HEADS = 8
PEER_KEYS = 128
PEER_HALF = 128
PEER_TOPK = 16
EPS = 1e-6
NEG = -1e30
LANES = 128

VMEM_LIMIT = 56 * 1024 * 1024


def _params(sem):
    return pltpu.CompilerParams(dimension_semantics=sem, vmem_limit_bytes=VMEM_LIMIT)


def _gelu(x):
    return 0.5 * x * (1.0 + jnp.tanh(math.sqrt(2.0 / math.pi) * (x + 0.044715 * x * x * x)))


def _sigmoid(x):
    return 1.0 / (1.0 + jnp.exp(-x))


def _softplus(x):
    return jnp.maximum(x, 0.0) + jnp.log(1.0 + jnp.exp(-jnp.abs(x)))


def _rmsnorm_kernel(x_ref, g_ref, o_ref):
    x = x_ref[...].astype(F32)
    ms = jnp.mean(x * x, axis=-1, keepdims=True)
    o_ref[...] = (x * lax.rsqrt(ms + EPS) * g_ref[...]).astype(o_ref.dtype)


def _rmsnorm(x, g, tm=512):
    t, d = x.shape
    return pl.pallas_call(
        _rmsnorm_kernel, grid=(t // tm,),
        in_specs=[pl.BlockSpec((tm, d), lambda i: (i, 0)), pl.BlockSpec((1, d), lambda i: (0, 0))],
        out_specs=pl.BlockSpec((tm, d), lambda i: (i, 0)),
        out_shape=jax.ShapeDtypeStruct((t, d), BF16),
        compiler_params=_params(("parallel",)), name="rmsnorm")(x, g.reshape(1, d).astype(F32))


def _mm_kernel(x_ref, w_ref, o_ref):
    o_ref[...] = jnp.dot(x_ref[...], w_ref[...], preferred_element_type=F32).astype(o_ref.dtype)


def _matmul(x, w, out_dtype, name, tm=512, tn=None):
    t, k = x.shape
    n = w.shape[1]
    tn = n if tn is None else tn
    return pl.pallas_call(
        _mm_kernel, grid=(n // tn, t // tm),
        in_specs=[pl.BlockSpec((tm, k), lambda j, i: (i, 0)), pl.BlockSpec((k, tn), lambda j, i: (0, j))],
        out_specs=pl.BlockSpec((tm, tn), lambda j, i: (i, j)),
        out_shape=jax.ShapeDtypeStruct((t, n), out_dtype),
        compiler_params=_params(("parallel", "parallel")), name=name)(x, w)


def _norm_rope(x, g, cos, sin_signed, bd):
    ms = jnp.dot((x * x).astype(BF16), bd, preferred_element_type=F32)
    y = x * lax.rsqrt(ms + EPS) * g
    lane = lax.broadcasted_iota(I32, y.shape, 1)
    first_half = (lane % HEAD_DIM) < (HEAD_DIM // 2)
    partner = jnp.where(first_half, pltpu.roll(y, LANES - HEAD_DIM // 2, 1), pltpu.roll(y, HEAD_DIM // 2, 1))
    return y * cos + partner * sin_signed


def _nsa_prep_kernel(aq_ref, akv_ref, pos_ref, inv_ref, sgn_ref, gq_ref, gk_ref, bd_ref,
                     q_ref, kc_ref, vc_ref, ks_ref, vs_ref, kw_ref, vw_ref):
    ang = pos_ref[...].astype(F32) * inv_ref[...]
    cos = jnp.cos(ang)
    sin_signed = jnp.sin(ang) * sgn_ref[...]
    bd = bd_ref[...]
    for c in range(D_MODEL // LANES):
        x = aq_ref[:, c * LANES:(c + 1) * LANES].astype(F32)
        y = (_norm_rope(x, gq_ref[...], cos, sin_signed, bd) * (SCALE * LOG2E)).astype(q_ref.dtype)
        q_ref[0, 2 * c] = y[:, :HEAD_DIM]
        q_ref[0, 2 * c + 1] = y[:, HEAD_DIM:]
    outs = (kc_ref, vc_ref, ks_ref, vs_ref, kw_ref, vw_ref)
    for c, o_ref in enumerate(outs):
        x = akv_ref[:, c * LANES:(c + 1) * LANES]
        if c % 2 == 0:
            y = _norm_rope(x.astype(F32), gk_ref[...], cos, sin_signed, bd).astype(o_ref.dtype)
        else:
            y = x.astype(o_ref.dtype)
        o_ref[0, 0] = y[:, :HEAD_DIM]
        o_ref[0, 1] = y[:, HEAD_DIM:]


def _nsa_prep(a_q, a_kv, positions, q_norm_g, k_norm_g, b, s, tm=512):
    half = HEAD_DIM // 2
    lane = jnp.arange(LANES)
    inv = (ROPE_THETA ** (-((lane % half).astype(F32)) / half)).reshape(1, LANES)
    sgn = jnp.where((lane % HEAD_DIM) < half, -1.0, 1.0).astype(F32).reshape(1, LANES)
    bd = jnp.where((lane[:, None] // HEAD_DIM) == (lane[None, :] // HEAD_DIM), 1.0 / HEAD_DIM, 0.0).astype(BF16)
    gq = jnp.tile(q_norm_g.astype(F32), 2).reshape(1, LANES)
    gk = jnp.tile(k_norm_g.astype(F32), 2).reshape(1, LANES)
    nt = s // tm
    row = lambda i: (i, 0)
    const = lambda i: (0, 0)
    kv_shape = jax.ShapeDtypeStruct((b, N_KV_GROUPS, s, HEAD_DIM), BF16)
    kv_spec = pl.BlockSpec((1, N_KV_GROUPS, tm, HEAD_DIM), lambda i: (i // nt, 0, i % nt, 0))
    return pl.pallas_call(
        _nsa_prep_kernel, grid=(b * nt,),
        in_specs=[pl.BlockSpec((tm, D_MODEL), row), pl.BlockSpec((tm, 6 * LANES), row),
                  pl.BlockSpec((tm, 1), row), pl.BlockSpec((1, LANES), const), pl.BlockSpec((1, LANES), const),
                  pl.BlockSpec((1, LANES), const), pl.BlockSpec((1, LANES), const),
                  pl.BlockSpec((LANES, LANES), const)],
        out_specs=[pl.BlockSpec((1, N_HEADS, tm, HEAD_DIM), lambda i: (i // nt, 0, i % nt, 0))] + [kv_spec] * 6,
        out_shape=[jax.ShapeDtypeStruct((b, N_HEADS, s, HEAD_DIM), BF16)] + [kv_shape] * 6,
        compiler_params=_params(("parallel",)), name="nsa_prep",
    )(a_q, a_kv, positions.reshape(b * s, 1).astype(I32), inv, sgn, gq, gk, bd)


def _compress_kernel(uk_ref, uv_ref, pek_ref, pev_ref, kw1_ref, kw2_ref, vw1_ref, vw2_ref, gk_ref,
                     kc_ref, vc_ref):
    half = CMP_STRIDE * HEAD_DIM

    def mlp(u, pe, w1_ref, w2_ref):
        n = u.shape[0]
        ha = jnp.dot(u, w1_ref[:half, :], preferred_element_type=F32)
        hb = jnp.dot(u, w1_ref[half:, :], preferred_element_type=F32)
        bias = jnp.dot(pe, w1_ref[...], preferred_element_type=F32)[0:1, :]
        pre = ha + pltpu.roll(hb, n - 1, 0) + bias
        return jnp.dot(_gelu(pre).astype(BF16), w2_ref[...], preferred_element_type=F32)

    k = mlp(uk_ref[0, 0], pek_ref[...], kw1_ref, kw2_ref)
    ms = jnp.mean(k * k, axis=-1, keepdims=True)
    kc_ref[0, 0] = (k * lax.rsqrt(ms + EPS) * gk_ref[...]).astype(kc_ref.dtype)
    vc_ref[0, 0] = mlp(uv_ref[0, 0], pev_ref[...], vw1_ref, vw2_ref).astype(vc_ref.dtype)


def _compress(kc, vc, pe_k, pe_v, kw1, kw2, vw1, vw2, k_norm_g):
    b, g, s, dh = kc.shape
    ng = s // CMP_STRIDE
    wide = CMP_BLOCK * dh
    uk = kc.reshape(b, g, ng, CMP_STRIDE * dh)
    uv = vc.reshape(b, g, ng, CMP_STRIDE * dh)
    pek = jnp.zeros((8, wide), BF16).at[0].set(pe_k.reshape(wide).astype(BF16))
    pev = jnp.zeros((8, wide), BF16).at[0].set(pe_v.reshape(wide).astype(BF16))
    u_spec = pl.BlockSpec((1, 1, ng, CMP_STRIDE * dh), lambda i, j: (i, j, 0, 0))
    c2 = lambda i, j: (0, 0)
    o_spec = pl.BlockSpec((1, 1, ng, dh), lambda i, j: (i, j, 0, 0))
    o_shape = jax.ShapeDtypeStruct((b, g, ng, dh), BF16)
    return pl.pallas_call(
        _compress_kernel, grid=(b, g),
        in_specs=[u_spec, u_spec, pl.BlockSpec((8, wide), c2), pl.BlockSpec((8, wide), c2),
                  pl.BlockSpec((wide, CMP_HIDDEN), c2), pl.BlockSpec((CMP_HIDDEN, dh), c2),
                  pl.BlockSpec((wide, CMP_HIDDEN), c2), pl.BlockSpec((CMP_HIDDEN, dh), c2),
                  pl.BlockSpec((1, dh), c2)],
        out_specs=[o_spec, o_spec], out_shape=[o_shape, o_shape],
        compiler_params=_params(("parallel", "parallel")), name="nsa_compress",
    )(uk, uv, pek, pev, kw1.astype(BF16), kw2.astype(BF16), vw1.astype(BF16), vw2.astype(BF16),
      k_norm_g.reshape(1, dh).astype(F32))


def _cmp_select_kernel(q_ref, kc_ref, vc_ref, ovt_ref, o_ref, bias_ref, *, tq, n_cmp, n_pick):
    i = pl.program_id(2)
    hg = q_ref.shape[1]
    nc = kc_ref.shape[2]
    nsel = ovt_ref.shape[0]
    kc = kc_ref[0, 0]
    vct = vc_ref[0, 0].astype(F32).T.astype(BF16)
    c = lax.broadcasted_iota(I32, (nc, tq), 0)
    t = i * tq + lax.broadcasted_iota(I32, (nc, tq), 1)
    mask = ((CMP_STRIDE * c + CMP_BLOCK - 1) <= t) & (c < n_cmp)
    score = lambda h: lax.dot_general(kc, q_ref[0, h], (((1,), (1,)), ((), ())), preferred_element_type=F32)
    st_next = score(0)
    psum = jnp.zeros((nc, tq), F32)
    for h in range(hg):
        st = st_next
        if h + 1 < hg:
            st_next = score(h + 1)
        st = jnp.where(mask, st, NEG)
        m = jnp.max(st, axis=0, keepdims=True)
        p = jnp.where(mask, jnp.exp2(st - m), 0.0)
        l = jnp.sum(p, axis=0, keepdims=True)
        p = p * jnp.where(l > 0.0, 1.0 / l, 0.0)
        o = jnp.dot(vct, p.astype(BF16), preferred_element_type=F32)
        o_ref[0, h] = o.T.astype(o_ref.dtype)
        psum = psum + p

    hi = psum.astype(BF16)
    lo = (psum - hi.astype(F32)).astype(BF16)
    imp = (jnp.dot(ovt_ref[...], hi, preferred_element_type=F32)
           + jnp.dot(ovt_ref[...], lo, preferred_element_type=F32))
    j = lax.broadcasted_iota(I32, (nsel, tq), 0)
    tt = i * tq + lax.broadcasted_iota(I32, (nsel, tq), 1)
    cur = tt // SEL_BLOCK
    forced = (j == 0) | (j == cur) | (j == cur - 1)
    sc = jnp.where(forced, SEL_FORCE, jnp.where(j * SEL_BLOCK <= tt, imp, -1.0))
    picked = jnp.zeros((nsel, tq), jnp.bool_)
    for _ in range(n_pick):
        mx = jnp.max(sc, axis=0, keepdims=True)
        idx = jnp.min(jnp.where(sc == mx, j, nsel), axis=0, keepdims=True)
        hit = j == idx
        picked = picked | hit
        sc = jnp.where(hit, -jnp.inf, sc)
    bias_ref[0, 0] = jnp.where(picked & (j <= cur), 0.0, NEG).T.astype(bias_ref.dtype)


def _cmp_select(q, k_cmp, v_cmp, s, tq=512):
    b, nh, _, dh = q.shape
    g, hg = N_KV_GROUPS, HEADS_PER_GROUP
    nc = k_cmp.shape[2]
    n_cmp = (s - CMP_BLOCK) // CMP_STRIDE + 1
    nsel = s // SEL_BLOCK
    n_pick = min(N_SELECT, nsel)
    cs = CMP_STRIDE * jnp.arange(nc)
    ss = SEL_BLOCK * jnp.arange(nsel)
    ovt = jnp.clip(jnp.minimum(cs[None, :] + CMP_BLOCK, ss[:, None] + SEL_BLOCK)
                   - jnp.maximum(cs[None, :], ss[:, None]), 0).astype(F32) / CMP_BLOCK
    ovt = jnp.where(jnp.arange(nc)[None, :] < n_cmp, ovt, 0.0).astype(BF16)
    kern = functools.partial(_cmp_select_kernel, tq=tq, n_cmp=n_cmp, n_pick=n_pick)
    return pl.pallas_call(
        kern, grid=(b, g, s // tq),
        in_specs=[pl.BlockSpec((1, hg, tq, dh), lambda bi, gi, i: (bi, gi, i, 0)),
                  pl.BlockSpec((1, 1, nc, dh), lambda bi, gi, i: (bi, gi, 0, 0)),
                  pl.BlockSpec((1, 1, nc, dh), lambda bi, gi, i: (bi, gi, 0, 0)),
                  pl.BlockSpec((nsel, nc), lambda bi, gi, i: (0, 0))],
        out_specs=[pl.BlockSpec((1, hg, tq, dh), lambda bi, gi, i: (bi, gi, i, 0)),
                   pl.BlockSpec((1, 1, tq, nsel), lambda bi, gi, i: (bi, gi, i, 0))],
        out_shape=[jax.ShapeDtypeStruct((b, nh, s, dh), BF16),
                   jax.ShapeDtypeStruct((b, g, s, nsel), BF16)],
        compiler_params=_params(("parallel", "parallel", "parallel")), name="nsa_cmp_select",
    )(q, k_cmp, v_cmp, ovt)


def _sel_attn_kernel(it_ref, jt_ref, q_ref, bias_ref, k_ref, v_ref, o_ref, qa_ref, m_ref, acc_ref, *, tq, tk):
    pid = pl.program_id(2)
    i = it_ref[pid]
    j = jt_ref[pid]
    hg = q_ref.shape[1]
    nsel = bias_ref.shape[3]
    last_j = ((i + 1) * tq - 1) // tk

    @pl.when(j == 0)
    def _():
        for h in range(hg):
            qa_ref[h, :, :nsel] = bias_ref[0, 0]
            qa_ref[h, :, nsel:] = q_ref[0, h]
        m_ref[...] = jnp.full(m_ref.shape, -jnp.inf, F32)
        acc_ref[...] = jnp.zeros(acc_ref.shape, F32)

    def accumulate(diagonal):
        key = j * tk + lax.broadcasted_iota(I32, (tk, nsel), 0)
        blk = lax.broadcasted_iota(I32, (tk, nsel), 1)
        onehot = jnp.where(key // SEL_BLOCK == blk, 1.0, 0.0).astype(BF16)
        ka = jnp.concatenate([onehot, k_ref[0, 0]], axis=1)
        va = jnp.concatenate([v_ref[0, 0].astype(F32), jnp.ones((tk, LANES - HEAD_DIM), F32)], axis=1)
        vat = va.T.astype(BF16)
        if diagonal:
            kp = j * tk + lax.broadcasted_iota(I32, (tk, tq), 0)
            t = i * tq + lax.broadcasted_iota(I32, (tk, tq), 1)
            causal = kp <= t
        score = lambda h: lax.dot_general(ka, qa_ref[h], (((1,), (1,)), ((), ())), preferred_element_type=F32)
        st_next = score(0)
        for h in range(hg):
            st = st_next
            if h + 1 < hg:
                st_next = score(h + 1)
            if diagonal:
                st = jnp.where(causal, st, NEG)
            m_old = m_ref[h]
            m_new = jnp.maximum(m_old, jnp.max(st, axis=0, keepdims=True))
            alpha = jnp.exp2(m_old - m_new)
            p = jnp.exp2(st - m_new)
            acc_ref[h] = alpha * acc_ref[h] + jnp.dot(vat, p.astype(BF16), preferred_element_type=F32)
            m_ref[h] = m_new

    @pl.when(j < last_j)
    def _():
        accumulate(False)

    @pl.when(j == last_j)
    def _():
        accumulate(True)
        for h in range(hg):
            acc = acc_ref[h]
            o = acc[:HEAD_DIM, :] / acc[HEAD_DIM:HEAD_DIM + 1, :]
            o_ref[0, h] = o.T.astype(o_ref.dtype)


def _sel_attn(q, bias, k_s, v_s, s, tq=512, tk=512):
    assert tq <= tk
    b, nh, _, dh = q.shape
    g, hg = N_KV_GROUPS, HEADS_PER_GROUP
    nsel = bias.shape[3]
    tk = min(tk, s)
    pairs = [(i, j) for i in range(s // tq) for j in range(((i + 1) * tq - 1) // tk + 1)]
    it = jnp.asarray([pr[0] for pr in pairs], I32)
    jt = jnp.asarray([pr[1] for pr in pairs], I32)
    q_map = lambda bi, gi, pid, it_ref, jt_ref: (bi, gi, it_ref[pid], 0)
    kv_map = lambda bi, gi, pid, it_ref, jt_ref: (bi, gi, jt_ref[pid], 0)
    kern = functools.partial(_sel_attn_kernel, tq=tq, tk=tk)
    grid_spec = pltpu.PrefetchScalarGridSpec(
        num_scalar_prefetch=2, grid=(b, g, len(pairs)),
        in_specs=[pl.BlockSpec((1, hg, tq, dh), q_map), pl.BlockSpec((1, 1, tq, nsel), q_map),
                  pl.BlockSpec((1, 1, tk, dh), kv_map), pl.BlockSpec((1, 1, tk, dh), kv_map)],
        out_specs=pl.BlockSpec((1, hg, tq, dh), q_map),
        scratch_shapes=[pltpu.VMEM((hg, tq, nsel + dh), BF16), pltpu.VMEM((hg, 1, tq), F32),
                        pltpu.VMEM((hg, LANES, tq), F32)])
    return pl.pallas_call(
        kern, grid_spec=grid_spec, out_shape=jax.ShapeDtypeStruct((b, nh, s, dh), BF16),
        compiler_params=_params(("parallel", "parallel", "arbitrary")), name="nsa_sel_attn",
    )(it, jt, q, bias, k_s, v_s)


def _win_attn_kernel(q_ref, *refs, tq, nwin):
    k_refs, v_refs, o_ref = refs[:nwin], refs[nwin:2 * nwin], refs[2 * nwin]
    i = pl.program_id(2)
    hg = q_ref.shape[1]
    nk = nwin * tq
    k = jnp.concatenate([r[0, 0] for r in k_refs], axis=0)
    vt = jnp.concatenate([r[0, 0] for r in v_refs], axis=0).astype(F32).T.astype(BF16)
    sp = i * tq - WINDOW + lax.broadcasted_iota(I32, (nk, tq), 0)
    t = i * tq + lax.broadcasted_iota(I32, (nk, tq), 1)
    mask = (sp <= t) & (sp > t - WINDOW) & (sp >= 0)
    score = lambda h: lax.dot_general(k, q_ref[0, h], (((1,), (1,)), ((), ())), preferred_element_type=F32)
    st_next = score(0)
    for h in range(hg):
        st = st_next
        if h + 1 < hg:
            st_next = score(h + 1)
        st = jnp.where(mask, st, NEG)
        m = jnp.max(st, axis=0, keepdims=True)
        p = jnp.where(mask, jnp.exp2(st - m), 0.0)
        l = jnp.sum(p, axis=0, keepdims=True)
        o = jnp.dot(vt, p.astype(BF16), preferred_element_type=F32) / l
        o_ref[0, h] = o.T.astype(o_ref.dtype)


def _win_attn(q, k_w, v_w, s, tq=512):
    b, nh, _, dh = q.shape
    g, hg = N_KV_GROUPS, HEADS_PER_GROUP
    nwin = WINDOW // tq + 1
    pad = ((0, 0), (0, 0), (WINDOW, 0), (0, 0))
    kp, vp = jnp.pad(k_w, pad), jnp.pad(v_w, pad)
    kv_specs = [pl.BlockSpec((1, 1, tq, dh), functools.partial(lambda bi, gi, i, w: (bi, gi, i + w, 0), w=w))
                for w in range(nwin)]
    kern = functools.partial(_win_attn_kernel, tq=tq, nwin=nwin)
    return pl.pallas_call(
        kern, grid=(b, g, s // tq),
        in_specs=[pl.BlockSpec((1, hg, tq, dh), lambda bi, gi, i: (bi, gi, i, 0))] + kv_specs + kv_specs,
        out_specs=pl.BlockSpec((1, hg, tq, dh), lambda bi, gi, i: (bi, gi, i, 0)),
        out_shape=jax.ShapeDtypeStruct((b, nh, s, dh), BF16),
        compiler_params=_params(("parallel", "parallel", "parallel")), name="nsa_win_attn",
    )(q, *([kp] * nwin), *([vp] * nwin))


def _causal_conv(x, tail, w_ref, b_ref):
    n = x.shape[0]
    xx = jnp.concatenate([tail, x], axis=0)
    out = b_ref[...] + w_ref[CONV_WIDTH - 1:CONV_WIDTH, :] * x
    for k in range(CONV_WIDTH - 1):
        off = 8 - (CONV_WIDTH - 1) + k
        out = out + w_ref[k:k + 1, :] * xx[off:off + n, :]
    return out


def _rglru_kernel(x_ref, gate_ref, pos_ref, cw_ref, cb_ref, wr_ref, br_ref, wi_ref, bi_ref, lam_ref,
                  o_ref, tail_ref, h_ref):
    @pl.when(pl.program_id(1) == 0)
    def _():
        tail_ref[...] = jnp.zeros(tail_ref.shape, F32)
        h_ref[...] = jnp.zeros(h_ref.shape, F32)

    n = x_ref.shape[0]
    x = x_ref[...].astype(F32)
    xc = _causal_conv(x, tail_ref[...], cw_ref, cb_ref)
    tail_ref[...] = x[n - 8:, :]
    xcb = xc.astype(BF16)
    r = _sigmoid(jnp.dot(xcb, wr_ref[...], preferred_element_type=F32) + br_ref[...])
    gi = _sigmoid(jnp.dot(xcb, wi_ref[...], preferred_element_type=F32) + bi_ref[...])
    log_a = -LRU_C * r * _softplus(-lam_ref[...])
    reset = pos_ref[...] == 0
    a = jnp.where(reset, 0.0, jnp.exp(log_a))
    mult = jnp.where(reset, 1.0, jnp.sqrt(jnp.maximum(1.0 - jnp.exp(2.0 * log_a), 0.0)))
    bb = mult * (gi * xc)
    row = lax.broadcasted_iota(I32, a.shape, 0)
    d = 1
    while d < n:
        a_sh = pltpu.roll(a, d, 0)
        b_sh = pltpu.roll(bb, d, 0)
        live = row >= d
        bb = jnp.where(live, a * b_sh + bb, bb)
        a = jnp.where(live, a * a_sh, a)
        d *= 2
    h = bb + a * h_ref[...]
    h_ref[...] = h[n - 1:n, :]
    o_ref[...] = (h * _gelu(gate_ref[...].astype(F32))).astype(o_ref.dtype)


def _block_diag(w):
    nb, bs, _ = w.shape
    eye = jnp.eye(nb, dtype=w.dtype)
    return (w[:, :, None, :] * eye[:, None, :, None]).reshape(nb * bs, nb * bs)


def _rglru(l_x, l_g, positions, conv_w, conv_b, w_r, b_r, w_i, b_i, lam, b, s, ts=256):
    d = l_x.shape[1]
    nt = s // ts
    row = lambda bi, i: (bi * nt + i, 0)
    c2 = lambda bi, i: (0, 0)
    vec = lambda v: v.reshape(1, d).astype(F32)
    return pl.pallas_call(
        _rglru_kernel, grid=(b, nt),
        in_specs=[pl.BlockSpec((ts, d), row), pl.BlockSpec((ts, d), row), pl.BlockSpec((ts, 1), row),
                  pl.BlockSpec((CONV_WIDTH, d), c2), pl.BlockSpec((1, d), c2),
                  pl.BlockSpec((d, d), c2), pl.BlockSpec((1, d), c2),
                  pl.BlockSpec((d, d), c2), pl.BlockSpec((1, d), c2), pl.BlockSpec((1, d), c2)],
        out_specs=pl.BlockSpec((ts, d), row),
        out_shape=jax.ShapeDtypeStruct((b * s, d), BF16),
        scratch_shapes=[pltpu.VMEM((8, d), F32), pltpu.VMEM((1, d), F32)],
        compiler_params=_params(("parallel", "arbitrary")), name="rglru",
    )(l_x, l_g, positions.reshape(b * s, 1).astype(I32), conv_w.astype(F32), vec(conv_b),
      _block_diag(w_r).astype(BF16), vec(b_r), _block_diag(w_i).astype(BF16), vec(b_i), vec(lam))


def _ssd_kernel(z_ref, xbc_ref, dt_ref, cw_ref, cb_ref, dtb_ref, alog_ref, dfull_ref, ng_ref,
                o_ref, tail_ref, state_ref, y_ref):
    @pl.when(pl.program_id(1) == 0)
    def _():
        tail_ref[...] = jnp.zeros(tail_ref.shape, F32)
        state_ref[...] = jnp.zeros(state_ref.shape, F32)

    n = xbc_ref.shape[0]
    hg = SSM_HEADS // SSM_GROUPS
    x = xbc_ref[...].astype(F32)
    xc = _causal_conv(x, tail_ref[...], cw_ref, cb_ref)
    tail_ref[...] = x[n - 8:, :]
    xc = xc * _sigmoid(xc)
    xs = xc[:, :SSM_INNER]
    dt = _softplus(dt_ref[...] + dtb_ref[...])
    adt = dt * (-jnp.exp(alog_ref[...]))
    row = lax.broadcasted_iota(I32, adt.shape, 0)
    acs = adt
    d = 1
    while d < n:
        acs = acs + jnp.where(row >= d, pltpu.roll(acs, d, 0), 0.0)
        d *= 2
    acs_t = acs.T
    li = lax.broadcasted_iota(I32, (n, n), 0)
    si = lax.broadcasted_iota(I32, (n, n), 1)
    tri = li >= si
    for g in range(SSM_GROUPS):
        bm = xc[:, SSM_INNER + g * SSM_STATE:SSM_INNER + (g + 1) * SSM_STATE].astype(BF16)
        cm = xc[:, SSM_INNER + (SSM_GROUPS + g) * SSM_STATE:SSM_INNER + (SSM_GROUPS + g + 1) * SSM_STATE].astype(BF16)
        cb = lax.dot_general(cm, bm, (((1,), (1,)), ((), ())), preferred_element_type=F32)
        bm_t = bm.T
        for hh in range(hg):
            h = g * hg + hh
            acol = acs[:, h:h + 1]
            arow = acs_t[h:h + 1, :]
            decay = jnp.exp(jnp.where(tri, acol - arow, NEG))
            xh = xs[:, h * SSM_HEAD_DIM:(h + 1) * SSM_HEAD_DIM] * dt[:, h:h + 1]
            a_last = acs[n - 1:n, h:h + 1]
            y = jnp.dot((cb * decay).astype(BF16), xh.astype(BF16), preferred_element_type=F32)
            st = state_ref[h]
            y = y + jnp.dot(cm, st.astype(BF16), preferred_element_type=F32) * jnp.exp(acol)
            upd = jnp.dot(bm_t, (xh * jnp.exp(a_last - acol)).astype(BF16), preferred_element_type=F32)
            state_ref[h] = jnp.exp(a_last) * st + upd
            y_ref[:, h * SSM_HEAD_DIM:(h + 1) * SSM_HEAD_DIM] = y
    z = z_ref[...].astype(F32)
    y = (y_ref[...] + dfull_ref[...] * xs) * (z * _sigmoid(z))
    gw = SSM_INNER // SSM_GROUPS
    for g in range(SSM_GROUPS):
        yg = y[:, g * gw:(g + 1) * gw]
        ms = jnp.mean(yg * yg, axis=-1, keepdims=True)
        o_ref[:, g * gw:(g + 1) * gw] = (yg * lax.rsqrt(ms + EPS) * ng_ref[:, g * gw:(g + 1) * gw]).astype(o_ref.dtype)


def _ssd(s_z, s_xbc, s_dt, conv_w, conv_b, dt_bias, a_log, d_skip, norm_g, b, s):
    n = SSM_CHUNK
    nt = s // n
    c = s_xbc.shape[1]
    row = lambda bi, i: (bi * nt + i, 0)
    c2 = lambda bi, i: (0, 0)
    pad_h = lambda v: jnp.zeros((1, LANES), F32).at[0, :SSM_HEADS].set(v.astype(F32))
    dfull = jnp.repeat(d_skip.astype(F32), SSM_HEAD_DIM).reshape(1, SSM_INNER)
    return pl.pallas_call(
        _ssd_kernel, grid=(b, nt),
        in_specs=[pl.BlockSpec((n, SSM_INNER), row), pl.BlockSpec((n, c), row), pl.BlockSpec((n, LANES), row),
                  pl.BlockSpec((CONV_WIDTH, c), c2), pl.BlockSpec((1, c), c2),
                  pl.BlockSpec((1, LANES), c2), pl.BlockSpec((1, LANES), c2),
                  pl.BlockSpec((1, SSM_INNER), c2), pl.BlockSpec((1, SSM_INNER), c2)],
        out_specs=pl.BlockSpec((n, SSM_INNER), row),
        out_shape=jax.ShapeDtypeStruct((b * s, SSM_INNER), BF16),
        scratch_shapes=[pltpu.VMEM((8, c), F32), pltpu.VMEM((SSM_HEADS, SSM_STATE, SSM_HEAD_DIM), F32),
                        pltpu.VMEM((n, SSM_INNER), F32)],
        compiler_params=_params(("parallel", "arbitrary")), name="ssd",
    )(s_z, s_xbc, s_dt, conv_w.astype(F32), conv_b.reshape(1, c).astype(F32), pad_h(dt_bias), pad_h(a_log),
      dfull, norm_g.reshape(1, SSM_INNER).astype(F32))


def _merge_kernel(oc_ref, os_ref, ow_ref, ag_ref, yb_ref, yc_ref, mg_ref, x_ref, wb_ref, wo_ref,
                  o_ref, ya_ref):
    gates = _sigmoid(ag_ref[...])
    for h in range(N_HEADS):
        ya = (gates[:, 3 * h:3 * h + 1] * oc_ref[0, h].astype(F32)
              + gates[:, 3 * h + 1:3 * h + 2] * os_ref[0, h].astype(F32)
              + gates[:, 3 * h + 2:3 * h + 3] * ow_ref[0, h].astype(F32))
        ya_ref[:, h * HEAD_DIM:(h + 1) * HEAD_DIM] = ya.astype(ya_ref.dtype)
    d = D_MODEL
    merged = _sigmoid(mg_ref[:, 0:d].astype(F32)) * jnp.dot(ya_ref[...], wb_ref[0], preferred_element_type=F32)
    merged += _sigmoid(mg_ref[:, d:2 * d].astype(F32)) * jnp.dot(yb_ref[...], wb_ref[1], preferred_element_type=F32)
    merged += _sigmoid(mg_ref[:, 2 * d:3 * d].astype(F32)) * jnp.dot(yc_ref[...], wb_ref[2], preferred_element_type=F32)
    o_ref[...] = x_ref[...] + jnp.dot(merged.astype(BF16), wo_ref[...], preferred_element_type=F32)


def _merge(o_c, o_s, o_w, a_g, y_b, y_c, m_g, x, w_branch, w_out, b, s, tm=256):
    d = D_MODEL
    nt = s // tm
    row = lambda i: (i, 0)
    o_spec = pl.BlockSpec((1, N_HEADS, tm, HEAD_DIM), lambda i: (i // nt, 0, i % nt, 0))
    return pl.pallas_call(
        _merge_kernel, grid=(b * nt,),
        in_specs=[o_spec, o_spec, o_spec, pl.BlockSpec((tm, LANES), row), pl.BlockSpec((tm, d), row),
                  pl.BlockSpec((tm, d), row), pl.BlockSpec((tm, 3 * d), row), pl.BlockSpec((tm, d), row),
                  pl.BlockSpec((3, d, d), lambda i: (0, 0, 0)), pl.BlockSpec((d, d), lambda i: (0, 0))],
        out_specs=pl.BlockSpec((tm, d), row),
        out_shape=jax.ShapeDtypeStruct((b * s, d), F32),
        scratch_shapes=[pltpu.VMEM((tm, d), BF16)],
        compiler_params=_params(("parallel",)), name="merge",
    )(o_c, o_s, o_w, a_g, y_b, y_c, m_g, x, w_branch.astype(BF16), w_out.astype(BF16))


def _topk_rows(s, k):
    n = s.shape[0]
    row = lax.broadcasted_iota(I32, s.shape, 0)
    vals, idxs = [], []
    for _ in range(k):
        m = jnp.max(s, axis=0, keepdims=True)
        idx = jnp.min(jnp.where(s == m, row, n), axis=0, keepdims=True)
        vals.append(m)
        idxs.append(idx)
        s = jnp.where(row == idx, -jnp.inf, s)
    return jnp.concatenate(vals, axis=0), jnp.concatenate(idxs, axis=0)


def _pick_rows(table, sel, k):
    out = jnp.zeros(sel.shape, table.dtype)
    for a in range(k):
        out = jnp.where(sel == a, table[a:a + 1, :], out)
    return out


def _peer_route_kernel(h_ref, wq_ref, keys_ref, i1_ref, i2_ref, g_ref):
    k = PEER_TOPK
    assert k == 16
    qt = lax.dot_general(wq_ref[...], h_ref[...], (((1,), (1,)), ((), ())), preferred_element_type=F32)
    qt = qt.astype(BF16)
    i1s, i2s, gs = [], [], []
    for hd in range(PEER_HEADS):
        tops = []
        for half in range(2):
            c = hd * 2 + half
            sc = jnp.dot(keys_ref[c], qt[c * PEER_HALF:(c + 1) * PEER_HALF, :], preferred_element_type=F32)
            tops.append(_topk_rows(sc, k))
        (s1, i1), (s2, i2) = tops
        cand = jnp.concatenate(
            [s1[0:1, :] + s2] + [s1[a:a + 1, :] + s2[0:8, :] for a in range(1, 8)] + [s1[8:16, :] + s2[0:1, :]],
            axis=0)
        sc, r = _topk_rows(cand, k)
        e = jnp.exp(sc - sc[0:1, :])
        gs.append(e / jnp.sum(e, axis=0, keepdims=True))
        a_sel = jnp.where(r < 16, 0, jnp.where(r < 72, 1 + (r - 16) // 8, r - 64))
        b_sel = jnp.where(r < 16, r, jnp.where(r < 72, (r - 16) % 8, 0))
        i1s.append(_pick_rows(i1, a_sel, k))
        i2s.append(_pick_rows(i2, b_sel, k))
    i1_ref[...] = jnp.concatenate(i1s, axis=0).T
    i2_ref[...] = jnp.concatenate(i2s, axis=0).T
    g_ref[...] = jnp.concatenate(gs, axis=0).T


def _peer_route(h, w_q, sub_keys, tm=256):
    t, d = h.shape
    nq = w_q.shape[1]
    wq_t = w_q.T.astype(BF16)
    keys = sub_keys.reshape(PEER_HEADS * 2, PEER_KEYS, PEER_HALF).astype(BF16)
    slots = PEER_HEADS * PEER_TOPK
    row = lambda i: (i, 0)
    return pl.pallas_call(
        _peer_route_kernel, grid=(t // tm,),
        in_specs=[pl.BlockSpec((tm, d), row), pl.BlockSpec((nq, d), lambda i: (0, 0)),
                  pl.BlockSpec((PEER_HEADS * 2, PEER_KEYS, PEER_HALF), lambda i: (0, 0, 0))],
        out_specs=[pl.BlockSpec((tm, slots), row)] * 3,
        out_shape=[jax.ShapeDtypeStruct((t, slots), I32), jax.ShapeDtypeStruct((t, slots), I32),
                   jax.ShapeDtypeStruct((t, slots), F32)],
        compiler_params=_params(("parallel",)), name="peer_route",
    )(h, wq_t, keys)


def _peer_u_kernel(h_ref, u_ref, i1_ref, i2_ref, o_ref, *, blocks):
    c = pl.program_id(0)
    a = lax.dot_general(h_ref[...], u_ref[...], (((1,), (1,)), ((), ())), preferred_element_type=F32)
    i1 = i1_ref[...]
    i2 = i2_ref[...]
    acc = jnp.zeros(i1.shape, F32)
    for bk in range(blocks):
        got = jnp.take_along_axis(a[:, bk * PEER_KEYS:(bk + 1) * PEER_KEYS], i2, axis=1)
        acc = jnp.where(i1 == c * blocks + bk, got, acc)
    o_ref[0] = acc


def _peer_u(h, u, i1, i2, tm=512, blocks=16):
    t, d = h.shape
    ne = u.shape[0]
    ec = blocks * PEER_KEYS
    nchunk = ne // ec
    slots = i1.shape[1]
    kern = functools.partial(_peer_u_kernel, blocks=blocks)
    return pl.pallas_call(
        kern, grid=(nchunk, t // tm),
        in_specs=[pl.BlockSpec((tm, d), lambda c, i: (i, 0)), pl.BlockSpec((ec, d), lambda c, i: (c, 0)),
                  pl.BlockSpec((tm, slots), lambda c, i: (i, 0)), pl.BlockSpec((tm, slots), lambda c, i: (i, 0))],
        out_specs=pl.BlockSpec((1, tm, slots), lambda c, i: (c, i, 0)),
        out_shape=jax.ShapeDtypeStruct((nchunk, t, slots), F32),
        compiler_params=_params(("parallel", "parallel")), name="peer_u",
    )(h, u, i1, i2)


def _peer_v_kernel(parts_ref, g_ref, i1_ref, i2_ref, v_ref, x_ref, o_ref, w_ref, wg_ref, *, tm, blocks):
    c = pl.program_id(1)
    nk = PEER_KEYS

    @pl.when(c == 0)
    def _():
        w_ref[...] = g_ref[...] * _gelu(jnp.sum(parts_ref[...], axis=0))
        o_ref[...] = x_ref[...]
        sub = lax.broadcasted_iota(I32, (nk, w_ref.shape[1]), 0)

        def per_token(t, carry):
            wrow = w_ref[pl.ds(t, 1), :]
            lhs = jnp.where(i1_ref[pl.ds(t, 1), :] == sub, wrow, 0.0).astype(BF16)
            rhs = jnp.where(i2_ref[pl.ds(t, 1), :] == sub, 1.0, 0.0).astype(BF16)
            grid = lax.dot_general(lhs, rhs, (((1,), (1,)), ((), ())), preferred_element_type=F32)
            wg_ref[pl.ds(pl.multiple_of(t * nk, nk), nk), :] = grid
            return carry

        lax.fori_loop(0, tm, per_token, 0, unroll=16)

    acc = jnp.zeros(o_ref.shape, F32)
    for bk in range(0, blocks, 2):
        i1 = c * blocks + bk
        lhs = jnp.concatenate([wg_ref[pl.ds(i1, tm, stride=nk), :], wg_ref[pl.ds(i1 + 1, tm, stride=nk), :]],
                              axis=1).astype(BF16)
        rhs = v_ref[bk:bk + 2].reshape(2 * nk, v_ref.shape[2])
        acc += jnp.dot(lhs, rhs, preferred_element_type=F32)
    o_ref[...] += acc


def _peer_v(parts, g, i1, i2, v, x, tm=256, blocks=32):
    t, d = x.shape
    nchunk_u = parts.shape[0]
    slots = g.shape[1]
    nk = PEER_KEYS
    v3 = v.reshape(nk, nk, d)
    kern = functools.partial(_peer_v_kernel, tm=tm, blocks=blocks)
    row = lambda i, c: (i, 0)
    return pl.pallas_call(
        kern, grid=(t // tm, nk // blocks),
        in_specs=[pl.BlockSpec((nchunk_u, tm, slots), lambda i, c: (0, i, 0)),
                  pl.BlockSpec((tm, slots), row), pl.BlockSpec((tm, slots), row), pl.BlockSpec((tm, slots), row),
                  pl.BlockSpec((blocks, nk, d), lambda i, c: (c, 0, 0)), pl.BlockSpec((tm, d), row)],
        out_specs=pl.BlockSpec((tm, d), row),
        out_shape=jax.ShapeDtypeStruct((t, d), F32),
        scratch_shapes=[pltpu.VMEM((tm, slots), F32), pltpu.VMEM((tm * nk, nk), F32)],
        compiler_params=_params(("parallel", "arbitrary")), name="peer_v",
    )(parts, g, i1, i2, v3, x)


IN_SIZES = (1024, 768, 48, 1024, 1024, 1024, 1536, 16, 3072)


def _pad_cols(w, n):
    return jnp.pad(w, ((0, 0), (0, n - w.shape[1])))


def _mixer(x, positions, b, s, p):
    xn = _rmsnorm(x, p["mix_norm_g"])
    offs = [0]
    for n in IN_SIZES:
        offs.append(offs[-1] + n)
    w = [p["w_in"][:, offs[i]:offs[i + 1]] for i in range(len(IN_SIZES))]
    a_q = _matmul(xn, w[0].astype(BF16), BF16, "proj_q")
    a_kv = _matmul(xn, w[1].astype(BF16), BF16, "proj_kv")
    a_g = _matmul(xn, _pad_cols(w[2], LANES).astype(BF16), F32, "proj_ag")
    l_x = _matmul(xn, w[3].astype(BF16), BF16, "proj_lx")
    l_g = _matmul(xn, w[4].astype(BF16), BF16, "proj_lg")
    s_z = _matmul(xn, w[5].astype(BF16), BF16, "proj_sz")
    s_xbc = _matmul(xn, w[6].astype(BF16), BF16, "proj_sxbc")
    s_dt = _matmul(xn, _pad_cols(w[7], LANES).astype(BF16), F32, "proj_sdt")
    m_g = _matmul(xn, w[8].astype(BF16), BF16, "proj_mg", tn=1024)

    q, kc, vc, ks, vs, kw, vw = _nsa_prep(a_q, a_kv, positions, p["q_norm_g"], p["k_norm_g"], b, s)
    k_cmp, v_cmp = _compress(kc, vc, p["cmp_pe_k"], p["cmp_pe_v"], p["cmp_k_w1"], p["cmp_k_w2"],
                             p["cmp_v_w1"], p["cmp_v_w2"], p["k_norm_g"])
    o_c, bias = _cmp_select(q, k_cmp, v_cmp, s)
    o_s = _sel_attn(q, bias, ks, vs, s)
    o_w = _win_attn(q, kw, vw, s)
    y_b = _rglru(l_x, l_g, positions, p["lru_conv_w"], p["lru_conv_b"], p["lru_w_r"], p["lru_b_r"],
                 p["lru_w_i"], p["lru_b_i"], p["lru_lambda"], b, s)
    y_c = _ssd(s_z, s_xbc, s_dt, p["ssm_conv_w"], p["ssm_conv_b"], p["ssm_dt_bias"], p["ssm_a_log"],
               p["ssm_d"], p["ssm_norm_g"], b, s)
    return _merge(o_c, o_s, o_w, a_g, y_b, y_c, m_g, x, p["w_branch"], p["w_out"], b, s)


def _peer(x, p):
    h = _rmsnorm(x, p["ffn_norm_g"])
    i1, i2, g = _peer_route(h, p["peer_w_q"], p["peer_sub_keys"])
    parts = _peer_u(h, p["peer_u"].astype(BF16), i1, i2)
    return _peer_v(parts, g, i1, i2, p["peer_v"].astype(BF16), x)


_LAYER_PARAMS = ("mix_norm_g", "w_in", "q_norm_g", "k_norm_g", "cmp_pe_k", "cmp_pe_v", "cmp_k_w1", "cmp_k_w2",
                 "cmp_v_w1", "cmp_v_w2", "lru_conv_w", "lru_conv_b", "lru_w_r", "lru_b_r", "lru_w_i", "lru_b_i",
                 "lru_lambda", "ssm_conv_w", "ssm_conv_b", "ssm_dt_bias", "ssm_a_log", "ssm_d", "ssm_norm_g",
                 "w_branch", "w_out", "ffn_norm_g", "peer_w_q", "peer_sub_keys", "peer_u", "peer_v")


def kernel(x, positions, mix_norm_g, w_in, q_norm_g, k_norm_g, cmp_pe_k, cmp_pe_v, cmp_k_w1, cmp_k_w2, cmp_v_w1, cmp_v_w2, lru_conv_w, lru_conv_b, lru_w_r, lru_b_r, lru_w_i, lru_b_i, lru_lambda, ssm_conv_w, ssm_conv_b, ssm_dt_bias, ssm_a_log, ssm_d, ssm_norm_g, w_branch, w_out, ffn_norm_g, peer_w_q, peer_sub_keys, peer_u, peer_v):
    stacked = dict(zip(_LAYER_PARAMS, (mix_norm_g, w_in, q_norm_g, k_norm_g, cmp_pe_k, cmp_pe_v, cmp_k_w1,
                                       cmp_k_w2, cmp_v_w1, cmp_v_w2, lru_conv_w, lru_conv_b, lru_w_r, lru_b_r,
                                       lru_w_i, lru_b_i, lru_lambda, ssm_conv_w, ssm_conv_b, ssm_dt_bias,
                                       ssm_a_log, ssm_d, ssm_norm_g, w_branch, w_out, ffn_norm_g, peer_w_q,
                                       peer_sub_keys, peer_u, peer_v)))
    b, s, d = x.shape
    xf = x.reshape(b * s, d).astype(F32)
    for layer in range(mix_norm_g.shape[0]):
        p = {name: arr[layer] for name, arr in stacked.items()}
        xf = _mixer(xf, positions, b, s, p)
        xf = _peer(xf, p)
    return xf.reshape(b, s, d).astype(x.dtype)
```

```python
import functools
import math

import jax
import jax.numpy as jnp
from jax import lax
from jax.experimental import pallas as pl
from jax.experimental.pallas import tpu as pltpu

F32 = jnp.float32
BF16 = jnp.bfloat16
I32 = jnp.int32

D_MODEL = 1024
HEAD_DIM = 64
N_HEADS = 16
N_KV_GROUPS = 2
HEADS_PER_GROUP = 8
CMP_BLOCK = 32
CMP_STRIDE = 16
CMP_HIDDEN = 256
SEL_BLOCK = 64
N_SELECT = 16
WINDOW = 512
SEL_FORCE = 100.0
ROPE_THETA = 10000.0
SCALE = HEAD_DIM ** -0.5
LOG2E = math.log2(math.e)
LRU_HEADS = 16
LRU_BLOCK = 64
LRU_C = 8.0
CONV_WIDTH = 4
SSM_HEADS = 16
SSM_HEAD_DIM = 64
SSM_GROUPS = 2
SSM_STATE = 128
SSM_CHUNK = 128
SSM_INNER = 1024
PEER_HEADS = 8
PEER_KEYS = 128
PEER_HALF = 128
PEER_TOPK = 16
EPS = 1e-6
NEG = -1e30
LANES = 128

VMEM_LIMIT = 56 * 1024 * 1024
SEL_LOOKAHEAD = 3


def _params(sem):
    return pltpu.CompilerParams(dimension_semantics=sem, vmem_limit_bytes=VMEM_LIMIT)


def _gelu(x):
    return 0.5 * x * (1.0 + jnp.tanh(math.sqrt(2.0 / math.pi) * (x + 0.044715 * x * x * x)))


def _sigmoid(x):
    return 0.5 * jnp.tanh(0.5 * x) + 0.5


def _softplus(x):
    return jnp.maximum(x, 0.0) + jnp.log(1.0 + jnp.exp(-jnp.abs(x)))


def _rmsnorm_kernel(x_ref, g_ref, o_ref):
    x = x_ref[...].astype(F32)
    ms = jnp.mean(x * x, axis=-1, keepdims=True)
    o_ref[...] = (x * lax.rsqrt(ms + EPS) * g_ref[...]).astype(o_ref.dtype)


def _rmsnorm(x, g, tm=512):
    t, d = x.shape
    return pl.pallas_call(
        _rmsnorm_kernel, grid=(t // tm,),
        in_specs=[pl.BlockSpec((tm, d), lambda i: (i, 0)), pl.BlockSpec((1, d), lambda i: (0, 0))],
        out_specs=pl.BlockSpec((tm, d), lambda i: (i, 0)),
        out_shape=jax.ShapeDtypeStruct((t, d), BF16),
        compiler_params=_params(("parallel",)), name="rmsnorm")(x, g.reshape(1, d).astype(F32))


def _mm_kernel(x_ref, w_ref, o_ref):
    o_ref[...] = jnp.dot(x_ref[...], w_ref[...], preferred_element_type=F32).astype(o_ref.dtype)


def _matmul(x, w, out_dtype, name, tm=512, tn=None):
    t, k = x.shape
    n = w.shape[1]
    tn = n if tn is None else tn
    return pl.pallas_call(
        _mm_kernel, grid=(n // tn, t // tm),
        in_specs=[pl.BlockSpec((tm, k), lambda j, i: (i, 0)), pl.BlockSpec((k, tn), lambda j, i: (0, j))],
        out_specs=pl.BlockSpec((tm, tn), lambda j, i: (i, j)),
        out_shape=jax.ShapeDtypeStruct((t, n), out_dtype),
        compiler_params=_params(("parallel", "parallel")), name=name)(x, w)


def _norm_rope(x, g, cos, sin_signed, bd):
    ms = jnp.dot((x * x).astype(BF16), bd, preferred_element_type=F32)
    y = x * lax.rsqrt(ms + EPS) * g
    lane = lax.broadcasted_iota(I32, y.shape, 1)
    first_half = (lane % HEAD_DIM) < (HEAD_DIM // 2)
    partner = jnp.where(first_half, pltpu.roll(y, LANES - HEAD_DIM // 2, 1), pltpu.roll(y, HEAD_DIM // 2, 1))
    return y * cos + partner * sin_signed


def _nsa_prep_kernel(aq_ref, akv_ref, pos_ref, inv_ref, sgn_ref, gq_ref, gk_ref, bd_ref,
                     q_ref, kc_ref, vc_ref, ks_ref, vs_ref, kw_ref, vw_ref):
    ang = pos_ref[...].astype(F32) * inv_ref[...]
    cos = jnp.cos(ang)
    sin_signed = jnp.sin(ang) * sgn_ref[...]
    bd = bd_ref[...]
    for c in range(D_MODEL // LANES):
        x = aq_ref[:, c * LANES:(c + 1) * LANES].astype(F32)
        y = (_norm_rope(x, gq_ref[...], cos, sin_signed, bd) * (SCALE * LOG2E)).astype(q_ref.dtype)
        q_ref[0, 2 * c] = y[:, :HEAD_DIM]
        q_ref[0, 2 * c + 1] = y[:, HEAD_DIM:]
    outs = (kc_ref, vc_ref, ks_ref, vs_ref, kw_ref, vw_ref)
    for c, o_ref in enumerate(outs):
        x = akv_ref[:, c * LANES:(c + 1) * LANES]
        if c % 2 == 0:
            y = _norm_rope(x.astype(F32), gk_ref[...], cos, sin_signed, bd).astype(o_ref.dtype)
        else:
            y = x.astype(o_ref.dtype)
        o_ref[0, 0] = y[:, :HEAD_DIM]
        o_ref[0, 1] = y[:, HEAD_DIM:]


def _nsa_prep(a_q, a_kv, positions, q_norm_g, k_norm_g, b, s, tm=512):
    half = HEAD_DIM // 2
    lane = jnp.arange(LANES)
    inv = (ROPE_THETA ** (-((lane % half).astype(F32)) / half)).reshape(1, LANES)
    sgn = jnp.where((lane % HEAD_DIM) < half, -1.0, 1.0).astype(F32).reshape(1, LANES)
    bd = jnp.where((lane[:, None] // HEAD_DIM) == (lane[None, :] // HEAD_DIM), 1.0 / HEAD_DIM, 0.0).astype(BF16)
    gq = jnp.tile(q_norm_g.astype(F32), 2).reshape(1, LANES)
    gk = jnp.tile(k_norm_g.astype(F32), 2).reshape(1, LANES)
    nt = s // tm
    row = lambda i: (i, 0)
    const = lambda i: (0, 0)
    kv_shape = jax.ShapeDtypeStruct((b, N_KV_GROUPS, s, HEAD_DIM), BF16)
    kv_spec = pl.BlockSpec((1, N_KV_GROUPS, tm, HEAD_DIM), lambda i: (i // nt, 0, i % nt, 0))
    return pl.pallas_call(
        _nsa_prep_kernel, grid=(b * nt,),
        in_specs=[pl.BlockSpec((tm, D_MODEL), row), pl.BlockSpec((tm, 6 * LANES), row),
                  pl.BlockSpec((tm, 1), row), pl.BlockSpec((1, LANES), const), pl.BlockSpec((1, LANES), const),
                  pl.BlockSpec((1, LANES), const), pl.BlockSpec((1, LANES), const),
                  pl.BlockSpec((LANES, LANES), const)],
        out_specs=[pl.BlockSpec((1, N_HEADS, tm, HEAD_DIM), lambda i: (i // nt, 0, i % nt, 0))] + [kv_spec] * 6,
        out_shape=[jax.ShapeDtypeStruct((b, N_HEADS, s, HEAD_DIM), BF16)] + [kv_shape] * 6,
        compiler_params=_params(("parallel",)), name="nsa_prep",
    )(a_q, a_kv, positions.reshape(b * s, 1).astype(I32), inv, sgn, gq, gk, bd)


def _compress_kernel(uk_ref, uv_ref, pek_ref, pev_ref, kw1_ref, kw2_ref, vw1_ref, vw2_ref, gk_ref,
                     kc_ref, vc_ref):
    half = CMP_STRIDE * HEAD_DIM

    def mlp(u, pe, w1_ref, w2_ref):
        n = u.shape[0]
        ha = jnp.dot(u, w1_ref[:half, :], preferred_element_type=F32)
        hb = jnp.dot(u, w1_ref[half:, :], preferred_element_type=F32)
        bias = jnp.dot(pe, w1_ref[...], preferred_element_type=F32)[0:1, :]
        pre = ha + pltpu.roll(hb, n - 1, 0) + bias
        return jnp.dot(_gelu(pre).astype(BF16), w2_ref[...], preferred_element_type=F32)

    k = mlp(uk_ref[0, 0], pek_ref[...], kw1_ref, kw2_ref)
    ms = jnp.mean(k * k, axis=-1, keepdims=True)
    kc_ref[0, 0] = (k * lax.rsqrt(ms + EPS) * gk_ref[...]).astype(kc_ref.dtype)
    vc_ref[0, 0] = mlp(uv_ref[0, 0], pev_ref[...], vw1_ref, vw2_ref).astype(vc_ref.dtype)


def _compress(kc, vc, pe_k, pe_v, kw1, kw2, vw1, vw2, k_norm_g):
    b, g, s, dh = kc.shape
    ng = s // CMP_STRIDE
    wide = CMP_BLOCK * dh
    uk = kc.reshape(b, g, ng, CMP_STRIDE * dh)
    uv = vc.reshape(b, g, ng, CMP_STRIDE * dh)
    pek = jnp.zeros((8, wide), BF16).at[0].set(pe_k.reshape(wide).astype(BF16))
    pev = jnp.zeros((8, wide), BF16).at[0].set(pe_v.reshape(wide).astype(BF16))
    u_spec = pl.BlockSpec((1, 1, ng, CMP_STRIDE * dh), lambda i, j: (i, j, 0, 0))
    c2 = lambda i, j: (0, 0)
    o_spec = pl.BlockSpec((1, 1, ng, dh), lambda i, j: (i, j, 0, 0))
    o_shape = jax.ShapeDtypeStruct((b, g, ng, dh), BF16)
    return pl.pallas_call(
        _compress_kernel, grid=(b, g),
        in_specs=[u_spec, u_spec, pl.BlockSpec((8, wide), c2), pl.BlockSpec((8, wide), c2),
                  pl.BlockSpec((wide, CMP_HIDDEN), c2), pl.BlockSpec((CMP_HIDDEN, dh), c2),
                  pl.BlockSpec((wide, CMP_HIDDEN), c2), pl.BlockSpec((CMP_HIDDEN, dh), c2),
                  pl.BlockSpec((1, dh), c2)],
        out_specs=[o_spec, o_spec], out_shape=[o_shape, o_shape],
        compiler_params=_params(("parallel", "parallel")), name="nsa_compress",
    )(uk, uv, pek, pev, kw1.astype(BF16), kw2.astype(BF16), vw1.astype(BF16), vw2.astype(BF16),
      k_norm_g.reshape(1, dh).astype(F32))


def _store_head_pair(o_ref, h, o_t, held):
    if h % 2 == 0:
        return o_t
    pair = jnp.concatenate([held, o_t], axis=0).T
    o_ref[0, :, (h - 1) * HEAD_DIM:(h + 1) * HEAD_DIM] = pair.astype(o_ref.dtype)
    return None


def _cmp_select_kernel(q_ref, kc_ref, vc_ref, ovt_ref, o_ref, bias_ref, *, tq, n_cmp, n_pick):
    i = pl.program_id(2)
    hg = q_ref.shape[1]
    nc = kc_ref.shape[2]
    nsel = ovt_ref.shape[0]
    kc = kc_ref[0, 0]
    vct = vc_ref[0, 0].astype(F32).T.astype(BF16)
    c = lax.broadcasted_iota(I32, (nc, tq), 0)
    t = i * tq + lax.broadcasted_iota(I32, (nc, tq), 1)
    mask = ((CMP_STRIDE * c + CMP_BLOCK - 1) <= t) & (c < n_cmp)
    score = lambda h: lax.dot_general(kc, q_ref[0, h], (((1,), (1,)), ((), ())), preferred_element_type=F32)
    st_next = score(0)
    psum = jnp.zeros((nc, tq), F32)
    held = None
    for h in range(hg):
        st = st_next
        if h + 1 < hg:
            st_next = score(h + 1)
        st = jnp.where(mask, st, NEG)
        m = jnp.max(st, axis=0, keepdims=True)
        p = jnp.where(mask, jnp.exp2(st - m), 0.0)
        l = jnp.sum(p, axis=0, keepdims=True)
        p = p * jnp.where(l > 0.0, 1.0 / l, 0.0)
        o = jnp.dot(vct, p.astype(BF16), preferred_element_type=F32)
        held = _store_head_pair(o_ref, h, o, held)
        psum = psum + p

    hi = psum.astype(BF16)
    lo = (psum - hi.astype(F32)).astype(BF16)
    imp = (jnp.dot(ovt_ref[...], hi, preferred_element_type=F32)
           + jnp.dot(ovt_ref[...], lo, preferred_element_type=F32))
    j = lax.broadcasted_iota(I32, (nsel, tq), 0)
    tt = i * tq + lax.broadcasted_iota(I32, (nsel, tq), 1)
    cur = tt // SEL_BLOCK
    forced = (j == 0) | (j == cur) | (j == cur - 1)
    sc = jnp.where(forced, SEL_FORCE, jnp.where(j * SEL_BLOCK <= tt, imp, -1.0))
    picked = jnp.zeros((nsel, tq), jnp.bool_)
    for _ in range(n_pick):
        mx = jnp.max(sc, axis=0, keepdims=True)
        idx = jnp.min(jnp.where(sc == mx, j, nsel), axis=0, keepdims=True)
        hit = j == idx
        picked = picked | hit
        sc = jnp.where(hit, -jnp.inf, sc)
    bias_ref[0, 0] = jnp.where(picked & (j <= cur), 0.0, NEG).T.astype(bias_ref.dtype)


def _cmp_select(q, k_cmp, v_cmp, s, tq=512):
    b, nh, _, dh = q.shape
    g, hg = N_KV_GROUPS, HEADS_PER_GROUP
    nc = k_cmp.shape[2]
    n_cmp = (s - CMP_BLOCK) // CMP_STRIDE + 1
    nsel = s // SEL_BLOCK
    n_pick = min(N_SELECT, nsel)
    cs = CMP_STRIDE * jnp.arange(nc)
    ss = SEL_BLOCK * jnp.arange(nsel)
    ovt = jnp.clip(jnp.minimum(cs[None, :] + CMP_BLOCK, ss[:, None] + SEL_BLOCK)
                   - jnp.maximum(cs[None, :], ss[:, None]), 0).astype(F32) / CMP_BLOCK
    ovt = jnp.where(jnp.arange(nc)[None, :] < n_cmp, ovt, 0.0).astype(BF16)
    kern = functools.partial(_cmp_select_kernel, tq=tq, n_cmp=n_cmp, n_pick=n_pick)
    return pl.pallas_call(
        kern, grid=(b, g, s // tq),
        in_specs=[pl.BlockSpec((1, hg, tq, dh), lambda bi, gi, i: (bi, gi, i, 0)),
                  pl.BlockSpec((1, 1, nc, dh), lambda bi, gi, i: (bi, gi, 0, 0)),
                  pl.BlockSpec((1, 1, nc, dh), lambda bi, gi, i: (bi, gi, 0, 0)),
                  pl.BlockSpec((nsel, nc), lambda bi, gi, i: (0, 0))],
        out_specs=[pl.BlockSpec((1, tq, hg * dh), lambda bi, gi, i: (bi, i, gi)),
                   pl.BlockSpec((1, 1, tq, nsel), lambda bi, gi, i: (bi, gi, i, 0))],
        out_shape=[jax.ShapeDtypeStruct((b, s, nh * dh), BF16),
                   jax.ShapeDtypeStruct((b, g, s, nsel), BF16)],
        compiler_params=_params(("parallel", "parallel", "parallel")), name="nsa_cmp_select",
    )(q, k_cmp, v_cmp, ovt)


def _sel_attn_kernel(it_ref, jt_ref, q_ref, bias_ref, k_ref, v_ref, o_ref, qa_ref, m_ref, acc_ref, *, tq, tk):
    pid = pl.program_id(2)
    i = it_ref[pid]
    j = jt_ref[pid]
    hg = q_ref.shape[1]
    nsel = bias_ref.shape[3]
    last_j = ((i + 1) * tq - 1) // tk

    @pl.when(j == 0)
    def _():
        for h in range(hg):
            qa_ref[h, :, :nsel] = bias_ref[0, 0]
            qa_ref[h, :, nsel:] = q_ref[0, h]
        m_ref[...] = jnp.full(m_ref.shape, -jnp.inf, F32)
        acc_ref[...] = jnp.zeros(acc_ref.shape, F32)

    def accumulate(diagonal):
        key = j * tk + lax.broadcasted_iota(I32, (tk, nsel), 0)
        blk = lax.broadcasted_iota(I32, (tk, nsel), 1)
        onehot = jnp.where(key // SEL_BLOCK == blk, 1.0, 0.0).astype(BF16)
        ka = jnp.concatenate([onehot, k_ref[0, 0]], axis=1)
        va = jnp.concatenate([v_ref[0, 0].astype(F32), jnp.ones((tk, LANES - HEAD_DIM), F32)], axis=1)
        vat = va.T.astype(BF16)
        if diagonal:
            kp = j * tk + lax.broadcasted_iota(I32, (tk, tq), 0)
            t = i * tq + lax.broadcasted_iota(I32, (tk, tq), 1)
            causal = kp <= t
        score = lambda h: lax.dot_general(ka, qa_ref[h], (((1,), (1,)), ((), ())), preferred_element_type=F32)
        ahead = [score(h) for h in range(SEL_LOOKAHEAD)]
        for h in range(hg):
            st = ahead.pop(0)
            if h + SEL_LOOKAHEAD < hg:
                ahead.append(score(h + SEL_LOOKAHEAD))
            if diagonal:
                st = jnp.where(causal, st, NEG)
            m_old = m_ref[h]
            m_new = jnp.maximum(m_old, jnp.max(st, axis=0, keepdims=True))
            alpha = jnp.exp2(m_old - m_new)
            p = jnp.exp2(st - m_new)
            acc_ref[h] = alpha * acc_ref[h] + jnp.dot(vat, p.astype(BF16), preferred_element_type=F32)
            m_ref[h] = m_new

    @pl.when(j < last_j)
    def _():
        accumulate(False)

    @pl.when(j == last_j)
    def _():
        accumulate(True)
        held = None
        for h in range(hg):
            acc = acc_ref[h]
            o = acc[:HEAD_DIM, :] / acc[HEAD_DIM:HEAD_DIM + 1, :]
            held = _store_head_pair(o_ref, h, o, held)


def _sel_attn(q, bias, k_s, v_s, s, tq=512, tk=512):
    assert tq <= tk
    b, nh, _, dh = q.shape
    g, hg = N_KV_GROUPS, HEADS_PER_GROUP
    nsel = bias.shape[3]
    tk = min(tk, s)
    pairs = [(i, j) for i in range(s // tq) for j in range(((i + 1) * tq - 1) // tk + 1)]
    it = jnp.asarray([pr[0] for pr in pairs], I32)
    jt = jnp.asarray([pr[1] for pr in pairs], I32)
    q_map = lambda bi, gi, pid, it_ref, jt_ref: (bi, gi, it_ref[pid], 0)
    kv_map = lambda bi, gi, pid, it_ref, jt_ref: (bi, gi, jt_ref[pid], 0)
    kern = functools.partial(_sel_attn_kernel, tq=tq, tk=tk)
    grid_spec = pltpu.PrefetchScalarGridSpec(
        num_scalar_prefetch=2, grid=(b, g, len(pairs)),
        in_specs=[pl.BlockSpec((1, hg, tq, dh), q_map), pl.BlockSpec((1, 1, tq, nsel), q_map),
                  pl.BlockSpec((1, 1, tk, dh), kv_map), pl.BlockSpec((1, 1, tk, dh), kv_map)],
        out_specs=pl.BlockSpec((1, tq, hg * dh), lambda bi, gi, pid, it_ref, jt_ref: (bi, it_ref[pid], gi)),
        scratch_shapes=[pltpu.VMEM((hg, tq, nsel + dh), BF16), pltpu.VMEM((hg, 1, tq), F32),
                        pltpu.VMEM((hg, LANES, tq), F32)])
    return pl.pallas_call(
        kern, grid_spec=grid_spec, out_shape=jax.ShapeDtypeStruct((b, s, nh * dh), BF16),
        compiler_params=_params(("parallel", "parallel", "arbitrary")), name="nsa_sel_attn",
    )(it, jt, q, bias, k_s, v_s)


def _win_attn_kernel(q_ref, *refs, tq, nwin):
    k_refs, v_refs, o_ref = refs[:nwin], refs[nwin:2 * nwin], refs[2 * nwin]
    i = pl.program_id(2)
    hg = q_ref.shape[1]
    nk = nwin * tq
    k = jnp.concatenate([r[0, 0] for r in k_refs], axis=0)
    vt = jnp.concatenate([r[0, 0] for r in v_refs], axis=0).astype(F32).T.astype(BF16)
    sp = i * tq - WINDOW + lax.broadcasted_iota(I32, (nk, tq), 0)
    t = i * tq + lax.broadcasted_iota(I32, (nk, tq), 1)
    mask = (sp <= t) & (sp > t - WINDOW) & (sp >= 0)
    score = lambda h: lax.dot_general(k, q_ref[0, h], (((1,), (1,)), ((), ())), preferred_element_type=F32)
    st_next = score(0)
    held = None
    for h in range(hg):
        st = st_next
        if h + 1 < hg:
            st_next = score(h + 1)
        st = jnp.where(mask, st, NEG)
        m = jnp.max(st, axis=0, keepdims=True)
        p = jnp.where(mask, jnp.exp2(st - m), 0.0)
        l = jnp.sum(p, axis=0, keepdims=True)
        o = jnp.dot(vt, p.astype(BF16), preferred_element_type=F32) / l
        held = _store_head_pair(o_ref, h, o, held)


def _win_attn(q, k_w, v_w, s, tq=512):
    b, nh, _, dh = q.shape
    g, hg = N_KV_GROUPS, HEADS_PER_GROUP
    nwin = WINDOW // tq + 1
    pad = ((0, 0), (0, 0), (WINDOW, 0), (0, 0))
    kp, vp = jnp.pad(k_w, pad), jnp.pad(v_w, pad)
    kv_specs = [pl.BlockSpec((1, 1, tq, dh), functools.partial(lambda bi, gi, i, w: (bi, gi, i + w, 0), w=w))
                for w in range(nwin)]
    kern = functools.partial(_win_attn_kernel, tq=tq, nwin=nwin)
    return pl.pallas_call(
        kern, grid=(b, g, s // tq),
        in_specs=[pl.BlockSpec((1, hg, tq, dh), lambda bi, gi, i: (bi, gi, i, 0))] + kv_specs + kv_specs,
        out_specs=pl.BlockSpec((1, tq, hg * dh), lambda bi, gi, i: (bi, i, gi)),
        out_shape=jax.ShapeDtypeStruct((b, s, nh * dh), BF16),
        compiler_params=_params(("parallel", "parallel", "parallel")), name="nsa_win_attn",
    )(q, *([kp] * nwin), *([vp] * nwin))


def _causal_conv(x, tail, w_ref, b_ref):
    n = x.shape[0]
    xx = jnp.concatenate([tail, x], axis=0)
    out = b_ref[...] + w_ref[CONV_WIDTH - 1:CONV_WIDTH, :] * x
    for k in range(CONV_WIDTH - 1):
        off = 8 - (CONV_WIDTH - 1) + k
        out = out + w_ref[k:k + 1, :] * xx[off:off + n, :]
    return out


def _rglru_kernel(x_ref, gate_ref, pos_ref, cw_ref, cb_ref, wr_ref, br_ref, wi_ref, bi_ref, lam_ref,
                  o_ref, tail_ref, h_ref):
    @pl.when(pl.program_id(1) == 0)
    def _():
        tail_ref[...] = jnp.zeros(tail_ref.shape, F32)
        h_ref[...] = jnp.zeros(h_ref.shape, F32)

    n = x_ref.shape[0]
    x = x_ref[...].astype(F32)
    xc = _causal_conv(x, tail_ref[...], cw_ref, cb_ref)
    tail_ref[...] = x[n - 8:, :]
    xcb = xc.astype(BF16)
    r = _sigmoid(jnp.dot(xcb, wr_ref[...], preferred_element_type=F32) + br_ref[...])
    gi = _sigmoid(jnp.dot(xcb, wi_ref[...], preferred_element_type=F32) + bi_ref[...])
    log_a = -LRU_C * r * _softplus(-lam_ref[...])
    reset = pos_ref[...] == 0
    a = jnp.where(reset, 0.0, jnp.exp(log_a))
    mult = jnp.where(reset, 1.0, jnp.sqrt(jnp.maximum(1.0 - jnp.exp(2.0 * log_a), 0.0)))
    bb = mult * (gi * xc)
    sub = lax.broadcasted_iota(I32, a.shape, 0) % 8
    d = 1
    while d < 8:
        a_sh = pltpu.roll(a, d, 0)
        b_sh = pltpu.roll(bb, d, 0)
        live = sub >= d
        bb = jnp.where(live, a * b_sh + bb, bb)
        a = jnp.where(live, a * a_sh, a)
        d *= 2
    carry = h_ref[...]
    hs = []
    for g in range(n // 8):
        h = bb[8 * g:8 * g + 8, :] + a[8 * g:8 * g + 8, :] * carry
        carry = h[7:8, :]
        hs.append(h)
    h_ref[...] = carry
    o_ref[...] = (jnp.concatenate(hs, axis=0) * _gelu(gate_ref[...].astype(F32))).astype(o_ref.dtype)


def _block_diag(w):
    nb, bs, _ = w.shape
    eye = jnp.eye(nb, dtype=w.dtype)
    return (w[:, :, None, :] * eye[:, None, :, None]).reshape(nb * bs, nb * bs)


def _rglru(l_x, l_g, positions, conv_w, conv_b, w_r, b_r, w_i, b_i, lam, b, s, ts=256):
    d = l_x.shape[1]
    nt = s // ts
    row = lambda bi, i: (bi * nt + i, 0)
    c2 = lambda bi, i: (0, 0)
    vec = lambda v: v.reshape(1, d).astype(F32)
    return pl.pallas_call(
        _rglru_kernel, grid=(b, nt),
        in_specs=[pl.BlockSpec((ts, d), row), pl.BlockSpec((ts, d), row), pl.BlockSpec((ts, 1), row),
                  pl.BlockSpec((CONV_WIDTH, d), c2), pl.BlockSpec((1, d), c2),
                  pl.BlockSpec((d, d), c2), pl.BlockSpec((1, d), c2),
                  pl.BlockSpec((d, d), c2), pl.BlockSpec((1, d), c2), pl.BlockSpec((1, d), c2)],
        out_specs=pl.BlockSpec((ts, d), row),
        out_shape=jax.ShapeDtypeStruct((b * s, d), BF16),
        scratch_shapes=[pltpu.VMEM((8, d), F32), pltpu.VMEM((1, d), F32)],
        compiler_params=_params(("parallel", "arbitrary")), name="rglru",
    )(l_x, l_g, positions.reshape(b * s, 1).astype(I32), conv_w.astype(F32), vec(conv_b),
      _block_diag(w_r).astype(BF16), vec(b_r), _block_diag(w_i).astype(BF16), vec(b_i), vec(lam))


def _ssd_kernel(z_ref, xbc_ref, dt_ref, cw_ref, cb_ref, dtb_ref, alog_ref, dfull_ref, ng_ref,
                o_ref, tail_ref, state_ref, y_ref):
    @pl.when(pl.program_id(1) == 0)
    def _():
        tail_ref[...] = jnp.zeros(tail_ref.shape, F32)
        state_ref[...] = jnp.zeros(state_ref.shape, F32)

    n = xbc_ref.shape[0]
    hg = SSM_HEADS // SSM_GROUPS
    x = xbc_ref[...].astype(F32)
    xc = _causal_conv(x, tail_ref[...], cw_ref, cb_ref)
    tail_ref[...] = x[n - 8:, :]
    xc = xc * _sigmoid(xc)
    xs = xc[:, :SSM_INNER]
    dt = _softplus(dt_ref[...] + dtb_ref[...])
    adt = dt * (-jnp.exp(alog_ref[...]))
    row = lax.broadcasted_iota(I32, adt.shape, 0)
    acs = adt
    d = 1
    while d < n:
        acs = acs + jnp.where(row >= d, pltpu.roll(acs, d, 0), 0.0)
        d *= 2
    acs_t = acs.T
    li = lax.broadcasted_iota(I32, (n, n), 0)
    si = lax.broadcasted_iota(I32, (n, n), 1)
    tri = li >= si
    for g in range(SSM_GROUPS):
        bm = xc[:, SSM_INNER + g * SSM_STATE:SSM_INNER + (g + 1) * SSM_STATE].astype(BF16)
        cm = xc[:, SSM_INNER + (SSM_GROUPS + g) * SSM_STATE:SSM_INNER + (SSM_GROUPS + g + 1) * SSM_STATE].astype(BF16)
        cb = lax.dot_general(cm, bm, (((1,), (1,)), ((), ())), preferred_element_type=F32)
        bm_t = bm.T
        for hh in range(hg):
            h = g * hg + hh
            acol = jnp.broadcast_to(acs[:, h:h + 1], (n, n))
            arow = acs_t[h:h + 1, :]
            decay = jnp.exp(jnp.where(tri, acol - arow, NEG))
            acol_p = acol[:, :SSM_HEAD_DIM]
            xh = xs[:, h * SSM_HEAD_DIM:(h + 1) * SSM_HEAD_DIM] * dt[:, h:h + 1]
            a_last = acol_p[n - 1:n, :]
            y = jnp.dot((cb * decay).astype(BF16), xh.astype(BF16), preferred_element_type=F32)
            st = state_ref[h]
            y = y + jnp.dot(cm, st.astype(BF16), preferred_element_type=F32) * jnp.exp(acol_p)
            upd = jnp.dot(bm_t, (xh * jnp.exp(a_last - acol_p)).astype(BF16), preferred_element_type=F32)
            state_ref[h] = jnp.exp(a_last) * st + upd
            y_ref[:, h * SSM_HEAD_DIM:(h + 1) * SSM_HEAD_DIM] = y
    z = z_ref[...].astype(F32)
    y = (y_ref[...] + dfull_ref[...] * xs) * (z * _sigmoid(z))
    gw = SSM_INNER // SSM_GROUPS
    for g in range(SSM_GROUPS):
        yg = y[:, g * gw:(g + 1) * gw]
        ms = jnp.mean(yg * yg, axis=-1, keepdims=True)
        o_ref[:, g * gw:(g + 1) * gw] = (yg * lax.rsqrt(ms + EPS) * ng_ref[:, g * gw:(g + 1) * gw]).astype(o_ref.dtype)


def _ssd(s_z, s_xbc, s_dt, conv_w, conv_b, dt_bias, a_log, d_skip, norm_g, b, s):
    n = SSM_CHUNK
    nt = s // n
    c = s_xbc.shape[1]
    row = lambda bi, i: (bi * nt + i, 0)
    c2 = lambda bi, i: (0, 0)
    pad_h = lambda v: jnp.zeros((1, LANES), F32).at[0, :SSM_HEADS].set(v.astype(F32))
    dfull = jnp.repeat(d_skip.astype(F32), SSM_HEAD_DIM).reshape(1, SSM_INNER)
    return pl.pallas_call(
        _ssd_kernel, grid=(b, nt),
        in_specs=[pl.BlockSpec((n, SSM_INNER), row), pl.BlockSpec((n, c), row), pl.BlockSpec((n, LANES), row),
                  pl.BlockSpec((CONV_WIDTH, c), c2), pl.BlockSpec((1, c), c2),
                  pl.BlockSpec((1, LANES), c2), pl.BlockSpec((1, LANES), c2),
                  pl.BlockSpec((1, SSM_INNER), c2), pl.BlockSpec((1, SSM_INNER), c2)],
        out_specs=pl.BlockSpec((n, SSM_INNER), row),
        out_shape=jax.ShapeDtypeStruct((b * s, SSM_INNER), BF16),
        scratch_shapes=[pltpu.VMEM((8, c), F32), pltpu.VMEM((SSM_HEADS, SSM_STATE, SSM_HEAD_DIM), F32),
                        pltpu.VMEM((n, SSM_INNER), F32)],
        compiler_params=_params(("parallel", "arbitrary")), name="ssd",
    )(s_z, s_xbc, s_dt, conv_w.astype(F32), conv_b.reshape(1, c).astype(F32), pad_h(dt_bias), pad_h(a_log),
      dfull, norm_g.reshape(1, SSM_INNER).astype(F32))


def _merge_kernel(oc_ref, os_ref, ow_ref, ag_ref, ex_ref, yb_ref, yc_ref, mg_ref, x_ref, wb_ref, wo_ref, o_ref):
    gates = _sigmoid(ag_ref[...])
    hi = gates.astype(BF16)
    lo = (gates - hi.astype(F32)).astype(BF16)
    spread = lambda k: (jnp.dot(hi, ex_ref[k], preferred_element_type=F32)
                        + jnp.dot(lo, ex_ref[k], preferred_element_type=F32))
    ya = (spread(0) * oc_ref[...].astype(F32) + spread(1) * os_ref[...].astype(F32)
          + spread(2) * ow_ref[...].astype(F32))
    d = D_MODEL
    merged = _sigmoid(mg_ref[:, 0:d].astype(F32)) * jnp.dot(ya.astype(BF16), wb_ref[0], preferred_element_type=F32)
    merged += _sigmoid(mg_ref[:, d:2 * d].astype(F32)) * jnp.dot(yb_ref[...], wb_ref[1], preferred_element_type=F32)
    merged += _sigmoid(mg_ref[:, 2 * d:3 * d].astype(F32)) * jnp.dot(yc_ref[...], wb_ref[2], preferred_element_type=F32)
    o_ref[...] = x_ref[...] + jnp.dot(merged.astype(BF16), wo_ref[...], preferred_element_type=F32)


def _merge(o_c, o_s, o_w, a_g, y_b, y_c, m_g, x, w_branch, w_out, tm=256):
    t, d = x.shape
    row = lambda i: (i, 0)
    lane = jnp.arange(LANES)[:, None]
    col = jnp.arange(d)[None, :]
    expand = jnp.stack([(lane == 3 * (col // HEAD_DIM) + k) for k in range(3)]).astype(BF16)
    return pl.pallas_call(
        _merge_kernel, grid=(t // tm,),
        in_specs=[pl.BlockSpec((tm, d), row), pl.BlockSpec((tm, d), row), pl.BlockSpec((tm, d), row),
                  pl.BlockSpec((tm, LANES), row), pl.BlockSpec((3, LANES, d), lambda i: (0, 0, 0)),
                  pl.BlockSpec((tm, d), row), pl.BlockSpec((tm, d), row), pl.BlockSpec((tm, 3 * d), row),
                  pl.BlockSpec((tm, d), row),
                  pl.BlockSpec((3, d, d), lambda i: (0, 0, 0)), pl.BlockSpec((d, d), lambda i: (0, 0))],
        out_specs=pl.BlockSpec((tm, d), row),
        out_shape=jax.ShapeDtypeStruct((t, d), F32),
        compiler_params=_params(("parallel",)), name="merge",
    )(o_c.reshape(t, d), o_s.reshape(t, d), o_w.reshape(t, d), a_g, expand, y_b, y_c, m_g, x,
      w_branch.astype(BF16), w_out.astype(BF16))


def _topk_rows(s, k):
    n = s.shape[0]
    row = lax.broadcasted_iota(I32, s.shape, 0)
    vals, idxs = [], []
    for _ in range(k):
        m = jnp.max(s, axis=0, keepdims=True)
        idx = jnp.min(jnp.where(s == m, row, n), axis=0, keepdims=True)
        vals.append(m)
        idxs.append(idx)
        s = jnp.where(row == idx, -jnp.inf, s)
    return jnp.concatenate(vals, axis=0), jnp.concatenate(idxs, axis=0)


def _pick_rows(table, sel, k):
    out = jnp.zeros(sel.shape, table.dtype)
    for a in range(k):
        out = jnp.where(sel == a, table[a:a + 1, :], out)
    return out


def _peer_route_kernel(h_ref, wq_ref, keys_ref, i1_ref, i2_ref, g_ref):
    k = PEER_TOPK
    assert k == 16
    qt = lax.dot_general(wq_ref[...], h_ref[...], (((1,), (1,)), ((), ())), preferred_element_type=F32)
    qt = qt.astype(BF16)
    i1s, i2s, gs = [], [], []
    for hd in range(PEER_HEADS):
        tops = []
        for half in range(2):
            c = hd * 2 + half
            sc = jnp.dot(keys_ref[c], qt[c * PEER_HALF:(c + 1) * PEER_HALF, :], preferred_element_type=F32)
            tops.append(_topk_rows(sc, k))
        (s1, i1), (s2, i2) = tops
        cand = jnp.concatenate(
            [s1[0:1, :] + s2] + [s1[a:a + 1, :] + s2[0:8, :] for a in range(1, 8)] + [s1[8:16, :] + s2[0:1, :]],
            axis=0)
        sc, r = _topk_rows(cand, k)
        e = jnp.exp(sc - sc[0:1, :])
        gs.append(e / jnp.sum(e, axis=0, keepdims=True))
        a_sel = jnp.where(r < 16, 0, jnp.where(r < 72, 1 + (r - 16) // 8, r - 64))
        b_sel = jnp.where(r < 16, r, jnp.where(r < 72, (r - 16) % 8, 0))
        i1s.append(_pick_rows(i1, a_sel, k))
        i2s.append(_pick_rows(i2, b_sel, k))
    i1_ref[...] = jnp.concatenate(i1s, axis=0).T
    i2_ref[...] = jnp.concatenate(i2s, axis=0).T
    g_ref[...] = jnp.concatenate(gs, axis=0).T


def _peer_route(h, w_q, sub_keys, tm=256):
    t, d = h.shape
    nq = w_q.shape[1]
    wq_t = w_q.T.astype(BF16)
    keys = sub_keys.reshape(PEER_HEADS * 2, PEER_KEYS, PEER_HALF).astype(BF16)
    slots = PEER_HEADS * PEER_TOPK
    row = lambda i: (i, 0)
    return pl.pallas_call(
        _peer_route_kernel, grid=(t // tm,),
        in_specs=[pl.BlockSpec((tm, d), row), pl.BlockSpec((nq, d), lambda i: (0, 0)),
                  pl.BlockSpec((PEER_HEADS * 2, PEER_KEYS, PEER_HALF), lambda i: (0, 0, 0))],
        out_specs=[pl.BlockSpec((tm, slots), row)] * 3,
        out_shape=[jax.ShapeDtypeStruct((t, slots), I32), jax.ShapeDtypeStruct((t, slots), I32),
                   jax.ShapeDtypeStruct((t, slots), F32)],
        compiler_params=_params(("parallel",)), name="peer_route",
    )(h, wq_t, keys)


def _peer_u_kernel(h_ref, u_ref, i1_ref, i2_ref, o_ref, *, blocks):
    c = pl.program_id(0)
    a = lax.dot_general(h_ref[...], u_ref[...], (((1,), (1,)), ((), ())), preferred_element_type=F32)
    i1 = i1_ref[...]
    i2 = i2_ref[...]
    acc = jnp.zeros(i1.shape, F32)
    for bk in range(blocks):
        got = jnp.take_along_axis(a[:, bk * PEER_KEYS:(bk + 1) * PEER_KEYS], i2, axis=1)
        acc = jnp.where(i1 == c * blocks + bk, got, acc)
    o_ref[0] = acc


def _peer_u(h, u, i1, i2, tm=512, blocks=16):
    t, d = h.shape
    ne = u.shape[0]
    ec = blocks * PEER_KEYS
    nchunk = ne // ec
    slots = i1.shape[1]
    kern = functools.partial(_peer_u_kernel, blocks=blocks)
    return pl.pallas_call(
        kern, grid=(nchunk, t // tm),
        in_specs=[pl.BlockSpec((tm, d), lambda c, i: (i, 0)), pl.BlockSpec((ec, d), lambda c, i: (c, 0)),
                  pl.BlockSpec((tm, slots), lambda c, i: (i, 0)), pl.BlockSpec((tm, slots), lambda c, i: (i, 0))],
        out_specs=pl.BlockSpec((1, tm, slots), lambda c, i: (c, i, 0)),
        out_shape=jax.ShapeDtypeStruct((nchunk, t, slots), F32),
        compiler_params=_params(("parallel", "parallel")), name="peer_u",
    )(h, u, i1, i2)


def _peer_v_kernel(parts_ref, g_ref, i1_ref, i2_ref, v_ref, x_ref, o_ref, w_ref, wg_ref, *, tm, blocks):
    c = pl.program_id(1)
    nk = PEER_KEYS

    @pl.when(c == 0)
    def _():
        w_ref[...] = g_ref[...] * _gelu(jnp.sum(parts_ref[...], axis=0))
        o_ref[...] = x_ref[...]
        sub = lax.broadcasted_iota(I32, (nk, w_ref.shape[1]), 0)

        def per_token(t, carry):
            wrow = w_ref[pl.ds(t, 1), :]
            lhs = jnp.where(i1_ref[pl.ds(t, 1), :] == sub, wrow, 0.0).astype(BF16)
            rhs = jnp.where(i2_ref[pl.ds(t, 1), :] == sub, 1.0, 0.0).astype(BF16)
            grid = lax.dot_general(lhs, rhs, (((1,), (1,)), ((), ())), preferred_element_type=F32)
            wg_ref[pl.ds(pl.multiple_of(t * nk, nk), nk), :] = grid
            return carry

        lax.fori_loop(0, tm, per_token, 0, unroll=16)

    acc = jnp.zeros(o_ref.shape, F32)
    for bk in range(0, blocks, 2):
        i1 = c * blocks + bk
        lhs = jnp.concatenate([wg_ref[pl.ds(i1, tm, stride=nk), :], wg_ref[pl.ds(i1 + 1, tm, stride=nk), :]],
                              axis=1).astype(BF16)
        rhs = v_ref[bk:bk + 2].reshape(2 * nk, v_ref.shape[2])
        acc += jnp.dot(lhs, rhs, preferred_element_type=F32)
    o_ref[...] += acc


def _peer_v(parts, g, i1, i2, v, x, tm=256, blocks=32):
    t, d = x.shape
    nchunk_u = parts.shape[0]
    slots = g.shape[1]
    nk = PEER_KEYS
    v3 = v.reshape(nk, nk, d)
    kern = functools.partial(_peer_v_kernel, tm=tm, blocks=blocks)
    row = lambda i, c: (i, 0)
    return pl.pallas_call(
        kern, grid=(t // tm, nk // blocks),
        in_specs=[pl.BlockSpec((nchunk_u, tm, slots), lambda i, c: (0, i, 0)),
                  pl.BlockSpec((tm, slots), row), pl.BlockSpec((tm, slots), row), pl.BlockSpec((tm, slots), row),
                  pl.BlockSpec((blocks, nk, d), lambda i, c: (c, 0, 0)), pl.BlockSpec((tm, d), row)],
        out_specs=pl.BlockSpec((tm, d), row),
        out_shape=jax.ShapeDtypeStruct((t, d), F32),
        scratch_shapes=[pltpu.VMEM((tm, slots), F32), pltpu.VMEM((tm * nk, nk), F32)],
        compiler_params=_params(("parallel", "arbitrary")), name="peer_v",
    )(parts, g, i1, i2, v3, x)


IN_SIZES = (1024, 768, 48, 1024, 1024, 1024, 1536, 16, 3072)


def _pad_cols(w, n):
    return jnp.pad(w, ((0, 0), (0, n - w.shape[1])))


def _mixer(x, positions, b, s, p):
    xn = _rmsnorm(x, p["mix_norm_g"])
    offs = [0]
    for n in IN_SIZES:
        offs.append(offs[-1] + n)
    w = [p["w_in"][:, offs[i]:offs[i + 1]] for i in range(len(IN_SIZES))]
    a_q = _matmul(xn, w[0].astype(BF16), BF16, "proj_q")
    a_kv = _matmul(xn, w[1].astype(BF16), BF16, "proj_kv")
    a_g = _matmul(xn, _pad_cols(w[2], LANES).astype(BF16), F32, "proj_ag")
    l_x = _matmul(xn, w[3].astype(BF16), BF16, "proj_lx")
    l_g = _matmul(xn, w[4].astype(BF16), BF16, "proj_lg")
    s_z = _matmul(xn, w[5].astype(BF16), BF16, "proj_sz")
    s_xbc = _matmul(xn, w[6].astype(BF16), BF16, "proj_sxbc")
    s_dt = _matmul(xn, _pad_cols(w[7], LANES).astype(BF16), F32, "proj_sdt")
    m_g = _matmul(xn, w[8].astype(BF16), BF16, "proj_mg", tn=1024)

    q, kc, vc, ks, vs, kw, vw = _nsa_prep(a_q, a_kv, positions, p["q_norm_g"], p["k_norm_g"], b, s)
    k_cmp, v_cmp = _compress(kc, vc, p["cmp_pe_k"], p["cmp_pe_v"], p["cmp_k_w1"], p["cmp_k_w2"],
                             p["cmp_v_w1"], p["cmp_v_w2"], p["k_norm_g"])
    o_c, bias = _cmp_select(q, k_cmp, v_cmp, s)
    o_s = _sel_attn(q, bias, ks, vs, s)
    o_w = _win_attn(q, kw, vw, s)
    y_b = _rglru(l_x, l_g, positions, p["lru_conv_w"], p["lru_conv_b"], p["lru_w_r"], p["lru_b_r"],
                 p["lru_w_i"], p["lru_b_i"], p["lru_lambda"], b, s)
    y_c = _ssd(s_z, s_xbc, s_dt, p["ssm_conv_w"], p["ssm_conv_b"], p["ssm_dt_bias"], p["ssm_a_log"],
               p["ssm_d"], p["ssm_norm_g"], b, s)
    return _merge(o_c, o_s, o_w, a_g, y_b, y_c, m_g, x, p["w_branch"], p["w_out"])


def _peer(x, p):
    h = _rmsnorm(x, p["ffn_norm_g"])
    i1, i2, g = _peer_route(h, p["peer_w_q"], p["peer_sub_keys"])
    parts = _peer_u(h, p["peer_u"].astype(BF16), i1, i2)
    return _peer_v(parts, g, i1, i2, p["peer_v"].astype(BF16), x)


_LAYER_PARAMS = ("mix_norm_g", "w_in", "q_norm_g", "k_norm_g", "cmp_pe_k", "cmp_pe_v", "cmp_k_w1", "cmp_k_w2",
                 "cmp_v_w1", "cmp_v_w2", "lru_conv_w", "lru_conv_b", "lru_w_r", "lru_b_r", "lru_w_i", "lru_b_i",
                 "lru_lambda", "ssm_conv_w", "ssm_conv_b", "ssm_dt_bias", "ssm_a_log", "ssm_d", "ssm_norm_g",
                 "w_branch", "w_out", "ffn_norm_g", "peer_w_q", "peer_sub_keys", "peer_u", "peer_v")


def kernel(x, positions, mix_norm_g, w_in, q_norm_g, k_norm_g, cmp_pe_k, cmp_pe_v, cmp_k_w1, cmp_k_w2, cmp_v_w1, cmp_v_w2, lru_conv_w, lru_conv_b, lru_w_r, lru_b_r, lru_w_i, lru_b_i, lru_lambda, ssm_conv_w, ssm_conv_b, ssm_dt_bias, ssm_a_log, ssm_d, ssm_norm_g, w_branch, w_out, ffn_norm_g, peer_w_q, peer_sub_keys, peer_u, peer_v):
    stacked = dict(zip(_LAYER_PARAMS, (mix_norm_g, w_in, q_norm_g, k_norm_g, cmp_pe_k, cmp_pe_v, cmp_k_w1,
                                       cmp_k_w2, cmp_v_w1, cmp_v_w2, lru_conv_w, lru_conv_b, lru_w_r, lru_b_r,
                                       lru_w_i, lru_b_i, lru_lambda, ssm_conv_w, ssm_conv_b, ssm_dt_bias,
                                       ssm_a_log, ssm_d, ssm_norm_g, w_branch, w_out, ffn_norm_g, peer_w_q,
                                       peer_sub_keys, peer_u, peer_v)))
    b, s, d = x.shape
    xf = x.reshape(b * s, d).astype(F32)
    for layer in range(mix_norm_g.shape[0]):
        p = {name: arr[layer] for name, arr in stacked.items()}
        xf = _mixer(xf, positions, b, s, p)
        xf = _peer(xf, p)
    return xf.reshape(b, s, d).astype(x.dtype)
```

```python
import functools
import math

import jax
import jax.numpy as jnp
from jax import lax
from jax.experimental import pallas as pl
from jax.experimental.pallas import tpu as pltpu

F32 = jnp.float32
BF16 = jnp.bfloat16
I32 = jnp.int32

D_MODEL = 1024
HEAD_DIM = 64
N_HEADS = 16
N_KV_GROUPS = 2
HEADS_PER_GROUP = 8
CMP_BLOCK = 32
CMP_STRIDE = 16
CMP_HIDDEN = 256
SEL_BLOCK = 64
N_SELECT = 16
WINDOW = 512
SEL_FORCE = 100.0
ROPE_THETA = 10000.0
SCALE = HEAD_DIM ** -0.5
LOG2E = math.log2(math.e)
LRU_HEADS = 16
LRU_BLOCK = 64
LRU_C = 8.0
CONV_WIDTH = 4
SSM_HEADS = 16
SSM_HEAD_DIM = 64
SSM_GROUPS = 2
SSM_STATE = 128
SSM_CHUNK = 128
SSM_INNER = 1024
PEER_HEADS = 8
PEER_KEYS = 128
PEER_HALF = 128
PEER_TOPK = 16
EPS = 1e-6
NEG = -1e30
LANES = 128

VMEM_LIMIT = 56 * 1024 * 1024
SEL_LOOKAHEAD = 3
ROUTE_LANES = 128


def _params(sem):
    return pltpu.CompilerParams(dimension_semantics=sem, vmem_limit_bytes=VMEM_LIMIT)


def _gelu(x):
    return 0.5 * x * (1.0 + jnp.tanh(math.sqrt(2.0 / math.pi) * (x + 0.044715 * x * x * x)))


def _sigmoid(x):
    return 0.5 * jnp.tanh(0.5 * x) + 0.5


def _softplus(x):
    return jnp.maximum(x, 0.0) + jnp.log(1.0 + jnp.exp(-jnp.abs(x)))


def _rmsnorm_kernel(x_ref, g_ref, o_ref):
    x = x_ref[...].astype(F32)
    ms = jnp.mean(x * x, axis=-1, keepdims=True)
    o_ref[...] = (x * lax.rsqrt(ms + EPS) * g_ref[...]).astype(o_ref.dtype)


def _rmsnorm(x, g, tm=512):
    t, d = x.shape
    return pl.pallas_call(
        _rmsnorm_kernel, grid=(t // tm,),
        in_specs=[pl.BlockSpec((tm, d), lambda i: (i, 0)), pl.BlockSpec((1, d), lambda i: (0, 0))],
        out_specs=pl.BlockSpec((tm, d), lambda i: (i, 0)),
        out_shape=jax.ShapeDtypeStruct((t, d), BF16),
        compiler_params=_params(("parallel",)), name="rmsnorm")(x, g.reshape(1, d).astype(F32))


def _mm_kernel(x_ref, w_ref, o_ref):
    o_ref[...] = jnp.dot(x_ref[...], w_ref[...], preferred_element_type=F32).astype(o_ref.dtype)


def _matmul(x, w, out_dtype, name, tm=512, tn=None):
    t, k = x.shape
    n = w.shape[1]
    tn = n if tn is None else tn
    return pl.pallas_call(
        _mm_kernel, grid=(n // tn, t // tm),
        in_specs=[pl.BlockSpec((tm, k), lambda j, i: (i, 0)), pl.BlockSpec((k, tn), lambda j, i: (0, j))],
        out_specs=pl.BlockSpec((tm, tn), lambda j, i: (i, j)),
        out_shape=jax.ShapeDtypeStruct((t, n), out_dtype),
        compiler_params=_params(("parallel", "parallel")), name=name)(x, w)


def _norm_rope(x, g, cos, sin_signed, bd):
    ms = jnp.dot((x * x).astype(BF16), bd, preferred_element_type=F32)
    y = x * lax.rsqrt(ms + EPS) * g
    lane = lax.broadcasted_iota(I32, y.shape, 1)
    first_half = (lane % HEAD_DIM) < (HEAD_DIM // 2)
    partner = jnp.where(first_half, pltpu.roll(y, LANES - HEAD_DIM // 2, 1), pltpu.roll(y, HEAD_DIM // 2, 1))
    return y * cos + partner * sin_signed


def _nsa_prep_kernel(aq_ref, akv_ref, pos_ref, inv_ref, sgn_ref, gq_ref, gk_ref, bd_ref,
                     q_ref, kc_ref, vc_ref, ks_ref, vs_ref, kw_ref, vw_ref):
    ang = pos_ref[...].astype(F32) * inv_ref[...]
    cos = jnp.cos(ang)
    sin_signed = jnp.sin(ang) * sgn_ref[...]
    bd = bd_ref[...]
    for c in range(D_MODEL // LANES):
        x = aq_ref[:, c * LANES:(c + 1) * LANES].astype(F32)
        y = (_norm_rope(x, gq_ref[...], cos, sin_signed, bd) * (SCALE * LOG2E)).astype(q_ref.dtype)
        q_ref[0, 2 * c] = y[:, :HEAD_DIM]
        q_ref[0, 2 * c + 1] = y[:, HEAD_DIM:]
    outs = (kc_ref, vc_ref, ks_ref, vs_ref, kw_ref, vw_ref)
    for c, o_ref in enumerate(outs):
        x = akv_ref[:, c * LANES:(c + 1) * LANES]
        if c % 2 == 0:
            y = _norm_rope(x.astype(F32), gk_ref[...], cos, sin_signed, bd).astype(o_ref.dtype)
        else:
            y = x.astype(o_ref.dtype)
        o_ref[0, 0] = y[:, :HEAD_DIM]
        o_ref[0, 1] = y[:, HEAD_DIM:]


def _nsa_prep(a_q, a_kv, positions, q_norm_g, k_norm_g, b, s, tm=512):
    half = HEAD_DIM // 2
    lane = jnp.arange(LANES)
    inv = (ROPE_THETA ** (-((lane % half).astype(F32)) / half)).reshape(1, LANES)
    sgn = jnp.where((lane % HEAD_DIM) < half, -1.0, 1.0).astype(F32).reshape(1, LANES)
    bd = jnp.where((lane[:, None] // HEAD_DIM) == (lane[None, :] // HEAD_DIM), 1.0 / HEAD_DIM, 0.0).astype(BF16)
    gq = jnp.tile(q_norm_g.astype(F32), 2).reshape(1, LANES)
    gk = jnp.tile(k_norm_g.astype(F32), 2).reshape(1, LANES)
    nt = s // tm
    row = lambda i: (i, 0)
    const = lambda i: (0, 0)
    kv_shape = jax.ShapeDtypeStruct((b, N_KV_GROUPS, s, HEAD_DIM), BF16)
    kv_spec = pl.BlockSpec((1, N_KV_GROUPS, tm, HEAD_DIM), lambda i: (i // nt, 0, i % nt, 0))
    return pl.pallas_call(
        _nsa_prep_kernel, grid=(b * nt,),
        in_specs=[pl.BlockSpec((tm, D_MODEL), row), pl.BlockSpec((tm, 6 * LANES), row),
                  pl.BlockSpec((tm, 1), row), pl.BlockSpec((1, LANES), const), pl.BlockSpec((1, LANES), const),
                  pl.BlockSpec((1, LANES), const), pl.BlockSpec((1, LANES), const),
                  pl.BlockSpec((LANES, LANES), const)],
        out_specs=[pl.BlockSpec((1, N_HEADS, tm, HEAD_DIM), lambda i: (i // nt, 0, i % nt, 0))] + [kv_spec] * 6,
        out_shape=[jax.ShapeDtypeStruct((b, N_HEADS, s, HEAD_DIM), BF16)] + [kv_shape] * 6,
        compiler_params=_params(("parallel",)), name="nsa_prep",
    )(a_q, a_kv, positions.reshape(b * s, 1).astype(I32), inv, sgn, gq, gk, bd)


def _compress_kernel(uk_ref, uv_ref, pek_ref, pev_ref, kw1_ref, kw2_ref, vw1_ref, vw2_ref, gk_ref,
                     kc_ref, vc_ref):
    half = CMP_STRIDE * HEAD_DIM

    def mlp(u, pe, w1_ref, w2_ref):
        n = u.shape[0]
        ha = jnp.dot(u, w1_ref[:half, :], preferred_element_type=F32)
        hb = jnp.dot(u, w1_ref[half:, :], preferred_element_type=F32)
        bias = jnp.dot(pe, w1_ref[...], preferred_element_type=F32)[0:1, :]
        pre = ha + pltpu.roll(hb, n - 1, 0) + bias
        return jnp.dot(_gelu(pre).astype(BF16), w2_ref[...], preferred_element_type=F32)

    k = mlp(uk_ref[0, 0], pek_ref[...], kw1_ref, kw2_ref)
    ms = jnp.mean(k * k, axis=-1, keepdims=True)
    kc_ref[0, 0] = (k * lax.rsqrt(ms + EPS) * gk_ref[...]).astype(kc_ref.dtype)
    vc_ref[0, 0] = mlp(uv_ref[0, 0], pev_ref[...], vw1_ref, vw2_ref).astype(vc_ref.dtype)


def _compress(kc, vc, pe_k, pe_v, kw1, kw2, vw1, vw2, k_norm_g):
    b, g, s, dh = kc.shape
    ng = s // CMP_STRIDE
    wide = CMP_BLOCK * dh
    uk = kc.reshape(b, g, ng, CMP_STRIDE * dh)
    uv = vc.reshape(b, g, ng, CMP_STRIDE * dh)
    pek = jnp.zeros((8, wide), BF16).at[0].set(pe_k.reshape(wide).astype(BF16))
    pev = jnp.zeros((8, wide), BF16).at[0].set(pe_v.reshape(wide).astype(BF16))
    u_spec = pl.BlockSpec((1, 1, ng, CMP_STRIDE * dh), lambda i, j: (i, j, 0, 0))
    c2 = lambda i, j: (0, 0)
    o_spec = pl.BlockSpec((1, 1, ng, dh), lambda i, j: (i, j, 0, 0))
    o_shape = jax.ShapeDtypeStruct((b, g, ng, dh), BF16)
    return pl.pallas_call(
        _compress_kernel, grid=(b, g),
        in_specs=[u_spec, u_spec, pl.BlockSpec((8, wide), c2), pl.BlockSpec((8, wide), c2),
                  pl.BlockSpec((wide, CMP_HIDDEN), c2), pl.BlockSpec((CMP_HIDDEN, dh), c2),
                  pl.BlockSpec((wide, CMP_HIDDEN), c2), pl.BlockSpec((CMP_HIDDEN, dh), c2),
                  pl.BlockSpec((1, dh), c2)],
        out_specs=[o_spec, o_spec], out_shape=[o_shape, o_shape],
        compiler_params=_params(("parallel", "parallel")), name="nsa_compress",
    )(uk, uv, pek, pev, kw1.astype(BF16), kw2.astype(BF16), vw1.astype(BF16), vw2.astype(BF16),
      k_norm_g.reshape(1, dh).astype(F32))


def _store_head_pair(o_ref, h, o_t, held):
    if h % 2 == 0:
        return o_t
    pair = jnp.concatenate([held, o_t], axis=0).T
    o_ref[0, :, (h - 1) * HEAD_DIM:(h + 1) * HEAD_DIM] = pair.astype(o_ref.dtype)
    return None


def _cmp_select_kernel(q_ref, kc_ref, vc_ref, ovt_ref, o_ref, bias_ref, *, tq, n_cmp, n_pick):
    i = pl.program_id(2)
    hg = q_ref.shape[1]
    nc = kc_ref.shape[2]
    nsel = ovt_ref.shape[0]
    kc = kc_ref[0, 0]
    vct = vc_ref[0, 0].astype(F32).T.astype(BF16)
    c = lax.broadcasted_iota(I32, (nc, tq), 0)
    t = i * tq + lax.broadcasted_iota(I32, (nc, tq), 1)
    mask = ((CMP_STRIDE * c + CMP_BLOCK - 1) <= t) & (c < n_cmp)
    score = lambda h: lax.dot_general(kc, q_ref[0, h], (((1,), (1,)), ((), ())), preferred_element_type=F32)
    st_next = score(0)
    psum = jnp.zeros((nc, tq), F32)
    held = None
    for h in range(hg):
        st = st_next
        if h + 1 < hg:
            st_next = score(h + 1)
        st = jnp.where(mask, st, NEG)
        m = jnp.max(st, axis=0, keepdims=True)
        p = jnp.where(mask, jnp.exp2(st - m), 0.0)
        l = jnp.sum(p, axis=0, keepdims=True)
        p = p * jnp.where(l > 0.0, 1.0 / l, 0.0)
        o = jnp.dot(vct, p.astype(BF16), preferred_element_type=F32)
        held = _store_head_pair(o_ref, h, o, held)
        psum = psum + p

    hi = psum.astype(BF16)
    lo = (psum - hi.astype(F32)).astype(BF16)
    imp = (jnp.dot(ovt_ref[...], hi, preferred_element_type=F32)
           + jnp.dot(ovt_ref[...], lo, preferred_element_type=F32))
    j = lax.broadcasted_iota(I32, (nsel, tq), 0)
    tt = i * tq + lax.broadcasted_iota(I32, (nsel, tq), 1)
    cur = tt // SEL_BLOCK
    forced = (j == 0) | (j == cur) | (j == cur - 1)
    sc = jnp.where(forced, SEL_FORCE, jnp.where(j * SEL_BLOCK <= tt, imp, -1.0))
    picked = jnp.zeros((nsel, tq), jnp.bool_)
    for _ in range(n_pick):
        mx = jnp.max(sc, axis=0, keepdims=True)
        idx = jnp.min(jnp.where(sc == mx, j, nsel), axis=0, keepdims=True)
        hit = j == idx
        picked = picked | hit
        sc = jnp.where(hit, -jnp.inf, sc)
    bias_ref[0, 0] = jnp.where(picked & (j <= cur), 0.0, NEG).T.astype(bias_ref.dtype)


def _cmp_select(q, k_cmp, v_cmp, s, tq=512):
    b, nh, _, dh = q.shape
    g, hg = N_KV_GROUPS, HEADS_PER_GROUP
    nc = k_cmp.shape[2]
    n_cmp = (s - CMP_BLOCK) // CMP_STRIDE + 1
    nsel = s // SEL_BLOCK
    n_pick = min(N_SELECT, nsel)
    cs = CMP_STRIDE * jnp.arange(nc)
    ss = SEL_BLOCK * jnp.arange(nsel)
    ovt = jnp.clip(jnp.minimum(cs[None, :] + CMP_BLOCK, ss[:, None] + SEL_BLOCK)
                   - jnp.maximum(cs[None, :], ss[:, None]), 0).astype(F32) / CMP_BLOCK
    ovt = jnp.where(jnp.arange(nc)[None, :] < n_cmp, ovt, 0.0).astype(BF16)
    kern = functools.partial(_cmp_select_kernel, tq=tq, n_cmp=n_cmp, n_pick=n_pick)
    return pl.pallas_call(
        kern, grid=(b, g, s // tq),
        in_specs=[pl.BlockSpec((1, hg, tq, dh), lambda bi, gi, i: (bi, gi, i, 0)),
                  pl.BlockSpec((1, 1, nc, dh), lambda bi, gi, i: (bi, gi, 0, 0)),
                  pl.BlockSpec((1, 1, nc, dh), lambda bi, gi, i: (bi, gi, 0, 0)),
                  pl.BlockSpec((nsel, nc), lambda bi, gi, i: (0, 0))],
        out_specs=[pl.BlockSpec((1, tq, hg * dh), lambda bi, gi, i: (bi, i, gi)),
                   pl.BlockSpec((1, 1, tq, nsel), lambda bi, gi, i: (bi, gi, i, 0))],
        out_shape=[jax.ShapeDtypeStruct((b, s, nh * dh), BF16),
                   jax.ShapeDtypeStruct((b, g, s, nsel), BF16)],
        compiler_params=_params(("parallel", "parallel", "parallel")), name="nsa_cmp_select",
    )(q, k_cmp, v_cmp, ovt)


def _sel_attn_kernel(it_ref, jt_ref, q_ref, bias_ref, k_ref, v_ref, o_ref, qa_ref, m_ref, acc_ref, *, tq, tk):
    pid = pl.program_id(2)
    i = it_ref[pid]
    j = jt_ref[pid]
    hg = q_ref.shape[1]
    nsel = bias_ref.shape[3]
    last_j = ((i + 1) * tq - 1) // tk

    @pl.when(j == 0)
    def _():
        for h in range(hg):
            qa_ref[h, :, :nsel] = bias_ref[0, 0]
            qa_ref[h, :, nsel:] = q_ref[0, h]
        m_ref[...] = jnp.full(m_ref.shape, -jnp.inf, F32)
        acc_ref[...] = jnp.zeros(acc_ref.shape, F32)

    def accumulate(diagonal):
        key = j * tk + lax.broadcasted_iota(I32, (tk, nsel), 0)
        blk = lax.broadcasted_iota(I32, (tk, nsel), 1)
        onehot = jnp.where(key // SEL_BLOCK == blk, 1.0, 0.0).astype(BF16)
        ka = jnp.concatenate([onehot, k_ref[0, 0]], axis=1)
        va = jnp.concatenate([v_ref[0, 0].astype(F32), jnp.ones((tk, LANES - HEAD_DIM), F32)], axis=1)
        vat = va.T.astype(BF16)
        if diagonal:
            kp = j * tk + lax.broadcasted_iota(I32, (tk, tq), 0)
            t = i * tq + lax.broadcasted_iota(I32, (tk, tq), 1)
            causal = kp <= t
        score = lambda h: lax.dot_general(ka, qa_ref[h], (((1,), (1,)), ((), ())), preferred_element_type=F32)
        ahead = [score(h) for h in range(SEL_LOOKAHEAD)]
        for h in range(hg):
            st = ahead.pop(0)
            if h + SEL_LOOKAHEAD < hg:
                ahead.append(score(h + SEL_LOOKAHEAD))
            if diagonal:
                st = jnp.where(causal, st, NEG)
            m_old = m_ref[h]
            m_new = jnp.maximum(m_old, jnp.max(st, axis=0, keepdims=True))
            alpha = jnp.exp2(m_old - m_new)
            p = jnp.exp2(st - m_new)
            acc_ref[h] = alpha * acc_ref[h] + jnp.dot(vat, p.astype(BF16), preferred_element_type=F32)
            m_ref[h] = m_new

    @pl.when(j < last_j)
    def _():
        accumulate(False)

    @pl.when(j == last_j)
    def _():
        accumulate(True)
        held = None
        for h in range(hg):
            acc = acc_ref[h]
            o = acc[:HEAD_DIM, :] / acc[HEAD_DIM:HEAD_DIM + 1, :]
            held = _store_head_pair(o_ref, h, o, held)


def _sel_attn(q, bias, k_s, v_s, s, tq=512, tk=512):
    assert tq <= tk
    b, nh, _, dh = q.shape
    g, hg = N_KV_GROUPS, HEADS_PER_GROUP
    nsel = bias.shape[3]
    tk = min(tk, s)
    pairs = [(i, j) for i in range(s // tq) for j in range(((i + 1) * tq - 1) // tk + 1)]
    it = jnp.asarray([pr[0] for pr in pairs], I32)
    jt = jnp.asarray([pr[1] for pr in pairs], I32)
    q_map = lambda bi, gi, pid, it_ref, jt_ref: (bi, gi, it_ref[pid], 0)
    kv_map = lambda bi, gi, pid, it_ref, jt_ref: (bi, gi, jt_ref[pid], 0)
    kern = functools.partial(_sel_attn_kernel, tq=tq, tk=tk)
    grid_spec = pltpu.PrefetchScalarGridSpec(
        num_scalar_prefetch=2, grid=(b, g, len(pairs)),
        in_specs=[pl.BlockSpec((1, hg, tq, dh), q_map), pl.BlockSpec((1, 1, tq, nsel), q_map),
                  pl.BlockSpec((1, 1, tk, dh), kv_map), pl.BlockSpec((1, 1, tk, dh), kv_map)],
        out_specs=pl.BlockSpec((1, tq, hg * dh), lambda bi, gi, pid, it_ref, jt_ref: (bi, it_ref[pid], gi)),
        scratch_shapes=[pltpu.VMEM((hg, tq, nsel + dh), BF16), pltpu.VMEM((hg, 1, tq), F32),
                        pltpu.VMEM((hg, LANES, tq), F32)])
    return pl.pallas_call(
        kern, grid_spec=grid_spec, out_shape=jax.ShapeDtypeStruct((b, s, nh * dh), BF16),
        compiler_params=_params(("parallel", "parallel", "arbitrary")), name="nsa_sel_attn",
    )(it, jt, q, bias, k_s, v_s)


def _win_attn_kernel(q_ref, *refs, tq, nwin):
    k_refs, v_refs, o_ref = refs[:nwin], refs[nwin:2 * nwin], refs[2 * nwin]
    i = pl.program_id(2)
    hg = q_ref.shape[1]
    nk = nwin * tq
    k = jnp.concatenate([r[0, 0] for r in k_refs], axis=0)
    vt = jnp.concatenate([r[0, 0] for r in v_refs], axis=0).astype(F32).T.astype(BF16)
    sp = i * tq - WINDOW + lax.broadcasted_iota(I32, (nk, tq), 0)
    t = i * tq + lax.broadcasted_iota(I32, (nk, tq), 1)
    mask = (sp <= t) & (sp > t - WINDOW) & (sp >= 0)
    score = lambda h: lax.dot_general(k, q_ref[0, h], (((1,), (1,)), ((), ())), preferred_element_type=F32)
    st_next = score(0)
    held = None
    for h in range(hg):
        st = st_next
        if h + 1 < hg:
            st_next = score(h + 1)
        st = jnp.where(mask, st, NEG)
        m = jnp.max(st, axis=0, keepdims=True)
        p = jnp.where(mask, jnp.exp2(st - m), 0.0)
        l = jnp.sum(p, axis=0, keepdims=True)
        o = jnp.dot(vt, p.astype(BF16), preferred_element_type=F32) / l
        held = _store_head_pair(o_ref, h, o, held)


def _win_attn(q, k_w, v_w, s, tq=512):
    b, nh, _, dh = q.shape
    g, hg = N_KV_GROUPS, HEADS_PER_GROUP
    nwin = WINDOW // tq + 1
    pad = ((0, 0), (0, 0), (WINDOW, 0), (0, 0))
    kp, vp = jnp.pad(k_w, pad), jnp.pad(v_w, pad)
    kv_specs = [pl.BlockSpec((1, 1, tq, dh), functools.partial(lambda bi, gi, i, w: (bi, gi, i + w, 0), w=w))
                for w in range(nwin)]
    kern = functools.partial(_win_attn_kernel, tq=tq, nwin=nwin)
    return pl.pallas_call(
        kern, grid=(b, g, s // tq),
        in_specs=[pl.BlockSpec((1, hg, tq, dh), lambda bi, gi, i: (bi, gi, i, 0))] + kv_specs + kv_specs,
        out_specs=pl.BlockSpec((1, tq, hg * dh), lambda bi, gi, i: (bi, i, gi)),
        out_shape=jax.ShapeDtypeStruct((b, s, nh * dh), BF16),
        compiler_params=_params(("parallel", "parallel", "parallel")), name="nsa_win_attn",
    )(q, *([kp] * nwin), *([vp] * nwin))


def _causal_conv(x, tail, w_ref, b_ref):
    n = x.shape[0]
    xx = jnp.concatenate([tail, x], axis=0)
    out = b_ref[...] + w_ref[CONV_WIDTH - 1:CONV_WIDTH, :] * x
    for k in range(CONV_WIDTH - 1):
        off = 8 - (CONV_WIDTH - 1) + k
        out = out + w_ref[k:k + 1, :] * xx[off:off + n, :]
    return out


def _rglru_kernel(x_ref, gate_ref, pos_ref, cw_ref, cb_ref, wr_ref, br_ref, wi_ref, bi_ref, lam_ref,
                  o_ref, tail_ref, h_ref):
    @pl.when(pl.program_id(1) == 0)
    def _():
        tail_ref[...] = jnp.zeros(tail_ref.shape, F32)
        h_ref[...] = jnp.zeros(h_ref.shape, F32)

    n = x_ref.shape[0]
    x = x_ref[...].astype(F32)
    xc = _causal_conv(x, tail_ref[...], cw_ref, cb_ref)
    tail_ref[...] = x[n - 8:, :]
    xcb = xc.astype(BF16)
    r = _sigmoid(jnp.dot(xcb, wr_ref[...], preferred_element_type=F32) + br_ref[...])
    gi = _sigmoid(jnp.dot(xcb, wi_ref[...], preferred_element_type=F32) + bi_ref[...])
    log_a = -LRU_C * r * _softplus(-lam_ref[...])
    reset = pos_ref[...] == 0
    a = jnp.where(reset, 0.0, jnp.exp(log_a))
    mult = jnp.where(reset, 1.0, jnp.sqrt(jnp.maximum(1.0 - jnp.exp(2.0 * log_a), 0.0)))
    bb = mult * (gi * xc)
    sub = lax.broadcasted_iota(I32, a.shape, 0) % 8
    d = 1
    while d < 8:
        a_sh = pltpu.roll(a, d, 0)
        b_sh = pltpu.roll(bb, d, 0)
        live = sub >= d
        bb = jnp.where(live, a * b_sh + bb, bb)
        a = jnp.where(live, a * a_sh, a)
        d *= 2
    carry = h_ref[...]
    hs = []
    for g in range(n // 8):
        h = bb[8 * g:8 * g + 8, :] + a[8 * g:8 * g + 8, :] * carry
        carry = h[7:8, :]
        hs.append(h)
    h_ref[...] = carry
    o_ref[...] = (jnp.concatenate(hs, axis=0) * _gelu(gate_ref[...].astype(F32))).astype(o_ref.dtype)


def _block_diag(w):
    nb, bs, _ = w.shape
    eye = jnp.eye(nb, dtype=w.dtype)
    return (w[:, :, None, :] * eye[:, None, :, None]).reshape(nb * bs, nb * bs)


def _rglru(l_x, l_g, positions, conv_w, conv_b, w_r, b_r, w_i, b_i, lam, b, s, ts=256):
    d = l_x.shape[1]
    nt = s // ts
    row = lambda bi, i: (bi * nt + i, 0)
    c2 = lambda bi, i: (0, 0)
    vec = lambda v: v.reshape(1, d).astype(F32)
    return pl.pallas_call(
        _rglru_kernel, grid=(b, nt),
        in_specs=[pl.BlockSpec((ts, d), row), pl.BlockSpec((ts, d), row), pl.BlockSpec((ts, 1), row),
                  pl.BlockSpec((CONV_WIDTH, d), c2), pl.BlockSpec((1, d), c2),
                  pl.BlockSpec((d, d), c2), pl.BlockSpec((1, d), c2),
                  pl.BlockSpec((d, d), c2), pl.BlockSpec((1, d), c2), pl.BlockSpec((1, d), c2)],
        out_specs=pl.BlockSpec((ts, d), row),
        out_shape=jax.ShapeDtypeStruct((b * s, d), BF16),
        scratch_shapes=[pltpu.VMEM((8, d), F32), pltpu.VMEM((1, d), F32)],
        compiler_params=_params(("parallel", "arbitrary")), name="rglru",
    )(l_x, l_g, positions.reshape(b * s, 1).astype(I32), conv_w.astype(F32), vec(conv_b),
      _block_diag(w_r).astype(BF16), vec(b_r), _block_diag(w_i).astype(BF16), vec(b_i), vec(lam))


def _ssd_kernel(z_ref, xbc_ref, dt_ref, cw_ref, cb_ref, dtb_ref, alog_ref, dfull_ref, ng_ref,
                o_ref, tail_ref, state_ref, y_ref):
    @pl.when(pl.program_id(1) == 0)
    def _():
        tail_ref[...] = jnp.zeros(tail_ref.shape, F32)
        state_ref[...] = jnp.zeros(state_ref.shape, F32)

    n = xbc_ref.shape[0]
    hg = SSM_HEADS // SSM_GROUPS
    x = xbc_ref[...].astype(F32)
    xc = _causal_conv(x, tail_ref[...], cw_ref, cb_ref)
    tail_ref[...] = x[n - 8:, :]
    xc = xc * _sigmoid(xc)
    xs = xc[:, :SSM_INNER]
    dt = _softplus(dt_ref[...] + dtb_ref[...])
    adt = dt * (-jnp.exp(alog_ref[...]))
    row = lax.broadcasted_iota(I32, adt.shape, 0)
    acs = adt
    d = 1
    while d < n:
        acs = acs + jnp.where(row >= d, pltpu.roll(acs, d, 0), 0.0)
        d *= 2
    acs_t = acs.T
    li = lax.broadcasted_iota(I32, (n, n), 0)
    si = lax.broadcasted_iota(I32, (n, n), 1)
    tri = li >= si
    for g in range(SSM_GROUPS):
        bm = xc[:, SSM_INNER + g * SSM_STATE:SSM_INNER + (g + 1) * SSM_STATE].astype(BF16)
        cm = xc[:, SSM_INNER + (SSM_GROUPS + g) * SSM_STATE:SSM_INNER + (SSM_GROUPS + g + 1) * SSM_STATE].astype(BF16)
        cb = lax.dot_general(cm, bm, (((1,), (1,)), ((), ())), preferred_element_type=F32)
        bm_t = bm.T
        for hh in range(hg):
            h = g * hg + hh
            acol = jnp.broadcast_to(acs[:, h:h + 1], (n, n))
            arow = acs_t[h:h + 1, :]
            decay = jnp.exp(jnp.where(tri, acol - arow, NEG))
            acol_p = acol[:, :SSM_HEAD_DIM]
            xh = xs[:, h * SSM_HEAD_DIM:(h + 1) * SSM_HEAD_DIM] * dt[:, h:h + 1]
            a_last = acol_p[n - 1:n, :]
            y = jnp.dot((cb * decay).astype(BF16), xh.astype(BF16), preferred_element_type=F32)
            st = state_ref[h]
            y = y + jnp.dot(cm, st.astype(BF16), preferred_element_type=F32) * jnp.exp(acol_p)
            upd = jnp.dot(bm_t, (xh * jnp.exp(a_last - acol_p)).astype(BF16), preferred_element_type=F32)
            state_ref[h] = jnp.exp(a_last) * st + upd
            y_ref[:, h * SSM_HEAD_DIM:(h + 1) * SSM_HEAD_DIM] = y
    z = z_ref[...].astype(F32)
    y = (y_ref[...] + dfull_ref[...] * xs) * (z * _sigmoid(z))
    gw = SSM_INNER // SSM_GROUPS
    for g in range(SSM_GROUPS):
        yg = y[:, g * gw:(g + 1) * gw]
        ms = jnp.mean(yg * yg, axis=-1, keepdims=True)
        o_ref[:, g * gw:(g + 1) * gw] = (yg * lax.rsqrt(ms + EPS) * ng_ref[:, g * gw:(g + 1) * gw]).astype(o_ref.dtype)


def _ssd(s_z, s_xbc, s_dt, conv_w, conv_b, dt_bias, a_log, d_skip, norm_g, b, s):
    n = SSM_CHUNK
    nt = s // n
    c = s_xbc.shape[1]
    row = lambda bi, i: (bi * nt + i, 0)
    c2 = lambda bi, i: (0, 0)
    pad_h = lambda v: jnp.zeros((1, LANES), F32).at[0, :SSM_HEADS].set(v.astype(F32))
    dfull = jnp.repeat(d_skip.astype(F32), SSM_HEAD_DIM).reshape(1, SSM_INNER)
    return pl.pallas_call(
        _ssd_kernel, grid=(b, nt),
        in_specs=[pl.BlockSpec((n, SSM_INNER), row), pl.BlockSpec((n, c), row), pl.BlockSpec((n, LANES), row),
                  pl.BlockSpec((CONV_WIDTH, c), c2), pl.BlockSpec((1, c), c2),
                  pl.BlockSpec((1, LANES), c2), pl.BlockSpec((1, LANES), c2),
                  pl.BlockSpec((1, SSM_INNER), c2), pl.BlockSpec((1, SSM_INNER), c2)],
        out_specs=pl.BlockSpec((n, SSM_INNER), row),
        out_shape=jax.ShapeDtypeStruct((b * s, SSM_INNER), BF16),
        scratch_shapes=[pltpu.VMEM((8, c), F32), pltpu.VMEM((SSM_HEADS, SSM_STATE, SSM_HEAD_DIM), F32),
                        pltpu.VMEM((n, SSM_INNER), F32)],
        compiler_params=_params(("parallel", "arbitrary")), name="ssd",
    )(s_z, s_xbc, s_dt, conv_w.astype(F32), conv_b.reshape(1, c).astype(F32), pad_h(dt_bias), pad_h(a_log),
      dfull, norm_g.reshape(1, SSM_INNER).astype(F32))


def _merge_kernel(oc_ref, os_ref, ow_ref, ag_ref, ex_ref, yb_ref, yc_ref, mg_ref, x_ref, wb_ref, wo_ref, o_ref):
    gates = _sigmoid(ag_ref[...])
    hi = gates.astype(BF16)
    lo = (gates - hi.astype(F32)).astype(BF16)
    spread = lambda k: (jnp.dot(hi, ex_ref[k], preferred_element_type=F32)
                        + jnp.dot(lo, ex_ref[k], preferred_element_type=F32))
    ya = (spread(0) * oc_ref[...].astype(F32) + spread(1) * os_ref[...].astype(F32)
          + spread(2) * ow_ref[...].astype(F32))
    d = D_MODEL
    merged = _sigmoid(mg_ref[:, 0:d].astype(F32)) * jnp.dot(ya.astype(BF16), wb_ref[0], preferred_element_type=F32)
    merged += _sigmoid(mg_ref[:, d:2 * d].astype(F32)) * jnp.dot(yb_ref[...], wb_ref[1], preferred_element_type=F32)
    merged += _sigmoid(mg_ref[:, 2 * d:3 * d].astype(F32)) * jnp.dot(yc_ref[...], wb_ref[2], preferred_element_type=F32)
    o_ref[...] = x_ref[...] + jnp.dot(merged.astype(BF16), wo_ref[...], preferred_element_type=F32)


def _merge(o_c, o_s, o_w, a_g, y_b, y_c, m_g, x, w_branch, w_out, tm=256):
    t, d = x.shape
    row = lambda i: (i, 0)
    lane = jnp.arange(LANES)[:, None]
    col = jnp.arange(d)[None, :]
    expand = jnp.stack([(lane == 3 * (col // HEAD_DIM) + k) for k in range(3)]).astype(BF16)
    return pl.pallas_call(
        _merge_kernel, grid=(t // tm,),
        in_specs=[pl.BlockSpec((tm, d), row), pl.BlockSpec((tm, d), row), pl.BlockSpec((tm, d), row),
                  pl.BlockSpec((tm, LANES), row), pl.BlockSpec((3, LANES, d), lambda i: (0, 0, 0)),
                  pl.BlockSpec((tm, d), row), pl.BlockSpec((tm, d), row), pl.BlockSpec((tm, 3 * d), row),
                  pl.BlockSpec((tm, d), row),
                  pl.BlockSpec((3, d, d), lambda i: (0, 0, 0)), pl.BlockSpec((d, d), lambda i: (0, 0))],
        out_specs=pl.BlockSpec((tm, d), row),
        out_shape=jax.ShapeDtypeStruct((t, d), F32),
        compiler_params=_params(("parallel",)), name="merge",
    )(o_c.reshape(t, d), o_s.reshape(t, d), o_w.reshape(t, d), a_g, expand, y_b, y_c, m_g, x,
      w_branch.astype(BF16), w_out.astype(BF16))


class _Interleaver:
    def __init__(self, pieces, every):
        self.pieces, self.every, self.count = list(pieces), every, 0

    def tick(self):
        self.count += 1
        if self.pieces and self.count % self.every == 0:
            self.pieces.pop(0)()

    def drain(self):
        while self.pieces:
            self.pieces.pop(0)()


def _sorting_network(n):
    pairs = []

    def merge(lo, m, r):
        step = r * 2
        if step < m:
            merge(lo, m, step)
            merge(lo + r, m, step)
            pairs.extend((i, i + r) for i in range(lo + r, lo + m - r, step))
        else:
            pairs.append((lo, lo + r))

    def sort(lo, m):
        if m > 1:
            sort(lo, m // 2)
            sort(lo + m // 2, m // 2)
            merge(lo, m, 1)

    sort(0, n)
    return pairs


def _topk_rows(s, k, tick):
    n, lanes = s.shape
    assert n == 8 * k
    sub = lax.broadcasted_iota(I32, (8, lanes), 0)
    vals = [s[8 * j:8 * j + 8, :] for j in range(k)]
    ids = [sub + 8 * j for j in range(k)]
    for count, (a, b) in enumerate(_sorting_network(k)):
        first = (vals[a] > vals[b]) | ((vals[a] == vals[b]) & (ids[a] < ids[b]))
        vals[a], vals[b] = jnp.where(first, vals[a], vals[b]), jnp.where(first, vals[b], vals[a])
        ids[a], ids[b] = jnp.where(first, ids[a], ids[b]), jnp.where(first, ids[b], ids[a])
        if count % 16 == 15:
            tick()
    top_v, top_i = [], []
    for r in range(k):
        best = jnp.max(vals[0], axis=0, keepdims=True)
        row = jnp.min(jnp.where(vals[0] == best, ids[0], n), axis=0, keepdims=True)
        top_v.append(best)
        top_i.append(row)
        won = ids[0] == row
        for d in range(k - 1 - r):
            vals[d] = jnp.where(won, vals[d + 1], vals[d])
            ids[d] = jnp.where(won, ids[d + 1], ids[d])
        tick()
    return jnp.concatenate(top_v, axis=0), jnp.concatenate(top_i, axis=0)


def _top_pairs(s1, s2, k, tick):
    lanes = s1.shape[1]
    sub = lax.broadcasted_iota(I32, (8, lanes), 0)
    lists = [jnp.where(sub <= k // (d + 1) - 1, s1[0:8, :] + s2[d:d + 1, :], -jnp.inf) for d in range(k)]
    tail = s1[8:16, :] + s2[0:1, :]
    tail_pos = (sub + 8) * k
    taken = jnp.zeros((8, lanes), I32)
    vs, aa, bb = [], [], []
    for r in range(k):
        best = jnp.maximum(jnp.max(lists[0], axis=0, keepdims=True), jnp.max(tail, axis=0, keepdims=True))
        head_pos = sub * k + taken
        pos = jnp.minimum(jnp.min(jnp.where(lists[0] == best, head_pos, k * k), axis=0, keepdims=True),
                          jnp.min(jnp.where(tail == best, tail_pos, k * k), axis=0, keepdims=True))
        vs.append(best)
        aa.append(pos // k)
        bb.append(pos % k)
        won = head_pos == pos
        tail = jnp.where(tail_pos == pos, -jnp.inf, tail)
        for d in range(k - 1 - r):
            lists[d] = jnp.where(won, lists[d + 1], lists[d])
        taken = taken + won.astype(I32)
        tick()
    return jnp.concatenate(vs, axis=0), jnp.concatenate(aa, axis=0), jnp.concatenate(bb, axis=0)


def _pick_rows(table, sel, k):
    out = jnp.zeros(sel.shape, table.dtype)
    for a in range(k):
        out = jnp.where(sel == a, table[a:a + 1, :], out)
    return out


def _route_head(qt, keys_ref, tick):
    k = PEER_TOPK
    assert k == 16 and PEER_KEYS == 8 * k
    tops = []
    for half in range(2):
        sc = jnp.dot(keys_ref[half], qt[half * PEER_HALF:(half + 1) * PEER_HALF, :], preferred_element_type=F32)
        tops.append(_topk_rows(sc, k, tick))
    (s1, i1), (s2, i2) = tops
    sc, a_sel, b_sel = _top_pairs(s1, s2, k, tick)
    e = jnp.exp(sc - sc[0:1, :])
    g = e / jnp.sum(e, axis=0, keepdims=True)
    return _pick_rows(i1, a_sel, k), _pick_rows(i2, b_sel, k), g


def _peer_route_u_kernel(hn_ref, hc_ref, wq_ref, keys_ref, u_ref, i1_ref, i2_ref, g_ref, act_ref,
                         i1t_ref, i2t_ref, gt_ref, i1c_ref, i2c_ref, *, blocks):
    o = pl.program_id(0)
    c = pl.program_id(1)
    k = PEER_TOPK

    @pl.when((o == 0) & (c == 0))
    def _():
        i1c_ref[...] = jnp.zeros(i1c_ref.shape, I32)
        i2c_ref[...] = jnp.zeros(i2c_ref.shape, I32)

    @pl.when(c == 0)
    def _():
        act_ref[...] = jnp.zeros(act_ref.shape, F32)

    i1c = i1c_ref[...]
    i2c = i2c_ref[...]
    hc = hc_ref[...]
    acc = [act_ref[...]]

    def piece(pc):
        def run():
            a = lax.dot_general(hc, u_ref[pl.ds(pc * 2 * PEER_KEYS, 2 * PEER_KEYS), :], (((1,), (1,)), ((), ())),
                                preferred_element_type=F32)
            for sub in range(2):
                got = jnp.take_along_axis(a[:, sub * PEER_KEYS:(sub + 1) * PEER_KEYS], i2c, axis=1,
                                          mode="promise_in_bounds")
                acc[0] = jnp.where(i1c == c * blocks + 2 * pc + sub, got, acc[0])
        return run

    tm = hn_ref.shape[0]
    groups = tm // ROUTE_LANES
    ticks = groups * (2 * (len(_sorting_network(k)) // 16 + k) + k)
    pieces = _Interleaver([piece(pc) for pc in range(blocks // 2)], every=ticks // (blocks // 2 + 3))
    qt = lax.dot_general(wq_ref[...], hn_ref[...], (((1,), (1,)), ((), ())), preferred_element_type=F32)
    qt = qt.astype(BF16)
    rows = pl.ds(pl.multiple_of(c * k, k), k)
    for gi in range(groups):
        cols = slice(gi * ROUTE_LANES, (gi + 1) * ROUTE_LANES)
        i1, i2, g = _route_head(qt[:, cols], keys_ref, pieces.tick)
        i1t_ref[rows, cols] = i1
        i2t_ref[rows, cols] = i2
        gt_ref[rows, cols] = g
    pieces.drain()
    act_ref[...] = acc[0]

    @pl.when(c == pl.num_programs(1) - 1)
    def _():
        i1n = i1t_ref[...].T
        i2n = i2t_ref[...].T
        i1_ref[...] = i1n
        i2_ref[...] = i2n
        g_ref[...] = gt_ref[...].T
        i1c_ref[...] = i1n
        i2c_ref[...] = i2n


def _peer_route_u(h, w_q, sub_keys, u, tm=512):
    t, d = h.shape
    nt = t // tm
    ne = u.shape[0]
    nchunk = PEER_HEADS
    ec = ne // nchunk
    blocks = ec // PEER_KEYS
    hq = w_q.shape[1] // PEER_HEADS
    wq_t = w_q.T.astype(BF16)
    keys = sub_keys.reshape(PEER_HEADS * 2, PEER_KEYS, PEER_HALF).astype(BF16)
    slots = PEER_HEADS * PEER_TOPK
    nxt = lambda o, c: (jnp.minimum(o, nt - 1), 0)
    cur = lambda o, c: (jnp.maximum(o - 1, 0), 0)
    kern = functools.partial(_peer_route_u_kernel, blocks=blocks)
    return pl.pallas_call(
        kern, grid=(nt + 1, nchunk),
        in_specs=[pl.BlockSpec((tm, d), nxt), pl.BlockSpec((tm, d), cur),
                  pl.BlockSpec((hq, d), lambda o, c: (c, 0)),
                  pl.BlockSpec((2, PEER_KEYS, PEER_HALF), lambda o, c: (c, 0, 0)),
                  pl.BlockSpec((ec, d), lambda o, c: (c, 0))],
        out_specs=[pl.BlockSpec((tm, slots), nxt)] * 3 + [pl.BlockSpec((tm, slots), cur)],
        out_shape=[jax.ShapeDtypeStruct((t, slots), I32), jax.ShapeDtypeStruct((t, slots), I32),
                   jax.ShapeDtypeStruct((t, slots), F32), jax.ShapeDtypeStruct((t, slots), F32)],
        scratch_shapes=[pltpu.VMEM((slots, tm), I32), pltpu.VMEM((slots, tm), I32), pltpu.VMEM((slots, tm), F32),
                        pltpu.VMEM((tm, slots), I32), pltpu.VMEM((tm, slots), I32)],
        compiler_params=_params(("arbitrary", "arbitrary")), name="peer_route_u",
    )(h, h, wq_t, keys, u)


def _peer_v_kernel(parts_ref, g_ref, i1_ref, i2_ref, v_ref, x_ref, o_ref, w_ref, wg_ref, *, tm, blocks):
    c = pl.program_id(1)
    nk = PEER_KEYS

    @pl.when(c == 0)
    def _():
        w_ref[...] = g_ref[...] * _gelu(jnp.sum(parts_ref[...], axis=0))
        o_ref[...] = x_ref[...]
        sub = lax.broadcasted_iota(I32, (nk, w_ref.shape[1]), 0)

        def per_token(t, carry):
            wrow = w_ref[pl.ds(t, 1), :]
            lhs = jnp.where(i1_ref[pl.ds(t, 1), :] == sub, wrow, 0.0).astype(BF16)
            rhs = jnp.where(i2_ref[pl.ds(t, 1), :] == sub, 1.0, 0.0).astype(BF16)
            grid = lax.dot_general(lhs, rhs, (((1,), (1,)), ((), ())), preferred_element_type=F32)
            wg_ref[pl.ds(pl.multiple_of(t * nk, nk), nk), :] = grid
            return carry

        lax.fori_loop(0, tm, per_token, 0, unroll=16)

    acc = jnp.zeros(o_ref.shape, F32)
    for bk in range(0, blocks, 2):
        i1 = c * blocks + bk
        lhs = jnp.concatenate([wg_ref[pl.ds(i1, tm, stride=nk), :], wg_ref[pl.ds(i1 + 1, tm, stride=nk), :]],
                              axis=1).astype(BF16)
        rhs = v_ref[bk:bk + 2].reshape(2 * nk, v_ref.shape[2])
        acc += jnp.dot(lhs, rhs, preferred_element_type=F32)
    o_ref[...] += acc


def _peer_v(parts, g, i1, i2, v, x, tm=256, blocks=32):
    t, d = x.shape
    nchunk_u = parts.shape[0]
    slots = g.shape[1]
    nk = PEER_KEYS
    v3 = v.reshape(nk, nk, d)
    kern = functools.partial(_peer_v_kernel, tm=tm, blocks=blocks)
    row = lambda i, c: (i, 0)
    return pl.pallas_call(
        kern, grid=(t // tm, nk // blocks),
        in_specs=[pl.BlockSpec((nchunk_u, tm, slots), lambda i, c: (0, i, 0)),
                  pl.BlockSpec((tm, slots), row), pl.BlockSpec((tm, slots), row), pl.BlockSpec((tm, slots), row),
                  pl.BlockSpec((blocks, nk, d), lambda i, c: (c, 0, 0)), pl.BlockSpec((tm, d), row)],
        out_specs=pl.BlockSpec((tm, d), row),
        out_shape=jax.ShapeDtypeStruct((t, d), F32),
        scratch_shapes=[pltpu.VMEM((tm, slots), F32), pltpu.VMEM((tm * nk, nk), F32)],
        compiler_params=_params(("parallel", "arbitrary")), name="peer_v",
    )(parts, g, i1, i2, v3, x)


IN_SIZES = (1024, 768, 48, 1024, 1024, 1024, 1536, 16, 3072)


def _pad_cols(w, n):
    return jnp.pad(w, ((0, 0), (0, n - w.shape[1])))


def _mixer(x, positions, b, s, p):
    xn = _rmsnorm(x, p["mix_norm_g"])
    offs = [0]
    for n in IN_SIZES:
        offs.append(offs[-1] + n)
    w = [p["w_in"][:, offs[i]:offs[i + 1]] for i in range(len(IN_SIZES))]
    a_q = _matmul(xn, w[0].astype(BF16), BF16, "proj_q")
    a_kv = _matmul(xn, w[1].astype(BF16), BF16, "proj_kv")
    a_g = _matmul(xn, _pad_cols(w[2], LANES).astype(BF16), F32, "proj_ag")
    l_x = _matmul(xn, w[3].astype(BF16), BF16, "proj_lx")
    l_g = _matmul(xn, w[4].astype(BF16), BF16, "proj_lg")
    s_z = _matmul(xn, w[5].astype(BF16), BF16, "proj_sz")
    s_xbc = _matmul(xn, w[6].astype(BF16), BF16, "proj_sxbc")
    s_dt = _matmul(xn, _pad_cols(w[7], LANES).astype(BF16), F32, "proj_sdt")
    m_g = _matmul(xn, w[8].astype(BF16), BF16, "proj_mg", tn=1024)

    q, kc, vc, ks, vs, kw, vw = _nsa_prep(a_q, a_kv, positions, p["q_norm_g"], p["k_norm_g"], b, s)
    k_cmp, v_cmp = _compress(kc, vc, p["cmp_pe_k"], p["cmp_pe_v"], p["cmp_k_w1"], p["cmp_k_w2"],
                             p["cmp_v_w1"], p["cmp_v_w2"], p["k_norm_g"])
    o_c, bias = _cmp_select(q, k_cmp, v_cmp, s)
    o_s = _sel_attn(q, bias, ks, vs, s)
    o_w = _win_attn(q, kw, vw, s)
    y_b = _rglru(l_x, l_g, positions, p["lru_conv_w"], p["lru_conv_b"], p["lru_w_r"], p["lru_b_r"],
                 p["lru_w_i"], p["lru_b_i"], p["lru_lambda"], b, s)
    y_c = _ssd(s_z, s_xbc, s_dt, p["ssm_conv_w"], p["ssm_conv_b"], p["ssm_dt_bias"], p["ssm_a_log"],
               p["ssm_d"], p["ssm_norm_g"], b, s)
    return _merge(o_c, o_s, o_w, a_g, y_b, y_c, m_g, x, p["w_branch"], p["w_out"])


def _peer(x, p):
    h = _rmsnorm(x, p["ffn_norm_g"])
    i1, i2, g, act = _peer_route_u(h, p["peer_w_q"], p["peer_sub_keys"], p["peer_u"].astype(BF16))
    return _peer_v(act[None], g, i1, i2, p["peer_v"].astype(BF16), x)


_LAYER_PARAMS = ("mix_norm_g", "w_in", "q_norm_g", "k_norm_g", "cmp_pe_k", "cmp_pe_v", "cmp_k_w1", "cmp_k_w2",
                 "cmp_v_w1", "cmp_v_w2", "lru_conv_w", "lru_conv_b", "lru_w_r", "lru_b_r", "lru_w_i", "lru_b_i",
                 "lru_lambda", "ssm_conv_w", "ssm_conv_b", "ssm_dt_bias", "ssm_a_log", "ssm_d", "ssm_norm_g",
                 "w_branch", "w_out", "ffn_norm_g", "peer_w_q", "peer_sub_keys", "peer_u", "peer_v")


def kernel(x, positions, mix_norm_g, w_in, q_norm_g, k_norm_g, cmp_pe_k, cmp_pe_v, cmp_k_w1, cmp_k_w2, cmp_v_w1, cmp_v_w2, lru_conv_w, lru_conv_b, lru_w_r, lru_b_r, lru_w_i, lru_b_i, lru_lambda, ssm_conv_w, ssm_conv_b, ssm_dt_bias, ssm_a_log, ssm_d, ssm_norm_g, w_branch, w_out, ffn_norm_g, peer_w_q, peer_sub_keys, peer_u, peer_v):
    stacked = dict(zip(_LAYER_PARAMS, (mix_norm_g, w_in, q_norm_g, k_norm_g, cmp_pe_k, cmp_pe_v, cmp_k_w1,
                                       cmp_k_w2, cmp_v_w1, cmp_v_w2, lru_conv_w, lru_conv_b, lru_w_r, lru_b_r,
                                       lru_w_i, lru_b_i, lru_lambda, ssm_conv_w, ssm_conv_b, ssm_dt_bias,
                                       ssm_a_log, ssm_d, ssm_norm_g, w_branch, w_out, ffn_norm_g, peer_w_q,
                                       peer_sub_keys, peer_u, peer_v)))
    b, s, d = x.shape
    xf = x.reshape(b * s, d).astype(F32)
    for layer in range(mix_norm_g.shape[0]):
        p = {name: arr[layer] for name, arr in stacked.items()}
        xf = _mixer(xf, positions, b, s, p)
        xf = _peer(xf, p)
    return xf.reshape(b, s, d).astype(x.dtype)
```

```python
import functools
import math

import jax
import jax.numpy as jnp
from jax import lax
from jax.experimental import pallas as pl
from jax.experimental.pallas import tpu as pltpu

F32 = jnp.float32
BF16 = jnp.bfloat16
I32 = jnp.int32

D_MODEL = 1024
HEAD_DIM = 64
N_HEADS = 16
N_KV_GROUPS = 2
HEADS_PER_GROUP = 8
CMP_BLOCK = 32
CMP_STRIDE = 16
CMP_HIDDEN = 256
SEL_BLOCK = 64
N_SELECT = 16
WINDOW = 512
SEL_FORCE = 100.0
ROPE_THETA = 10000.0
SCALE = HEAD_DIM ** -0.5
LOG2E = math.log2(math.e)
LRU_HEADS = 16
LRU_BLOCK = 64
LRU_C = 8.0
CONV_WIDTH = 4
SSM_HEADS = 16
SSM_HEAD_DIM = 64
SSM_GROUPS = 2
SSM_STATE = 128
SSM_CHUNK = 128
SSM_INNER = 1024
PEER_HEADS = 8
PEER_KEYS = 128
PEER_HALF = 128
PEER_TOPK = 16
EPS = 1e-6
NEG = -1e30
LANES = 128

VMEM_LIMIT = 56 * 1024 * 1024
SEL_LOOKAHEAD = 3
WG_PITCH = PEER_KEYS + 8
ROUTE_LANES = 128


def _params(sem):
    return pltpu.CompilerParams(dimension_semantics=sem, vmem_limit_bytes=VMEM_LIMIT)


def _gelu(x):
    return 0.5 * x * (1.0 + jnp.tanh(math.sqrt(2.0 / math.pi) * (x + 0.044715 * x * x * x)))


def _sigmoid(x):
    return 0.5 * jnp.tanh(0.5 * x) + 0.5


def _softplus(x):
    return jnp.maximum(x, 0.0) + jnp.log(1.0 + jnp.exp(-jnp.abs(x)))


def _rmsnorm_kernel(x_ref, g_ref, o_ref):
    x = x_ref[...].astype(F32)
    ms = jnp.mean(x * x, axis=-1, keepdims=True)
    o_ref[...] = (x * lax.rsqrt(ms + EPS) * g_ref[...]).astype(o_ref.dtype)


def _rmsnorm(x, g, tm=512):
    t, d = x.shape
    return pl.pallas_call(
        _rmsnorm_kernel, grid=(t // tm,),
        in_specs=[pl.BlockSpec((tm, d), lambda i: (i, 0)), pl.BlockSpec((1, d), lambda i: (0, 0))],
        out_specs=pl.BlockSpec((tm, d), lambda i: (i, 0)),
        out_shape=jax.ShapeDtypeStruct((t, d), BF16),
        compiler_params=_params(("parallel",)), name="rmsnorm")(x, g.reshape(1, d).astype(F32))


def _mm_kernel(x_ref, w_ref, o_ref):
    o_ref[...] = jnp.dot(x_ref[...], w_ref[...], preferred_element_type=F32).astype(o_ref.dtype)


def _matmul(x, w, out_dtype, name, tm=512, tn=None):
    t, k = x.shape
    n = w.shape[1]
    tn = n if tn is None else tn
    return pl.pallas_call(
        _mm_kernel, grid=(n // tn, t // tm),
        in_specs=[pl.BlockSpec((tm, k), lambda j, i: (i, 0)), pl.BlockSpec((k, tn), lambda j, i: (0, j))],
        out_specs=pl.BlockSpec((tm, tn), lambda j, i: (i, j)),
        out_shape=jax.ShapeDtypeStruct((t, n), out_dtype),
        compiler_params=_params(("parallel", "parallel")), name=name)(x, w)


def _norm_rope(x, g, cos, sin_signed, bd):
    ms = jnp.dot((x * x).astype(BF16), bd, preferred_element_type=F32)
    y = x * lax.rsqrt(ms + EPS) * g
    lane = lax.broadcasted_iota(I32, y.shape, 1)
    first_half = (lane % HEAD_DIM) < (HEAD_DIM // 2)
    partner = jnp.where(first_half, pltpu.roll(y, LANES - HEAD_DIM // 2, 1), pltpu.roll(y, HEAD_DIM // 2, 1))
    return y * cos + partner * sin_signed


def _nsa_prep_kernel(aq_ref, akv_ref, pos_ref, inv_ref, sgn_ref, gq_ref, gk_ref, bd_ref,
                     q_ref, kc_ref, vc_ref, ks_ref, vs_ref, kw_ref, vw_ref):
    ang = pos_ref[...].astype(F32) * inv_ref[...]
    cos = jnp.cos(ang)
    sin_signed = jnp.sin(ang) * sgn_ref[...]
    bd = bd_ref[...]
    for c in range(D_MODEL // LANES):
        x = aq_ref[:, c * LANES:(c + 1) * LANES].astype(F32)
        y = (_norm_rope(x, gq_ref[...], cos, sin_signed, bd) * (SCALE * LOG2E)).astype(q_ref.dtype)
        q_ref[0, 2 * c] = y[:, :HEAD_DIM]
        q_ref[0, 2 * c + 1] = y[:, HEAD_DIM:]
    outs = (kc_ref, vc_ref, ks_ref, vs_ref, kw_ref, vw_ref)
    for c, o_ref in enumerate(outs):
        x = akv_ref[:, c * LANES:(c + 1) * LANES]
        if c % 2 == 0:
            y = _norm_rope(x.astype(F32), gk_ref[...], cos, sin_signed, bd).astype(o_ref.dtype)
        else:
            y = x.astype(o_ref.dtype)
        o_ref[0, 0] = y[:, :HEAD_DIM]
        o_ref[0, 1] = y[:, HEAD_DIM:]


def _nsa_prep(a_q, a_kv, positions, q_norm_g, k_norm_g, b, s, tm=512):
    half = HEAD_DIM // 2
    lane = jnp.arange(LANES)
    inv = (ROPE_THETA ** (-((lane % half).astype(F32)) / half)).reshape(1, LANES)
    sgn = jnp.where((lane % HEAD_DIM) < half, -1.0, 1.0).astype(F32).reshape(1, LANES)
    bd = jnp.where((lane[:, None] // HEAD_DIM) == (lane[None, :] // HEAD_DIM), 1.0 / HEAD_DIM, 0.0).astype(BF16)
    gq = jnp.tile(q_norm_g.astype(F32), 2).reshape(1, LANES)
    gk = jnp.tile(k_norm_g.astype(F32), 2).reshape(1, LANES)
    nt = s // tm
    row = lambda i: (i, 0)
    const = lambda i: (0, 0)
    kv_shape = jax.ShapeDtypeStruct((b, N_KV_GROUPS, s, HEAD_DIM), BF16)
    kv_spec = pl.BlockSpec((1, N_KV_GROUPS, tm, HEAD_DIM), lambda i: (i // nt, 0, i % nt, 0))
    return pl.pallas_call(
        _nsa_prep_kernel, grid=(b * nt,),
        in_specs=[pl.BlockSpec((tm, D_MODEL), row), pl.BlockSpec((tm, 6 * LANES), row),
                  pl.BlockSpec((tm, 1), row), pl.BlockSpec((1, LANES), const), pl.BlockSpec((1, LANES), const),
                  pl.BlockSpec((1, LANES), const), pl.BlockSpec((1, LANES), const),
                  pl.BlockSpec((LANES, LANES), const)],
        out_specs=[pl.BlockSpec((1, N_HEADS, tm, HEAD_DIM), lambda i: (i // nt, 0, i % nt, 0))] + [kv_spec] * 6,
        out_shape=[jax.ShapeDtypeStruct((b, N_HEADS, s, HEAD_DIM), BF16)] + [kv_shape] * 6,
        compiler_params=_params(("parallel",)), name="nsa_prep",
    )(a_q, a_kv, positions.reshape(b * s, 1).astype(I32), inv, sgn, gq, gk, bd)


def _compress_kernel(uk_ref, uv_ref, pek_ref, pev_ref, kw1_ref, kw2_ref, vw1_ref, vw2_ref, gk_ref,
                     kc_ref, vc_ref):
    half = CMP_STRIDE * HEAD_DIM

    def mlp(u, pe, w1_ref, w2_ref):
        n = u.shape[0]
        ha = jnp.dot(u, w1_ref[:half, :], preferred_element_type=F32)
        hb = jnp.dot(u, w1_ref[half:, :], preferred_element_type=F32)
        bias = jnp.dot(pe, w1_ref[...], preferred_element_type=F32)[0:1, :]
        pre = ha + pltpu.roll(hb, n - 1, 0) + bias
        return jnp.dot(_gelu(pre).astype(BF16), w2_ref[...], preferred_element_type=F32)

    k = mlp(uk_ref[0, 0], pek_ref[...], kw1_ref, kw2_ref)
    ms = jnp.mean(k * k, axis=-1, keepdims=True)
    kc_ref[0, 0] = (k * lax.rsqrt(ms + EPS) * gk_ref[...]).astype(kc_ref.dtype)
    vc_ref[0, 0] = mlp(uv_ref[0, 0], pev_ref[...], vw1_ref, vw2_ref).astype(vc_ref.dtype)


def _compress(kc, vc, pe_k, pe_v, kw1, kw2, vw1, vw2, k_norm_g):
    b, g, s, dh = kc.shape
    ng = s // CMP_STRIDE
    wide = CMP_BLOCK * dh
    uk = kc.reshape(b, g, ng, CMP_STRIDE * dh)
    uv = vc.reshape(b, g, ng, CMP_STRIDE * dh)
    pek = jnp.zeros((8, wide), BF16).at[0].set(pe_k.reshape(wide).astype(BF16))
    pev = jnp.zeros((8, wide), BF16).at[0].set(pe_v.reshape(wide).astype(BF16))
    u_spec = pl.BlockSpec((1, 1, ng, CMP_STRIDE * dh), lambda i, j: (i, j, 0, 0))
    c2 = lambda i, j: (0, 0)
    o_spec = pl.BlockSpec((1, 1, ng, dh), lambda i, j: (i, j, 0, 0))
    o_shape = jax.ShapeDtypeStruct((b, g, ng, dh), BF16)
    return pl.pallas_call(
        _compress_kernel, grid=(b, g),
        in_specs=[u_spec, u_spec, pl.BlockSpec((8, wide), c2), pl.BlockSpec((8, wide), c2),
                  pl.BlockSpec((wide, CMP_HIDDEN), c2), pl.BlockSpec((CMP_HIDDEN, dh), c2),
                  pl.BlockSpec((wide, CMP_HIDDEN), c2), pl.BlockSpec((CMP_HIDDEN, dh), c2),
                  pl.BlockSpec((1, dh), c2)],
        out_specs=[o_spec, o_spec], out_shape=[o_shape, o_shape],
        compiler_params=_params(("parallel", "parallel")), name="nsa_compress",
    )(uk, uv, pek, pev, kw1.astype(BF16), kw2.astype(BF16), vw1.astype(BF16), vw2.astype(BF16),
      k_norm_g.reshape(1, dh).astype(F32))


def _store_head_pair(o_ref, h, o_t, held):
    if h % 2 == 0:
        return o_t
    pair = jnp.concatenate([held, o_t], axis=0).T
    o_ref[0, :, (h - 1) * HEAD_DIM:(h + 1) * HEAD_DIM] = pair.astype(o_ref.dtype)
    return None


def _cmp_select_kernel(q_ref, kc_ref, vc_ref, ovt_ref, o_ref, bias_ref, *, tq, n_cmp, n_pick):
    i = pl.program_id(2)
    hg = q_ref.shape[1]
    nc = kc_ref.shape[2]
    nsel = ovt_ref.shape[0]
    kc = kc_ref[0, 0]
    vct = vc_ref[0, 0].astype(F32).T.astype(BF16)
    c = lax.broadcasted_iota(I32, (nc, tq), 0)
    t = i * tq + lax.broadcasted_iota(I32, (nc, tq), 1)
    mask = ((CMP_STRIDE * c + CMP_BLOCK - 1) <= t) & (c < n_cmp)
    score = lambda h: lax.dot_general(kc, q_ref[0, h], (((1,), (1,)), ((), ())), preferred_element_type=F32)
    st_next = score(0)
    psum = jnp.zeros((nc, tq), F32)
    held = None
    for h in range(hg):
        st = st_next
        if h + 1 < hg:
            st_next = score(h + 1)
        st = jnp.where(mask, st, NEG)
        m = jnp.max(st, axis=0, keepdims=True)
        p = jnp.where(mask, jnp.exp2(st - m), 0.0)
        l = jnp.sum(p, axis=0, keepdims=True)
        p = p * jnp.where(l > 0.0, 1.0 / l, 0.0)
        o = jnp.dot(vct, p.astype(BF16), preferred_element_type=F32)
        held = _store_head_pair(o_ref, h, o, held)
        psum = psum + p

    hi = psum.astype(BF16)
    lo = (psum - hi.astype(F32)).astype(BF16)
    imp = (jnp.dot(ovt_ref[...], hi, preferred_element_type=F32)
           + jnp.dot(ovt_ref[...], lo, preferred_element_type=F32))
    j = lax.broadcasted_iota(I32, (nsel, tq), 0)
    tt = i * tq + lax.broadcasted_iota(I32, (nsel, tq), 1)
    cur = tt // SEL_BLOCK
    forced = (j == 0) | (j == cur) | (j == cur - 1)
    sc = jnp.where(forced, SEL_FORCE, jnp.where(j * SEL_BLOCK <= tt, imp, -1.0))
    picked = jnp.zeros((nsel, tq), jnp.bool_)
    for _ in range(n_pick):
        mx = jnp.max(sc, axis=0, keepdims=True)
        idx = jnp.min(jnp.where(sc == mx, j, nsel), axis=0, keepdims=True)
        hit = j == idx
        picked = picked | hit
        sc = jnp.where(hit, -jnp.inf, sc)
    bias_ref[0, 0] = jnp.where(picked & (j <= cur), 0.0, NEG).T.astype(bias_ref.dtype)


def _cmp_select(q, k_cmp, v_cmp, s, tq=512):
    b, nh, _, dh = q.shape
    g, hg = N_KV_GROUPS, HEADS_PER_GROUP
    nc = k_cmp.shape[2]
    n_cmp = (s - CMP_BLOCK) // CMP_STRIDE + 1
    nsel = s // SEL_BLOCK
    n_pick = min(N_SELECT, nsel)
    cs = CMP_STRIDE * jnp.arange(nc)
    ss = SEL_BLOCK * jnp.arange(nsel)
    ovt = jnp.clip(jnp.minimum(cs[None, :] + CMP_BLOCK, ss[:, None] + SEL_BLOCK)
                   - jnp.maximum(cs[None, :], ss[:, None]), 0).astype(F32) / CMP_BLOCK
    ovt = jnp.where(jnp.arange(nc)[None, :] < n_cmp, ovt, 0.0).astype(BF16)
    kern = functools.partial(_cmp_select_kernel, tq=tq, n_cmp=n_cmp, n_pick=n_pick)
    return pl.pallas_call(
        kern, grid=(b, g, s // tq),
        in_specs=[pl.BlockSpec((1, hg, tq, dh), lambda bi, gi, i: (bi, gi, i, 0)),
                  pl.BlockSpec((1, 1, nc, dh), lambda bi, gi, i: (bi, gi, 0, 0)),
                  pl.BlockSpec((1, 1, nc, dh), lambda bi, gi, i: (bi, gi, 0, 0)),
                  pl.BlockSpec((nsel, nc), lambda bi, gi, i: (0, 0))],
        out_specs=[pl.BlockSpec((1, tq, hg * dh), lambda bi, gi, i: (bi, i, gi)),
                   pl.BlockSpec((1, 1, tq, nsel), lambda bi, gi, i: (bi, gi, i, 0))],
        out_shape=[jax.ShapeDtypeStruct((b, s, nh * dh), BF16),
                   jax.ShapeDtypeStruct((b, g, s, nsel), BF16)],
        compiler_params=_params(("parallel", "parallel", "parallel")), name="nsa_cmp_select",
    )(q, k_cmp, v_cmp, ovt)


def _sel_attn_kernel(it_ref, jt_ref, q_ref, bias_ref, k_ref, v_ref, o_ref, qa_ref, m_ref, acc_ref, *, tq, tk):
    pid = pl.program_id(2)
    i = it_ref[pid]
    j = jt_ref[pid]
    hg = q_ref.shape[1]
    nsel = bias_ref.shape[3]
    last_j = ((i + 1) * tq - 1) // tk

    @pl.when(j == 0)
    def _():
        for h in range(hg):
            qa_ref[h, :, :nsel] = bias_ref[0, 0]
            qa_ref[h, :, nsel:] = q_ref[0, h]
        m_ref[...] = jnp.full(m_ref.shape, -jnp.inf, F32)
        acc_ref[...] = jnp.zeros(acc_ref.shape, F32)

    def accumulate(diagonal):
        key = j * tk + lax.broadcasted_iota(I32, (tk, nsel), 0)
        blk = lax.broadcasted_iota(I32, (tk, nsel), 1)
        onehot = jnp.where(key // SEL_BLOCK == blk, 1.0, 0.0).astype(BF16)
        ka = jnp.concatenate([onehot, k_ref[0, 0]], axis=1)
        va = jnp.concatenate([v_ref[0, 0].astype(F32), jnp.ones((tk, LANES - HEAD_DIM), F32)], axis=1)
        vat = va.T.astype(BF16)
        if diagonal:
            kp = j * tk + lax.broadcasted_iota(I32, (tk, tq), 0)
            t = i * tq + lax.broadcasted_iota(I32, (tk, tq), 1)
            causal = kp <= t
        score = lambda h: lax.dot_general(ka, qa_ref[h], (((1,), (1,)), ((), ())), preferred_element_type=F32)
        ahead = [score(h) for h in range(SEL_LOOKAHEAD)]
        for h in range(hg):
            st = ahead.pop(0)
            if h + SEL_LOOKAHEAD < hg:
                ahead.append(score(h + SEL_LOOKAHEAD))
            if diagonal:
                st = jnp.where(causal, st, NEG)
            m_old = m_ref[h]
            m_new = jnp.maximum(m_old, jnp.max(st, axis=0, keepdims=True))
            alpha = jnp.exp2(m_old - m_new)
            p = jnp.exp2(st - m_new)
            acc_ref[h] = alpha * acc_ref[h] + jnp.dot(vat, p.astype(BF16), preferred_element_type=F32)
            m_ref[h] = m_new

    @pl.when(j < last_j)
    def _():
        accumulate(False)

    @pl.when(j == last_j)
    def _():
        accumulate(True)
        held = None
        for h in range(hg):
            acc = acc_ref[h]
            o = acc[:HEAD_DIM, :] / acc[HEAD_DIM:HEAD_DIM + 1, :]
            held = _store_head_pair(o_ref, h, o, held)


def _sel_attn(q, bias, k_s, v_s, s, tq=512, tk=512):
    assert tq <= tk
    b, nh, _, dh = q.shape
    g, hg = N_KV_GROUPS, HEADS_PER_GROUP
    nsel = bias.shape[3]
    tk = min(tk, s)
    pairs = [(i, j) for i in range(s // tq) for j in range(((i + 1) * tq - 1) // tk + 1)]
    it = jnp.asarray([pr[0] for pr in pairs], I32)
    jt = jnp.asarray([pr[1] for pr in pairs], I32)
    q_map = lambda bi, gi, pid, it_ref, jt_ref: (bi, gi, it_ref[pid], 0)
    kv_map = lambda bi, gi, pid, it_ref, jt_ref: (bi, gi, jt_ref[pid], 0)
    kern = functools.partial(_sel_attn_kernel, tq=tq, tk=tk)
    grid_spec = pltpu.PrefetchScalarGridSpec(
        num_scalar_prefetch=2, grid=(b, g, len(pairs)),
        in_specs=[pl.BlockSpec((1, hg, tq, dh), q_map), pl.BlockSpec((1, 1, tq, nsel), q_map),
                  pl.BlockSpec((1, 1, tk, dh), kv_map), pl.BlockSpec((1, 1, tk, dh), kv_map)],
        out_specs=pl.BlockSpec((1, tq, hg * dh), lambda bi, gi, pid, it_ref, jt_ref: (bi, it_ref[pid], gi)),
        scratch_shapes=[pltpu.VMEM((hg, tq, nsel + dh), BF16), pltpu.VMEM((hg, 1, tq), F32),
                        pltpu.VMEM((hg, LANES, tq), F32)])
    return pl.pallas_call(
        kern, grid_spec=grid_spec, out_shape=jax.ShapeDtypeStruct((b, s, nh * dh), BF16),
        compiler_params=_params(("parallel", "parallel", "arbitrary")), name="nsa_sel_attn",
    )(it, jt, q, bias, k_s, v_s)


def _win_attn_kernel(q_ref, *refs, tq, nwin):
    k_refs, v_refs, o_ref = refs[:nwin], refs[nwin:2 * nwin], refs[2 * nwin]
    i = pl.program_id(2)
    hg = q_ref.shape[1]
    nk = nwin * tq
    k = jnp.concatenate([r[0, 0] for r in k_refs], axis=0)
    vt = jnp.concatenate([r[0, 0] for r in v_refs], axis=0).astype(F32).T.astype(BF16)
    sp = i * tq - WINDOW + lax.broadcasted_iota(I32, (nk, tq), 0)
    t = i * tq + lax.broadcasted_iota(I32, (nk, tq), 1)
    mask = (sp <= t) & (sp > t - WINDOW) & (sp >= 0)
    score = lambda h: lax.dot_general(k, q_ref[0, h], (((1,), (1,)), ((), ())), preferred_element_type=F32)
    st_next = score(0)
    held = None
    for h in range(hg):
        st = st_next
        if h + 1 < hg:
            st_next = score(h + 1)
        st = jnp.where(mask, st, NEG)
        m = jnp.max(st, axis=0, keepdims=True)
        p = jnp.where(mask, jnp.exp2(st - m), 0.0)
        l = jnp.sum(p, axis=0, keepdims=True)
        o = jnp.dot(vt, p.astype(BF16), preferred_element_type=F32) / l
        held = _store_head_pair(o_ref, h, o, held)


def _win_attn(q, k_w, v_w, s, tq=512):
    b, nh, _, dh = q.shape
    g, hg = N_KV_GROUPS, HEADS_PER_GROUP
    nwin = WINDOW // tq + 1
    pad = ((0, 0), (0, 0), (WINDOW, 0), (0, 0))
    kp, vp = jnp.pad(k_w, pad), jnp.pad(v_w, pad)
    kv_specs = [pl.BlockSpec((1, 1, tq, dh), functools.partial(lambda bi, gi, i, w: (bi, gi, i + w, 0), w=w))
                for w in range(nwin)]
    kern = functools.partial(_win_attn_kernel, tq=tq, nwin=nwin)
    return pl.pallas_call(
        kern, grid=(b, g, s // tq),
        in_specs=[pl.BlockSpec((1, hg, tq, dh), lambda bi, gi, i: (bi, gi, i, 0))] + kv_specs + kv_specs,
        out_specs=pl.BlockSpec((1, tq, hg * dh), lambda bi, gi, i: (bi, i, gi)),
        out_shape=jax.ShapeDtypeStruct((b, s, nh * dh), BF16),
        compiler_params=_params(("parallel", "parallel", "parallel")), name="nsa_win_attn",
    )(q, *([kp] * nwin), *([vp] * nwin))


def _causal_conv(x, tail, w_ref, b_ref):
    n = x.shape[0]
    xx = jnp.concatenate([tail, x], axis=0)
    out = b_ref[...] + w_ref[CONV_WIDTH - 1:CONV_WIDTH, :] * x
    for k in range(CONV_WIDTH - 1):
        off = 8 - (CONV_WIDTH - 1) + k
        out = out + w_ref[k:k + 1, :] * xx[off:off + n, :]
    return out


def _rglru_kernel(x_ref, gate_ref, pos_ref, cw_ref, cb_ref, wr_ref, br_ref, wi_ref, bi_ref, lam_ref,
                  o_ref, tail_ref, h_ref):
    @pl.when(pl.program_id(1) == 0)
    def _():
        tail_ref[...] = jnp.zeros(tail_ref.shape, F32)
        h_ref[...] = jnp.zeros(h_ref.shape, F32)

    n = x_ref.shape[0]
    x = x_ref[...].astype(F32)
    xc = _causal_conv(x, tail_ref[...], cw_ref, cb_ref)
    tail_ref[...] = x[n - 8:, :]
    xcb = xc.astype(BF16)
    r = _sigmoid(jnp.dot(xcb, wr_ref[...], preferred_element_type=F32) + br_ref[...])
    gi = _sigmoid(jnp.dot(xcb, wi_ref[...], preferred_element_type=F32) + bi_ref[...])
    log_a = -LRU_C * r * _softplus(-lam_ref[...])
    reset = pos_ref[...] == 0
    a = jnp.where(reset, 0.0, jnp.exp(log_a))
    mult = jnp.where(reset, 1.0, jnp.sqrt(jnp.maximum(1.0 - jnp.exp(2.0 * log_a), 0.0)))
    bb = mult * (gi * xc)
    sub = lax.broadcasted_iota(I32, a.shape, 0) % 8
    d = 1
    while d < 8:
        a_sh = pltpu.roll(a, d, 0)
        b_sh = pltpu.roll(bb, d, 0)
        live = sub >= d
        bb = jnp.where(live, a * b_sh + bb, bb)
        a = jnp.where(live, a * a_sh, a)
        d *= 2
    carry = h_ref[...]
    hs = []
    for g in range(n // 8):
        h = bb[8 * g:8 * g + 8, :] + a[8 * g:8 * g + 8, :] * carry
        carry = h[7:8, :]
        hs.append(h)
    h_ref[...] = carry
    o_ref[...] = (jnp.concatenate(hs, axis=0) * _gelu(gate_ref[...].astype(F32))).astype(o_ref.dtype)


def _block_diag(w):
    nb, bs, _ = w.shape
    eye = jnp.eye(nb, dtype=w.dtype)
    return (w[:, :, None, :] * eye[:, None, :, None]).reshape(nb * bs, nb * bs)


def _rglru(l_x, l_g, positions, conv_w, conv_b, w_r, b_r, w_i, b_i, lam, b, s, ts=256):
    d = l_x.shape[1]
    nt = s // ts
    row = lambda bi, i: (bi * nt + i, 0)
    c2 = lambda bi, i: (0, 0)
    vec = lambda v: v.reshape(1, d).astype(F32)
    return pl.pallas_call(
        _rglru_kernel, grid=(b, nt),
        in_specs=[pl.BlockSpec((ts, d), row), pl.BlockSpec((ts, d), row), pl.BlockSpec((ts, 1), row),
                  pl.BlockSpec((CONV_WIDTH, d), c2), pl.BlockSpec((1, d), c2),
                  pl.BlockSpec((d, d), c2), pl.BlockSpec((1, d), c2),
                  pl.BlockSpec((d, d), c2), pl.BlockSpec((1, d), c2), pl.BlockSpec((1, d), c2)],
        out_specs=pl.BlockSpec((ts, d), row),
        out_shape=jax.ShapeDtypeStruct((b * s, d), BF16),
        scratch_shapes=[pltpu.VMEM((8, d), F32), pltpu.VMEM((1, d), F32)],
        compiler_params=_params(("parallel", "arbitrary")), name="rglru",
    )(l_x, l_g, positions.reshape(b * s, 1).astype(I32), conv_w.astype(F32), vec(conv_b),
      _block_diag(w_r).astype(BF16), vec(b_r), _block_diag(w_i).astype(BF16), vec(b_i), vec(lam))


def _ssd_kernel(z_ref, xbc_ref, dt_ref, cw_ref, cb_ref, dtb_ref, alog_ref, dfull_ref, ng_ref,
                o_ref, tail_ref, state_ref, y_ref):
    @pl.when(pl.program_id(1) == 0)
    def _():
        tail_ref[...] = jnp.zeros(tail_ref.shape, F32)
        state_ref[...] = jnp.zeros(state_ref.shape, F32)

    n = xbc_ref.shape[0]
    hg = SSM_HEADS // SSM_GROUPS
    x = xbc_ref[...].astype(F32)
    xc = _causal_conv(x, tail_ref[...], cw_ref, cb_ref)
    tail_ref[...] = x[n - 8:, :]
    xc = xc * _sigmoid(xc)
    xs = xc[:, :SSM_INNER]
    dt = _softplus(dt_ref[...] + dtb_ref[...])
    adt = dt * (-jnp.exp(alog_ref[...]))
    row = lax.broadcasted_iota(I32, adt.shape, 0)
    acs = adt
    d = 1
    while d < n:
        acs = acs + jnp.where(row >= d, pltpu.roll(acs, d, 0), 0.0)
        d *= 2
    acs_t = acs.T
    li = lax.broadcasted_iota(I32, (n, n), 0)
    si = lax.broadcasted_iota(I32, (n, n), 1)
    tri = li >= si
    for g in range(SSM_GROUPS):
        bm = xc[:, SSM_INNER + g * SSM_STATE:SSM_INNER + (g + 1) * SSM_STATE].astype(BF16)
        cm = xc[:, SSM_INNER + (SSM_GROUPS + g) * SSM_STATE:SSM_INNER + (SSM_GROUPS + g + 1) * SSM_STATE].astype(BF16)
        cb = lax.dot_general(cm, bm, (((1,), (1,)), ((), ())), preferred_element_type=F32)
        bm_t = bm.T
        for hh in range(hg):
            h = g * hg + hh
            acol = jnp.broadcast_to(acs[:, h:h + 1], (n, n))
            arow = acs_t[h:h + 1, :]
            decay = jnp.exp(jnp.where(tri, acol - arow, NEG))
            acol_p = acol[:, :SSM_HEAD_DIM]
            xh = xs[:, h * SSM_HEAD_DIM:(h + 1) * SSM_HEAD_DIM] * dt[:, h:h + 1]
            a_last = acol_p[n - 1:n, :]
            y = jnp.dot((cb * decay).astype(BF16), xh.astype(BF16), preferred_element_type=F32)
            st = state_ref[h]
            y = y + jnp.dot(cm, st.astype(BF16), preferred_element_type=F32) * jnp.exp(acol_p)
            upd = jnp.dot(bm_t, (xh * jnp.exp(a_last - acol_p)).astype(BF16), preferred_element_type=F32)
            state_ref[h] = jnp.exp(a_last) * st + upd
            y_ref[:, h * SSM_HEAD_DIM:(h + 1) * SSM_HEAD_DIM] = y
    z = z_ref[...].astype(F32)
    y = (y_ref[...] + dfull_ref[...] * xs) * (z * _sigmoid(z))
    gw = SSM_INNER // SSM_GROUPS
    for g in range(SSM_GROUPS):
        yg = y[:, g * gw:(g + 1) * gw]
        ms = jnp.mean(yg * yg, axis=-1, keepdims=True)
        o_ref[:, g * gw:(g + 1) * gw] = (yg * lax.rsqrt(ms + EPS) * ng_ref[:, g * gw:(g + 1) * gw]).astype(o_ref.dtype)


def _ssd(s_z, s_xbc, s_dt, conv_w, conv_b, dt_bias, a_log, d_skip, norm_g, b, s):
    n = SSM_CHUNK
    nt = s // n
    c = s_xbc.shape[1]
    row = lambda bi, i: (bi * nt + i, 0)
    c2 = lambda bi, i: (0, 0)
    pad_h = lambda v: jnp.zeros((1, LANES), F32).at[0, :SSM_HEADS].set(v.astype(F32))
    dfull = jnp.repeat(d_skip.astype(F32), SSM_HEAD_DIM).reshape(1, SSM_INNER)
    return pl.pallas_call(
        _ssd_kernel, grid=(b, nt),
        in_specs=[pl.BlockSpec((n, SSM_INNER), row), pl.BlockSpec((n, c), row), pl.BlockSpec((n, LANES), row),
                  pl.BlockSpec((CONV_WIDTH, c), c2), pl.BlockSpec((1, c), c2),
                  pl.BlockSpec((1, LANES), c2), pl.BlockSpec((1, LANES), c2),
                  pl.BlockSpec((1, SSM_INNER), c2), pl.BlockSpec((1, SSM_INNER), c2)],
        out_specs=pl.BlockSpec((n, SSM_INNER), row),
        out_shape=jax.ShapeDtypeStruct((b * s, SSM_INNER), BF16),
        scratch_shapes=[pltpu.VMEM((8, c), F32), pltpu.VMEM((SSM_HEADS, SSM_STATE, SSM_HEAD_DIM), F32),
                        pltpu.VMEM((n, SSM_INNER), F32)],
        compiler_params=_params(("parallel", "arbitrary")), name="ssd",
    )(s_z, s_xbc, s_dt, conv_w.astype(F32), conv_b.reshape(1, c).astype(F32), pad_h(dt_bias), pad_h(a_log),
      dfull, norm_g.reshape(1, SSM_INNER).astype(F32))


def _merge_kernel(oc_ref, os_ref, ow_ref, ag_ref, ex_ref, yb_ref, yc_ref, mg_ref, x_ref, wb_ref, wo_ref, o_ref):
    gates = _sigmoid(ag_ref[...])
    hi = gates.astype(BF16)
    lo = (gates - hi.astype(F32)).astype(BF16)
    spread = lambda k: (jnp.dot(hi, ex_ref[k], preferred_element_type=F32)
                        + jnp.dot(lo, ex_ref[k], preferred_element_type=F32))
    ya = (spread(0) * oc_ref[...].astype(F32) + spread(1) * os_ref[...].astype(F32)
          + spread(2) * ow_ref[...].astype(F32))
    d = D_MODEL
    merged = _sigmoid(mg_ref[:, 0:d].astype(F32)) * jnp.dot(ya.astype(BF16), wb_ref[0], preferred_element_type=F32)
    merged += _sigmoid(mg_ref[:, d:2 * d].astype(F32)) * jnp.dot(yb_ref[...], wb_ref[1], preferred_element_type=F32)
    merged += _sigmoid(mg_ref[:, 2 * d:3 * d].astype(F32)) * jnp.dot(yc_ref[...], wb_ref[2], preferred_element_type=F32)
    o_ref[...] = x_ref[...] + jnp.dot(merged.astype(BF16), wo_ref[...], preferred_element_type=F32)


def _merge(o_c, o_s, o_w, a_g, y_b, y_c, m_g, x, w_branch, w_out, tm=256):
    t, d = x.shape
    row = lambda i: (i, 0)
    lane = jnp.arange(LANES)[:, None]
    col = jnp.arange(d)[None, :]
    expand = jnp.stack([(lane == 3 * (col // HEAD_DIM) + k) for k in range(3)]).astype(BF16)
    return pl.pallas_call(
        _merge_kernel, grid=(t // tm,),
        in_specs=[pl.BlockSpec((tm, d), row), pl.BlockSpec((tm, d), row), pl.BlockSpec((tm, d), row),
                  pl.BlockSpec((tm, LANES), row), pl.BlockSpec((3, LANES, d), lambda i: (0, 0, 0)),
                  pl.BlockSpec((tm, d), row), pl.BlockSpec((tm, d), row), pl.BlockSpec((tm, 3 * d), row),
                  pl.BlockSpec((tm, d), row),
                  pl.BlockSpec((3, d, d), lambda i: (0, 0, 0)), pl.BlockSpec((d, d), lambda i: (0, 0))],
        out_specs=pl.BlockSpec((tm, d), row),
        out_shape=jax.ShapeDtypeStruct((t, d), F32),
        compiler_params=_params(("parallel",)), name="merge",
    )(o_c.reshape(t, d), o_s.reshape(t, d), o_w.reshape(t, d), a_g, expand, y_b, y_c, m_g, x,
      w_branch.astype(BF16), w_out.astype(BF16))


class _Interleaver:
    def __init__(self, pieces, every):
        self.pieces, self.every, self.count = list(pieces), every, 0

    def tick(self):
        self.count += 1
        if self.pieces and self.count % self.every == 0:
            self.pieces.pop(0)()

    def drain(self):
        while self.pieces:
            self.pieces.pop(0)()


def _sorting_network(n):
    pairs = []

    def merge(lo, m, r):
        step = r * 2
        if step < m:
            merge(lo, m, step)
            merge(lo + r, m, step)
            pairs.extend((i, i + r) for i in range(lo + r, lo + m - r, step))
        else:
            pairs.append((lo, lo + r))

    def sort(lo, m):
        if m > 1:
            sort(lo, m // 2)
            sort(lo + m // 2, m // 2)
            merge(lo, m, 1)

    sort(0, n)
    return pairs


def _topk_rows(s, k, tick):
    n, lanes = s.shape
    assert n == 8 * k
    sub = lax.broadcasted_iota(I32, (8, lanes), 0)
    vals = [s[8 * j:8 * j + 8, :] for j in range(k)]
    ids = [sub + 8 * j for j in range(k)]
    for count, (a, b) in enumerate(_sorting_network(k)):
        first = (vals[a] > vals[b]) | ((vals[a] == vals[b]) & (ids[a] < ids[b]))
        vals[a], vals[b] = jnp.where(first, vals[a], vals[b]), jnp.where(first, vals[b], vals[a])
        ids[a], ids[b] = jnp.where(first, ids[a], ids[b]), jnp.where(first, ids[b], ids[a])
        if count % 16 == 15:
            tick()
    top_v, top_i = [], []
    for r in range(k):
        best = jnp.max(vals[0], axis=0, keepdims=True)
        row = jnp.min(jnp.where(vals[0] == best, ids[0], n), axis=0, keepdims=True)
        top_v.append(best)
        top_i.append(row)
        won = ids[0] == row
        for d in range(k - 1 - r):
            vals[d] = jnp.where(won, vals[d + 1], vals[d])
            ids[d] = jnp.where(won, ids[d + 1], ids[d])
        tick()
    return jnp.concatenate(top_v, axis=0), jnp.concatenate(top_i, axis=0)


def _top_pairs(s1, s2, k, tick):
    lanes = s1.shape[1]
    sub = lax.broadcasted_iota(I32, (8, lanes), 0)
    lists = [jnp.where(sub <= k // (d + 1) - 1, s1[0:8, :] + s2[d:d + 1, :], -jnp.inf) for d in range(k)]
    tail = s1[8:16, :] + s2[0:1, :]
    tail_pos = (sub + 8) * k
    taken = jnp.zeros((8, lanes), I32)
    vs, aa, bb = [], [], []
    for r in range(k):
        best = jnp.maximum(jnp.max(lists[0], axis=0, keepdims=True), jnp.max(tail, axis=0, keepdims=True))
        head_pos = sub * k + taken
        pos = jnp.minimum(jnp.min(jnp.where(lists[0] == best, head_pos, k * k), axis=0, keepdims=True),
                          jnp.min(jnp.where(tail == best, tail_pos, k * k), axis=0, keepdims=True))
        vs.append(best)
        aa.append(pos // k)
        bb.append(pos % k)
        won = head_pos == pos
        tail = jnp.where(tail_pos == pos, -jnp.inf, tail)
        for d in range(k - 1 - r):
            lists[d] = jnp.where(won, lists[d + 1], lists[d])
        taken = taken + won.astype(I32)
        tick()
    return jnp.concatenate(vs, axis=0), jnp.concatenate(aa, axis=0), jnp.concatenate(bb, axis=0)


def _pick_rows(table, sel, k):
    out = jnp.zeros(sel.shape, table.dtype)
    for a in range(k):
        out = jnp.where(sel == a, table[a:a + 1, :], out)
    return out


def _route_head(qt, keys_ref, tick):
    k = PEER_TOPK
    assert k == 16 and PEER_KEYS == 8 * k
    tops = []
    for half in range(2):
        sc = jnp.dot(keys_ref[half], qt[half * PEER_HALF:(half + 1) * PEER_HALF, :], preferred_element_type=F32)
        tops.append(_topk_rows(sc, k, tick))
    (s1, i1), (s2, i2) = tops
    sc, a_sel, b_sel = _top_pairs(s1, s2, k, tick)
    e = jnp.exp(sc - sc[0:1, :])
    g = e / jnp.sum(e, axis=0, keepdims=True)
    return _pick_rows(i1, a_sel, k), _pick_rows(i2, b_sel, k), g


def _peer_route_u_kernel(hn_ref, hc_ref, wq_ref, keys_ref, u_ref, i1_ref, i2_ref, g_ref, act_ref,
                         i1t_ref, i2t_ref, gt_ref, i1c_ref, i2c_ref, *, blocks):
    o = pl.program_id(0)
    c = pl.program_id(1)
    k = PEER_TOPK

    @pl.when((o == 0) & (c == 0))
    def _():
        i1c_ref[...] = jnp.zeros(i1c_ref.shape, I32)
        i2c_ref[...] = jnp.zeros(i2c_ref.shape, I32)

    @pl.when(c == 0)
    def _():
        act_ref[...] = jnp.zeros(act_ref.shape, F32)

    i1c = i1c_ref[...]
    i2c = i2c_ref[...]
    hc = hc_ref[...]
    acc = [act_ref[...]]

    def piece(pc):
        def run():
            a = lax.dot_general(hc, u_ref[pl.ds(pc * 2 * PEER_KEYS, 2 * PEER_KEYS), :], (((1,), (1,)), ((), ())),
                                preferred_element_type=F32)
            for sub in range(2):
                got = jnp.take_along_axis(a[:, sub * PEER_KEYS:(sub + 1) * PEER_KEYS], i2c, axis=1,
                                          mode="promise_in_bounds")
                acc[0] = jnp.where(i1c == c * blocks + 2 * pc + sub, got, acc[0])
        return run

    tm = hn_ref.shape[0]
    groups = tm // ROUTE_LANES
    hps = keys_ref.shape[0] // 2
    hq = wq_ref.shape[0] // hps
    ticks = hps * groups * (2 * (len(_sorting_network(k)) // 16 + k) + k)
    pieces = _Interleaver([piece(pc) for pc in range(blocks // 2)], every=ticks // (blocks // 2 + 3))
    qt = lax.dot_general(wq_ref[...], hn_ref[...], (((1,), (1,)), ((), ())), preferred_element_type=F32)
    qt = qt.astype(BF16)
    for hd in range(hps):
        rows = pl.ds(pl.multiple_of((c * hps + hd) * k, k), k)
        for gi in range(groups):
            cols = slice(gi * ROUTE_LANES, (gi + 1) * ROUTE_LANES)
            i1, i2, g = _route_head(qt[hd * hq:(hd + 1) * hq, cols], keys_ref.at[pl.ds(2 * hd, 2)], pieces.tick)
            i1t_ref[rows, cols] = i1
            i2t_ref[rows, cols] = i2
            gt_ref[rows, cols] = g
    pieces.drain()
    act_ref[...] = acc[0]

    @pl.when(c == pl.num_programs(1) - 1)
    def _():
        i1n = i1t_ref[...].T
        i2n = i2t_ref[...].T
        i1_ref[...] = i1n
        i2_ref[...] = i2n
        g_ref[...] = gt_ref[...].T
        i1c_ref[...] = i1n
        i2c_ref[...] = i2n


def _peer_route_u(h, w_q, sub_keys, u, tm=512, hps=2):
    t, d = h.shape
    nt = t // tm
    ne = u.shape[0]
    nchunk = PEER_HEADS // hps
    ec = ne // nchunk
    blocks = ec // PEER_KEYS
    hq = hps * (w_q.shape[1] // PEER_HEADS)
    wq_t = w_q.T.astype(BF16)
    keys = sub_keys.reshape(PEER_HEADS * 2, PEER_KEYS, PEER_HALF).astype(BF16)
    slots = PEER_HEADS * PEER_TOPK
    nxt = lambda o, c: (jnp.minimum(o, nt - 1), 0)
    cur = lambda o, c: (jnp.maximum(o - 1, 0), 0)
    kern = functools.partial(_peer_route_u_kernel, blocks=blocks)
    return pl.pallas_call(
        kern, grid=(nt + 1, nchunk),
        in_specs=[pl.BlockSpec((tm, d), nxt), pl.BlockSpec((tm, d), cur),
                  pl.BlockSpec((hq, d), lambda o, c: (c, 0)),
                  pl.BlockSpec((2 * hps, PEER_KEYS, PEER_HALF), lambda o, c: (c, 0, 0)),
                  pl.BlockSpec((ec, d), lambda o, c: (c, 0))],
        out_specs=[pl.BlockSpec((tm, slots), nxt)] * 3 + [pl.BlockSpec((tm, slots), cur)],
        out_shape=[jax.ShapeDtypeStruct((t, slots), I32), jax.ShapeDtypeStruct((t, slots), I32),
                   jax.ShapeDtypeStruct((t, slots), F32), jax.ShapeDtypeStruct((t, slots), F32)],
        scratch_shapes=[pltpu.VMEM((slots, tm), I32), pltpu.VMEM((slots, tm), I32), pltpu.VMEM((slots, tm), F32),
                        pltpu.VMEM((tm, slots), I32), pltpu.VMEM((tm, slots), I32)],
        compiler_params=_params(("arbitrary", "arbitrary")), name="peer_route_u",
    )(h, h, wq_t, keys, u)


def _peer_v_kernel(parts_ref, g_ref, i1_ref, i2_ref, v_ref, x_ref, o_ref, w_ref, wg_ref, *, tm, blocks):
    c = pl.program_id(1)
    nk = PEER_KEYS

    @pl.when(c == 0)
    def _():
        w_ref[...] = g_ref[...] * _gelu(jnp.sum(parts_ref[...], axis=0))
        o_ref[...] = x_ref[...]
        sub = lax.broadcasted_iota(I32, (nk, w_ref.shape[1]), 0)

        def per_token(t, carry):
            wrow = w_ref[pl.ds(t, 1), :]
            lhs = jnp.where(i1_ref[pl.ds(t, 1), :] == sub, wrow, 0.0).astype(BF16)
            rhs = jnp.where(i2_ref[pl.ds(t, 1), :] == sub, 1.0, 0.0).astype(BF16)
            grid = lax.dot_general(lhs, rhs, (((1,), (1,)), ((), ())), preferred_element_type=F32)
            wg_ref[pl.ds(pl.multiple_of(t * WG_PITCH, 8), nk), :] = grid
            return carry

        lax.fori_loop(0, tm, per_token, 0, unroll=16)

    acc = jnp.zeros(o_ref.shape, F32)
    for bk in range(0, blocks, 2):
        i1 = c * blocks + bk
        lhs = jnp.concatenate([wg_ref[pl.ds(i1, tm, stride=WG_PITCH), :],
                               wg_ref[pl.ds(i1 + 1, tm, stride=WG_PITCH), :]], axis=1).astype(BF16)
        rhs = v_ref[bk:bk + 2].reshape(2 * nk, v_ref.shape[2])
        acc += jnp.dot(lhs, rhs, preferred_element_type=F32)
    o_ref[...] += acc


def _peer_v(parts, g, i1, i2, v, x, tm=256, blocks=32):
    t, d = x.shape
    nchunk_u = parts.shape[0]
    slots = g.shape[1]
    nk = PEER_KEYS
    v3 = v.reshape(nk, nk, d)
    kern = functools.partial(_peer_v_kernel, tm=tm, blocks=blocks)
    row = lambda i, c: (i, 0)
    return pl.pallas_call(
        kern, grid=(t // tm, nk // blocks),
        in_specs=[pl.BlockSpec((nchunk_u, tm, slots), lambda i, c: (0, i, 0)),
                  pl.BlockSpec((tm, slots), row), pl.BlockSpec((tm, slots), row), pl.BlockSpec((tm, slots), row),
                  pl.BlockSpec((blocks, nk, d), lambda i, c: (c, 0, 0)), pl.BlockSpec((tm, d), row)],
        out_specs=pl.BlockSpec((tm, d), row),
        out_shape=jax.ShapeDtypeStruct((t, d), F32),
        scratch_shapes=[pltpu.VMEM((tm, slots), F32), pltpu.VMEM((tm * WG_PITCH, nk), F32)],
        compiler_params=_params(("parallel", "arbitrary")), name="peer_v",
    )(parts, g, i1, i2, v3, x)


IN_SIZES = (1024, 768, 48, 1024, 1024, 1024, 1536, 16, 3072)


def _pad_cols(w, n):
    return jnp.pad(w, ((0, 0), (0, n - w.shape[1])))


def _mixer(x, positions, b, s, p):
    xn = _rmsnorm(x, p["mix_norm_g"])
    offs = [0]
    for n in IN_SIZES:
        offs.append(offs[-1] + n)
    w = [p["w_in"][:, offs[i]:offs[i + 1]] for i in range(len(IN_SIZES))]
    a_q = _matmul(xn, w[0].astype(BF16), BF16, "proj_q")
    a_kv = _matmul(xn, w[1].astype(BF16), BF16, "proj_kv")
    a_g = _matmul(xn, _pad_cols(w[2], LANES).astype(BF16), F32, "proj_ag")
    l_x = _matmul(xn, w[3].astype(BF16), BF16, "proj_lx")
    l_g = _matmul(xn, w[4].astype(BF16), BF16, "proj_lg")
    s_z = _matmul(xn, w[5].astype(BF16), BF16, "proj_sz")
    s_xbc = _matmul(xn, w[6].astype(BF16), BF16, "proj_sxbc")
    s_dt = _matmul(xn, _pad_cols(w[7], LANES).astype(BF16), F32, "proj_sdt")
    m_g = _matmul(xn, w[8].astype(BF16), BF16, "proj_mg", tn=1024)

    q, kc, vc, ks, vs, kw, vw = _nsa_prep(a_q, a_kv, positions, p["q_norm_g"], p["k_norm_g"], b, s)
    k_cmp, v_cmp = _compress(kc, vc, p["cmp_pe_k"], p["cmp_pe_v"], p["cmp_k_w1"], p["cmp_k_w2"],
                             p["cmp_v_w1"], p["cmp_v_w2"], p["k_norm_g"])
    o_c, bias = _cmp_select(q, k_cmp, v_cmp, s)
    o_s = _sel_attn(q, bias, ks, vs, s)
    o_w = _win_attn(q, kw, vw, s)
    y_b = _rglru(l_x, l_g, positions, p["lru_conv_w"], p["lru_conv_b"], p["lru_w_r"], p["lru_b_r"],
                 p["lru_w_i"], p["lru_b_i"], p["lru_lambda"], b, s)
    y_c = _ssd(s_z, s_xbc, s_dt, p["ssm_conv_w"], p["ssm_conv_b"], p["ssm_dt_bias"], p["ssm_a_log"],
               p["ssm_d"], p["ssm_norm_g"], b, s)
    return _merge(o_c, o_s, o_w, a_g, y_b, y_c, m_g, x, p["w_branch"], p["w_out"])


def _peer(x, p):
    h = _rmsnorm(x, p["ffn_norm_g"])
    i1, i2, g, act = _peer_route_u(h, p["peer_w_q"], p["peer_sub_keys"], p["peer_u"].astype(BF16))
    return _peer_v(act[None], g, i1, i2, p["peer_v"].astype(BF16), x)


_LAYER_PARAMS = ("mix_norm_g", "w_in", "q_norm_g", "k_norm_g", "cmp_pe_k", "cmp_pe_v", "cmp_k_w1", "cmp_k_w2",
                 "cmp_v_w1", "cmp_v_w2", "lru_conv_w", "lru_conv_b", "lru_w_r", "lru_b_r", "lru_w_i", "lru_b_i",
                 "lru_lambda", "ssm_conv_w", "ssm_conv_b", "ssm_dt_bias", "ssm_a_log", "ssm_d", "ssm_norm_g",
                 "w_branch", "w_out", "ffn_norm_g", "peer_w_q", "peer_sub_keys", "peer_u", "peer_v")


def kernel(x, positions, mix_norm_g, w_in, q_norm_g, k_norm_g, cmp_pe_k, cmp_pe_v, cmp_k_w1, cmp_k_w2, cmp_v_w1, cmp_v_w2, lru_conv_w, lru_conv_b, lru_w_r, lru_b_r, lru_w_i, lru_b_i, lru_lambda, ssm_conv_w, ssm_conv_b, ssm_dt_bias, ssm_a_log, ssm_d, ssm_norm_g, w_branch, w_out, ffn_norm_g, peer_w_q, peer_sub_keys, peer_u, peer_v):
    stacked = dict(zip(_LAYER_PARAMS, (mix_norm_g, w_in, q_norm_g, k_norm_g, cmp_pe_k, cmp_pe_v, cmp_k_w1,
                                       cmp_k_w2, cmp_v_w1, cmp_v_w2, lru_conv_w, lru_conv_b, lru_w_r, lru_b_r,
                                       lru_w_i, lru_b_i, lru_lambda, ssm_conv_w, ssm_conv_b, ssm_dt_bias,
                                       ssm_a_log, ssm_d, ssm_norm_g, w_branch, w_out, ffn_norm_g, peer_w_q,
                                       peer_sub_keys, peer_u, peer_v)))
    b, s, d = x.shape
    xf = x.reshape(b * s, d).astype(F32)
    for layer in range(mix_norm_g.shape[0]):
        p = {name: arr[layer] for name, arr in stacked.items()}
        xf = _mixer(xf, positions, b, s, p)
        xf = _peer(xf, p)
    return xf.reshape(b, s, d).astype(x.dtype)
```

```python
import functools
import math

import jax
import jax.numpy as jnp
from jax import lax
from jax.experimental import pallas as pl
from jax.experimental.pallas import tpu as pltpu

F32 = jnp.float32
BF16 = jnp.bfloat16
I32 = jnp.int32

D_MODEL = 1024
HEAD_DIM = 64
N_HEADS = 16
N_KV_GROUPS = 2
HEADS_PER_GROUP = 8
CMP_BLOCK = 32
CMP_STRIDE = 16
CMP_HIDDEN = 256
SEL_BLOCK = 64
N_SELECT = 16
WINDOW = 512
SEL_FORCE = 100.0
ROPE_THETA = 10000.0
SCALE = HEAD_DIM ** -0.5
LOG2E = math.log2(math.e)
LRU_HEADS = 16
LRU_BLOCK = 64
LRU_C = 8.0
CONV_WIDTH = 4
SSM_HEADS = 16
SSM_HEAD_DIM = 64
SSM_GROUPS = 2
SSM_STATE = 128
SSM_CHUNK = 128
SSM_INNER = 1024
PEER_HEADS = 8
PEER_KEYS = 128
PEER_HALF = 128
PEER_TOPK = 16
EPS = 1e-6
NEG = -1e30
LANES = 128

VMEM_LIMIT = 56 * 1024 * 1024
SEL_LOOKAHEAD = 3
WG_PITCH = PEER_KEYS + 8
ROUTE_LANES = 128


def _params(sem):
    return pltpu.CompilerParams(dimension_semantics=sem, vmem_limit_bytes=VMEM_LIMIT)


def _gelu(x):
    return 0.5 * x * (1.0 + jnp.tanh(math.sqrt(2.0 / math.pi) * (x + 0.044715 * x * x * x)))


def _sigmoid(x):
    return 0.5 * jnp.tanh(0.5 * x) + 0.5


def _softplus(x):
    return jnp.maximum(x, 0.0) + jnp.log(1.0 + jnp.exp(-jnp.abs(x)))


def _rmsnorm_kernel(x_ref, g_ref, o_ref):
    x = x_ref[...].astype(F32)
    ms = jnp.mean(x * x, axis=-1, keepdims=True)
    o_ref[...] = (x * lax.rsqrt(ms + EPS) * g_ref[...]).astype(o_ref.dtype)


def _rmsnorm(x, g, tm=512):
    t, d = x.shape
    return pl.pallas_call(
        _rmsnorm_kernel, grid=(t // tm,),
        in_specs=[pl.BlockSpec((tm, d), lambda i: (i, 0)), pl.BlockSpec((1, d), lambda i: (0, 0))],
        out_specs=pl.BlockSpec((tm, d), lambda i: (i, 0)),
        out_shape=jax.ShapeDtypeStruct((t, d), BF16),
        compiler_params=_params(("parallel",)), name="rmsnorm")(x, g.reshape(1, d).astype(F32))


def _norm_rope(x, g, cos, sin_signed, bd):
    ms = jnp.dot((x * x).astype(BF16), bd, preferred_element_type=F32)
    y = x * lax.rsqrt(ms + EPS) * g
    lane = lax.broadcasted_iota(I32, y.shape, 1)
    first_half = (lane % HEAD_DIM) < (HEAD_DIM // 2)
    partner = jnp.where(first_half, pltpu.roll(y, LANES - HEAD_DIM // 2, 1), pltpu.roll(y, HEAD_DIM // 2, 1))
    return y * cos + partner * sin_signed


def _nsa_prep_kernel(aq_ref, akv_ref, pos_ref, inv_ref, sgn_ref, gq_ref, gk_ref, bd_ref,
                     q_ref, kc_ref, vc_ref, ks_ref, vs_ref, kw_ref, vw_ref):
    ang = pos_ref[...].astype(F32) * inv_ref[...]
    cos = jnp.cos(ang)
    sin_signed = jnp.sin(ang) * sgn_ref[...]
    bd = bd_ref[...]
    for c in range(D_MODEL // LANES):
        x = aq_ref[:, c * LANES:(c + 1) * LANES].astype(F32)
        y = (_norm_rope(x, gq_ref[...], cos, sin_signed, bd) * (SCALE * LOG2E)).astype(q_ref.dtype)
        q_ref[0, 2 * c] = y[:, :HEAD_DIM]
        q_ref[0, 2 * c + 1] = y[:, HEAD_DIM:]
    outs = (kc_ref, vc_ref, ks_ref, vs_ref, kw_ref, vw_ref)
    for c, o_ref in enumerate(outs):
        x = akv_ref[:, c * LANES:(c + 1) * LANES]
        if c % 2 == 0:
            y = _norm_rope(x.astype(F32), gk_ref[...], cos, sin_signed, bd).astype(o_ref.dtype)
        else:
            y = x.astype(o_ref.dtype)
        o_ref[0, 0] = y[:, :HEAD_DIM]
        o_ref[0, 1] = y[:, HEAD_DIM:]


def _nsa_prep(a_q, a_kv, positions, q_norm_g, k_norm_g, b, s, tm=512):
    half = HEAD_DIM // 2
    lane = jnp.arange(LANES)
    inv = (ROPE_THETA ** (-((lane % half).astype(F32)) / half)).reshape(1, LANES)
    sgn = jnp.where((lane % HEAD_DIM) < half, -1.0, 1.0).astype(F32).reshape(1, LANES)
    bd = jnp.where((lane[:, None] // HEAD_DIM) == (lane[None, :] // HEAD_DIM), 1.0 / HEAD_DIM, 0.0).astype(BF16)
    gq = jnp.tile(q_norm_g.astype(F32), 2).reshape(1, LANES)
    gk = jnp.tile(k_norm_g.astype(F32), 2).reshape(1, LANES)
    nt = s // tm
    row = lambda i: (i, 0)
    const = lambda i: (0, 0)
    kv_shape = jax.ShapeDtypeStruct((b, N_KV_GROUPS, s, HEAD_DIM), BF16)
    kv_spec = pl.BlockSpec((1, N_KV_GROUPS, tm, HEAD_DIM), lambda i: (i // nt, 0, i % nt, 0))
    return pl.pallas_call(
        _nsa_prep_kernel, grid=(b * nt,),
        in_specs=[pl.BlockSpec((tm, D_MODEL), row), pl.BlockSpec((tm, 6 * LANES), row),
                  pl.BlockSpec((tm, 1), row), pl.BlockSpec((1, LANES), const), pl.BlockSpec((1, LANES), const),
                  pl.BlockSpec((1, LANES), const), pl.BlockSpec((1, LANES), const),
                  pl.BlockSpec((LANES, LANES), const)],
        out_specs=[pl.BlockSpec((1, N_HEADS, tm, HEAD_DIM), lambda i: (i // nt, 0, i % nt, 0))] + [kv_spec] * 6,
        out_shape=[jax.ShapeDtypeStruct((b, N_HEADS, s, HEAD_DIM), BF16)] + [kv_shape] * 6,
        compiler_params=_params(("parallel",)), name="nsa_prep",
    )(a_q, a_kv, positions.reshape(b * s, 1).astype(I32), inv, sgn, gq, gk, bd)


def _compress_kernel(uk_ref, uv_ref, pek_ref, pev_ref, kw1_ref, kw2_ref, vw1_ref, vw2_ref, gk_ref,
                     kc_ref, vc_ref):
    half = CMP_STRIDE * HEAD_DIM

    def mlp(u, pe, w1_ref, w2_ref):
        n = u.shape[0]
        ha = jnp.dot(u, w1_ref[:half, :], preferred_element_type=F32)
        hb = jnp.dot(u, w1_ref[half:, :], preferred_element_type=F32)
        bias = jnp.dot(pe, w1_ref[...], preferred_element_type=F32)[0:1, :]
        pre = ha + pltpu.roll(hb, n - 1, 0) + bias
        return jnp.dot(_gelu(pre).astype(BF16), w2_ref[...], preferred_element_type=F32)

    k = mlp(uk_ref[0, 0], pek_ref[...], kw1_ref, kw2_ref)
    ms = jnp.mean(k * k, axis=-1, keepdims=True)
    kc_ref[0, 0] = (k * lax.rsqrt(ms + EPS) * gk_ref[...]).astype(kc_ref.dtype)
    vc_ref[0, 0] = mlp(uv_ref[0, 0], pev_ref[...], vw1_ref, vw2_ref).astype(vc_ref.dtype)


def _compress(kc, vc, pe_k, pe_v, kw1, kw2, vw1, vw2, k_norm_g):
    b, g, s, dh = kc.shape
    ng = s // CMP_STRIDE
    wide = CMP_BLOCK * dh
    uk = kc.reshape(b, g, ng, CMP_STRIDE * dh)
    uv = vc.reshape(b, g, ng, CMP_STRIDE * dh)
    pek = jnp.zeros((8, wide), BF16).at[0].set(pe_k.reshape(wide).astype(BF16))
    pev = jnp.zeros((8, wide), BF16).at[0].set(pe_v.reshape(wide).astype(BF16))
    u_spec = pl.BlockSpec((1, 1, ng, CMP_STRIDE * dh), lambda i, j: (i, j, 0, 0))
    c2 = lambda i, j: (0, 0)
    o_spec = pl.BlockSpec((1, 1, ng, dh), lambda i, j: (i, j, 0, 0))
    o_shape = jax.ShapeDtypeStruct((b, g, ng, dh), BF16)
    return pl.pallas_call(
        _compress_kernel, grid=(b, g),
        in_specs=[u_spec, u_spec, pl.BlockSpec((8, wide), c2), pl.BlockSpec((8, wide), c2),
                  pl.BlockSpec((wide, CMP_HIDDEN), c2), pl.BlockSpec((CMP_HIDDEN, dh), c2),
                  pl.BlockSpec((wide, CMP_HIDDEN), c2), pl.BlockSpec((CMP_HIDDEN, dh), c2),
                  pl.BlockSpec((1, dh), c2)],
        out_specs=[o_spec, o_spec], out_shape=[o_shape, o_shape],
        compiler_params=_params(("parallel", "parallel")), name="nsa_compress",
    )(uk, uv, pek, pev, kw1.astype(BF16), kw2.astype(BF16), vw1.astype(BF16), vw2.astype(BF16),
      k_norm_g.reshape(1, dh).astype(F32))


def _store_head_pair(o_ref, h, o_t, held):
    if h % 2 == 0:
        return o_t
    pair = jnp.concatenate([held, o_t], axis=0).T
    o_ref[0, :, (h - 1) * HEAD_DIM:(h + 1) * HEAD_DIM] = pair.astype(o_ref.dtype)
    return None


def _cmp_select_kernel(q_ref, kc_ref, vc_ref, ovt_ref, o_ref, bias_ref, *, tq, n_cmp, n_pick):
    i = pl.program_id(2)
    chunk = min(LANES, kc_ref.shape[2])
    visible = ((i + 1) * tq - CMP_BLOCK) // CMP_STRIDE + 1
    nchunks = jnp.clip((visible + chunk - 1) // chunk, 1, kc_ref.shape[2] // chunk)
    for n in range(1, kc_ref.shape[2] // chunk + 1):
        pl.when(nchunks == n)(functools.partial(
            _cmp_select_keys, q_ref, kc_ref, vc_ref, ovt_ref, o_ref, bias_ref, nc=n * chunk, i=i, tq=tq,
            n_cmp=n_cmp, n_pick=n_pick))


def _cmp_select_keys(q_ref, kc_ref, vc_ref, ovt_ref, o_ref, bias_ref, *, nc, i, tq, n_cmp, n_pick):
    hg = q_ref.shape[1]
    nsel = ovt_ref.shape[0]
    kc = kc_ref[0, 0, :nc, :]
    vct = vc_ref[0, 0, :nc, :].astype(F32).T.astype(BF16)
    ovt = ovt_ref[:, :nc]
    c = lax.broadcasted_iota(I32, (nc, tq), 0)
    t = i * tq + lax.broadcasted_iota(I32, (nc, tq), 1)
    mask = ((CMP_STRIDE * c + CMP_BLOCK - 1) <= t) & (c < n_cmp)
    score = lambda h: lax.dot_general(kc, q_ref[0, h], (((1,), (1,)), ((), ())), preferred_element_type=F32)
    st_next = score(0)
    psum = jnp.zeros((nc, tq), F32)
    held = None
    for h in range(hg):
        st = st_next
        if h + 1 < hg:
            st_next = score(h + 1)
        st = jnp.where(mask, st, NEG)
        m = jnp.max(st, axis=0, keepdims=True)
        p = jnp.where(mask, jnp.exp2(st - m), 0.0)
        l = jnp.sum(p, axis=0, keepdims=True)
        p = p * jnp.where(l > 0.0, 1.0 / l, 0.0)
        o = jnp.dot(vct, p.astype(BF16), preferred_element_type=F32)
        held = _store_head_pair(o_ref, h, o, held)
        psum = psum + p

    hi = psum.astype(BF16)
    lo = (psum - hi.astype(F32)).astype(BF16)
    imp = jnp.dot(ovt, hi, preferred_element_type=F32) + jnp.dot(ovt, lo, preferred_element_type=F32)
    j = lax.broadcasted_iota(I32, (nsel, tq), 0)
    tt = i * tq + lax.broadcasted_iota(I32, (nsel, tq), 1)
    cur = tt // SEL_BLOCK
    forced = (j == 0) | (j == cur) | (j == cur - 1)
    sc = jnp.where(forced, SEL_FORCE, jnp.where(j * SEL_BLOCK <= tt, imp, -1.0))
    picked = jnp.zeros((nsel, tq), jnp.bool_)
    for _ in range(n_pick):
        mx = jnp.max(sc, axis=0, keepdims=True)
        idx = jnp.min(jnp.where(sc == mx, j, nsel), axis=0, keepdims=True)
        hit = j == idx
        picked = picked | hit
        sc = jnp.where(hit, -jnp.inf, sc)
    bias_ref[0, 0] = jnp.where(picked & (j <= cur), 0.0, NEG).T.astype(bias_ref.dtype)


def _cmp_select(q, k_cmp, v_cmp, s, tq=512):
    b, nh, _, dh = q.shape
    g, hg = N_KV_GROUPS, HEADS_PER_GROUP
    nc = k_cmp.shape[2]
    n_cmp = (s - CMP_BLOCK) // CMP_STRIDE + 1
    nsel = s // SEL_BLOCK
    n_pick = min(N_SELECT, nsel)
    cs = CMP_STRIDE * jnp.arange(nc)
    ss = SEL_BLOCK * jnp.arange(nsel)
    ovt = jnp.clip(jnp.minimum(cs[None, :] + CMP_BLOCK, ss[:, None] + SEL_BLOCK)
                   - jnp.maximum(cs[None, :], ss[:, None]), 0).astype(F32) / CMP_BLOCK
    ovt = jnp.where(jnp.arange(nc)[None, :] < n_cmp, ovt, 0.0).astype(BF16)
    kern = functools.partial(_cmp_select_kernel, tq=tq, n_cmp=n_cmp, n_pick=n_pick)
    return pl.pallas_call(
        kern, grid=(b, g, s // tq),
        in_specs=[pl.BlockSpec((1, hg, tq, dh), lambda bi, gi, i: (bi, gi, i, 0)),
                  pl.BlockSpec((1, 1, nc, dh), lambda bi, gi, i: (bi, gi, 0, 0)),
                  pl.BlockSpec((1, 1, nc, dh), lambda bi, gi, i: (bi, gi, 0, 0)),
                  pl.BlockSpec((nsel, nc), lambda bi, gi, i: (0, 0))],
        out_specs=[pl.BlockSpec((1, tq, hg * dh), lambda bi, gi, i: (bi, i, gi)),
                   pl.BlockSpec((1, 1, tq, nsel), lambda bi, gi, i: (bi, gi, i, 0))],
        out_shape=[jax.ShapeDtypeStruct((b, s, nh * dh), BF16),
                   jax.ShapeDtypeStruct((b, g, s, nsel), BF16)],
        compiler_params=_params(("parallel", "parallel", "parallel")), name="nsa_cmp_select",
    )(q, k_cmp, v_cmp, ovt)


def _sel_attn_kernel(it_ref, jt_ref, q_ref, bias_ref, k_ref, v_ref, o_ref, qa_ref, m_ref, acc_ref, *, tq, tk):
    pid = pl.program_id(2)
    i = it_ref[pid]
    j = jt_ref[pid]
    hg = q_ref.shape[1]
    nsel = bias_ref.shape[3]
    last_j = ((i + 1) * tq - 1) // tk

    @pl.when(j == 0)
    def _():
        for h in range(hg):
            qa_ref[h, :, :nsel] = bias_ref[0, 0]
            qa_ref[h, :, nsel:] = q_ref[0, h]
        m_ref[...] = jnp.full(m_ref.shape, -jnp.inf, F32)
        acc_ref[...] = jnp.zeros(acc_ref.shape, F32)

    def accumulate(diagonal):
        key = j * tk + lax.broadcasted_iota(I32, (tk, nsel), 0)
        blk = lax.broadcasted_iota(I32, (tk, nsel), 1)
        onehot = jnp.where(key // SEL_BLOCK == blk, 1.0, 0.0).astype(BF16)
        ka = jnp.concatenate([onehot, k_ref[0, 0]], axis=1)
        va = jnp.concatenate([v_ref[0, 0].astype(F32), jnp.ones((tk, LANES - HEAD_DIM), F32)], axis=1)
        vat = va.T.astype(BF16)
        if diagonal:
            kp = j * tk + lax.broadcasted_iota(I32, (tk, tq), 0)
            t = i * tq + lax.broadcasted_iota(I32, (tk, tq), 1)
            causal = kp <= t
        score = lambda h: lax.dot_general(ka, qa_ref[h], (((1,), (1,)), ((), ())), preferred_element_type=F32)
        ahead = [score(h) for h in range(SEL_LOOKAHEAD)]
        for h in range(hg):
            st = ahead.pop(0)
            if h + SEL_LOOKAHEAD < hg:
                ahead.append(score(h + SEL_LOOKAHEAD))
            if diagonal:
                st = jnp.where(causal, st, NEG)
            m_old = m_ref[h]
            m_new = jnp.maximum(m_old, jnp.max(st, axis=0, keepdims=True))
            alpha = jnp.exp2(m_old - m_new)
            p = jnp.exp2(st - m_new)
            acc_ref[h] = alpha * acc_ref[h] + jnp.dot(vat, p.astype(BF16), preferred_element_type=F32)
            m_ref[h] = m_new

    @pl.when(j < last_j)
    def _():
        accumulate(False)

    @pl.when(j == last_j)
    def _():
        accumulate(True)
        held = None
        for h in range(hg):
            acc = acc_ref[h]
            o = acc[:HEAD_DIM, :] / acc[HEAD_DIM:HEAD_DIM + 1, :]
            held = _store_head_pair(o_ref, h, o, held)


def _sel_attn(q, bias, k_s, v_s, s, tq=512, tk=512):
    assert tq <= tk
    b, nh, _, dh = q.shape
    g, hg = N_KV_GROUPS, HEADS_PER_GROUP
    nsel = bias.shape[3]
    tk = min(tk, s)
    pairs = [(i, j) for i in range(s // tq) for j in range(((i + 1) * tq - 1) // tk + 1)]
    it = jnp.asarray([pr[0] for pr in pairs], I32)
    jt = jnp.asarray([pr[1] for pr in pairs], I32)
    q_map = lambda bi, gi, pid, it_ref, jt_ref: (bi, gi, it_ref[pid], 0)
    kv_map = lambda bi, gi, pid, it_ref, jt_ref: (bi, gi, jt_ref[pid], 0)
    kern = functools.partial(_sel_attn_kernel, tq=tq, tk=tk)
    grid_spec = pltpu.PrefetchScalarGridSpec(
        num_scalar_prefetch=2, grid=(b, g, len(pairs)),
        in_specs=[pl.BlockSpec((1, hg, tq, dh), q_map), pl.BlockSpec((1, 1, tq, nsel), q_map),
                  pl.BlockSpec((1, 1, tk, dh), kv_map), pl.BlockSpec((1, 1, tk, dh), kv_map)],
        out_specs=pl.BlockSpec((1, tq, hg * dh), lambda bi, gi, pid, it_ref, jt_ref: (bi, it_ref[pid], gi)),
        scratch_shapes=[pltpu.VMEM((hg, tq, nsel + dh), BF16), pltpu.VMEM((hg, 1, tq), F32),
                        pltpu.VMEM((hg, LANES, tq), F32)])
    return pl.pallas_call(
        kern, grid_spec=grid_spec, out_shape=jax.ShapeDtypeStruct((b, s, nh * dh), BF16),
        compiler_params=_params(("parallel", "parallel", "arbitrary")), name="nsa_sel_attn",
    )(it, jt, q, bias, k_s, v_s)


def _win_attn_kernel(q_ref, *refs, tq, nwin):
    k_refs, v_refs, o_ref = refs[:nwin], refs[nwin:2 * nwin], refs[2 * nwin]
    i = pl.program_id(2)
    hg = q_ref.shape[1]
    nk = nwin * tq
    k = jnp.concatenate([r[0, 0] for r in k_refs], axis=0)
    vt = jnp.concatenate([r[0, 0] for r in v_refs], axis=0).astype(F32).T.astype(BF16)
    sp = i * tq - WINDOW + lax.broadcasted_iota(I32, (nk, tq), 0)
    t = i * tq + lax.broadcasted_iota(I32, (nk, tq), 1)
    mask = (sp <= t) & (sp > t - WINDOW) & (sp >= 0)
    score = lambda h: lax.dot_general(k, q_ref[0, h], (((1,), (1,)), ((), ())), preferred_element_type=F32)
    st_next = score(0)
    held = None
    for h in range(hg):
        st = st_next
        if h + 1 < hg:
            st_next = score(h + 1)
        st = jnp.where(mask, st, NEG)
        m = jnp.max(st, axis=0, keepdims=True)
        p = jnp.where(mask, jnp.exp2(st - m), 0.0)
        l = jnp.sum(p, axis=0, keepdims=True)
        o = jnp.dot(vt, p.astype(BF16), preferred_element_type=F32) / l
        held = _store_head_pair(o_ref, h, o, held)


def _win_attn(q, k_w, v_w, s, tq=512):
    b, nh, _, dh = q.shape
    g, hg = N_KV_GROUPS, HEADS_PER_GROUP
    nwin = WINDOW // tq + 1
    pad = ((0, 0), (0, 0), (WINDOW, 0), (0, 0))
    kp, vp = jnp.pad(k_w, pad), jnp.pad(v_w, pad)
    kv_specs = [pl.BlockSpec((1, 1, tq, dh), functools.partial(lambda bi, gi, i, w: (bi, gi, i + w, 0), w=w))
                for w in range(nwin)]
    kern = functools.partial(_win_attn_kernel, tq=tq, nwin=nwin)
    return pl.pallas_call(
        kern, grid=(b, g, s // tq),
        in_specs=[pl.BlockSpec((1, hg, tq, dh), lambda bi, gi, i: (bi, gi, i, 0))] + kv_specs + kv_specs,
        out_specs=pl.BlockSpec((1, tq, hg * dh), lambda bi, gi, i: (bi, i, gi)),
        out_shape=jax.ShapeDtypeStruct((b, s, nh * dh), BF16),
        compiler_params=_params(("parallel", "parallel", "parallel")), name="nsa_win_attn",
    )(q, *([kp] * nwin), *([vp] * nwin))


def _causal_conv(x, tail, w_ref, b_ref):
    n = x.shape[0]
    xx = jnp.concatenate([tail, x], axis=0)
    out = b_ref[...] + w_ref[CONV_WIDTH - 1:CONV_WIDTH, :] * x
    for k in range(CONV_WIDTH - 1):
        off = 8 - (CONV_WIDTH - 1) + k
        out = out + w_ref[k:k + 1, :] * xx[off:off + n, :]
    return out


def _rglru_kernel(x_ref, gate_ref, pos_ref, cw_ref, cb_ref, wr_ref, br_ref, wi_ref, bi_ref, lam_ref,
                  o_ref, tail_ref, h_ref):
    @pl.when(pl.program_id(1) == 0)
    def _():
        tail_ref[...] = jnp.zeros(tail_ref.shape, F32)
        h_ref[...] = jnp.zeros(h_ref.shape, F32)

    n = x_ref.shape[0]
    x = x_ref[...].astype(F32)
    xc = _causal_conv(x, tail_ref[...], cw_ref, cb_ref)
    tail_ref[...] = x[n - 8:, :]
    xcb = xc.astype(BF16)
    r = _sigmoid(jnp.dot(xcb, wr_ref[...], preferred_element_type=F32) + br_ref[...])
    gi = _sigmoid(jnp.dot(xcb, wi_ref[...], preferred_element_type=F32) + bi_ref[...])
    log_a = -LRU_C * r * _softplus(-lam_ref[...])
    reset = pos_ref[...] == 0
    a = jnp.where(reset, 0.0, jnp.exp(log_a))
    mult = jnp.where(reset, 1.0, jnp.sqrt(jnp.maximum(1.0 - jnp.exp(2.0 * log_a), 0.0)))
    bb = mult * (gi * xc)
    sub = lax.broadcasted_iota(I32, a.shape, 0) % 8
    d = 1
    while d < 8:
        a_sh = pltpu.roll(a, d, 0)
        b_sh = pltpu.roll(bb, d, 0)
        live = sub >= d
        bb = jnp.where(live, a * b_sh + bb, bb)
        a = jnp.where(live, a * a_sh, a)
        d *= 2
    carry = h_ref[...]
    hs = []
    for g in range(n // 8):
        h = bb[8 * g:8 * g + 8, :] + a[8 * g:8 * g + 8, :] * carry
        carry = h[7:8, :]
        hs.append(h)
    h_ref[...] = carry
    o_ref[...] = (jnp.concatenate(hs, axis=0) * _gelu(gate_ref[...].astype(F32))).astype(o_ref.dtype)


def _block_diag(w):
    nb, bs, _ = w.shape
    eye = jnp.eye(nb, dtype=w.dtype)
    return (w[:, :, None, :] * eye[:, None, :, None]).reshape(nb * bs, nb * bs)


def _rglru(l_x, l_g, positions, conv_w, conv_b, w_r, b_r, w_i, b_i, lam, b, s, ts=256):
    d = l_x.shape[1]
    nt = s // ts
    row = lambda bi, i: (bi * nt + i, 0)
    c2 = lambda bi, i: (0, 0)
    vec = lambda v: v.reshape(1, d).astype(F32)
    return pl.pallas_call(
        _rglru_kernel, grid=(b, nt),
        in_specs=[pl.BlockSpec((ts, d), row), pl.BlockSpec((ts, d), row), pl.BlockSpec((ts, 1), row),
                  pl.BlockSpec((CONV_WIDTH, d), c2), pl.BlockSpec((1, d), c2),
                  pl.BlockSpec((d, d), c2), pl.BlockSpec((1, d), c2),
                  pl.BlockSpec((d, d), c2), pl.BlockSpec((1, d), c2), pl.BlockSpec((1, d), c2)],
        out_specs=pl.BlockSpec((ts, d), row),
        out_shape=jax.ShapeDtypeStruct((b * s, d), BF16),
        scratch_shapes=[pltpu.VMEM((8, d), F32), pltpu.VMEM((1, d), F32)],
        compiler_params=_params(("parallel", "arbitrary")), name="rglru",
    )(l_x, l_g, positions.reshape(b * s, 1).astype(I32), conv_w.astype(F32), vec(conv_b),
      _block_diag(w_r).astype(BF16), vec(b_r), _block_diag(w_i).astype(BF16), vec(b_i), vec(lam))


def _ssd_kernel(z_ref, xbc_ref, dt_ref, cw_ref, cb_ref, dtb_ref, alog_ref, dfull_ref, ng_ref,
                o_ref, tail_ref, state_ref, y_ref):
    @pl.when(pl.program_id(1) == 0)
    def _():
        tail_ref[...] = jnp.zeros(tail_ref.shape, F32)
        state_ref[...] = jnp.zeros(state_ref.shape, F32)

    n = xbc_ref.shape[0]
    hg = SSM_HEADS // SSM_GROUPS
    x = xbc_ref[...].astype(F32)
    xc = _causal_conv(x, tail_ref[...], cw_ref, cb_ref)
    tail_ref[...] = x[n - 8:, :]
    xc = xc * _sigmoid(xc)
    xs = xc[:, :SSM_INNER]
    dt = _softplus(dt_ref[...] + dtb_ref[...])
    adt = dt * (-jnp.exp(alog_ref[...]))
    row = lax.broadcasted_iota(I32, adt.shape, 0)
    acs = adt
    d = 1
    while d < n:
        acs = acs + jnp.where(row >= d, pltpu.roll(acs, d, 0), 0.0)
        d *= 2
    acs_t = acs.T
    li = lax.broadcasted_iota(I32, (n, n), 0)
    si = lax.broadcasted_iota(I32, (n, n), 1)
    tri = li >= si
    for g in range(SSM_GROUPS):
        bm = xc[:, SSM_INNER + g * SSM_STATE:SSM_INNER + (g + 1) * SSM_STATE].astype(BF16)
        cm = xc[:, SSM_INNER + (SSM_GROUPS + g) * SSM_STATE:SSM_INNER + (SSM_GROUPS + g + 1) * SSM_STATE].astype(BF16)
        cb = lax.dot_general(cm, bm, (((1,), (1,)), ((), ())), preferred_element_type=F32)
        bm_t = bm.T
        for hh in range(hg):
            h = g * hg + hh
            acol = jnp.broadcast_to(acs[:, h:h + 1], (n, n))
            arow = acs_t[h:h + 1, :]
            decay = jnp.exp(jnp.where(tri, acol - arow, NEG))
            acol_p = acol[:, :SSM_HEAD_DIM]
            xh = xs[:, h * SSM_HEAD_DIM:(h + 1) * SSM_HEAD_DIM] * dt[:, h:h + 1]
            a_last = acol_p[n - 1:n, :]
            y = jnp.dot((cb * decay).astype(BF16), xh.astype(BF16), preferred_element_type=F32)
            st = state_ref[h]
            y = y + jnp.dot(cm, st.astype(BF16), preferred_element_type=F32) * jnp.exp(acol_p)
            upd = jnp.dot(bm_t, (xh * jnp.exp(a_last - acol_p)).astype(BF16), preferred_element_type=F32)
            state_ref[h] = jnp.exp(a_last) * st + upd
            y_ref[:, h * SSM_HEAD_DIM:(h + 1) * SSM_HEAD_DIM] = y
    z = z_ref[...].astype(F32)
    y = (y_ref[...] + dfull_ref[...] * xs) * (z * _sigmoid(z))
    gw = SSM_INNER // SSM_GROUPS
    for g in range(SSM_GROUPS):
        yg = y[:, g * gw:(g + 1) * gw]
        ms = jnp.mean(yg * yg, axis=-1, keepdims=True)
        o_ref[:, g * gw:(g + 1) * gw] = (yg * lax.rsqrt(ms + EPS) * ng_ref[:, g * gw:(g + 1) * gw]).astype(o_ref.dtype)


def _ssd(s_z, s_xbc, s_dt, conv_w, conv_b, dt_bias, a_log, d_skip, norm_g, b, s):
    n = SSM_CHUNK
    nt = s // n
    c = s_xbc.shape[1]
    row = lambda bi, i: (bi * nt + i, 0)
    c2 = lambda bi, i: (0, 0)
    pad_h = lambda v: jnp.zeros((1, LANES), F32).at[0, :SSM_HEADS].set(v.astype(F32))
    dfull = jnp.repeat(d_skip.astype(F32), SSM_HEAD_DIM).reshape(1, SSM_INNER)
    return pl.pallas_call(
        _ssd_kernel, grid=(b, nt),
        in_specs=[pl.BlockSpec((n, SSM_INNER), row), pl.BlockSpec((n, c), row), pl.BlockSpec((n, LANES), row),
                  pl.BlockSpec((CONV_WIDTH, c), c2), pl.BlockSpec((1, c), c2),
                  pl.BlockSpec((1, LANES), c2), pl.BlockSpec((1, LANES), c2),
                  pl.BlockSpec((1, SSM_INNER), c2), pl.BlockSpec((1, SSM_INNER), c2)],
        out_specs=pl.BlockSpec((n, SSM_INNER), row),
        out_shape=jax.ShapeDtypeStruct((b * s, SSM_INNER), BF16),
        scratch_shapes=[pltpu.VMEM((8, c), F32), pltpu.VMEM((SSM_HEADS, SSM_STATE, SSM_HEAD_DIM), F32),
                        pltpu.VMEM((n, SSM_INNER), F32)],
        compiler_params=_params(("parallel", "arbitrary")), name="ssd",
    )(s_z, s_xbc, s_dt, conv_w.astype(F32), conv_b.reshape(1, c).astype(F32), pad_h(dt_bias), pad_h(a_log),
      dfull, norm_g.reshape(1, SSM_INNER).astype(F32))


def _merge_kernel(oc_ref, os_ref, ow_ref, ag_ref, ex_ref, yb_ref, yc_ref, mg_ref, x_ref, wb_ref, wo_ref, o_ref):
    gates = _sigmoid(ag_ref[...])
    hi = gates.astype(BF16)
    lo = (gates - hi.astype(F32)).astype(BF16)
    spread = lambda k: (jnp.dot(hi, ex_ref[k], preferred_element_type=F32)
                        + jnp.dot(lo, ex_ref[k], preferred_element_type=F32))
    ya = (spread(0) * oc_ref[...].astype(F32) + spread(1) * os_ref[...].astype(F32)
          + spread(2) * ow_ref[...].astype(F32))
    d = D_MODEL
    merged = _sigmoid(mg_ref[:, 0:d].astype(F32)) * jnp.dot(ya.astype(BF16), wb_ref[0], preferred_element_type=F32)
    merged += _sigmoid(mg_ref[:, d:2 * d].astype(F32)) * jnp.dot(yb_ref[...], wb_ref[1], preferred_element_type=F32)
    merged += _sigmoid(mg_ref[:, 2 * d:3 * d].astype(F32)) * jnp.dot(yc_ref[...], wb_ref[2], preferred_element_type=F32)
    o_ref[...] = x_ref[...] + jnp.dot(merged.astype(BF16), wo_ref[...], preferred_element_type=F32)


def _merge(o_c, o_s, o_w, a_g, y_b, y_c, m_g, x, w_branch, w_out, tm=256):
    t, d = x.shape
    row = lambda i: (i, 0)
    lane = jnp.arange(LANES)[:, None]
    col = jnp.arange(d)[None, :]
    expand = jnp.stack([(lane == 3 * (col // HEAD_DIM) + k) for k in range(3)]).astype(BF16)
    return pl.pallas_call(
        _merge_kernel, grid=(t // tm,),
        in_specs=[pl.BlockSpec((tm, d), row), pl.BlockSpec((tm, d), row), pl.BlockSpec((tm, d), row),
                  pl.BlockSpec((tm, LANES), row), pl.BlockSpec((3, LANES, d), lambda i: (0, 0, 0)),
                  pl.BlockSpec((tm, d), row), pl.BlockSpec((tm, d), row), pl.BlockSpec((tm, 3 * d), row),
                  pl.BlockSpec((tm, d), row),
                  pl.BlockSpec((3, d, d), lambda i: (0, 0, 0)), pl.BlockSpec((d, d), lambda i: (0, 0))],
        out_specs=pl.BlockSpec((tm, d), row),
        out_shape=jax.ShapeDtypeStruct((t, d), F32),
        compiler_params=_params(("parallel",)), name="merge",
    )(o_c.reshape(t, d), o_s.reshape(t, d), o_w.reshape(t, d), a_g, expand, y_b, y_c, m_g, x,
      w_branch.astype(BF16), w_out.astype(BF16))


class _Interleaver:
    def __init__(self, pieces, every):
        self.pieces, self.every, self.count = list(pieces), every, 0

    def tick(self):
        self.count += 1
        if self.pieces and self.count % self.every == 0:
            self.pieces.pop(0)()

    def drain(self):
        while self.pieces:
            self.pieces.pop(0)()


def _sorting_network(n):
    pairs = []

    def merge(lo, m, r):
        step = r * 2
        if step < m:
            merge(lo, m, step)
            merge(lo + r, m, step)
            pairs.extend((i, i + r) for i in range(lo + r, lo + m - r, step))
        else:
            pairs.append((lo, lo + r))

    def sort(lo, m):
        if m > 1:
            sort(lo, m // 2)
            sort(lo + m // 2, m // 2)
            merge(lo, m, 1)

    sort(0, n)
    return pairs


def _topk_rows(s, k, tick):
    n, lanes = s.shape
    assert n == 8 * k
    sub = lax.broadcasted_iota(I32, (8, lanes), 0)
    vals = [s[8 * j:8 * j + 8, :] for j in range(k)]
    ids = [sub + 8 * j for j in range(k)]
    for count, (a, b) in enumerate(_sorting_network(k)):
        first = (vals[a] > vals[b]) | ((vals[a] == vals[b]) & (ids[a] < ids[b]))
        vals[a], vals[b] = jnp.where(first, vals[a], vals[b]), jnp.where(first, vals[b], vals[a])
        ids[a], ids[b] = jnp.where(first, ids[a], ids[b]), jnp.where(first, ids[b], ids[a])
        if count % 16 == 15:
            tick()
    top_v, top_i = [], []
    for r in range(k):
        best = jnp.max(vals[0], axis=0, keepdims=True)
        row = jnp.min(jnp.where(vals[0] == best, ids[0], n), axis=0, keepdims=True)
        top_v.append(best)
        top_i.append(row)
        won = ids[0] == row
        for d in range(k - 1 - r):
            vals[d] = jnp.where(won, vals[d + 1], vals[d])
            ids[d] = jnp.where(won, ids[d + 1], ids[d])
        tick()
    return jnp.concatenate(top_v, axis=0), jnp.concatenate(top_i, axis=0)


def _top_pairs(s1, s2, k, tick):
    lanes = s1.shape[1]
    sub = lax.broadcasted_iota(I32, (8, lanes), 0)
    lists = [jnp.where(sub <= k // (d + 1) - 1, s1[0:8, :] + s2[d:d + 1, :], -jnp.inf) for d in range(k)]
    tail = s1[8:16, :] + s2[0:1, :]
    tail_pos = (sub + 8) * k
    taken = jnp.zeros((8, lanes), I32)
    vs, aa, bb = [], [], []
    for r in range(k):
        best = jnp.maximum(jnp.max(lists[0], axis=0, keepdims=True), jnp.max(tail, axis=0, keepdims=True))
        head_pos = sub * k + taken
        pos = jnp.minimum(jnp.min(jnp.where(lists[0] == best, head_pos, k * k), axis=0, keepdims=True),
                          jnp.min(jnp.where(tail == best, tail_pos, k * k), axis=0, keepdims=True))
        vs.append(best)
        aa.append(pos // k)
        bb.append(pos % k)
        won = head_pos == pos
        tail = jnp.where(tail_pos == pos, -jnp.inf, tail)
        for d in range(k - 1 - r):
            lists[d] = jnp.where(won, lists[d + 1], lists[d])
        taken = taken + won.astype(I32)
        tick()
    return jnp.concatenate(vs, axis=0), jnp.concatenate(aa, axis=0), jnp.concatenate(bb, axis=0)


def _pick_rows(table, sel, k):
    out = jnp.zeros(sel.shape, table.dtype)
    for a in range(k):
        out = jnp.where(sel == a, table[a:a + 1, :], out)
    return out


def _route_head(qt, keys_ref, tick):
    k = PEER_TOPK
    assert k == 16 and PEER_KEYS == 8 * k
    tops = []
    for half in range(2):
        sc = jnp.dot(keys_ref[half], qt[half * PEER_HALF:(half + 1) * PEER_HALF, :], preferred_element_type=F32)
        tops.append(_topk_rows(sc, k, tick))
    (s1, i1), (s2, i2) = tops
    sc, a_sel, b_sel = _top_pairs(s1, s2, k, tick)
    e = jnp.exp(sc - sc[0:1, :])
    g = e / jnp.sum(e, axis=0, keepdims=True)
    return _pick_rows(i1, a_sel, k), _pick_rows(i2, b_sel, k), g


def _peer_route_u_kernel(hn_ref, hc_ref, wq_ref, keys_ref, u_ref, i1_ref, i2_ref, g_ref, act_ref,
                         i1t_ref, i2t_ref, gt_ref, i1c_ref, i2c_ref, *, blocks):
    o = pl.program_id(0)
    c = pl.program_id(1)
    k = PEER_TOPK

    @pl.when((o == 0) & (c == 0))
    def _():
        i1c_ref[...] = jnp.zeros(i1c_ref.shape, I32)
        i2c_ref[...] = jnp.zeros(i2c_ref.shape, I32)

    @pl.when(c == 0)
    def _():
        act_ref[...] = jnp.zeros(act_ref.shape, F32)

    i1c = i1c_ref[...]
    i2c = i2c_ref[...]
    hc = hc_ref[...]
    acc = [act_ref[...]]

    def piece(pc):
        def run():
            a = lax.dot_general(hc, u_ref[pl.ds(pc * 2 * PEER_KEYS, 2 * PEER_KEYS), :], (((1,), (1,)), ((), ())),
                                preferred_element_type=F32)
            for sub in range(2):
                got = jnp.take_along_axis(a[:, sub * PEER_KEYS:(sub + 1) * PEER_KEYS], i2c, axis=1,
                                          mode="promise_in_bounds")
                acc[0] = jnp.where(i1c == c * blocks + 2 * pc + sub, got, acc[0])
        return run

    tm = hn_ref.shape[0]
    groups = tm // ROUTE_LANES
    hps = keys_ref.shape[0] // 2
    hq = wq_ref.shape[0] // hps
    ticks = hps * groups * (2 * (len(_sorting_network(k)) // 16 + k) + k)
    pieces = _Interleaver([piece(pc) for pc in range(blocks // 2)], every=ticks // (blocks // 2 + 3))
    qt = lax.dot_general(wq_ref[...], hn_ref[...], (((1,), (1,)), ((), ())), preferred_element_type=F32)
    qt = qt.astype(BF16)
    for hd in range(hps):
        rows = pl.ds(pl.multiple_of((c * hps + hd) * k, k), k)
        for gi in range(groups):
            cols = slice(gi * ROUTE_LANES, (gi + 1) * ROUTE_LANES)
            i1, i2, g = _route_head(qt[hd * hq:(hd + 1) * hq, cols], keys_ref.at[pl.ds(2 * hd, 2)], pieces.tick)
            i1t_ref[rows, cols] = i1
            i2t_ref[rows, cols] = i2
            gt_ref[rows, cols] = g
    pieces.drain()
    act_ref[...] = acc[0]

    @pl.when(c == pl.num_programs(1) - 1)
    def _():
        i1n = i1t_ref[...].T
        i2n = i2t_ref[...].T
        i1_ref[...] = i1n
        i2_ref[...] = i2n
        g_ref[...] = gt_ref[...].T
        i1c_ref[...] = i1n
        i2c_ref[...] = i2n


def _peer_route_u(h, w_q, sub_keys, u, tm=512, hps=2):
    t, d = h.shape
    nt = t // tm
    ne = u.shape[0]
    nchunk = PEER_HEADS // hps
    ec = ne // nchunk
    blocks = ec // PEER_KEYS
    hq = hps * (w_q.shape[1] // PEER_HEADS)
    wq_t = w_q.T.astype(BF16)
    keys = sub_keys.reshape(PEER_HEADS * 2, PEER_KEYS, PEER_HALF).astype(BF16)
    slots = PEER_HEADS * PEER_TOPK
    nxt = lambda o, c: (jnp.minimum(o, nt - 1), 0)
    cur = lambda o, c: (jnp.maximum(o - 1, 0), 0)
    kern = functools.partial(_peer_route_u_kernel, blocks=blocks)
    return pl.pallas_call(
        kern, grid=(nt + 1, nchunk),
        in_specs=[pl.BlockSpec((tm, d), nxt), pl.BlockSpec((tm, d), cur),
                  pl.BlockSpec((hq, d), lambda o, c: (c, 0)),
                  pl.BlockSpec((2 * hps, PEER_KEYS, PEER_HALF), lambda o, c: (c, 0, 0)),
                  pl.BlockSpec((ec, d), lambda o, c: (c, 0))],
        out_specs=[pl.BlockSpec((tm, slots), nxt)] * 3 + [pl.BlockSpec((tm, slots), cur)],
        out_shape=[jax.ShapeDtypeStruct((t, slots), I32), jax.ShapeDtypeStruct((t, slots), I32),
                   jax.ShapeDtypeStruct((t, slots), F32), jax.ShapeDtypeStruct((t, slots), F32)],
        scratch_shapes=[pltpu.VMEM((slots, tm), I32), pltpu.VMEM((slots, tm), I32), pltpu.VMEM((slots, tm), F32),
                        pltpu.VMEM((tm, slots), I32), pltpu.VMEM((tm, slots), I32)],
        compiler_params=_params(("arbitrary", "arbitrary")), name="peer_route_u",
    )(h, h, wq_t, keys, u)


def _peer_v_kernel(parts_ref, g_ref, i1_ref, i2_ref, v_ref, x_ref, o_ref, w_ref, wg_ref, *, tm, blocks):
    c = pl.program_id(1)
    nk = PEER_KEYS

    @pl.when(c == 0)
    def _():
        w_ref[...] = g_ref[...] * _gelu(jnp.sum(parts_ref[...], axis=0))
        o_ref[...] = x_ref[...]
        sub = lax.broadcasted_iota(I32, (nk, w_ref.shape[1]), 0)

        def per_token(t, carry):
            wrow = w_ref[pl.ds(t, 1), :]
            lhs = jnp.where(i1_ref[pl.ds(t, 1), :] == sub, wrow, 0.0).astype(BF16)
            rhs = jnp.where(i2_ref[pl.ds(t, 1), :] == sub, 1.0, 0.0).astype(BF16)
            grid = lax.dot_general(lhs, rhs, (((1,), (1,)), ((), ())), preferred_element_type=F32)
            wg_ref[pl.ds(pl.multiple_of(t * WG_PITCH, 8), nk), :] = grid
            return carry

        lax.fori_loop(0, tm, per_token, 0, unroll=64)

    acc = jnp.zeros(o_ref.shape, F32)
    for bk in range(0, blocks, 2):
        i1 = c * blocks + bk
        lhs = jnp.concatenate([wg_ref[pl.ds(i1, tm, stride=WG_PITCH), :],
                               wg_ref[pl.ds(i1 + 1, tm, stride=WG_PITCH), :]], axis=1).astype(BF16)
        rhs = v_ref[bk:bk + 2].reshape(2 * nk, v_ref.shape[2])
        acc += jnp.dot(lhs, rhs, preferred_element_type=F32)
    o_ref[...] += acc


def _peer_v(parts, g, i1, i2, v, x, tm=256, blocks=32):
    t, d = x.shape
    nchunk_u = parts.shape[0]
    slots = g.shape[1]
    nk = PEER_KEYS
    v3 = v.reshape(nk, nk, d)
    kern = functools.partial(_peer_v_kernel, tm=tm, blocks=blocks)
    row = lambda i, c: (i, 0)
    return pl.pallas_call(
        kern, grid=(t // tm, nk // blocks),
        in_specs=[pl.BlockSpec((nchunk_u, tm, slots), lambda i, c: (0, i, 0)),
                  pl.BlockSpec((tm, slots), row), pl.BlockSpec((tm, slots), row), pl.BlockSpec((tm, slots), row),
                  pl.BlockSpec((blocks, nk, d), lambda i, c: (c, 0, 0)), pl.BlockSpec((tm, d), row)],
        out_specs=pl.BlockSpec((tm, d), row),
        out_shape=jax.ShapeDtypeStruct((t, d), F32),
        scratch_shapes=[pltpu.VMEM((tm, slots), F32), pltpu.VMEM((tm * WG_PITCH, nk), F32)],
        compiler_params=_params(("parallel", "arbitrary")), name="peer_v",
    )(parts, g, i1, i2, v3, x)


def _pad_cols(w, n):
    return jnp.pad(w, ((0, 0), (0, n - w.shape[1])))


def _in_proj_kernel(x_ref, g_ref, w_ref, *rest, bounds):
    outs, xn_ref = rest[:-1], rest[-1]
    j = pl.program_id(1)

    @pl.when(j == 0)
    def _():
        x = x_ref[...]
        ms = jnp.mean(x * x, axis=-1, keepdims=True)
        xn_ref[...] = (x * lax.rsqrt(ms + EPS) * g_ref[...]).astype(xn_ref.dtype)

    y = jnp.dot(xn_ref[...], w_ref[...], preferred_element_type=F32)
    for (lo, hi), o_ref in zip(bounds, outs):
        @pl.when((j >= lo) & (j < hi))
        def _(o_ref=o_ref):
            o_ref[...] = y.astype(o_ref.dtype)


PROJ_TILE = 512
IN_GROUPS = ((1024, BF16), (768, BF16), (48, F32), (1024, BF16), (1024, BF16), (1024, BF16), (1536, BF16),
             (16, F32), (3072, BF16))


def _in_proj(x, g, w_in, tm=1024):
    t, d = x.shape
    tn = PROJ_TILE
    ws, bounds, shapes, off, tile = [], [], [], 0, 0
    for n, dtype in IN_GROUPS:
        width = -(-n // tn) * tn
        ws.append(_pad_cols(w_in[:, off:off + n], width))
        bounds.append((tile, tile + width // tn))
        shapes.append(jax.ShapeDtypeStruct((t, width), dtype))
        off += n
        tile += width // tn
    w_all = jnp.concatenate(ws, axis=1).astype(BF16)
    out_specs = [pl.BlockSpec((tm, tn), functools.partial(lambda i, j, lo, hi: (i, jnp.clip(j - lo, 0, hi - lo - 1)),
                                                          lo=lo, hi=hi)) for lo, hi in bounds]
    kern = functools.partial(_in_proj_kernel, bounds=tuple(bounds))
    return pl.pallas_call(
        kern, grid=(t // tm, tile),
        in_specs=[pl.BlockSpec((tm, d), lambda i, j: (i, 0)), pl.BlockSpec((1, d), lambda i, j: (0, 0)),
                  pl.BlockSpec((d, tn), lambda i, j: (0, j))],
        out_specs=out_specs, out_shape=shapes,
        scratch_shapes=[pltpu.VMEM((tm, d), BF16)],
        compiler_params=_params(("parallel", "arbitrary")), name="in_proj",
    )(x, g.reshape(1, d).astype(F32), w_all)


def _mixer(x, positions, b, s, p):
    a_q, a_kv, a_g, l_x, l_g, s_z, s_xbc, s_dt, m_g = _in_proj(x, p["mix_norm_g"], p["w_in"])

    q, kc, vc, ks, vs, kw, vw = _nsa_prep(a_q, a_kv, positions, p["q_norm_g"], p["k_norm_g"], b, s)
    k_cmp, v_cmp = _compress(kc, vc, p["cmp_pe_k"], p["cmp_pe_v"], p["cmp_k_w1"], p["cmp_k_w2"],
                             p["cmp_v_w1"], p["cmp_v_w2"], p["k_norm_g"])
    o_c, bias = _cmp_select(q, k_cmp, v_cmp, s)
    o_s = _sel_attn(q, bias, ks, vs, s)
    o_w = _win_attn(q, kw, vw, s)
    y_b = _rglru(l_x, l_g, positions, p["lru_conv_w"], p["lru_conv_b"], p["lru_w_r"], p["lru_b_r"],
                 p["lru_w_i"], p["lru_b_i"], p["lru_lambda"], b, s)
    y_c = _ssd(s_z, s_xbc, s_dt, p["ssm_conv_w"], p["ssm_conv_b"], p["ssm_dt_bias"], p["ssm_a_log"],
               p["ssm_d"], p["ssm_norm_g"], b, s)
    return _merge(o_c, o_s, o_w, a_g, y_b, y_c, m_g, x, p["w_branch"], p["w_out"])


def _peer(x, p):
    h = _rmsnorm(x, p["ffn_norm_g"])
    i1, i2, g, act = _peer_route_u(h, p["peer_w_q"], p["peer_sub_keys"], p["peer_u"].astype(BF16))
    return _peer_v(act[None], g, i1, i2, p["peer_v"].astype(BF16), x)


_LAYER_PARAMS = ("mix_norm_g", "w_in", "q_norm_g", "k_norm_g", "cmp_pe_k", "cmp_pe_v", "cmp_k_w1", "cmp_k_w2",
                 "cmp_v_w1", "cmp_v_w2", "lru_conv_w", "lru_conv_b", "lru_w_r", "lru_b_r", "lru_w_i", "lru_b_i",
                 "lru_lambda", "ssm_conv_w", "ssm_conv_b", "ssm_dt_bias", "ssm_a_log", "ssm_d", "ssm_norm_g",
                 "w_branch", "w_out", "ffn_norm_g", "peer_w_q", "peer_sub_keys", "peer_u", "peer_v")


def kernel(x, positions, mix_norm_g, w_in, q_norm_g, k_norm_g, cmp_pe_k, cmp_pe_v, cmp_k_w1, cmp_k_w2, cmp_v_w1, cmp_v_w2, lru_conv_w, lru_conv_b, lru_w_r, lru_b_r, lru_w_i, lru_b_i, lru_lambda, ssm_conv_w, ssm_conv_b, ssm_dt_bias, ssm_a_log, ssm_d, ssm_norm_g, w_branch, w_out, ffn_norm_g, peer_w_q, peer_sub_keys, peer_u, peer_v):
    stacked = dict(zip(_LAYER_PARAMS, (mix_norm_g, w_in, q_norm_g, k_norm_g, cmp_pe_k, cmp_pe_v, cmp_k_w1,
                                       cmp_k_w2, cmp_v_w1, cmp_v_w2, lru_conv_w, lru_conv_b, lru_w_r, lru_b_r,
                                       lru_w_i, lru_b_i, lru_lambda, ssm_conv_w, ssm_conv_b, ssm_dt_bias,
                                       ssm_a_log, ssm_d, ssm_norm_g, w_branch, w_out, ffn_norm_g, peer_w_q,
                                       peer_sub_keys, peer_u, peer_v)))
    b, s, d = x.shape
    xf = x.reshape(b * s, d).astype(F32)
    for layer in range(mix_norm_g.shape[0]):
        p = {name: arr[layer] for name, arr in stacked.items()}
        xf = _mixer(xf, positions, b, s, p)
        xf = _peer(xf, p)
    return xf.reshape(b, s, d).astype(x.dtype)
```

```python
import functools
import math

import jax
import jax.numpy as jnp
from jax import lax
from jax.experimental import pallas as pl
from jax.experimental.pallas import tpu as pltpu

F32 = jnp.float32
BF16 = jnp.bfloat16
I32 = jnp.int32

D_MODEL = 1024
HEAD_DIM = 64
N_HEADS = 16
N_KV_GROUPS = 2
HEADS_PER_GROUP = 8
CMP_BLOCK = 32
CMP_STRIDE = 16
CMP_HIDDEN = 256
SEL_BLOCK = 64
N_SELECT = 16
WINDOW = 512
SEL_FORCE = 100.0
ROPE_THETA = 10000.0
SCALE = HEAD_DIM ** -0.5
LOG2E = math.log2(math.e)
LRU_HEADS = 16
LRU_BLOCK = 64
LRU_C = 8.0
CONV_WIDTH = 4
SSM_HEADS = 16
SSM_HEAD_DIM = 64
SSM_GROUPS = 2
SSM_STATE = 128
SSM_CHUNK = 128
SSM_INNER = 1024
PEER_HEADS = 8
PEER_KEYS = 128
PEER_HALF = 128
PEER_TOPK = 16
EPS = 1e-6
NEG = -1e30
LANES = 128

VMEM_LIMIT = 56 * 1024 * 1024
SEL_LOOKAHEAD = 3
SEL_ACC_ROWS = HEAD_DIM + 16
WG_PITCH = PEER_KEYS + 8
ROUTE_LANES = 128


def _params(sem):
    return pltpu.CompilerParams(dimension_semantics=sem, vmem_limit_bytes=VMEM_LIMIT)


def _gelu(x):
    return 0.5 * x * (1.0 + jnp.tanh(math.sqrt(2.0 / math.pi) * (x + 0.044715 * x * x * x)))


def _sigmoid(x):
    return 0.5 * jnp.tanh(0.5 * x) + 0.5


def _softplus(x):
    return jnp.maximum(x, 0.0) + jnp.log(1.0 + jnp.exp(-jnp.abs(x)))


def _rmsnorm_kernel(x_ref, g_ref, o_ref):
    x = x_ref[...].astype(F32)
    ms = jnp.mean(x * x, axis=-1, keepdims=True)
    o_ref[...] = (x * lax.rsqrt(ms + EPS) * g_ref[...]).astype(o_ref.dtype)


def _rmsnorm(x, g, tm=512):
    t, d = x.shape
    return pl.pallas_call(
        _rmsnorm_kernel, grid=(t // tm,),
        in_specs=[pl.BlockSpec((tm, d), lambda i: (i, 0)), pl.BlockSpec((1, d), lambda i: (0, 0))],
        out_specs=pl.BlockSpec((tm, d), lambda i: (i, 0)),
        out_shape=jax.ShapeDtypeStruct((t, d), BF16),
        compiler_params=_params(("parallel",)), name="rmsnorm")(x, g.reshape(1, d).astype(F32))


def _norm_rope(x, g, cos, sin_signed, bd):
    ms = jnp.dot((x * x).astype(BF16), bd, preferred_element_type=F32)
    y = x * lax.rsqrt(ms + EPS) * g
    lane = lax.broadcasted_iota(I32, y.shape, 1)
    first_half = (lane % HEAD_DIM) < (HEAD_DIM // 2)
    partner = jnp.where(first_half, pltpu.roll(y, LANES - HEAD_DIM // 2, 1), pltpu.roll(y, HEAD_DIM // 2, 1))
    return y * cos + partner * sin_signed


def _nsa_prep_kernel(aq_ref, akv_ref, pos_ref, inv_ref, sgn_ref, gq_ref, gk_ref, bd_ref,
                     q_ref, kc_ref, vc_ref, ks_ref, vs_ref, kw_ref, vw_ref):
    ang = pos_ref[...].astype(F32) * inv_ref[...]
    cos = jnp.cos(ang)
    sin_signed = jnp.sin(ang) * sgn_ref[...]
    bd = bd_ref[...]
    for c in range(D_MODEL // LANES):
        x = aq_ref[:, c * LANES:(c + 1) * LANES].astype(F32)
        y = (_norm_rope(x, gq_ref[...], cos, sin_signed, bd) * (SCALE * LOG2E)).astype(q_ref.dtype)
        q_ref[0, 2 * c] = y[:, :HEAD_DIM]
        q_ref[0, 2 * c + 1] = y[:, HEAD_DIM:]
    outs = (kc_ref, vc_ref, ks_ref, vs_ref, kw_ref, vw_ref)
    for c, o_ref in enumerate(outs):
        x = akv_ref[:, c * LANES:(c + 1) * LANES]
        if c % 2 == 0:
            y = _norm_rope(x.astype(F32), gk_ref[...], cos, sin_signed, bd).astype(o_ref.dtype)
        else:
            y = x.astype(o_ref.dtype)
        o_ref[0, 0] = y[:, :HEAD_DIM]
        o_ref[0, 1] = y[:, HEAD_DIM:]


def _nsa_prep(a_q, a_kv, positions, q_norm_g, k_norm_g, b, s, tm=512):
    half = HEAD_DIM // 2
    lane = jnp.arange(LANES)
    inv = (ROPE_THETA ** (-((lane % half).astype(F32)) / half)).reshape(1, LANES)
    sgn = jnp.where((lane % HEAD_DIM) < half, -1.0, 1.0).astype(F32).reshape(1, LANES)
    bd = jnp.where((lane[:, None] // HEAD_DIM) == (lane[None, :] // HEAD_DIM), 1.0 / HEAD_DIM, 0.0).astype(BF16)
    gq = jnp.tile(q_norm_g.astype(F32), 2).reshape(1, LANES)
    gk = jnp.tile(k_norm_g.astype(F32), 2).reshape(1, LANES)
    nt = s // tm
    row = lambda i: (i, 0)
    const = lambda i: (0, 0)
    kv_shape = jax.ShapeDtypeStruct((b, N_KV_GROUPS, s, HEAD_DIM), BF16)
    kv_spec = pl.BlockSpec((1, N_KV_GROUPS, tm, HEAD_DIM), lambda i: (i // nt, 0, i % nt, 0))
    return pl.pallas_call(
        _nsa_prep_kernel, grid=(b * nt,),
        in_specs=[pl.BlockSpec((tm, D_MODEL), row), pl.BlockSpec((tm, 6 * LANES), row),
                  pl.BlockSpec((tm, 1), row), pl.BlockSpec((1, LANES), const), pl.BlockSpec((1, LANES), const),
                  pl.BlockSpec((1, LANES), const), pl.BlockSpec((1, LANES), const),
                  pl.BlockSpec((LANES, LANES), const)],
        out_specs=[pl.BlockSpec((1, N_HEADS, tm, HEAD_DIM), lambda i: (i // nt, 0, i % nt, 0))] + [kv_spec] * 6,
        out_shape=[jax.ShapeDtypeStruct((b, N_HEADS, s, HEAD_DIM), BF16)] + [kv_shape] * 6,
        compiler_params=_params(("parallel",)), name="nsa_prep",
    )(a_q, a_kv, positions.reshape(b * s, 1).astype(I32), inv, sgn, gq, gk, bd)


def _compress_kernel(uk_ref, uv_ref, pek_ref, pev_ref, kw1_ref, kw2_ref, vw1_ref, vw2_ref, gk_ref,
                     kc_ref, vc_ref):
    half = CMP_STRIDE * HEAD_DIM

    def mlp(u, pe, w1_ref, w2_ref):
        n = u.shape[0]
        ha = jnp.dot(u, w1_ref[:half, :], preferred_element_type=F32)
        hb = jnp.dot(u, w1_ref[half:, :], preferred_element_type=F32)
        bias = jnp.dot(pe, w1_ref[...], preferred_element_type=F32)[0:1, :]
        pre = ha + pltpu.roll(hb, n - 1, 0) + bias
        return jnp.dot(_gelu(pre).astype(BF16), w2_ref[...], preferred_element_type=F32)

    k = mlp(uk_ref[0, 0], pek_ref[...], kw1_ref, kw2_ref)
    ms = jnp.mean(k * k, axis=-1, keepdims=True)
    kc_ref[0, 0] = (k * lax.rsqrt(ms + EPS) * gk_ref[...]).astype(kc_ref.dtype)
    vc_ref[0, 0] = mlp(uv_ref[0, 0], pev_ref[...], vw1_ref, vw2_ref).astype(vc_ref.dtype)


def _compress(kc, vc, pe_k, pe_v, kw1, kw2, vw1, vw2, k_norm_g):
    b, g, s, dh = kc.shape
    ng = s // CMP_STRIDE
    wide = CMP_BLOCK * dh
    uk = kc.reshape(b, g, ng, CMP_STRIDE * dh)
    uv = vc.reshape(b, g, ng, CMP_STRIDE * dh)
    pek = jnp.zeros((8, wide), BF16).at[0].set(pe_k.reshape(wide).astype(BF16))
    pev = jnp.zeros((8, wide), BF16).at[0].set(pe_v.reshape(wide).astype(BF16))
    u_spec = pl.BlockSpec((1, 1, ng, CMP_STRIDE * dh), lambda i, j: (i, j, 0, 0))
    c2 = lambda i, j: (0, 0)
    o_spec = pl.BlockSpec((1, 1, ng, dh), lambda i, j: (i, j, 0, 0))
    o_shape = jax.ShapeDtypeStruct((b, g, ng, dh), BF16)
    return pl.pallas_call(
        _compress_kernel, grid=(b, g),
        in_specs=[u_spec, u_spec, pl.BlockSpec((8, wide), c2), pl.BlockSpec((8, wide), c2),
                  pl.BlockSpec((wide, CMP_HIDDEN), c2), pl.BlockSpec((CMP_HIDDEN, dh), c2),
                  pl.BlockSpec((wide, CMP_HIDDEN), c2), pl.BlockSpec((CMP_HIDDEN, dh), c2),
                  pl.BlockSpec((1, dh), c2)],
        out_specs=[o_spec, o_spec], out_shape=[o_shape, o_shape],
        compiler_params=_params(("parallel", "parallel")), name="nsa_compress",
    )(uk, uv, pek, pev, kw1.astype(BF16), kw2.astype(BF16), vw1.astype(BF16), vw2.astype(BF16),
      k_norm_g.reshape(1, dh).astype(F32))


def _store_head_pair(o_ref, h, o_t, held):
    if h % 2 == 0:
        return o_t
    pair = jnp.concatenate([held, o_t], axis=0).T
    o_ref[0, :, (h - 1) * HEAD_DIM:(h + 1) * HEAD_DIM] = pair.astype(o_ref.dtype)
    return None


def _cmp_select_kernel(q_ref, kc_ref, vc_ref, ovt_ref, o_ref, bias_ref, *, tq, n_cmp, n_pick):
    i = pl.program_id(2)
    chunk = min(LANES, kc_ref.shape[2])
    visible = ((i + 1) * tq - CMP_BLOCK) // CMP_STRIDE + 1
    nchunks = jnp.clip((visible + chunk - 1) // chunk, 1, kc_ref.shape[2] // chunk)
    for n in range(1, kc_ref.shape[2] // chunk + 1):
        pl.when(nchunks == n)(functools.partial(
            _cmp_select_keys, q_ref, kc_ref, vc_ref, ovt_ref, o_ref, bias_ref, nc=n * chunk, i=i, tq=tq,
            n_cmp=n_cmp, n_pick=n_pick))


def _cmp_select_keys(q_ref, kc_ref, vc_ref, ovt_ref, o_ref, bias_ref, *, nc, i, tq, n_cmp, n_pick):
    hg = q_ref.shape[1]
    nsel = ovt_ref.shape[0]
    kc = kc_ref[0, 0, :nc, :]
    vct = vc_ref[0, 0, :nc, :].astype(F32).T.astype(BF16)
    ovt = ovt_ref[:, :nc]
    c = lax.broadcasted_iota(I32, (nc, tq), 0)
    t = i * tq + lax.broadcasted_iota(I32, (nc, tq), 1)
    mask = ((CMP_STRIDE * c + CMP_BLOCK - 1) <= t) & (c < n_cmp)
    score = lambda h: lax.dot_general(kc, q_ref[0, h], (((1,), (1,)), ((), ())), preferred_element_type=F32)
    st_next = score(0)
    psum = jnp.zeros((nc, tq), F32)
    held = None
    for h in range(hg):
        st = st_next
        if h + 1 < hg:
            st_next = score(h + 1)
        st = jnp.where(mask, st, NEG)
        m = jnp.max(st, axis=0, keepdims=True)
        p = jnp.where(mask, jnp.exp2(st - m), 0.0)
        l = jnp.sum(p, axis=0, keepdims=True)
        p = p * jnp.where(l > 0.0, 1.0 / l, 0.0)
        o = jnp.dot(vct, p.astype(BF16), preferred_element_type=F32)
        held = _store_head_pair(o_ref, h, o, held)
        psum = psum + p

    hi = psum.astype(BF16)
    lo = (psum - hi.astype(F32)).astype(BF16)
    imp = jnp.dot(ovt, hi, preferred_element_type=F32) + jnp.dot(ovt, lo, preferred_element_type=F32)
    j = lax.broadcasted_iota(I32, (nsel, tq), 0)
    tt = i * tq + lax.broadcasted_iota(I32, (nsel, tq), 1)
    cur = tt // SEL_BLOCK
    forced = (j == 0) | (j == cur) | (j == cur - 1)
    sc = jnp.where(forced, SEL_FORCE, jnp.where(j * SEL_BLOCK <= tt, imp, -1.0))
    picked = jnp.zeros((nsel, tq), jnp.bool_)
    for _ in range(n_pick):
        mx = jnp.max(sc, axis=0, keepdims=True)
        idx = jnp.min(jnp.where(sc == mx, j, nsel), axis=0, keepdims=True)
        hit = j == idx
        picked = picked | hit
        sc = jnp.where(hit, -jnp.inf, sc)
    bias_ref[0, 0] = jnp.where(picked & (j <= cur), 0.0, NEG).T.astype(bias_ref.dtype)


def _cmp_select(q, k_cmp, v_cmp, s, tq=512):
    b, nh, _, dh = q.shape
    g, hg = N_KV_GROUPS, HEADS_PER_GROUP
    nc = k_cmp.shape[2]
    n_cmp = (s - CMP_BLOCK) // CMP_STRIDE + 1
    nsel = s // SEL_BLOCK
    n_pick = min(N_SELECT, nsel)
    cs = CMP_STRIDE * jnp.arange(nc)
    ss = SEL_BLOCK * jnp.arange(nsel)
    ovt = jnp.clip(jnp.minimum(cs[None, :] + CMP_BLOCK, ss[:, None] + SEL_BLOCK)
                   - jnp.maximum(cs[None, :], ss[:, None]), 0).astype(F32) / CMP_BLOCK
    ovt = jnp.where(jnp.arange(nc)[None, :] < n_cmp, ovt, 0.0).astype(BF16)
    kern = functools.partial(_cmp_select_kernel, tq=tq, n_cmp=n_cmp, n_pick=n_pick)
    return pl.pallas_call(
        kern, grid=(b, g, s // tq),
        in_specs=[pl.BlockSpec((1, hg, tq, dh), lambda bi, gi, i: (bi, gi, i, 0)),
                  pl.BlockSpec((1, 1, nc, dh), lambda bi, gi, i: (bi, gi, 0, 0)),
                  pl.BlockSpec((1, 1, nc, dh), lambda bi, gi, i: (bi, gi, 0, 0)),
                  pl.BlockSpec((nsel, nc), lambda bi, gi, i: (0, 0))],
        out_specs=[pl.BlockSpec((1, tq, hg * dh), lambda bi, gi, i: (bi, i, gi)),
                   pl.BlockSpec((1, 1, tq, nsel), lambda bi, gi, i: (bi, gi, i, 0))],
        out_shape=[jax.ShapeDtypeStruct((b, s, nh * dh), BF16),
                   jax.ShapeDtypeStruct((b, g, s, nsel), BF16)],
        compiler_params=_params(("parallel", "parallel", "parallel")), name="nsa_cmp_select",
    )(q, k_cmp, v_cmp, ovt)


def _sel_attn_kernel(it_ref, jt_ref, q_ref, bias_ref, k_ref, v_ref, o_ref, qa_ref, m_ref, acc_ref, *, tq, tk):
    pid = pl.program_id(2)
    i = it_ref[pid]
    j = jt_ref[pid]
    hg = q_ref.shape[1]
    nsel = bias_ref.shape[3]
    last_j = ((i + 1) * tq - 1) // tk

    @pl.when(j == 0)
    def _():
        for h in range(hg):
            qa_ref[h, :, :nsel] = bias_ref[0, 0]
            qa_ref[h, :, nsel:] = q_ref[0, h]
        m_ref[...] = jnp.full(m_ref.shape, -jnp.inf, F32)
        acc_ref[...] = jnp.zeros(acc_ref.shape, F32)

    def accumulate(diagonal):
        key = j * tk + lax.broadcasted_iota(I32, (tk, nsel), 0)
        blk = lax.broadcasted_iota(I32, (tk, nsel), 1)
        onehot = jnp.where(key // SEL_BLOCK == blk, 1.0, 0.0).astype(BF16)
        ka = jnp.concatenate([onehot, k_ref[0, 0]], axis=1)
        va = jnp.concatenate([v_ref[0, 0].astype(F32), jnp.ones((tk, LANES - HEAD_DIM), F32)], axis=1)
        vat = va.T[:SEL_ACC_ROWS, :].astype(BF16)
        if diagonal:
            kp = j * tk + lax.broadcasted_iota(I32, (tk, tq), 0)
            t = i * tq + lax.broadcasted_iota(I32, (tk, tq), 1)
            causal = kp <= t
        score = lambda h: lax.dot_general(ka, qa_ref[h], (((1,), (1,)), ((), ())), preferred_element_type=F32)
        ahead = [score(h) for h in range(SEL_LOOKAHEAD)]
        for h in range(hg):
            st = ahead.pop(0)
            if h + SEL_LOOKAHEAD < hg:
                ahead.append(score(h + SEL_LOOKAHEAD))
            if diagonal:
                st = jnp.where(causal, st, NEG)
            m_old = m_ref[h]
            m_new = jnp.maximum(m_old, jnp.max(st, axis=0, keepdims=True))
            alpha = jnp.exp2(m_old - m_new)
            p = jnp.exp2(st - m_new)
            acc_ref[h] = alpha * acc_ref[h] + jnp.dot(vat, p.astype(BF16), preferred_element_type=F32)
            m_ref[h] = m_new

    @pl.when(j < last_j)
    def _():
        accumulate(False)

    @pl.when(j == last_j)
    def _():
        accumulate(True)
        held = None
        for h in range(hg):
            acc = acc_ref[h]
            o = acc[:HEAD_DIM, :] / acc[HEAD_DIM:HEAD_DIM + 1, :]
            held = _store_head_pair(o_ref, h, o, held)


def _sel_attn(q, bias, k_s, v_s, s, tq=512, tk=512):
    assert tq <= tk
    b, nh, _, dh = q.shape
    g, hg = N_KV_GROUPS, HEADS_PER_GROUP
    nsel = bias.shape[3]
    tk = min(tk, s)
    pairs = [(i, j) for i in range(s // tq) for j in range(((i + 1) * tq - 1) // tk + 1)]
    it = jnp.asarray([pr[0] for pr in pairs], I32)
    jt = jnp.asarray([pr[1] for pr in pairs], I32)
    q_map = lambda bi, gi, pid, it_ref, jt_ref: (bi, gi, it_ref[pid], 0)
    kv_map = lambda bi, gi, pid, it_ref, jt_ref: (bi, gi, jt_ref[pid], 0)
    kern = functools.partial(_sel_attn_kernel, tq=tq, tk=tk)
    grid_spec = pltpu.PrefetchScalarGridSpec(
        num_scalar_prefetch=2, grid=(b, g, len(pairs)),
        in_specs=[pl.BlockSpec((1, hg, tq, dh), q_map), pl.BlockSpec((1, 1, tq, nsel), q_map),
                  pl.BlockSpec((1, 1, tk, dh), kv_map), pl.BlockSpec((1, 1, tk, dh), kv_map)],
        out_specs=pl.BlockSpec((1, tq, hg * dh), lambda bi, gi, pid, it_ref, jt_ref: (bi, it_ref[pid], gi)),
        scratch_shapes=[pltpu.VMEM((hg, tq, nsel + dh), BF16), pltpu.VMEM((hg, 1, tq), F32),
                        pltpu.VMEM((hg, SEL_ACC_ROWS, tq), F32)])
    return pl.pallas_call(
        kern, grid_spec=grid_spec, out_shape=jax.ShapeDtypeStruct((b, s, nh * dh), BF16),
        compiler_params=_params(("parallel", "parallel", "arbitrary")), name="nsa_sel_attn",
    )(it, jt, q, bias, k_s, v_s)


def _win_attn_kernel(q_ref, *refs, tq, nwin):
    k_refs, v_refs, o_ref = refs[:nwin], refs[nwin:2 * nwin], refs[2 * nwin]
    i = pl.program_id(2)
    hg = q_ref.shape[1]
    nk = nwin * tq
    k = jnp.concatenate([r[0, 0] for r in k_refs], axis=0)
    vt = jnp.concatenate([r[0, 0] for r in v_refs], axis=0).astype(F32).T.astype(BF16)
    sp = i * tq - WINDOW + lax.broadcasted_iota(I32, (nk, tq), 0)
    t = i * tq + lax.broadcasted_iota(I32, (nk, tq), 1)
    mask = (sp <= t) & (sp > t - WINDOW) & (sp >= 0)
    score = lambda h: lax.dot_general(k, q_ref[0, h], (((1,), (1,)), ((), ())), preferred_element_type=F32)
    st_next = score(0)
    held = None
    for h in range(hg):
        st = st_next
        if h + 1 < hg:
            st_next = score(h + 1)
        st = jnp.where(mask, st, NEG)
        m = jnp.max(st, axis=0, keepdims=True)
        p = jnp.where(mask, jnp.exp2(st - m), 0.0)
        l = jnp.sum(p, axis=0, keepdims=True)
        o = jnp.dot(vt, p.astype(BF16), preferred_element_type=F32) / l
        held = _store_head_pair(o_ref, h, o, held)


def _win_attn(q, k_w, v_w, s, tq=512):
    b, nh, _, dh = q.shape
    g, hg = N_KV_GROUPS, HEADS_PER_GROUP
    nwin = WINDOW // tq + 1
    pad = ((0, 0), (0, 0), (WINDOW, 0), (0, 0))
    kp, vp = jnp.pad(k_w, pad), jnp.pad(v_w, pad)
    kv_specs = [pl.BlockSpec((1, 1, tq, dh), functools.partial(lambda bi, gi, i, w: (bi, gi, i + w, 0), w=w))
                for w in range(nwin)]
    kern = functools.partial(_win_attn_kernel, tq=tq, nwin=nwin)
    return pl.pallas_call(
        kern, grid=(b, g, s // tq),
        in_specs=[pl.BlockSpec((1, hg, tq, dh), lambda bi, gi, i: (bi, gi, i, 0))] + kv_specs + kv_specs,
        out_specs=pl.BlockSpec((1, tq, hg * dh), lambda bi, gi, i: (bi, i, gi)),
        out_shape=jax.ShapeDtypeStruct((b, s, nh * dh), BF16),
        compiler_params=_params(("parallel", "parallel", "parallel")), name="nsa_win_attn",
    )(q, *([kp] * nwin), *([vp] * nwin))


def _causal_conv(x, tail, w_ref, b_ref):
    n = x.shape[0]
    xx = jnp.concatenate([tail, x], axis=0)
    out = b_ref[...] + w_ref[CONV_WIDTH - 1:CONV_WIDTH, :] * x
    for k in range(CONV_WIDTH - 1):
        off = 8 - (CONV_WIDTH - 1) + k
        out = out + w_ref[k:k + 1, :] * xx[off:off + n, :]
    return out


def _rglru_kernel(x_ref, gate_ref, pos_ref, cw_ref, cb_ref, wr_ref, br_ref, wi_ref, bi_ref, lam_ref,
                  o_ref, tail_ref, h_ref):
    @pl.when(pl.program_id(1) == 0)
    def _():
        tail_ref[...] = jnp.zeros(tail_ref.shape, F32)
        h_ref[...] = jnp.zeros(h_ref.shape, F32)

    n = x_ref.shape[0]
    x = x_ref[...].astype(F32)
    xc = _causal_conv(x, tail_ref[...], cw_ref, cb_ref)
    tail_ref[...] = x[n - 8:, :]
    xcb = xc.astype(BF16)
    r = _sigmoid(jnp.dot(xcb, wr_ref[...], preferred_element_type=F32) + br_ref[...])
    gi = _sigmoid(jnp.dot(xcb, wi_ref[...], preferred_element_type=F32) + bi_ref[...])
    log_a = -LRU_C * r * _softplus(-lam_ref[...])
    reset = pos_ref[...] == 0
    a = jnp.where(reset, 0.0, jnp.exp(log_a))
    mult = jnp.where(reset, 1.0, jnp.sqrt(jnp.maximum(1.0 - jnp.exp(2.0 * log_a), 0.0)))
    bb = mult * (gi * xc)
    sub = lax.broadcasted_iota(I32, a.shape, 0) % 8
    d = 1
    while d < 8:
        a_sh = pltpu.roll(a, d, 0)
        b_sh = pltpu.roll(bb, d, 0)
        live = sub >= d
        bb = jnp.where(live, a * b_sh + bb, bb)
        a = jnp.where(live, a * a_sh, a)
        d *= 2
    carry = h_ref[...]
    hs = []
    for g in range(n // 8):
        h = bb[8 * g:8 * g + 8, :] + a[8 * g:8 * g + 8, :] * carry
        carry = h[7:8, :]
        hs.append(h)
    h_ref[...] = carry
    o_ref[...] = (jnp.concatenate(hs, axis=0) * _gelu(gate_ref[...].astype(F32))).astype(o_ref.dtype)


def _block_diag(w):
    nb, bs, _ = w.shape
    eye = jnp.eye(nb, dtype=w.dtype)
    return (w[:, :, None, :] * eye[:, None, :, None]).reshape(nb * bs, nb * bs)


def _rglru(l_x, l_g, positions, conv_w, conv_b, w_r, b_r, w_i, b_i, lam, b, s, ts=256):
    d = l_x.shape[1]
    nt = s // ts
    row = lambda bi, i: (bi * nt + i, 0)
    c2 = lambda bi, i: (0, 0)
    vec = lambda v: v.reshape(1, d).astype(F32)
    return pl.pallas_call(
        _rglru_kernel, grid=(b, nt),
        in_specs=[pl.BlockSpec((ts, d), row), pl.BlockSpec((ts, d), row), pl.BlockSpec((ts, 1), row),
                  pl.BlockSpec((CONV_WIDTH, d), c2), pl.BlockSpec((1, d), c2),
                  pl.BlockSpec((d, d), c2), pl.BlockSpec((1, d), c2),
                  pl.BlockSpec((d, d), c2), pl.BlockSpec((1, d), c2), pl.BlockSpec((1, d), c2)],
        out_specs=pl.BlockSpec((ts, d), row),
        out_shape=jax.ShapeDtypeStruct((b * s, d), BF16),
        scratch_shapes=[pltpu.VMEM((8, d), F32), pltpu.VMEM((1, d), F32)],
        compiler_params=_params(("parallel", "arbitrary")), name="rglru",
    )(l_x, l_g, positions.reshape(b * s, 1).astype(I32), conv_w.astype(F32), vec(conv_b),
      _block_diag(w_r).astype(BF16), vec(b_r), _block_diag(w_i).astype(BF16), vec(b_i), vec(lam))


def _ssd_kernel(z_ref, xbc_ref, dt_ref, cw_ref, cb_ref, dtb_ref, alog_ref, dfull_ref, ng_ref,
                o_ref, tail_ref, state_ref, y_ref):
    @pl.when(pl.program_id(1) == 0)
    def _():
        tail_ref[...] = jnp.zeros(tail_ref.shape, F32)
        state_ref[...] = jnp.zeros(state_ref.shape, F32)

    n = xbc_ref.shape[0]
    hg = SSM_HEADS // SSM_GROUPS
    x = xbc_ref[...].astype(F32)
    xc = _causal_conv(x, tail_ref[...], cw_ref, cb_ref)
    tail_ref[...] = x[n - 8:, :]
    xc = xc * _sigmoid(xc)
    xs = xc[:, :SSM_INNER]
    dt = _softplus(dt_ref[...] + dtb_ref[...])
    adt = dt * (-jnp.exp(alog_ref[...]))
    row = lax.broadcasted_iota(I32, adt.shape, 0)
    acs = adt
    d = 1
    while d < n:
        acs = acs + jnp.where(row >= d, pltpu.roll(acs, d, 0), 0.0)
        d *= 2
    acs_t = acs.T
    li = lax.broadcasted_iota(I32, (n, n), 0)
    si = lax.broadcasted_iota(I32, (n, n), 1)
    tri = li >= si
    for g in range(SSM_GROUPS):
        bm = xc[:, SSM_INNER + g * SSM_STATE:SSM_INNER + (g + 1) * SSM_STATE].astype(BF16)
        cm = xc[:, SSM_INNER + (SSM_GROUPS + g) * SSM_STATE:SSM_INNER + (SSM_GROUPS + g + 1) * SSM_STATE].astype(BF16)
        cb = lax.dot_general(cm, bm, (((1,), (1,)), ((), ())), preferred_element_type=F32)
        bm_t = bm.T
        for hh in range(hg):
            h = g * hg + hh
            acol = jnp.broadcast_to(acs[:, h:h + 1], (n, n))
            arow = acs_t[h:h + 1, :]
            decay = jnp.exp(jnp.where(tri, acol - arow, NEG))
            acol_p = acol[:, :SSM_HEAD_DIM]
            xh = xs[:, h * SSM_HEAD_DIM:(h + 1) * SSM_HEAD_DIM] * dt[:, h:h + 1]
            a_last = acol_p[n - 1:n, :]
            y = jnp.dot((cb * decay).astype(BF16), xh.astype(BF16), preferred_element_type=F32)
            st = state_ref[h]
            y = y + jnp.dot(cm, st.astype(BF16), preferred_element_type=F32) * jnp.exp(acol_p)
            upd = jnp.dot(bm_t, (xh * jnp.exp(a_last - acol_p)).astype(BF16), preferred_element_type=F32)
            state_ref[h] = jnp.exp(a_last) * st + upd
            y_ref[:, h * SSM_HEAD_DIM:(h + 1) * SSM_HEAD_DIM] = y
    z = z_ref[...].astype(F32)
    y = (y_ref[...] + dfull_ref[...] * xs) * (z * _sigmoid(z))
    gw = SSM_INNER // SSM_GROUPS
    for g in range(SSM_GROUPS):
        yg = y[:, g * gw:(g + 1) * gw]
        ms = jnp.mean(yg * yg, axis=-1, keepdims=True)
        o_ref[:, g * gw:(g + 1) * gw] = (yg * lax.rsqrt(ms + EPS) * ng_ref[:, g * gw:(g + 1) * gw]).astype(o_ref.dtype)


def _ssd(s_z, s_xbc, s_dt, conv_w, conv_b, dt_bias, a_log, d_skip, norm_g, b, s):
    n = SSM_CHUNK
    nt = s // n
    c = s_xbc.shape[1]
    row = lambda bi, i: (bi * nt + i, 0)
    c2 = lambda bi, i: (0, 0)
    pad_h = lambda v: jnp.zeros((1, LANES), F32).at[0, :SSM_HEADS].set(v.astype(F32))
    dfull = jnp.repeat(d_skip.astype(F32), SSM_HEAD_DIM).reshape(1, SSM_INNER)
    return pl.pallas_call(
        _ssd_kernel, grid=(b, nt),
        in_specs=[pl.BlockSpec((n, SSM_INNER), row), pl.BlockSpec((n, c), row), pl.BlockSpec((n, LANES), row),
                  pl.BlockSpec((CONV_WIDTH, c), c2), pl.BlockSpec((1, c), c2),
                  pl.BlockSpec((1, LANES), c2), pl.BlockSpec((1, LANES), c2),
                  pl.BlockSpec((1, SSM_INNER), c2), pl.BlockSpec((1, SSM_INNER), c2)],
        out_specs=pl.BlockSpec((n, SSM_INNER), row),
        out_shape=jax.ShapeDtypeStruct((b * s, SSM_INNER), BF16),
        scratch_shapes=[pltpu.VMEM((8, c), F32), pltpu.VMEM((SSM_HEADS, SSM_STATE, SSM_HEAD_DIM), F32),
                        pltpu.VMEM((n, SSM_INNER), F32)],
        compiler_params=_params(("parallel", "arbitrary")), name="ssd",
    )(s_z, s_xbc, s_dt, conv_w.astype(F32), conv_b.reshape(1, c).astype(F32), pad_h(dt_bias), pad_h(a_log),
      dfull, norm_g.reshape(1, SSM_INNER).astype(F32))


def _merge_kernel(oc_ref, os_ref, ow_ref, ag_ref, ex_ref, yb_ref, yc_ref, mg_ref, x_ref, wb_ref, wo_ref, o_ref):
    gates = _sigmoid(ag_ref[...])
    hi = gates.astype(BF16)
    lo = (gates - hi.astype(F32)).astype(BF16)
    spread = lambda k: (jnp.dot(hi, ex_ref[k], preferred_element_type=F32)
                        + jnp.dot(lo, ex_ref[k], preferred_element_type=F32))
    ya = (spread(0) * oc_ref[...].astype(F32) + spread(1) * os_ref[...].astype(F32)
          + spread(2) * ow_ref[...].astype(F32))
    d = D_MODEL
    merged = _sigmoid(mg_ref[:, 0:d].astype(F32)) * jnp.dot(ya.astype(BF16), wb_ref[0], preferred_element_type=F32)
    merged += _sigmoid(mg_ref[:, d:2 * d].astype(F32)) * jnp.dot(yb_ref[...], wb_ref[1], preferred_element_type=F32)
    merged += _sigmoid(mg_ref[:, 2 * d:3 * d].astype(F32)) * jnp.dot(yc_ref[...], wb_ref[2], preferred_element_type=F32)
    o_ref[...] = x_ref[...] + jnp.dot(merged.astype(BF16), wo_ref[...], preferred_element_type=F32)


def _merge(o_c, o_s, o_w, a_g, y_b, y_c, m_g, x, w_branch, w_out, tm=256):
    t, d = x.shape
    row = lambda i: (i, 0)
    lane = jnp.arange(LANES)[:, None]
    col = jnp.arange(d)[None, :]
    expand = jnp.stack([(lane == 3 * (col // HEAD_DIM) + k) for k in range(3)]).astype(BF16)
    return pl.pallas_call(
        _merge_kernel, grid=(t // tm,),
        in_specs=[pl.BlockSpec((tm, d), row), pl.BlockSpec((tm, d), row), pl.BlockSpec((tm, d), row),
                  pl.BlockSpec((tm, LANES), row), pl.BlockSpec((3, LANES, d), lambda i: (0, 0, 0)),
                  pl.BlockSpec((tm, d), row), pl.BlockSpec((tm, d), row), pl.BlockSpec((tm, 3 * d), row),
                  pl.BlockSpec((tm, d), row),
                  pl.BlockSpec((3, d, d), lambda i: (0, 0, 0)), pl.BlockSpec((d, d), lambda i: (0, 0))],
        out_specs=pl.BlockSpec((tm, d), row),
        out_shape=jax.ShapeDtypeStruct((t, d), F32),
        compiler_params=_params(("parallel",)), name="merge",
    )(o_c.reshape(t, d), o_s.reshape(t, d), o_w.reshape(t, d), a_g, expand, y_b, y_c, m_g, x,
      w_branch.astype(BF16), w_out.astype(BF16))


class _Interleaver:
    def __init__(self, pieces, every):
        self.pieces, self.every, self.count = list(pieces), every, 0

    def tick(self):
        self.count += 1
        if self.pieces and self.count % self.every == 0:
            self.pieces.pop(0)()

    def drain(self):
        while self.pieces:
            self.pieces.pop(0)()


def _sorting_network(n):
    pairs = []

    def merge(lo, m, r):
        step = r * 2
        if step < m:
            merge(lo, m, step)
            merge(lo + r, m, step)
            pairs.extend((i, i + r) for i in range(lo + r, lo + m - r, step))
        else:
            pairs.append((lo, lo + r))

    def sort(lo, m):
        if m > 1:
            sort(lo, m // 2)
            sort(lo + m // 2, m // 2)
            merge(lo, m, 1)

    sort(0, n)
    return pairs


def _topk_rows(s, k, tick):
    n, lanes = s.shape
    assert n == 8 * k
    sub = lax.broadcasted_iota(I32, (8, lanes), 0)
    vals = [s[8 * j:8 * j + 8, :] for j in range(k)]
    ids = [sub + 8 * j for j in range(k)]
    for count, (a, b) in enumerate(_sorting_network(k)):
        first = (vals[a] > vals[b]) | ((vals[a] == vals[b]) & (ids[a] < ids[b]))
        vals[a], vals[b] = jnp.where(first, vals[a], vals[b]), jnp.where(first, vals[b], vals[a])
        ids[a], ids[b] = jnp.where(first, ids[a], ids[b]), jnp.where(first, ids[b], ids[a])
        if count % 16 == 15:
            tick()
    top_v, top_i = [], []
    for r in range(k):
        best = jnp.max(vals[0], axis=0, keepdims=True)
        row = jnp.min(jnp.where(vals[0] == best, ids[0], n), axis=0, keepdims=True)
        top_v.append(best)
        top_i.append(row)
        won = ids[0] == row
        for d in range(k - 1 - r):
            vals[d] = jnp.where(won, vals[d + 1], vals[d])
            ids[d] = jnp.where(won, ids[d + 1], ids[d])
        tick()
    return jnp.concatenate(top_v, axis=0), jnp.concatenate(top_i, axis=0)


def _top_pairs(s1, s2, k, tick):
    lanes = s1.shape[1]
    sub = lax.broadcasted_iota(I32, (8, lanes), 0)
    lists = [jnp.where(sub <= k // (d + 1) - 1, s1[0:8, :] + s2[d:d + 1, :], -jnp.inf) for d in range(k)]
    tail = s1[8:16, :] + s2[0:1, :]
    tail_pos = (sub + 8) * k
    taken = jnp.zeros((8, lanes), I32)
    vs, aa, bb = [], [], []
    for r in range(k):
        best = jnp.maximum(jnp.max(lists[0], axis=0, keepdims=True), jnp.max(tail, axis=0, keepdims=True))
        head_pos = sub * k + taken
        pos = jnp.minimum(jnp.min(jnp.where(lists[0] == best, head_pos, k * k), axis=0, keepdims=True),
                          jnp.min(jnp.where(tail == best, tail_pos, k * k), axis=0, keepdims=True))
        vs.append(best)
        aa.append(pos // k)
        bb.append(pos % k)
        won = head_pos == pos
        tail = jnp.where(tail_pos == pos, -jnp.inf, tail)
        for d in range(k - 1 - r):
            lists[d] = jnp.where(won, lists[d + 1], lists[d])
        taken = taken + won.astype(I32)
        tick()
    return jnp.concatenate(vs, axis=0), jnp.concatenate(aa, axis=0), jnp.concatenate(bb, axis=0)


def _pick_rows(table, sel, k):
    out = jnp.zeros(sel.shape, table.dtype)
    for a in range(k):
        out = jnp.where(sel == a, table[a:a + 1, :], out)
    return out


def _route_head(qt, keys_ref, tick):
    k = PEER_TOPK
    assert k == 16 and PEER_KEYS == 8 * k
    tops = []
    for half in range(2):
        sc = jnp.dot(keys_ref[half], qt[half * PEER_HALF:(half + 1) * PEER_HALF, :], preferred_element_type=F32)
        tops.append(_topk_rows(sc, k, tick))
    (s1, i1), (s2, i2) = tops
    sc, a_sel, b_sel = _top_pairs(s1, s2, k, tick)
    e = jnp.exp(sc - sc[0:1, :])
    g = e / jnp.sum(e, axis=0, keepdims=True)
    return _pick_rows(i1, a_sel, k), _pick_rows(i2, b_sel, k), g


def _peer_route_u_kernel(hn_ref, hc_ref, wq_ref, keys_ref, u_ref, i1_ref, i2_ref, g_ref, act_ref,
                         i1t_ref, i2t_ref, gt_ref, i1c_ref, i2c_ref, *, blocks):
    o = pl.program_id(0)
    c = pl.program_id(1)
    k = PEER_TOPK

    @pl.when((o == 0) & (c == 0))
    def _():
        i1c_ref[...] = jnp.zeros(i1c_ref.shape, I32)
        i2c_ref[...] = jnp.zeros(i2c_ref.shape, I32)

    @pl.when(c == 0)
    def _():
        act_ref[...] = jnp.zeros(act_ref.shape, F32)

    i1c = i1c_ref[...]
    i2c = i2c_ref[...]
    hc = hc_ref[...]
    acc = [act_ref[...]]

    def piece(pc):
        def run():
            a = lax.dot_general(hc, u_ref[pl.ds(pc * 2 * PEER_KEYS, 2 * PEER_KEYS), :], (((1,), (1,)), ((), ())),
                                preferred_element_type=F32)
            for sub in range(2):
                got = jnp.take_along_axis(a[:, sub * PEER_KEYS:(sub + 1) * PEER_KEYS], i2c, axis=1,
                                          mode="promise_in_bounds")
                acc[0] = jnp.where(i1c == c * blocks + 2 * pc + sub, got, acc[0])
        return run

    tm = hn_ref.shape[0]
    groups = tm // ROUTE_LANES
    hps = keys_ref.shape[0] // 2
    hq = wq_ref.shape[0] // hps
    ticks = hps * groups * (2 * (len(_sorting_network(k)) // 16 + k) + k)
    pieces = _Interleaver([piece(pc) for pc in range(blocks // 2)], every=ticks // (blocks // 2 + 3))
    qt = lax.dot_general(wq_ref[...], hn_ref[...], (((1,), (1,)), ((), ())), preferred_element_type=F32)
    qt = qt.astype(BF16)
    for hd in range(hps):
        rows = pl.ds(pl.multiple_of((c * hps + hd) * k, k), k)
        for gi in range(groups):
            cols = slice(gi * ROUTE_LANES, (gi + 1) * ROUTE_LANES)
            i1, i2, g = _route_head(qt[hd * hq:(hd + 1) * hq, cols], keys_ref.at[pl.ds(2 * hd, 2)], pieces.tick)
            i1t_ref[rows, cols] = i1
            i2t_ref[rows, cols] = i2
            gt_ref[rows, cols] = g
    pieces.drain()
    act_ref[...] = acc[0]

    @pl.when(c == pl.num_programs(1) - 1)
    def _():
        i1n = i1t_ref[...].T
        i2n = i2t_ref[...].T
        i1_ref[...] = i1n
        i2_ref[...] = i2n
        g_ref[...] = gt_ref[...].T
        i1c_ref[...] = i1n
        i2c_ref[...] = i2n


def _peer_route_u(h, w_q, sub_keys, u, tm=512, hps=2):
    t, d = h.shape
    nt = t // tm
    ne = u.shape[0]
    nchunk = PEER_HEADS // hps
    ec = ne // nchunk
    blocks = ec // PEER_KEYS
    hq = hps * (w_q.shape[1] // PEER_HEADS)
    wq_t = w_q.T.astype(BF16)
    keys = sub_keys.reshape(PEER_HEADS * 2, PEER_KEYS, PEER_HALF).astype(BF16)
    slots = PEER_HEADS * PEER_TOPK
    nxt = lambda o, c: (jnp.minimum(o, nt - 1), 0)
    cur = lambda o, c: (jnp.maximum(o - 1, 0), 0)
    kern = functools.partial(_peer_route_u_kernel, blocks=blocks)
    return pl.pallas_call(
        kern, grid=(nt + 1, nchunk),
        in_specs=[pl.BlockSpec((tm, d), nxt), pl.BlockSpec((tm, d), cur),
                  pl.BlockSpec((hq, d), lambda o, c: (c, 0)),
                  pl.BlockSpec((2 * hps, PEER_KEYS, PEER_HALF), lambda o, c: (c, 0, 0)),
                  pl.BlockSpec((ec, d), lambda o, c: (c, 0))],
        out_specs=[pl.BlockSpec((tm, slots), nxt)] * 3 + [pl.BlockSpec((tm, slots), cur)],
        out_shape=[jax.ShapeDtypeStruct((t, slots), I32), jax.ShapeDtypeStruct((t, slots), I32),
                   jax.ShapeDtypeStruct((t, slots), F32), jax.ShapeDtypeStruct((t, slots), F32)],
        scratch_shapes=[pltpu.VMEM((slots, tm), I32), pltpu.VMEM((slots, tm), I32), pltpu.VMEM((slots, tm), F32),
                        pltpu.VMEM((tm, slots), I32), pltpu.VMEM((tm, slots), I32)],
        compiler_params=_params(("arbitrary", "arbitrary")), name="peer_route_u",
    )(h, h, wq_t, keys, u)


def _peer_v_kernel(parts_ref, g_ref, i1_ref, i2_ref, v_ref, x_ref, o_ref, w_ref, wg_ref, *, tm, blocks):
    c = pl.program_id(1)
    nk = PEER_KEYS

    @pl.when(c == 0)
    def _():
        w_ref[...] = g_ref[...] * _gelu(jnp.sum(parts_ref[...], axis=0))
        o_ref[...] = x_ref[...]
        sub = lax.broadcasted_iota(I32, (nk, w_ref.shape[1]), 0)

        def per_token(t, carry):
            wrow = w_ref[pl.ds(t, 1), :]
            lhs = jnp.where(i1_ref[pl.ds(t, 1), :] == sub, wrow, 0.0).astype(BF16)
            rhs = jnp.where(i2_ref[pl.ds(t, 1), :] == sub, 1.0, 0.0).astype(BF16)
            grid = lax.dot_general(lhs, rhs, (((1,), (1,)), ((), ())), preferred_element_type=F32)
            wg_ref[pl.ds(pl.multiple_of(t * WG_PITCH, 8), nk), :] = grid
            return carry

        lax.fori_loop(0, tm, per_token, 0, unroll=64)

    acc = jnp.zeros(o_ref.shape, F32)
    for bk in range(0, blocks, 2):
        i1 = c * blocks + bk
        lhs = jnp.concatenate([wg_ref[pl.ds(i1, tm, stride=WG_PITCH), :],
                               wg_ref[pl.ds(i1 + 1, tm, stride=WG_PITCH), :]], axis=1).astype(BF16)
        rhs = v_ref[bk:bk + 2].reshape(2 * nk, v_ref.shape[2])
        acc += jnp.dot(lhs, rhs, preferred_element_type=F32)
    o_ref[...] += acc


def _peer_v(parts, g, i1, i2, v, x, tm=256, blocks=32):
    t, d = x.shape
    nchunk_u = parts.shape[0]
    slots = g.shape[1]
    nk = PEER_KEYS
    v3 = v.reshape(nk, nk, d)
    kern = functools.partial(_peer_v_kernel, tm=tm, blocks=blocks)
    row = lambda i, c: (i, 0)
    return pl.pallas_call(
        kern, grid=(t // tm, nk // blocks),
        in_specs=[pl.BlockSpec((nchunk_u, tm, slots), lambda i, c: (0, i, 0)),
                  pl.BlockSpec((tm, slots), row), pl.BlockSpec((tm, slots), row), pl.BlockSpec((tm, slots), row),
                  pl.BlockSpec((blocks, nk, d), lambda i, c: (c, 0, 0)), pl.BlockSpec((tm, d), row)],
        out_specs=pl.BlockSpec((tm, d), row),
        out_shape=jax.ShapeDtypeStruct((t, d), F32),
        scratch_shapes=[pltpu.VMEM((tm, slots), F32), pltpu.VMEM((tm * WG_PITCH, nk), F32)],
        compiler_params=_params(("parallel", "arbitrary")), name="peer_v",
    )(parts, g, i1, i2, v3, x)


def _pad_cols(w, n):
    return jnp.pad(w, ((0, 0), (0, n - w.shape[1])))


def _in_proj_kernel(x_ref, g_ref, w_ref, *rest, bounds):
    outs, xn_ref = rest[:-1], rest[-1]
    j = pl.program_id(1)

    @pl.when(j == 0)
    def _():
        x = x_ref[...]
        ms = jnp.mean(x * x, axis=-1, keepdims=True)
        xn_ref[...] = (x * lax.rsqrt(ms + EPS) * g_ref[...]).astype(xn_ref.dtype)

    for (lo, hi), o_ref in zip(bounds, outs):
        @pl.when((j >= lo) & (j < hi))
        def _(o_ref=o_ref):
            o_ref[...] = jnp.dot(xn_ref[...], w_ref[...], preferred_element_type=F32).astype(o_ref.dtype)


PROJ_TILE = 512
IN_GROUPS = ((1024, BF16), (768, BF16), (48, F32), (1024, BF16), (1024, BF16), (1024, BF16), (1536, BF16),
             (16, F32), (3072, BF16))


def _in_proj(x, g, w_in, tm=1024):
    t, d = x.shape
    tn = PROJ_TILE
    ws, bounds, shapes, off, tile = [], [], [], 0, 0
    for n, dtype in IN_GROUPS:
        width = -(-n // tn) * tn
        ws.append(_pad_cols(w_in[:, off:off + n], width))
        bounds.append((tile, tile + width // tn))
        shapes.append(jax.ShapeDtypeStruct((t, width), dtype))
        off += n
        tile += width // tn
    w_all = jnp.concatenate(ws, axis=1).astype(BF16)
    out_specs = [pl.BlockSpec((tm, tn), functools.partial(lambda i, j, lo, hi: (i, jnp.clip(j - lo, 0, hi - lo - 1)),
                                                          lo=lo, hi=hi)) for lo, hi in bounds]
    kern = functools.partial(_in_proj_kernel, bounds=tuple(bounds))
    return pl.pallas_call(
        kern, grid=(t // tm, tile),
        in_specs=[pl.BlockSpec((tm, d), lambda i, j: (i, 0)), pl.BlockSpec((1, d), lambda i, j: (0, 0)),
                  pl.BlockSpec((d, tn), lambda i, j: (0, j))],
        out_specs=out_specs, out_shape=shapes,
        scratch_shapes=[pltpu.VMEM((tm, d), BF16)],
        compiler_params=_params(("parallel", "arbitrary")), name="in_proj",
    )(x, g.reshape(1, d).astype(F32), w_all)


def _mixer(x, positions, b, s, p):
    a_q, a_kv, a_g, l_x, l_g, s_z, s_xbc, s_dt, m_g = _in_proj(x, p["mix_norm_g"], p["w_in"])

    q, kc, vc, ks, vs, kw, vw = _nsa_prep(a_q, a_kv, positions, p["q_norm_g"], p["k_norm_g"], b, s)
    k_cmp, v_cmp = _compress(kc, vc, p["cmp_pe_k"], p["cmp_pe_v"], p["cmp_k_w1"], p["cmp_k_w2"],
                             p["cmp_v_w1"], p["cmp_v_w2"], p["k_norm_g"])
    o_c, bias = _cmp_select(q, k_cmp, v_cmp, s)
    o_s = _sel_attn(q, bias, ks, vs, s)
    o_w = _win_attn(q, kw, vw, s)
    y_b = _rglru(l_x, l_g, positions, p["lru_conv_w"], p["lru_conv_b"], p["lru_w_r"], p["lru_b_r"],
                 p["lru_w_i"], p["lru_b_i"], p["lru_lambda"], b, s)
    y_c = _ssd(s_z, s_xbc, s_dt, p["ssm_conv_w"], p["ssm_conv_b"], p["ssm_dt_bias"], p["ssm_a_log"],
               p["ssm_d"], p["ssm_norm_g"], b, s)
    return _merge(o_c, o_s, o_w, a_g, y_b, y_c, m_g, x, p["w_branch"], p["w_out"])


def _peer(x, p):
    h = _rmsnorm(x, p["ffn_norm_g"])
    i1, i2, g, act = _peer_route_u(h, p["peer_w_q"], p["peer_sub_keys"], p["peer_u"].astype(BF16))
    return _peer_v(act[None], g, i1, i2, p["peer_v"].astype(BF16), x)


_LAYER_PARAMS = ("mix_norm_g", "w_in", "q_norm_g", "k_norm_g", "cmp_pe_k", "cmp_pe_v", "cmp_k_w1", "cmp_k_w2",
                 "cmp_v_w1", "cmp_v_w2", "lru_conv_w", "lru_conv_b", "lru_w_r", "lru_b_r", "lru_w_i", "lru_b_i",
                 "lru_lambda", "ssm_conv_w", "ssm_conv_b", "ssm_dt_bias", "ssm_a_log", "ssm_d", "ssm_norm_g",
                 "w_branch", "w_out", "ffn_norm_g", "peer_w_q", "peer_sub_keys", "peer_u", "peer_v")


def kernel(x, positions, mix_norm_g, w_in, q_norm_g, k_norm_g, cmp_pe_k, cmp_pe_v, cmp_k_w1, cmp_k_w2, cmp_v_w1, cmp_v_w2, lru_conv_w, lru_conv_b, lru_w_r, lru_b_r, lru_w_i, lru_b_i, lru_lambda, ssm_conv_w, ssm_conv_b, ssm_dt_bias, ssm_a_log, ssm_d, ssm_norm_g, w_branch, w_out, ffn_norm_g, peer_w_q, peer_sub_keys, peer_u, peer_v):
    stacked = dict(zip(_LAYER_PARAMS, (mix_norm_g, w_in, q_norm_g, k_norm_g, cmp_pe_k, cmp_pe_v, cmp_k_w1,
                                       cmp_k_w2, cmp_v_w1, cmp_v_w2, lru_conv_w, lru_conv_b, lru_w_r, lru_b_r,
                                       lru_w_i, lru_b_i, lru_lambda, ssm_conv_w, ssm_conv_b, ssm_dt_bias,
                                       ssm_a_log, ssm_d, ssm_norm_g, w_branch, w_out, ffn_norm_g, peer_w_q,
                                       peer_sub_keys, peer_u, peer_v)))
    b, s, d = x.shape
    xf = x.reshape(b * s, d).astype(F32)
    for layer in range(mix_norm_g.shape[0]):
        p = {name: arr[layer] for name, arr in stacked.items()}
        xf = _mixer(xf, positions, b, s, p)
        xf = _peer(xf, p)
    return xf.reshape(b, s, d).astype(x.dtype)
```

```python
import functools
import math

import jax
import jax.numpy as jnp
from jax import lax
from jax.experimental import pallas as pl
from jax.experimental.pallas import tpu as pltpu

F32 = jnp.float32
BF16 = jnp.bfloat16
I32 = jnp.int32

D_MODEL = 1024
HEAD_DIM = 64
N_HEADS = 16
N_KV_GROUPS = 2
HEADS_PER_GROUP = 8
CMP_BLOCK = 32
CMP_STRIDE = 16
CMP_HIDDEN = 256
SEL_BLOCK = 64
N_SELECT = 16
WINDOW = 512
SEL_FORCE = 100.0
ROPE_THETA = 10000.0
SCALE = HEAD_DIM ** -0.5
LOG2E = math.log2(math.e)
LRU_C = 8.0
CONV_WIDTH = 4
SSM_HEADS = 16
SSM_HEAD_DIM = 64
SSM_GROUPS = 2
SSM_STATE = 128
SSM_CHUNK = 128
SSM_INNER = 1024
PEER_HEADS = 8
PEER_KEYS = 128
PEER_HALF = 128
PEER_TOPK = 16
EPS = 1e-6
NEG = -1e30
LANES = 128

VMEM_LIMIT = 56 * 1024 * 1024
SEL_LOOKAHEAD = 3
WIN_SUBTILE = 256
SEL_ACC_ROWS = HEAD_DIM + 16
WG_PITCH = PEER_KEYS + 8
ROUTE_LANES = 128


def _params(sem):
    return pltpu.CompilerParams(dimension_semantics=sem, vmem_limit_bytes=VMEM_LIMIT)


def _gelu(x):
    return 0.5 * x * (1.0 + jnp.tanh(math.sqrt(2.0 / math.pi) * (x + 0.044715 * x * x * x)))


def _sigmoid(x):
    return 0.5 * jnp.tanh(0.5 * x) + 0.5


def _softplus(x):
    return jnp.maximum(x, 0.0) + jnp.log(1.0 + jnp.exp(-jnp.abs(x)))


def _rmsnorm_kernel(x_ref, g_ref, o_ref):
    x = x_ref[...].astype(F32)
    ms = jnp.mean(x * x, axis=-1, keepdims=True)
    o_ref[...] = (x * lax.rsqrt(ms + EPS) * g_ref[...]).astype(o_ref.dtype)


def _rmsnorm(x, g, tm=512):
    t, d = x.shape
    return pl.pallas_call(
        _rmsnorm_kernel, grid=(t // tm,),
        in_specs=[pl.BlockSpec((tm, d), lambda i: (i, 0)), pl.BlockSpec((1, d), lambda i: (0, 0))],
        out_specs=pl.BlockSpec((tm, d), lambda i: (i, 0)),
        out_shape=jax.ShapeDtypeStruct((t, d), BF16),
        compiler_params=_params(("parallel",)), name="rmsnorm")(x, g.reshape(1, d).astype(F32))


def _norm_rope(x, g, cos, sin_signed, bd):
    ms = jnp.dot((x * x).astype(BF16), bd, preferred_element_type=F32)
    y = x * lax.rsqrt(ms + EPS) * g
    lane = lax.broadcasted_iota(I32, y.shape, 1)
    first_half = (lane % HEAD_DIM) < (HEAD_DIM // 2)
    partner = jnp.where(first_half, pltpu.roll(y, LANES - HEAD_DIM // 2, 1), pltpu.roll(y, HEAD_DIM // 2, 1))
    return y * cos + partner * sin_signed


def _nsa_prep_kernel(aq_ref, akv_ref, pos_ref, inv_ref, sgn_ref, gq_ref, gk_ref, bd_ref,
                     q_ref, kc_ref, vc_ref, ks_ref, vs_ref, kw_ref, vw_ref):
    ang = pos_ref[...].astype(F32) * inv_ref[...]
    cos = jnp.cos(ang)
    sin_signed = jnp.sin(ang) * sgn_ref[...]
    bd = bd_ref[...]
    for c in range(D_MODEL // LANES):
        x = aq_ref[:, c * LANES:(c + 1) * LANES].astype(F32)
        y = (_norm_rope(x, gq_ref[...], cos, sin_signed, bd) * (SCALE * LOG2E)).astype(q_ref.dtype)
        q_ref[0, 2 * c] = y[:, :HEAD_DIM]
        q_ref[0, 2 * c + 1] = y[:, HEAD_DIM:]
    outs = (kc_ref, vc_ref, ks_ref, vs_ref, kw_ref, vw_ref)
    for c, o_ref in enumerate(outs):
        x = akv_ref[:, c * LANES:(c + 1) * LANES]
        if c % 2 == 0:
            y = _norm_rope(x.astype(F32), gk_ref[...], cos, sin_signed, bd).astype(o_ref.dtype)
        else:
            y = x.astype(o_ref.dtype)
        o_ref[0, 0] = y[:, :HEAD_DIM]
        o_ref[0, 1] = y[:, HEAD_DIM:]


def _nsa_prep(a_q, a_kv, positions, q_norm_g, k_norm_g, b, s, tm=512):
    half = HEAD_DIM // 2
    lane = jnp.arange(LANES)
    inv = (ROPE_THETA ** (-((lane % half).astype(F32)) / half)).reshape(1, LANES)
    sgn = jnp.where((lane % HEAD_DIM) < half, -1.0, 1.0).astype(F32).reshape(1, LANES)
    bd = jnp.where((lane[:, None] // HEAD_DIM) == (lane[None, :] // HEAD_DIM), 1.0 / HEAD_DIM, 0.0).astype(BF16)
    gq = jnp.tile(q_norm_g.astype(F32), 2).reshape(1, LANES)
    gk = jnp.tile(k_norm_g.astype(F32), 2).reshape(1, LANES)
    nt = s // tm
    row = lambda i: (i, 0)
    const = lambda i: (0, 0)
    kv_shape = jax.ShapeDtypeStruct((b, N_KV_GROUPS, s, HEAD_DIM), BF16)
    kv_spec = pl.BlockSpec((1, N_KV_GROUPS, tm, HEAD_DIM), lambda i: (i // nt, 0, i % nt, 0))
    return pl.pallas_call(
        _nsa_prep_kernel, grid=(b * nt,),
        in_specs=[pl.BlockSpec((tm, D_MODEL), row), pl.BlockSpec((tm, 6 * LANES), row),
                  pl.BlockSpec((tm, 1), row), pl.BlockSpec((1, LANES), const), pl.BlockSpec((1, LANES), const),
                  pl.BlockSpec((1, LANES), const), pl.BlockSpec((1, LANES), const),
                  pl.BlockSpec((LANES, LANES), const)],
        out_specs=[pl.BlockSpec((1, N_HEADS, tm, HEAD_DIM), lambda i: (i // nt, 0, i % nt, 0))] + [kv_spec] * 6,
        out_shape=[jax.ShapeDtypeStruct((b, N_HEADS, s, HEAD_DIM), BF16)] + [kv_shape] * 6,
        compiler_params=_params(("parallel",)), name="nsa_prep",
    )(a_q, a_kv, positions.reshape(b * s, 1).astype(I32), inv, sgn, gq, gk, bd)


def _compress_kernel(uk_ref, uv_ref, pek_ref, pev_ref, kw1_ref, kw2_ref, vw1_ref, vw2_ref, gk_ref,
                     kc_ref, vc_ref):
    half = CMP_STRIDE * HEAD_DIM

    def mlp(u, pe, w1_ref, w2_ref):
        n = u.shape[0]
        ha = jnp.dot(u, w1_ref[:half, :], preferred_element_type=F32)
        hb = jnp.dot(u, w1_ref[half:, :], preferred_element_type=F32)
        bias = jnp.dot(pe, w1_ref[...], preferred_element_type=F32)[0:1, :]
        pre = ha + pltpu.roll(hb, n - 1, 0) + bias
        return jnp.dot(_gelu(pre).astype(BF16), w2_ref[...], preferred_element_type=F32)

    k = mlp(uk_ref[0, 0], pek_ref[...], kw1_ref, kw2_ref)
    ms = jnp.mean(k * k, axis=-1, keepdims=True)
    kc_ref[0, 0] = (k * lax.rsqrt(ms + EPS) * gk_ref[...]).astype(kc_ref.dtype)
    vc_ref[0, 0] = mlp(uv_ref[0, 0], pev_ref[...], vw1_ref, vw2_ref).astype(vc_ref.dtype)


def _compress(kc, vc, pe_k, pe_v, kw1, kw2, vw1, vw2, k_norm_g):
    b, g, s, dh = kc.shape
    ng = s // CMP_STRIDE
    wide = CMP_BLOCK * dh
    uk = kc.reshape(b, g, ng, CMP_STRIDE * dh)
    uv = vc.reshape(b, g, ng, CMP_STRIDE * dh)
    pek = jnp.zeros((8, wide), BF16).at[0].set(pe_k.reshape(wide).astype(BF16))
    pev = jnp.zeros((8, wide), BF16).at[0].set(pe_v.reshape(wide).astype(BF16))
    u_spec = pl.BlockSpec((1, 1, ng, CMP_STRIDE * dh), lambda i, j: (i, j, 0, 0))
    c2 = lambda i, j: (0, 0)
    o_spec = pl.BlockSpec((1, 1, ng, dh), lambda i, j: (i, j, 0, 0))
    o_shape = jax.ShapeDtypeStruct((b, g, ng, dh), BF16)
    return pl.pallas_call(
        _compress_kernel, grid=(b, g),
        in_specs=[u_spec, u_spec, pl.BlockSpec((8, wide), c2), pl.BlockSpec((8, wide), c2),
                  pl.BlockSpec((wide, CMP_HIDDEN), c2), pl.BlockSpec((CMP_HIDDEN, dh), c2),
                  pl.BlockSpec((wide, CMP_HIDDEN), c2), pl.BlockSpec((CMP_HIDDEN, dh), c2),
                  pl.BlockSpec((1, dh), c2)],
        out_specs=[o_spec, o_spec], out_shape=[o_shape, o_shape],
        compiler_params=_params(("parallel", "parallel")), name="nsa_compress",
    )(uk, uv, pek, pev, kw1.astype(BF16), kw2.astype(BF16), vw1.astype(BF16), vw2.astype(BF16),
      k_norm_g.reshape(1, dh).astype(F32))


def _store_head_pair(o_ref, h, o_t, held):
    if h % 2 == 0:
        return o_t
    pair = jnp.concatenate([held, o_t], axis=0).T
    o_ref[0, :, (h - 1) * HEAD_DIM:(h + 1) * HEAD_DIM] = pair.astype(o_ref.dtype)
    return None


def _cmp_select_kernel(q_ref, kc_ref, vc_ref, ovt_ref, o_ref, bias_ref, *, tq, n_cmp, n_pick):
    i = pl.program_id(2)
    chunk = min(LANES, kc_ref.shape[2])
    visible = ((i + 1) * tq - CMP_BLOCK) // CMP_STRIDE + 1
    nchunks = jnp.clip((visible + chunk - 1) // chunk, 1, kc_ref.shape[2] // chunk)
    for n in range(1, kc_ref.shape[2] // chunk + 1):
        pl.when(nchunks == n)(functools.partial(
            _cmp_select_keys, q_ref, kc_ref, vc_ref, ovt_ref, o_ref, bias_ref, nc=n * chunk, i=i, tq=tq,
            n_cmp=n_cmp, n_pick=n_pick))


def _cmp_select_keys(q_ref, kc_ref, vc_ref, ovt_ref, o_ref, bias_ref, *, nc, i, tq, n_cmp, n_pick):
    hg = q_ref.shape[1]
    nsel = ovt_ref.shape[0]
    kc = kc_ref[0, 0, :nc, :]
    vct = vc_ref[0, 0, :nc, :].astype(F32).T.astype(BF16)
    ovt = ovt_ref[:, :nc]
    c = lax.broadcasted_iota(I32, (nc, tq), 0)
    t = i * tq + lax.broadcasted_iota(I32, (nc, tq), 1)
    mask = ((CMP_STRIDE * c + CMP_BLOCK - 1) <= t) & (c < n_cmp)
    score = lambda h: lax.dot_general(kc, q_ref[0, h], (((1,), (1,)), ((), ())), preferred_element_type=F32)
    st_next = score(0)
    psum = jnp.zeros((nc, tq), F32)
    held = None
    for h in range(hg):
        st = st_next
        if h + 1 < hg:
            st_next = score(h + 1)
        st = jnp.where(mask, st, NEG)
        m = jnp.max(st, axis=0, keepdims=True)
        p = jnp.where(mask, jnp.exp2(st - m), 0.0)
        l = jnp.sum(p, axis=0, keepdims=True)
        p = p * jnp.where(l > 0.0, 1.0 / l, 0.0)
        o = jnp.dot(vct, p.astype(BF16), preferred_element_type=F32)
        held = _store_head_pair(o_ref, h, o, held)
        psum = psum + p

    hi = psum.astype(BF16)
    lo = (psum - hi.astype(F32)).astype(BF16)
    imp = jnp.dot(ovt, hi, preferred_element_type=F32) + jnp.dot(ovt, lo, preferred_element_type=F32)
    j = lax.broadcasted_iota(I32, (nsel, tq), 0)
    tt = i * tq + lax.broadcasted_iota(I32, (nsel, tq), 1)
    cur = tt // SEL_BLOCK
    forced = (j == 0) | (j == cur) | (j == cur - 1)
    sc = jnp.where(forced, SEL_FORCE, jnp.where(j * SEL_BLOCK <= tt, imp, -1.0))
    picked = jnp.zeros((nsel, tq), jnp.bool_)
    for _ in range(n_pick):
        mx = jnp.max(sc, axis=0, keepdims=True)
        idx = jnp.min(jnp.where(sc == mx, j, nsel), axis=0, keepdims=True)
        hit = j == idx
        picked = picked | hit
        sc = jnp.where(hit, -jnp.inf, sc)
    bias_ref[0, 0] = jnp.where(picked & (j <= cur), 0.0, NEG).T.astype(bias_ref.dtype)


def _cmp_select(q, k_cmp, v_cmp, s, tq=512):
    b, nh, _, dh = q.shape
    g, hg = N_KV_GROUPS, HEADS_PER_GROUP
    nc = k_cmp.shape[2]
    n_cmp = (s - CMP_BLOCK) // CMP_STRIDE + 1
    nsel = s // SEL_BLOCK
    n_pick = min(N_SELECT, nsel)
    cs = CMP_STRIDE * jnp.arange(nc)
    ss = SEL_BLOCK * jnp.arange(nsel)
    ovt = jnp.clip(jnp.minimum(cs[None, :] + CMP_BLOCK, ss[:, None] + SEL_BLOCK)
                   - jnp.maximum(cs[None, :], ss[:, None]), 0).astype(F32) / CMP_BLOCK
    ovt = jnp.where(jnp.arange(nc)[None, :] < n_cmp, ovt, 0.0).astype(BF16)
    kern = functools.partial(_cmp_select_kernel, tq=tq, n_cmp=n_cmp, n_pick=n_pick)
    return pl.pallas_call(
        kern, grid=(b, g, s // tq),
        in_specs=[pl.BlockSpec((1, hg, tq, dh), lambda bi, gi, i: (bi, gi, i, 0)),
                  pl.BlockSpec((1, 1, nc, dh), lambda bi, gi, i: (bi, gi, 0, 0)),
                  pl.BlockSpec((1, 1, nc, dh), lambda bi, gi, i: (bi, gi, 0, 0)),
                  pl.BlockSpec((nsel, nc), lambda bi, gi, i: (0, 0))],
        out_specs=[pl.BlockSpec((1, tq, hg * dh), lambda bi, gi, i: (bi, i, gi)),
                   pl.BlockSpec((1, 1, tq, nsel), lambda bi, gi, i: (bi, gi, i, 0))],
        out_shape=[jax.ShapeDtypeStruct((b, s, nh * dh), BF16),
                   jax.ShapeDtypeStruct((b, g, s, nsel), BF16)],
        compiler_params=_params(("parallel", "parallel", "parallel")), name="nsa_cmp_select",
    )(q, k_cmp, v_cmp, ovt)


def _sel_attn_kernel(it_ref, jt_ref, q_ref, bias_ref, k_ref, v_ref, o_ref, qa_ref, m_ref, acc_ref, *, tq, tk):
    pid = pl.program_id(2)
    i = it_ref[pid]
    j = jt_ref[pid]
    hg = q_ref.shape[1]
    nsel = bias_ref.shape[3]
    last_j = ((i + 1) * tq - 1) // tk

    @pl.when(j == 0)
    def _():
        for h in range(hg):
            qa_ref[h, :, :nsel] = bias_ref[0, 0]
            qa_ref[h, :, nsel:] = q_ref[0, h]
        m_ref[...] = jnp.full(m_ref.shape, -jnp.inf, F32)
        acc_ref[...] = jnp.zeros(acc_ref.shape, F32)

    def accumulate(diagonal):
        key = j * tk + lax.broadcasted_iota(I32, (tk, nsel), 0)
        blk = lax.broadcasted_iota(I32, (tk, nsel), 1)
        onehot = jnp.where(key // SEL_BLOCK == blk, 1.0, 0.0).astype(BF16)
        ka = jnp.concatenate([onehot, k_ref[0, 0]], axis=1)
        va = jnp.concatenate([v_ref[0, 0].astype(F32), jnp.ones((tk, LANES - HEAD_DIM), F32)], axis=1)
        vat = va.T[:SEL_ACC_ROWS, :].astype(BF16)
        if diagonal:
            kp = j * tk + lax.broadcasted_iota(I32, (tk, tq), 0)
            t = i * tq + lax.broadcasted_iota(I32, (tk, tq), 1)
            causal = kp <= t
        score = lambda h: lax.dot_general(ka, qa_ref[h], (((1,), (1,)), ((), ())), preferred_element_type=F32)
        ahead = [score(h) for h in range(SEL_LOOKAHEAD)]
        for h in range(hg):
            st = ahead.pop(0)
            if h + SEL_LOOKAHEAD < hg:
                ahead.append(score(h + SEL_LOOKAHEAD))
            if diagonal:
                st = jnp.where(causal, st, NEG)
            m_old = m_ref[h]
            m_new = jnp.maximum(m_old, jnp.max(st, axis=0, keepdims=True))
            alpha = jnp.exp2(m_old - m_new)
            p = jnp.exp2(st - m_new)
            acc_ref[h] = alpha * acc_ref[h] + jnp.dot(vat, p.astype(BF16), preferred_element_type=F32)
            m_ref[h] = m_new

    @pl.when(j < last_j)
    def _():
        accumulate(False)

    @pl.when(j == last_j)
    def _():
        accumulate(True)
        held = None
        for h in range(hg):
            acc = acc_ref[h]
            o = acc[:HEAD_DIM, :] / acc[HEAD_DIM:HEAD_DIM + 1, :]
            held = _store_head_pair(o_ref, h, o, held)


def _sel_attn(q, bias, k_s, v_s, s, tq=512, tk=512):
    assert tq <= tk
    b, nh, _, dh = q.shape
    g, hg = N_KV_GROUPS, HEADS_PER_GROUP
    nsel = bias.shape[3]
    tk = min(tk, s)
    pairs = [(i, j) for i in range(s // tq) for j in range(((i + 1) * tq - 1) // tk + 1)]
    it = jnp.asarray([pr[0] for pr in pairs], I32)
    jt = jnp.asarray([pr[1] for pr in pairs], I32)
    q_map = lambda bi, gi, pid, it_ref, jt_ref: (bi, gi, it_ref[pid], 0)
    kv_map = lambda bi, gi, pid, it_ref, jt_ref: (bi, gi, jt_ref[pid], 0)
    kern = functools.partial(_sel_attn_kernel, tq=tq, tk=tk)
    grid_spec = pltpu.PrefetchScalarGridSpec(
        num_scalar_prefetch=2, grid=(b, g, len(pairs)),
        in_specs=[pl.BlockSpec((1, hg, tq, dh), q_map), pl.BlockSpec((1, 1, tq, nsel), q_map),
                  pl.BlockSpec((1, 1, tk, dh), kv_map), pl.BlockSpec((1, 1, tk, dh), kv_map)],
        out_specs=pl.BlockSpec((1, tq, hg * dh), lambda bi, gi, pid, it_ref, jt_ref: (bi, it_ref[pid], gi)),
        scratch_shapes=[pltpu.VMEM((hg, tq, nsel + dh), BF16), pltpu.VMEM((hg, 1, tq), F32),
                        pltpu.VMEM((hg, SEL_ACC_ROWS, tq), F32)])
    return pl.pallas_call(
        kern, grid_spec=grid_spec, out_shape=jax.ShapeDtypeStruct((b, s, nh * dh), BF16),
        compiler_params=_params(("parallel", "parallel", "arbitrary")), name="nsa_sel_attn",
    )(it, jt, q, bias, k_s, v_s)


def _win_attn_kernel(q_ref, *refs, tq, nwin):
    k_refs, v_refs, o_ref = refs[:nwin], refs[nwin:2 * nwin], refs[2 * nwin]
    i = pl.program_id(2)
    hg = q_ref.shape[1]
    k = jnp.concatenate([r[0, 0] for r in k_refs], axis=0)
    vt = jnp.concatenate([r[0, 0] for r in v_refs], axis=0).astype(F32).T.astype(BF16)
    qw = WIN_SUBTILE
    nk = WINDOW + qw
    r = lax.broadcasted_iota(I32, (nk, qw), 0)
    c = lax.broadcasted_iota(I32, (nk, qw), 1)
    band = (r > c) & (r <= c + WINDOW)
    items = [(h, b) for h in range(hg) for b in range(tq // qw)]

    def score(item):
        h, b = item
        return lax.dot_general(k[b * qw:b * qw + nk, :], q_ref[0, h, b * qw:(b + 1) * qw, :],
                               (((1,), (1,)), ((), ())), preferred_element_type=F32)

    st_next = score(items[0])
    held, parts = None, []
    for n, (h, b) in enumerate(items):
        st = st_next
        if n + 1 < len(items):
            st_next = score(items[n + 1])
        mask = band & (i * tq - WINDOW + b * qw + r >= 0)
        st = jnp.where(mask, st, NEG)
        m = jnp.max(st, axis=0, keepdims=True)
        p = jnp.where(mask, jnp.exp2(st - m), 0.0)
        l = jnp.sum(p, axis=0, keepdims=True)
        parts.append(jnp.dot(vt[:, b * qw:b * qw + nk], p.astype(BF16), preferred_element_type=F32) / l)
        if len(parts) == tq // qw:
            held = _store_head_pair(o_ref, h, jnp.concatenate(parts, axis=1), held)
            parts = []


def _win_attn(q, k_w, v_w, s, tq=512):
    b, nh, _, dh = q.shape
    g, hg = N_KV_GROUPS, HEADS_PER_GROUP
    nwin = WINDOW // tq + 1
    pad = ((0, 0), (0, 0), (WINDOW, 0), (0, 0))
    kp, vp = jnp.pad(k_w, pad), jnp.pad(v_w, pad)
    kv_specs = [pl.BlockSpec((1, 1, tq, dh), functools.partial(lambda bi, gi, i, w: (bi, gi, i + w, 0), w=w))
                for w in range(nwin)]
    kern = functools.partial(_win_attn_kernel, tq=tq, nwin=nwin)
    return pl.pallas_call(
        kern, grid=(b, g, s // tq),
        in_specs=[pl.BlockSpec((1, hg, tq, dh), lambda bi, gi, i: (bi, gi, i, 0))] + kv_specs + kv_specs,
        out_specs=pl.BlockSpec((1, tq, hg * dh), lambda bi, gi, i: (bi, i, gi)),
        out_shape=jax.ShapeDtypeStruct((b, s, nh * dh), BF16),
        compiler_params=_params(("parallel", "parallel", "parallel")), name="nsa_win_attn",
    )(q, *([kp] * nwin), *([vp] * nwin))


def _causal_conv(x, win_ref, w_ref, b_ref):
    n = x.shape[0]
    win_ref[8:, :] = x
    out = b_ref[...] + w_ref[CONV_WIDTH - 1:CONV_WIDTH, :] * x
    for k in range(CONV_WIDTH - 1):
        off = 8 - (CONV_WIDTH - 1) + k
        out = out + w_ref[k:k + 1, :] * win_ref[off:off + n, :]
    win_ref[0:8, :] = x[n - 8:, :]
    return out


def _rglru_kernel(x_ref, gate_ref, pos_ref, cw_ref, cb_ref, wr_ref, br_ref, wi_ref, bi_ref, lam_ref,
                  o_ref, tail_ref, h_ref):
    @pl.when(pl.program_id(1) == 0)
    def _():
        tail_ref[0:8, :] = jnp.zeros((8, tail_ref.shape[1]), F32)
        h_ref[...] = jnp.zeros(h_ref.shape, F32)

    n = x_ref.shape[0]
    x = x_ref[...].astype(F32)
    xc = _causal_conv(x, tail_ref, cw_ref, cb_ref)
    xcb = xc.astype(BF16)
    r = _sigmoid(jnp.dot(xcb, wr_ref[...], preferred_element_type=F32) + br_ref[...])
    gi = _sigmoid(jnp.dot(xcb, wi_ref[...], preferred_element_type=F32) + bi_ref[...])
    log_a = -LRU_C * r * _softplus(-lam_ref[...])
    reset = pos_ref[...] == 0
    a = jnp.where(reset, 0.0, jnp.exp(log_a))
    mult = jnp.where(reset, 1.0, jnp.sqrt(jnp.maximum(1.0 - jnp.exp(2.0 * log_a), 0.0)))
    bb = mult * (gi * xc)
    sub = lax.broadcasted_iota(I32, a.shape, 0) % 8
    d = 1
    while d < 8:
        a_sh = pltpu.roll(a, d, 0)
        b_sh = pltpu.roll(bb, d, 0)
        live = sub >= d
        bb = jnp.where(live, a * b_sh + bb, bb)
        a = jnp.where(live, a * a_sh, a)
        d *= 2
    carry = h_ref[...]
    hs = []
    for g in range(n // 8):
        h = bb[8 * g:8 * g + 8, :] + a[8 * g:8 * g + 8, :] * carry
        carry = h[7:8, :]
        hs.append(h)
    h_ref[...] = carry
    o_ref[...] = (jnp.concatenate(hs, axis=0) * _gelu(gate_ref[...].astype(F32))).astype(o_ref.dtype)


def _block_diag(w):
    nb, bs, _ = w.shape
    eye = jnp.eye(nb, dtype=w.dtype)
    return (w[:, :, None, :] * eye[:, None, :, None]).reshape(nb * bs, nb * bs)


def _rglru(l_x, l_g, positions, conv_w, conv_b, w_r, b_r, w_i, b_i, lam, b, s, ts=256):
    d = l_x.shape[1]
    nt = s // ts
    row = lambda bi, i: (bi * nt + i, 0)
    c2 = lambda bi, i: (0, 0)
    vec = lambda v: v.reshape(1, d).astype(F32)
    return pl.pallas_call(
        _rglru_kernel, grid=(b, nt),
        in_specs=[pl.BlockSpec((ts, d), row), pl.BlockSpec((ts, d), row), pl.BlockSpec((ts, 1), row),
                  pl.BlockSpec((CONV_WIDTH, d), c2), pl.BlockSpec((1, d), c2),
                  pl.BlockSpec((d, d), c2), pl.BlockSpec((1, d), c2),
                  pl.BlockSpec((d, d), c2), pl.BlockSpec((1, d), c2), pl.BlockSpec((1, d), c2)],
        out_specs=pl.BlockSpec((ts, d), row),
        out_shape=jax.ShapeDtypeStruct((b * s, d), BF16),
        scratch_shapes=[pltpu.VMEM((8 + ts, d), F32), pltpu.VMEM((1, d), F32)],
        compiler_params=_params(("parallel", "arbitrary")), name="rglru",
    )(l_x, l_g, positions.reshape(b * s, 1).astype(I32), conv_w.astype(F32), vec(conv_b),
      _block_diag(w_r).astype(BF16), vec(b_r), _block_diag(w_i).astype(BF16), vec(b_i), vec(lam))


def _ssd_kernel(z_ref, xbc_ref, dt_ref, cw_ref, cb_ref, dtb_ref, alog_ref, dfull_ref, ng_ref,
                o_ref, tail_ref, state_ref, y_ref):
    @pl.when(pl.program_id(1) == 0)
    def _():
        tail_ref[0:8, :] = jnp.zeros((8, tail_ref.shape[1]), F32)
        state_ref[...] = jnp.zeros(state_ref.shape, F32)

    n = xbc_ref.shape[0]
    hg = SSM_HEADS // SSM_GROUPS
    x = xbc_ref[...].astype(F32)
    xc = _causal_conv(x, tail_ref, cw_ref, cb_ref)
    xc = xc * _sigmoid(xc)
    xs = xc[:, :SSM_INNER]
    dt = _softplus(dt_ref[...] + dtb_ref[...])
    adt = dt * (-jnp.exp(alog_ref[...]))
    row = lax.broadcasted_iota(I32, adt.shape, 0)
    acs = adt
    d = 1
    while d < n:
        acs = acs + jnp.where(row >= d, pltpu.roll(acs, d, 0), 0.0)
        d *= 2
    acs_t = acs.T
    li = lax.broadcasted_iota(I32, (n, n), 0)
    si = lax.broadcasted_iota(I32, (n, n), 1)
    tri = li >= si
    for g in range(SSM_GROUPS):
        bm = xc[:, SSM_INNER + g * SSM_STATE:SSM_INNER + (g + 1) * SSM_STATE].astype(BF16)
        cm = xc[:, SSM_INNER + (SSM_GROUPS + g) * SSM_STATE:SSM_INNER + (SSM_GROUPS + g + 1) * SSM_STATE].astype(BF16)
        cb = lax.dot_general(cm, bm, (((1,), (1,)), ((), ())), preferred_element_type=F32)
        bm_t = bm.T
        for hh in range(hg):
            h = g * hg + hh
            acol = jnp.broadcast_to(acs[:, h:h + 1], (n, n))
            arow = acs_t[h:h + 1, :]
            decay = jnp.exp(jnp.where(tri, acol - arow, NEG))
            acol_p = acol[:, :SSM_HEAD_DIM]
            xh = xs[:, h * SSM_HEAD_DIM:(h + 1) * SSM_HEAD_DIM] * dt[:, h:h + 1]
            a_last = acol_p[n - 1:n, :]
            y = jnp.dot((cb * decay).astype(BF16), xh.astype(BF16), preferred_element_type=F32)
            st = state_ref[h]
            y = y + jnp.dot(cm, st.astype(BF16), preferred_element_type=F32) * jnp.exp(acol_p)
            upd = jnp.dot(bm_t, (xh * jnp.exp(a_last - acol_p)).astype(BF16), preferred_element_type=F32)
            state_ref[h] = jnp.exp(a_last) * st + upd
            y_ref[:, h * SSM_HEAD_DIM:(h + 1) * SSM_HEAD_DIM] = y
    z = z_ref[...].astype(F32)
    y = (y_ref[...] + dfull_ref[...] * xs) * (z * _sigmoid(z))
    gw = SSM_INNER // SSM_GROUPS
    for g in range(SSM_GROUPS):
        yg = y[:, g * gw:(g + 1) * gw]
        ms = jnp.mean(yg * yg, axis=-1, keepdims=True)
        o_ref[:, g * gw:(g + 1) * gw] = (yg * lax.rsqrt(ms + EPS) * ng_ref[:, g * gw:(g + 1) * gw]).astype(o_ref.dtype)


def _ssd(s_z, s_xbc, s_dt, conv_w, conv_b, dt_bias, a_log, d_skip, norm_g, b, s):
    n = SSM_CHUNK
    nt = s // n
    c = s_xbc.shape[1]
    row = lambda bi, i: (bi * nt + i, 0)
    c2 = lambda bi, i: (0, 0)
    pad_h = lambda v: jnp.zeros((1, LANES), F32).at[0, :SSM_HEADS].set(v.astype(F32))
    dfull = jnp.repeat(d_skip.astype(F32), SSM_HEAD_DIM).reshape(1, SSM_INNER)
    return pl.pallas_call(
        _ssd_kernel, grid=(b, nt),
        in_specs=[pl.BlockSpec((n, SSM_INNER), row), pl.BlockSpec((n, c), row), pl.BlockSpec((n, LANES), row),
                  pl.BlockSpec((CONV_WIDTH, c), c2), pl.BlockSpec((1, c), c2),
                  pl.BlockSpec((1, LANES), c2), pl.BlockSpec((1, LANES), c2),
                  pl.BlockSpec((1, SSM_INNER), c2), pl.BlockSpec((1, SSM_INNER), c2)],
        out_specs=pl.BlockSpec((n, SSM_INNER), row),
        out_shape=jax.ShapeDtypeStruct((b * s, SSM_INNER), BF16),
        scratch_shapes=[pltpu.VMEM((8 + n, c), F32), pltpu.VMEM((SSM_HEADS, SSM_STATE, SSM_HEAD_DIM), F32),
                        pltpu.VMEM((n, SSM_INNER), F32)],
        compiler_params=_params(("parallel", "arbitrary")), name="ssd",
    )(s_z, s_xbc, s_dt, conv_w.astype(F32), conv_b.reshape(1, c).astype(F32), pad_h(dt_bias), pad_h(a_log),
      dfull, norm_g.reshape(1, SSM_INNER).astype(F32))


def _merge_kernel(oc_ref, os_ref, ow_ref, ag_ref, ex_ref, yb_ref, yc_ref, mg_ref, x_ref, wb_ref, wo_ref, o_ref):
    gates = _sigmoid(ag_ref[...])
    hi = gates.astype(BF16)
    lo = (gates - hi.astype(F32)).astype(BF16)
    spread = lambda k: (jnp.dot(hi, ex_ref[k], preferred_element_type=F32)
                        + jnp.dot(lo, ex_ref[k], preferred_element_type=F32))
    ya = (spread(0) * oc_ref[...].astype(F32) + spread(1) * os_ref[...].astype(F32)
          + spread(2) * ow_ref[...].astype(F32))
    d = D_MODEL
    merged = _sigmoid(mg_ref[:, 0:d].astype(F32)) * jnp.dot(ya.astype(BF16), wb_ref[0], preferred_element_type=F32)
    merged += _sigmoid(mg_ref[:, d:2 * d].astype(F32)) * jnp.dot(yb_ref[...], wb_ref[1], preferred_element_type=F32)
    merged += _sigmoid(mg_ref[:, 2 * d:3 * d].astype(F32)) * jnp.dot(yc_ref[...], wb_ref[2], preferred_element_type=F32)
    o_ref[...] = x_ref[...] + jnp.dot(merged.astype(BF16), wo_ref[...], preferred_element_type=F32)


def _merge(o_c, o_s, o_w, a_g, y_b, y_c, m_g, x, w_branch, w_out, tm=512):
    t, d = x.shape
    row = lambda i: (i, 0)
    lane = jnp.arange(LANES)[:, None]
    col = jnp.arange(d)[None, :]
    expand = jnp.stack([(lane == 3 * (col // HEAD_DIM) + k) for k in range(3)]).astype(BF16)
    return pl.pallas_call(
        _merge_kernel, grid=(t // tm,),
        in_specs=[pl.BlockSpec((tm, d), row), pl.BlockSpec((tm, d), row), pl.BlockSpec((tm, d), row),
                  pl.BlockSpec((tm, LANES), row), pl.BlockSpec((3, LANES, d), lambda i: (0, 0, 0)),
                  pl.BlockSpec((tm, d), row), pl.BlockSpec((tm, d), row), pl.BlockSpec((tm, 3 * d), row),
                  pl.BlockSpec((tm, d), row),
                  pl.BlockSpec((3, d, d), lambda i: (0, 0, 0)), pl.BlockSpec((d, d), lambda i: (0, 0))],
        out_specs=pl.BlockSpec((tm, d), row),
        out_shape=jax.ShapeDtypeStruct((t, d), F32),
        compiler_params=_params(("parallel",)), name="merge",
    )(o_c.reshape(t, d), o_s.reshape(t, d), o_w.reshape(t, d), a_g, expand, y_b, y_c, m_g, x,
      w_branch.astype(BF16), w_out.astype(BF16))


class _Interleaver:
    def __init__(self, pieces, every):
        self.pieces, self.every, self.count = list(pieces), every, 0

    def tick(self):
        self.count += 1
        if self.pieces and self.count % self.every == 0:
            self.pieces.pop(0)()

    def drain(self):
        while self.pieces:
            self.pieces.pop(0)()


def _sorting_network(n):
    pairs = []

    def merge(lo, m, r):
        step = r * 2
        if step < m:
            merge(lo, m, step)
            merge(lo + r, m, step)
            pairs.extend((i, i + r) for i in range(lo + r, lo + m - r, step))
        else:
            pairs.append((lo, lo + r))

    def sort(lo, m):
        if m > 1:
            sort(lo, m // 2)
            sort(lo + m // 2, m // 2)
            merge(lo, m, 1)

    sort(0, n)
    return pairs


def _topk_rows(s, k, tick):
    n, lanes = s.shape
    assert n == 8 * k
    sub = lax.broadcasted_iota(I32, (8, lanes), 0)
    vals = [s[8 * j:8 * j + 8, :] for j in range(k)]
    ids = [sub + 8 * j for j in range(k)]
    for count, (a, b) in enumerate(_sorting_network(k)):
        first = (vals[a] > vals[b]) | ((vals[a] == vals[b]) & (ids[a] < ids[b]))
        vals[a], vals[b] = jnp.where(first, vals[a], vals[b]), jnp.where(first, vals[b], vals[a])
        ids[a], ids[b] = jnp.where(first, ids[a], ids[b]), jnp.where(first, ids[b], ids[a])
        if count % 16 == 15:
            tick()
    top_v, top_i = [], []
    for r in range(k):
        best = jnp.max(vals[0], axis=0, keepdims=True)
        row = jnp.min(jnp.where(vals[0] == best, ids[0], n), axis=0, keepdims=True)
        top_v.append(best)
        top_i.append(row)
        won = ids[0] == row
        for d in range(k - 1 - r):
            vals[d] = jnp.where(won, vals[d + 1], vals[d])
            ids[d] = jnp.where(won, ids[d + 1], ids[d])
        tick()
    return jnp.concatenate(top_v, axis=0), jnp.concatenate(top_i, axis=0)


def _top_pairs(s1, s2, k, tick):
    lanes = s1.shape[1]
    sub = lax.broadcasted_iota(I32, (8, lanes), 0)
    lists = [jnp.where(sub <= k // (d + 1) - 1, s1[0:8, :] + s2[d:d + 1, :], -jnp.inf) for d in range(k)]
    tail = s1[8:16, :] + s2[0:1, :]
    tail_pos = (sub + 8) * k
    taken = jnp.zeros((8, lanes), I32)
    vs, aa, bb = [], [], []
    for r in range(k):
        best = jnp.maximum(jnp.max(lists[0], axis=0, keepdims=True), jnp.max(tail, axis=0, keepdims=True))
        head_pos = sub * k + taken
        pos = jnp.minimum(jnp.min(jnp.where(lists[0] == best, head_pos, k * k), axis=0, keepdims=True),
                          jnp.min(jnp.where(tail == best, tail_pos, k * k), axis=0, keepdims=True))
        vs.append(best)
        aa.append(pos // k)
        bb.append(pos % k)
        won = head_pos == pos
        tail = jnp.where(tail_pos == pos, -jnp.inf, tail)
        for d in range(k - 1 - r):
            lists[d] = jnp.where(won, lists[d + 1], lists[d])
        taken = taken + won.astype(I32)
        tick()
    return jnp.concatenate(vs, axis=0), jnp.concatenate(aa, axis=0), jnp.concatenate(bb, axis=0)


def _pick_rows(table, sel, k):
    out = jnp.zeros(sel.shape, table.dtype)
    for a in range(k):
        out = jnp.where(sel == a, table[a:a + 1, :], out)
    return out


def _route_head(qt, keys_ref, tick):
    k = PEER_TOPK
    assert k == 16 and PEER_KEYS == 8 * k
    tops = []
    for half in range(2):
        sc = jnp.dot(keys_ref[half], qt[half * PEER_HALF:(half + 1) * PEER_HALF, :], preferred_element_type=F32)
        tops.append(_topk_rows(sc, k, tick))
    (s1, i1), (s2, i2) = tops
    sc, a_sel, b_sel = _top_pairs(s1, s2, k, tick)
    e = jnp.exp(sc - sc[0:1, :])
    g = e / jnp.sum(e, axis=0, keepdims=True)
    return _pick_rows(i1, a_sel, k), _pick_rows(i2, b_sel, k), g


def _peer_route_u_kernel(hn_ref, hc_ref, wq_ref, keys_ref, u_ref, i1_ref, i2_ref, g_ref, act_ref,
                         i1t_ref, i2t_ref, gt_ref, i1c_ref, i2c_ref, *, blocks):
    o = pl.program_id(0)
    c = pl.program_id(1)
    k = PEER_TOPK

    @pl.when((o == 0) & (c == 0))
    def _():
        i1c_ref[...] = jnp.zeros(i1c_ref.shape, I32)
        i2c_ref[...] = jnp.zeros(i2c_ref.shape, I32)

    @pl.when(c == 0)
    def _():
        act_ref[...] = jnp.zeros(act_ref.shape, F32)

    i1c = i1c_ref[...]
    i2c = i2c_ref[...]
    hc = hc_ref[...]
    acc = [act_ref[...]]

    def piece(pc):
        def run():
            a = lax.dot_general(hc, u_ref[pl.ds(pc * 2 * PEER_KEYS, 2 * PEER_KEYS), :], (((1,), (1,)), ((), ())),
                                preferred_element_type=F32)
            for sub in range(2):
                got = jnp.take_along_axis(a[:, sub * PEER_KEYS:(sub + 1) * PEER_KEYS], i2c, axis=1,
                                          mode="promise_in_bounds")
                acc[0] = jnp.where(i1c == c * blocks + 2 * pc + sub, got, acc[0])
        return run

    tm = hn_ref.shape[0]
    groups = tm // ROUTE_LANES
    hps = keys_ref.shape[0] // 2
    hq = wq_ref.shape[0] // hps
    ticks = hps * groups * (2 * (len(_sorting_network(k)) // 16 + k) + k)
    pieces = _Interleaver([piece(pc) for pc in range(blocks // 2)], every=ticks // (blocks // 2 + 3))
    qt = lax.dot_general(wq_ref[...], hn_ref[...], (((1,), (1,)), ((), ())), preferred_element_type=F32)
    qt = qt.astype(BF16)
    for hd in range(hps):
        rows = pl.ds(pl.multiple_of((c * hps + hd) * k, k), k)
        for gi in range(groups):
            cols = slice(gi * ROUTE_LANES, (gi + 1) * ROUTE_LANES)
            i1, i2, g = _route_head(qt[hd * hq:(hd + 1) * hq, cols], keys_ref.at[pl.ds(2 * hd, 2)], pieces.tick)
            i1t_ref[rows, cols] = i1
            i2t_ref[rows, cols] = i2
            gt_ref[rows, cols] = g
    pieces.drain()
    act_ref[...] = acc[0]

    @pl.when(c == pl.num_programs(1) - 1)
    def _():
        i1n = i1t_ref[...].T
        i2n = i2t_ref[...].T
        i1_ref[...] = i1n
        i2_ref[...] = i2n
        g_ref[...] = gt_ref[...].T
        i1c_ref[...] = i1n
        i2c_ref[...] = i2n


def _peer_route_u(h, w_q, sub_keys, u, tm=512, hps=2):
    t, d = h.shape
    nt = t // tm
    ne = u.shape[0]
    nchunk = PEER_HEADS // hps
    ec = ne // nchunk
    blocks = ec // PEER_KEYS
    hq = hps * (w_q.shape[1] // PEER_HEADS)
    wq_t = w_q.T.astype(BF16)
    keys = sub_keys.reshape(PEER_HEADS * 2, PEER_KEYS, PEER_HALF).astype(BF16)
    slots = PEER_HEADS * PEER_TOPK
    nxt = lambda o, c: (jnp.minimum(o, nt - 1), 0)
    cur = lambda o, c: (jnp.maximum(o - 1, 0), 0)
    kern = functools.partial(_peer_route_u_kernel, blocks=blocks)
    return pl.pallas_call(
        kern, grid=(nt + 1, nchunk),
        in_specs=[pl.BlockSpec((tm, d), nxt), pl.BlockSpec((tm, d), cur),
                  pl.BlockSpec((hq, d), lambda o, c: (c, 0)),
                  pl.BlockSpec((2 * hps, PEER_KEYS, PEER_HALF), lambda o, c: (c, 0, 0)),
                  pl.BlockSpec((ec, d), lambda o, c: (c, 0))],
        out_specs=[pl.BlockSpec((tm, slots), nxt)] * 3 + [pl.BlockSpec((tm, slots), cur)],
        out_shape=[jax.ShapeDtypeStruct((t, slots), I32), jax.ShapeDtypeStruct((t, slots), I32),
                   jax.ShapeDtypeStruct((t, slots), F32), jax.ShapeDtypeStruct((t, slots), F32)],
        scratch_shapes=[pltpu.VMEM((slots, tm), I32), pltpu.VMEM((slots, tm), I32), pltpu.VMEM((slots, tm), F32),
                        pltpu.VMEM((tm, slots), I32), pltpu.VMEM((tm, slots), I32)],
        compiler_params=_params(("arbitrary", "arbitrary")), name="peer_route_u",
    )(h, h, wq_t, keys, u)


def _peer_v_kernel(act_ref, g_ref, i1_ref, i2_ref, v_ref, x_ref, o_ref, w_ref, wg_ref, *, tm, blocks):
    c = pl.program_id(1)
    nk = PEER_KEYS

    @pl.when(c == 0)
    def _():
        w_ref[...] = g_ref[...] * _gelu(act_ref[...])
        o_ref[...] = x_ref[...]
        sub = lax.broadcasted_iota(I32, (nk, w_ref.shape[1]), 0)

        def per_token(t, carry):
            wrow = w_ref[pl.ds(t, 1), :]
            lhs = jnp.where(i1_ref[pl.ds(t, 1), :] == sub, wrow, 0.0).astype(BF16)
            rhs = jnp.where(i2_ref[pl.ds(t, 1), :] == sub, 1.0, 0.0).astype(BF16)
            grid = lax.dot_general(lhs, rhs, (((1,), (1,)), ((), ())), preferred_element_type=F32)
            wg_ref[pl.ds(pl.multiple_of(t * WG_PITCH, 8), nk), :] = grid
            return carry

        lax.fori_loop(0, tm, per_token, 0, unroll=64)

    acc = jnp.zeros(o_ref.shape, F32)
    for bk in range(0, blocks, 2):
        i1 = c * blocks + bk
        lhs = jnp.concatenate([wg_ref[pl.ds(i1, tm, stride=WG_PITCH), :],
                               wg_ref[pl.ds(i1 + 1, tm, stride=WG_PITCH), :]], axis=1).astype(BF16)
        rhs = v_ref[bk:bk + 2].reshape(2 * nk, v_ref.shape[2])
        acc += jnp.dot(lhs, rhs, preferred_element_type=F32)
    o_ref[...] += acc


def _peer_v(act, g, i1, i2, v, x, tm=256, blocks=32):
    t, d = x.shape
    slots = g.shape[1]
    nk = PEER_KEYS
    v3 = v.reshape(nk, nk, d)
    kern = functools.partial(_peer_v_kernel, tm=tm, blocks=blocks)
    row = lambda i, c: (i, 0)
    return pl.pallas_call(
        kern, grid=(t // tm, nk // blocks),
        in_specs=[pl.BlockSpec((tm, slots), row),
                  pl.BlockSpec((tm, slots), row), pl.BlockSpec((tm, slots), row), pl.BlockSpec((tm, slots), row),
                  pl.BlockSpec((blocks, nk, d), lambda i, c: (c, 0, 0)), pl.BlockSpec((tm, d), row)],
        out_specs=pl.BlockSpec((tm, d), row),
        out_shape=jax.ShapeDtypeStruct((t, d), F32),
        scratch_shapes=[pltpu.VMEM((tm, slots), F32), pltpu.VMEM((tm * WG_PITCH, nk), F32)],
        compiler_params=_params(("parallel", "arbitrary")), name="peer_v",
    )(act, g, i1, i2, v3, x)


def _pad_cols(w, n):
    return jnp.pad(w, ((0, 0), (0, n - w.shape[1])))


def _in_proj_kernel(x_ref, g_ref, w_ref, *rest, bounds):
    outs, xn_ref = rest[:-1], rest[-1]
    j = pl.program_id(1)

    @pl.when(j == 0)
    def _():
        x = x_ref[...]
        ms = jnp.mean(x * x, axis=-1, keepdims=True)
        xn_ref[...] = (x * lax.rsqrt(ms + EPS) * g_ref[...]).astype(xn_ref.dtype)

    for (lo, hi), o_ref in zip(bounds, outs):
        @pl.when((j >= lo) & (j < hi))
        def _(o_ref=o_ref):
            o_ref[...] = jnp.dot(xn_ref[...], w_ref[...], preferred_element_type=F32).astype(o_ref.dtype)


PROJ_TILE = 512
IN_GROUPS = ((1024, BF16), (768, BF16), (48, F32), (1024, BF16), (1024, BF16), (1024, BF16), (1536, BF16),
             (16, F32), (3072, BF16))


def _in_proj(x, g, w_in, tm=1024):
    t, d = x.shape
    tn = PROJ_TILE
    ws, bounds, shapes, off, tile = [], [], [], 0, 0
    for n, dtype in IN_GROUPS:
        width = -(-n // tn) * tn
        ws.append(_pad_cols(w_in[:, off:off + n], width))
        bounds.append((tile, tile + width // tn))
        shapes.append(jax.ShapeDtypeStruct((t, width), dtype))
        off += n
        tile += width // tn
    w_all = jnp.concatenate(ws, axis=1).astype(BF16)
    out_specs = [pl.BlockSpec((tm, tn), functools.partial(lambda i, j, lo, hi: (i, jnp.clip(j - lo, 0, hi - lo - 1)),
                                                          lo=lo, hi=hi)) for lo, hi in bounds]
    kern = functools.partial(_in_proj_kernel, bounds=tuple(bounds))
    return pl.pallas_call(
        kern, grid=(t // tm, tile),
        in_specs=[pl.BlockSpec((tm, d), lambda i, j: (i, 0)), pl.BlockSpec((1, d), lambda i, j: (0, 0)),
                  pl.BlockSpec((d, tn), lambda i, j: (0, j))],
        out_specs=out_specs, out_shape=shapes,
        scratch_shapes=[pltpu.VMEM((tm, d), BF16)],
        compiler_params=_params(("parallel", "arbitrary")), name="in_proj",
    )(x, g.reshape(1, d).astype(F32), w_all)


def _mixer(x, positions, b, s, p):
    a_q, a_kv, a_g, l_x, l_g, s_z, s_xbc, s_dt, m_g = _in_proj(x, p["mix_norm_g"], p["w_in"])

    q, kc, vc, ks, vs, kw, vw = _nsa_prep(a_q, a_kv, positions, p["q_norm_g"], p["k_norm_g"], b, s)
    k_cmp, v_cmp = _compress(kc, vc, p["cmp_pe_k"], p["cmp_pe_v"], p["cmp_k_w1"], p["cmp_k_w2"],
                             p["cmp_v_w1"], p["cmp_v_w2"], p["k_norm_g"])
    o_c, bias = _cmp_select(q, k_cmp, v_cmp, s)
    o_s = _sel_attn(q, bias, ks, vs, s)
    o_w = _win_attn(q, kw, vw, s)
    y_b = _rglru(l_x, l_g, positions, p["lru_conv_w"], p["lru_conv_b"], p["lru_w_r"], p["lru_b_r"],
                 p["lru_w_i"], p["lru_b_i"], p["lru_lambda"], b, s)
    y_c = _ssd(s_z, s_xbc, s_dt, p["ssm_conv_w"], p["ssm_conv_b"], p["ssm_dt_bias"], p["ssm_a_log"],
               p["ssm_d"], p["ssm_norm_g"], b, s)
    return _merge(o_c, o_s, o_w, a_g, y_b, y_c, m_g, x, p["w_branch"], p["w_out"])


def _peer(x, p):
    h = _rmsnorm(x, p["ffn_norm_g"])
    i1, i2, g, act = _peer_route_u(h, p["peer_w_q"], p["peer_sub_keys"], p["peer_u"].astype(BF16))
    return _peer_v(act, g, i1, i2, p["peer_v"].astype(BF16), x)


_LAYER_PARAMS = ("mix_norm_g", "w_in", "q_norm_g", "k_norm_g", "cmp_pe_k", "cmp_pe_v", "cmp_k_w1", "cmp_k_w2",
                 "cmp_v_w1", "cmp_v_w2", "lru_conv_w", "lru_conv_b", "lru_w_r", "lru_b_r", "lru_w_i", "lru_b_i",
                 "lru_lambda", "ssm_conv_w", "ssm_conv_b", "ssm_dt_bias", "ssm_a_log", "ssm_d", "ssm_norm_g",
                 "w_branch", "w_out", "ffn_norm_g", "peer_w_q", "peer_sub_keys", "peer_u", "peer_v")


def kernel(x, positions, mix_norm_g, w_in, q_norm_g, k_norm_g, cmp_pe_k, cmp_pe_v, cmp_k_w1, cmp_k_w2, cmp_v_w1, cmp_v_w2, lru_conv_w, lru_conv_b, lru_w_r, lru_b_r, lru_w_i, lru_b_i, lru_lambda, ssm_conv_w, ssm_conv_b, ssm_dt_bias, ssm_a_log, ssm_d, ssm_norm_g, w_branch, w_out, ffn_norm_g, peer_w_q, peer_sub_keys, peer_u, peer_v):
    stacked = dict(zip(_LAYER_PARAMS, (mix_norm_g, w_in, q_norm_g, k_norm_g, cmp_pe_k, cmp_pe_v, cmp_k_w1,
                                       cmp_k_w2, cmp_v_w1, cmp_v_w2, lru_conv_w, lru_conv_b, lru_w_r, lru_b_r,
                                       lru_w_i, lru_b_i, lru_lambda, ssm_conv_w, ssm_conv_b, ssm_dt_bias,
                                       ssm_a_log, ssm_d, ssm_norm_g, w_branch, w_out, ffn_norm_g, peer_w_q,
                                       peer_sub_keys, peer_u, peer_v)))
    b, s, d = x.shape
    xf = x.reshape(b * s, d).astype(F32)
    for layer in range(mix_norm_g.shape[0]):
        p = {name: arr[layer] for name, arr in stacked.items()}
        xf = _mixer(xf, positions, b, s, p)
        xf = _peer(xf, p)
    return xf.reshape(b, s, d).astype(x.dtype)
```

```python
import functools
import math

import jax
import jax.numpy as jnp
from jax import lax
from jax.experimental import pallas as pl
from jax.experimental.pallas import tpu as pltpu

F32 = jnp.float32
BF16 = jnp.bfloat16
I32 = jnp.int32

D_MODEL = 1024
HEAD_DIM = 64
N_HEADS = 16
N_KV_GROUPS = 2
HEADS_PER_GROUP = 8
CMP_BLOCK = 32
CMP_STRIDE = 16
CMP_HIDDEN = 256
SEL_BLOCK = 64
N_SELECT = 16
WINDOW = 512
SEL_FORCE = 100.0
ROPE_THETA = 10000.0
SCALE = HEAD_DIM ** -0.5
LOG2E = math.log2(math.e)
LRU_C = 8.0
CONV_WIDTH = 4
SSM_HEADS = 16
SSM_HEAD_DIM = 64
SSM_GROUPS = 2
SSM_STATE = 128
SSM_CHUNK = 128
SSM_INNER = 1024
PEER_HEADS = 8
PEER_KEYS = 128
PEER_HALF = 128
PEER_TOPK = 16
EPS = 1e-6
NEG = -1e30
LANES = 128

VMEM_LIMIT = 56 * 1024 * 1024
SEL_LOOKAHEAD = 3
WIN_SUBTILE = 256
SEL_SUBTILE = 256
SEL_ACC_ROWS = HEAD_DIM + 16
WG_PITCH = PEER_KEYS + 8
ROUTE_LANES = 128


def _params(sem):
    return pltpu.CompilerParams(dimension_semantics=sem, vmem_limit_bytes=VMEM_LIMIT)


def _gelu(x):
    return 0.5 * x * (1.0 + jnp.tanh(math.sqrt(2.0 / math.pi) * (x + 0.044715 * x * x * x)))


def _sigmoid(x):
    return 0.5 * jnp.tanh(0.5 * x) + 0.5


def _softplus(x):
    return jnp.maximum(x, 0.0) + jnp.log(1.0 + jnp.exp(-jnp.abs(x)))


def _rmsnorm_kernel(x_ref, g_ref, o_ref):
    x = x_ref[...].astype(F32)
    ms = jnp.mean(x * x, axis=-1, keepdims=True)
    o_ref[...] = (x * lax.rsqrt(ms + EPS) * g_ref[...]).astype(o_ref.dtype)


def _rmsnorm(x, g, tm=512):
    t, d = x.shape
    return pl.pallas_call(
        _rmsnorm_kernel, grid=(t // tm,),
        in_specs=[pl.BlockSpec((tm, d), lambda i: (i, 0)), pl.BlockSpec((1, d), lambda i: (0, 0))],
        out_specs=pl.BlockSpec((tm, d), lambda i: (i, 0)),
        out_shape=jax.ShapeDtypeStruct((t, d), BF16),
        compiler_params=_params(("parallel",)), name="rmsnorm")(x, g.reshape(1, d).astype(F32))


def _norm_rope(x, g, cos, sin_signed, bd):
    ms = jnp.dot((x * x).astype(BF16), bd, preferred_element_type=F32)
    y = x * lax.rsqrt(ms + EPS) * g
    lane = lax.broadcasted_iota(I32, y.shape, 1)
    first_half = (lane % HEAD_DIM) < (HEAD_DIM // 2)
    partner = jnp.where(first_half, pltpu.roll(y, LANES - HEAD_DIM // 2, 1), pltpu.roll(y, HEAD_DIM // 2, 1))
    return y * cos + partner * sin_signed


def _nsa_prep_kernel(aq_ref, akv_ref, pos_ref, inv_ref, sgn_ref, gq_ref, gk_ref, bd_ref,
                     q_ref, kc_ref, vc_ref, ks_ref, vs_ref, kw_ref, vw_ref):
    ang = pos_ref[...].astype(F32) * inv_ref[...]
    cos = jnp.cos(ang)
    sin_signed = jnp.sin(ang) * sgn_ref[...]
    bd = bd_ref[...]
    for c in range(D_MODEL // LANES):
        x = aq_ref[:, c * LANES:(c + 1) * LANES].astype(F32)
        y = (_norm_rope(x, gq_ref[...], cos, sin_signed, bd) * (SCALE * LOG2E)).astype(q_ref.dtype)
        q_ref[0, 2 * c] = y[:, :HEAD_DIM]
        q_ref[0, 2 * c + 1] = y[:, HEAD_DIM:]
    outs = (kc_ref, vc_ref, ks_ref, vs_ref, kw_ref, vw_ref)
    for c, o_ref in enumerate(outs):
        x = akv_ref[:, c * LANES:(c + 1) * LANES]
        if c % 2 == 0:
            y = _norm_rope(x.astype(F32), gk_ref[...], cos, sin_signed, bd).astype(o_ref.dtype)
        else:
            y = x.astype(o_ref.dtype)
        o_ref[0, 0] = y[:, :HEAD_DIM]
        o_ref[0, 1] = y[:, HEAD_DIM:]


def _nsa_prep(a_q, a_kv, positions, q_norm_g, k_norm_g, b, s, tm=512):
    half = HEAD_DIM // 2
    lane = jnp.arange(LANES)
    inv = (ROPE_THETA ** (-((lane % half).astype(F32)) / half)).reshape(1, LANES)
    sgn = jnp.where((lane % HEAD_DIM) < half, -1.0, 1.0).astype(F32).reshape(1, LANES)
    bd = jnp.where((lane[:, None] // HEAD_DIM) == (lane[None, :] // HEAD_DIM), 1.0 / HEAD_DIM, 0.0).astype(BF16)
    gq = jnp.tile(q_norm_g.astype(F32), 2).reshape(1, LANES)
    gk = jnp.tile(k_norm_g.astype(F32), 2).reshape(1, LANES)
    nt = s // tm
    row = lambda i: (i, 0)
    const = lambda i: (0, 0)
    kv_shape = jax.ShapeDtypeStruct((b, N_KV_GROUPS, s, HEAD_DIM), BF16)
    kv_spec = pl.BlockSpec((1, N_KV_GROUPS, tm, HEAD_DIM), lambda i: (i // nt, 0, i % nt, 0))
    return pl.pallas_call(
        _nsa_prep_kernel, grid=(b * nt,),
        in_specs=[pl.BlockSpec((tm, D_MODEL), row), pl.BlockSpec((tm, 6 * LANES), row),
                  pl.BlockSpec((tm, 1), row), pl.BlockSpec((1, LANES), const), pl.BlockSpec((1, LANES), const),
                  pl.BlockSpec((1, LANES), const), pl.BlockSpec((1, LANES), const),
                  pl.BlockSpec((LANES, LANES), const)],
        out_specs=[pl.BlockSpec((1, N_HEADS, tm, HEAD_DIM), lambda i: (i // nt, 0, i % nt, 0))] + [kv_spec] * 6,
        out_shape=[jax.ShapeDtypeStruct((b, N_HEADS, s, HEAD_DIM), BF16)] + [kv_shape] * 6,
        compiler_params=_params(("parallel",)), name="nsa_prep",
    )(a_q, a_kv, positions.reshape(b * s, 1).astype(I32), inv, sgn, gq, gk, bd)


def _compress_kernel(uk_ref, uv_ref, pek_ref, pev_ref, kw1_ref, kw2_ref, vw1_ref, vw2_ref, gk_ref,
                     kc_ref, vc_ref):
    half = CMP_STRIDE * HEAD_DIM

    def mlp(u, pe, w1_ref, w2_ref):
        n = u.shape[0]
        ha = jnp.dot(u, w1_ref[:half, :], preferred_element_type=F32)
        hb = jnp.dot(u, w1_ref[half:, :], preferred_element_type=F32)
        bias = jnp.dot(pe, w1_ref[...], preferred_element_type=F32)[0:1, :]
        pre = ha + pltpu.roll(hb, n - 1, 0) + bias
        return jnp.dot(_gelu(pre).astype(BF16), w2_ref[...], preferred_element_type=F32)

    k = mlp(uk_ref[0, 0], pek_ref[...], kw1_ref, kw2_ref)
    ms = jnp.mean(k * k, axis=-1, keepdims=True)
    kc_ref[0, 0] = (k * lax.rsqrt(ms + EPS) * gk_ref[...]).astype(kc_ref.dtype)
    vc_ref[0, 0] = mlp(uv_ref[0, 0], pev_ref[...], vw1_ref, vw2_ref).astype(vc_ref.dtype)


def _compress(kc, vc, pe_k, pe_v, kw1, kw2, vw1, vw2, k_norm_g):
    b, g, s, dh = kc.shape
    ng = s // CMP_STRIDE
    wide = CMP_BLOCK * dh
    uk = kc.reshape(b, g, ng, CMP_STRIDE * dh)
    uv = vc.reshape(b, g, ng, CMP_STRIDE * dh)
    pek = jnp.zeros((8, wide), BF16).at[0].set(pe_k.reshape(wide).astype(BF16))
    pev = jnp.zeros((8, wide), BF16).at[0].set(pe_v.reshape(wide).astype(BF16))
    u_spec = pl.BlockSpec((1, 1, ng, CMP_STRIDE * dh), lambda i, j: (i, j, 0, 0))
    c2 = lambda i, j: (0, 0)
    o_spec = pl.BlockSpec((1, 1, ng, dh), lambda i, j: (i, j, 0, 0))
    o_shape = jax.ShapeDtypeStruct((b, g, ng, dh), BF16)
    return pl.pallas_call(
        _compress_kernel, grid=(b, g),
        in_specs=[u_spec, u_spec, pl.BlockSpec((8, wide), c2), pl.BlockSpec((8, wide), c2),
                  pl.BlockSpec((wide, CMP_HIDDEN), c2), pl.BlockSpec((CMP_HIDDEN, dh), c2),
                  pl.BlockSpec((wide, CMP_HIDDEN), c2), pl.BlockSpec((CMP_HIDDEN, dh), c2),
                  pl.BlockSpec((1, dh), c2)],
        out_specs=[o_spec, o_spec], out_shape=[o_shape, o_shape],
        compiler_params=_params(("parallel", "parallel")), name="nsa_compress",
    )(uk, uv, pek, pev, kw1.astype(BF16), kw2.astype(BF16), vw1.astype(BF16), vw2.astype(BF16),
      k_norm_g.reshape(1, dh).astype(F32))


def _store_head_pair(o_ref, h, o_t, held):
    if h % 2 == 0:
        return o_t
    pair = jnp.concatenate([held, o_t], axis=0).T
    o_ref[0, :, (h - 1) * HEAD_DIM:(h + 1) * HEAD_DIM] = pair.astype(o_ref.dtype)
    return None


def _cmp_select_kernel(q_ref, kc_ref, vc_ref, ovt_ref, o_ref, bias_ref, *, tq, n_cmp, n_pick):
    i = pl.program_id(2)
    chunk = min(LANES, kc_ref.shape[2])
    visible = ((i + 1) * tq - CMP_BLOCK) // CMP_STRIDE + 1
    nchunks = jnp.clip((visible + chunk - 1) // chunk, 1, kc_ref.shape[2] // chunk)
    for n in range(1, kc_ref.shape[2] // chunk + 1):
        pl.when(nchunks == n)(functools.partial(
            _cmp_select_keys, q_ref, kc_ref, vc_ref, ovt_ref, o_ref, bias_ref, nc=n * chunk, i=i, tq=tq,
            n_cmp=n_cmp, n_pick=n_pick))


def _cmp_select_keys(q_ref, kc_ref, vc_ref, ovt_ref, o_ref, bias_ref, *, nc, i, tq, n_cmp, n_pick):
    hg = q_ref.shape[1]
    nsel = ovt_ref.shape[0]
    kc = kc_ref[0, 0, :nc, :]
    vct = vc_ref[0, 0, :nc, :].astype(F32).T.astype(BF16)
    ovt = ovt_ref[:, :nc]
    c = lax.broadcasted_iota(I32, (nc, tq), 0)
    t = i * tq + lax.broadcasted_iota(I32, (nc, tq), 1)
    mask = ((CMP_STRIDE * c + CMP_BLOCK - 1) <= t) & (c < n_cmp)
    score = lambda h: lax.dot_general(kc, q_ref[0, h], (((1,), (1,)), ((), ())), preferred_element_type=F32)
    st_next = score(0)
    psum = jnp.zeros((nc, tq), F32)
    held = None
    for h in range(hg):
        st = st_next
        if h + 1 < hg:
            st_next = score(h + 1)
        st = jnp.where(mask, st, NEG)
        m = jnp.max(st, axis=0, keepdims=True)
        p = jnp.where(mask, jnp.exp2(st - m), 0.0)
        l = jnp.sum(p, axis=0, keepdims=True)
        p = p * jnp.where(l > 0.0, 1.0 / l, 0.0)
        o = jnp.dot(vct, p.astype(BF16), preferred_element_type=F32)
        held = _store_head_pair(o_ref, h, o, held)
        psum = psum + p

    hi = psum.astype(BF16)
    lo = (psum - hi.astype(F32)).astype(BF16)
    imp = jnp.dot(ovt, hi, preferred_element_type=F32) + jnp.dot(ovt, lo, preferred_element_type=F32)
    j = lax.broadcasted_iota(I32, (nsel, tq), 0)
    tt = i * tq + lax.broadcasted_iota(I32, (nsel, tq), 1)
    cur = tt // SEL_BLOCK
    forced = (j == 0) | (j == cur) | (j == cur - 1)
    sc = jnp.where(forced, SEL_FORCE, jnp.where(j * SEL_BLOCK <= tt, imp, -1.0))
    picked = jnp.zeros((nsel, tq), jnp.bool_)
    for _ in range(n_pick):
        mx = jnp.max(sc, axis=0, keepdims=True)
        idx = jnp.min(jnp.where(sc == mx, j, nsel), axis=0, keepdims=True)
        hit = j == idx
        picked = picked | hit
        sc = jnp.where(hit, -jnp.inf, sc)
    bias_ref[0, 0] = jnp.where(picked & (j <= cur), 0.0, NEG).T.astype(bias_ref.dtype)


def _cmp_select(q, k_cmp, v_cmp, s, tq=512):
    b, nh, _, dh = q.shape
    g, hg = N_KV_GROUPS, HEADS_PER_GROUP
    nc = k_cmp.shape[2]
    n_cmp = (s - CMP_BLOCK) // CMP_STRIDE + 1
    nsel = s // SEL_BLOCK
    n_pick = min(N_SELECT, nsel)
    cs = CMP_STRIDE * jnp.arange(nc)
    ss = SEL_BLOCK * jnp.arange(nsel)
    ovt = jnp.clip(jnp.minimum(cs[None, :] + CMP_BLOCK, ss[:, None] + SEL_BLOCK)
                   - jnp.maximum(cs[None, :], ss[:, None]), 0).astype(F32) / CMP_BLOCK
    ovt = jnp.where(jnp.arange(nc)[None, :] < n_cmp, ovt, 0.0).astype(BF16)
    kern = functools.partial(_cmp_select_kernel, tq=tq, n_cmp=n_cmp, n_pick=n_pick)
    return pl.pallas_call(
        kern, grid=(b, g, s // tq),
        in_specs=[pl.BlockSpec((1, hg, tq, dh), lambda bi, gi, i: (bi, gi, i, 0)),
                  pl.BlockSpec((1, 1, nc, dh), lambda bi, gi, i: (bi, gi, 0, 0)),
                  pl.BlockSpec((1, 1, nc, dh), lambda bi, gi, i: (bi, gi, 0, 0)),
                  pl.BlockSpec((nsel, nc), lambda bi, gi, i: (0, 0))],
        out_specs=[pl.BlockSpec((1, tq, hg * dh), lambda bi, gi, i: (bi, i, gi)),
                   pl.BlockSpec((1, 1, tq, nsel), lambda bi, gi, i: (bi, gi, i, 0))],
        out_shape=[jax.ShapeDtypeStruct((b, s, nh * dh), BF16),
                   jax.ShapeDtypeStruct((b, g, s, nsel), BF16)],
        compiler_params=_params(("parallel", "parallel", "parallel")), name="nsa_cmp_select",
    )(q, k_cmp, v_cmp, ovt)


def _sel_attn_kernel(it_ref, jt_ref, q_ref, bias_ref, k_ref, v_ref, o_ref, qa_ref, m_ref, acc_ref, *, tq, tk):
    pid = pl.program_id(2)
    i = it_ref[pid]
    j = jt_ref[pid]
    hg = q_ref.shape[1]
    nsel = bias_ref.shape[3]
    last_j = ((i + 1) * tq - 1) // tk

    @pl.when(j == 0)
    def _():
        for h in range(hg):
            qa_ref[h, :, :nsel] = bias_ref[0, 0]
            qa_ref[h, :, nsel:] = q_ref[0, h]
        m_ref[...] = jnp.full(m_ref.shape, -jnp.inf, F32)
        acc_ref[...] = jnp.zeros(acc_ref.shape, F32)

    def accumulate(diagonal):
        key = j * tk + lax.broadcasted_iota(I32, (tk, nsel), 0)
        blk = lax.broadcasted_iota(I32, (tk, nsel), 1)
        onehot = jnp.where(key // SEL_BLOCK == blk, 1.0, 0.0).astype(BF16)
        ka = jnp.concatenate([onehot, k_ref[0, 0]], axis=1)
        va = jnp.concatenate([v_ref[0, 0].astype(F32), jnp.ones((tk, LANES - HEAD_DIM), F32)], axis=1)
        vat = va.T[:SEL_ACC_ROWS, :].astype(BF16)
        qw = min(SEL_SUBTILE, tq)
        items = [(h, b) for h in range(hg) for b in range(tq // qw)]
        rows = lambda b: (b + 1) * qw if diagonal else tk

        def score(item):
            h, b = item
            return lax.dot_general(ka[:rows(b), :], qa_ref[h, b * qw:(b + 1) * qw, :], (((1,), (1,)), ((), ())),
                                   preferred_element_type=F32)

        ahead = [score(item) for item in items[:SEL_LOOKAHEAD]]
        for n, (h, b) in enumerate(items):
            st = ahead.pop(0)
            if n + SEL_LOOKAHEAD < len(items):
                ahead.append(score(items[n + SEL_LOOKAHEAD]))
            if diagonal:
                kp = lax.broadcasted_iota(I32, st.shape, 0)
                t = b * qw + lax.broadcasted_iota(I32, st.shape, 1)
                st = jnp.where(kp <= t, st, NEG)
            lanes = slice(b * qw, (b + 1) * qw)
            m_old = m_ref[h, :, lanes]
            m_new = jnp.maximum(m_old, jnp.max(st, axis=0, keepdims=True))
            alpha = jnp.exp2(m_old - m_new)
            p = jnp.exp2(st - m_new)
            acc_ref[h, :, lanes] = alpha * acc_ref[h, :, lanes] + jnp.dot(
                vat[:, :rows(b)], p.astype(BF16), preferred_element_type=F32)
            m_ref[h, :, lanes] = m_new

    @pl.when(j < last_j)
    def _():
        accumulate(False)

    @pl.when(j == last_j)
    def _():
        accumulate(True)
        held = None
        for h in range(hg):
            acc = acc_ref[h]
            o = acc[:HEAD_DIM, :] / acc[HEAD_DIM:HEAD_DIM + 1, :]
            held = _store_head_pair(o_ref, h, o, held)


def _sel_attn(q, bias, k_s, v_s, s, tq=1024):
    tk = tq
    b, nh, _, dh = q.shape
    g, hg = N_KV_GROUPS, HEADS_PER_GROUP
    nsel = bias.shape[3]
    pairs = [(i, j) for i in range(s // tq) for j in range(((i + 1) * tq - 1) // tk + 1)]
    it = jnp.asarray([pr[0] for pr in pairs], I32)
    jt = jnp.asarray([pr[1] for pr in pairs], I32)
    q_map = lambda bi, gi, pid, it_ref, jt_ref: (bi, gi, it_ref[pid], 0)
    kv_map = lambda bi, gi, pid, it_ref, jt_ref: (bi, gi, jt_ref[pid], 0)
    kern = functools.partial(_sel_attn_kernel, tq=tq, tk=tk)
    grid_spec = pltpu.PrefetchScalarGridSpec(
        num_scalar_prefetch=2, grid=(b, g, len(pairs)),
        in_specs=[pl.BlockSpec((1, hg, tq, dh), q_map), pl.BlockSpec((1, 1, tq, nsel), q_map),
                  pl.BlockSpec((1, 1, tk, dh), kv_map), pl.BlockSpec((1, 1, tk, dh), kv_map)],
        out_specs=pl.BlockSpec((1, tq, hg * dh), lambda bi, gi, pid, it_ref, jt_ref: (bi, it_ref[pid], gi)),
        scratch_shapes=[pltpu.VMEM((hg, tq, nsel + dh), BF16), pltpu.VMEM((hg, 1, tq), F32),
                        pltpu.VMEM((hg, SEL_ACC_ROWS, tq), F32)])
    return pl.pallas_call(
        kern, grid_spec=grid_spec, out_shape=jax.ShapeDtypeStruct((b, s, nh * dh), BF16),
        compiler_params=_params(("parallel", "parallel", "arbitrary")), name="nsa_sel_attn",
    )(it, jt, q, bias, k_s, v_s)


def _win_attn_kernel(q_ref, *refs, tq, nwin):
    k_refs, v_refs, o_ref = refs[:nwin], refs[nwin:2 * nwin], refs[2 * nwin]
    i = pl.program_id(2)
    hg = q_ref.shape[1]
    k = jnp.concatenate([r[0, 0] for r in k_refs], axis=0)
    vt = jnp.concatenate([r[0, 0] for r in v_refs], axis=0).astype(F32).T.astype(BF16)
    qw = WIN_SUBTILE
    nk = WINDOW + qw
    r = lax.broadcasted_iota(I32, (nk, qw), 0)
    c = lax.broadcasted_iota(I32, (nk, qw), 1)
    band = (r > c) & (r <= c + WINDOW)
    items = [(h, b) for h in range(hg) for b in range(tq // qw)]

    def score(item):
        h, b = item
        return lax.dot_general(k[b * qw:b * qw + nk, :], q_ref[0, h, b * qw:(b + 1) * qw, :],
                               (((1,), (1,)), ((), ())), preferred_element_type=F32)

    st_next = score(items[0])
    held, parts = None, []
    for n, (h, b) in enumerate(items):
        st = st_next
        if n + 1 < len(items):
            st_next = score(items[n + 1])
        mask = band & (i * tq - WINDOW + b * qw + r >= 0)
        st = jnp.where(mask, st, NEG)
        m = jnp.max(st, axis=0, keepdims=True)
        p = jnp.where(mask, jnp.exp2(st - m), 0.0)
        l = jnp.sum(p, axis=0, keepdims=True)
        parts.append(jnp.dot(vt[:, b * qw:b * qw + nk], p.astype(BF16), preferred_element_type=F32) / l)
        if len(parts) == tq // qw:
            held = _store_head_pair(o_ref, h, jnp.concatenate(parts, axis=1), held)
            parts = []


def _win_attn(q, k_w, v_w, s, tq=512):
    b, nh, _, dh = q.shape
    g, hg = N_KV_GROUPS, HEADS_PER_GROUP
    nwin = WINDOW // tq + 1
    pad = ((0, 0), (0, 0), (WINDOW, 0), (0, 0))
    kp, vp = jnp.pad(k_w, pad), jnp.pad(v_w, pad)
    kv_specs = [pl.BlockSpec((1, 1, tq, dh), functools.partial(lambda bi, gi, i, w: (bi, gi, i + w, 0), w=w))
                for w in range(nwin)]
    kern = functools.partial(_win_attn_kernel, tq=tq, nwin=nwin)
    return pl.pallas_call(
        kern, grid=(b, g, s // tq),
        in_specs=[pl.BlockSpec((1, hg, tq, dh), lambda bi, gi, i: (bi, gi, i, 0))] + kv_specs + kv_specs,
        out_specs=pl.BlockSpec((1, tq, hg * dh), lambda bi, gi, i: (bi, i, gi)),
        out_shape=jax.ShapeDtypeStruct((b, s, nh * dh), BF16),
        compiler_params=_params(("parallel", "parallel", "parallel")), name="nsa_win_attn",
    )(q, *([kp] * nwin), *([vp] * nwin))


def _causal_conv(x, win_ref, w_ref, b_ref):
    n = x.shape[0]
    win_ref[8:, :] = x
    out = b_ref[...] + w_ref[CONV_WIDTH - 1:CONV_WIDTH, :] * x
    for k in range(CONV_WIDTH - 1):
        off = 8 - (CONV_WIDTH - 1) + k
        out = out + w_ref[k:k + 1, :] * win_ref[off:off + n, :]
    win_ref[0:8, :] = x[n - 8:, :]
    return out


def _rglru_kernel(x_ref, gate_ref, pos_ref, cw_ref, cb_ref, wr_ref, br_ref, wi_ref, bi_ref, lam_ref,
                  o_ref, tail_ref, h_ref):
    @pl.when(pl.program_id(1) == 0)
    def _():
        tail_ref[0:8, :] = jnp.zeros((8, tail_ref.shape[1]), F32)
        h_ref[...] = jnp.zeros(h_ref.shape, F32)

    n = x_ref.shape[0]
    x = x_ref[...].astype(F32)
    xc = _causal_conv(x, tail_ref, cw_ref, cb_ref)
    xcb = xc.astype(BF16)
    r = _sigmoid(jnp.dot(xcb, wr_ref[...], preferred_element_type=F32) + br_ref[...])
    gi = _sigmoid(jnp.dot(xcb, wi_ref[...], preferred_element_type=F32) + bi_ref[...])
    log_a = -LRU_C * r * _softplus(-lam_ref[...])
    reset = pos_ref[...] == 0
    a = jnp.where(reset, 0.0, jnp.exp(log_a))
    mult = jnp.where(reset, 1.0, jnp.sqrt(jnp.maximum(1.0 - jnp.exp(2.0 * log_a), 0.0)))
    bb = mult * (gi * xc)
    sub = lax.broadcasted_iota(I32, a.shape, 0) % 8
    d = 1
    while d < 8:
        a_sh = pltpu.roll(a, d, 0)
        b_sh = pltpu.roll(bb, d, 0)
        live = sub >= d
        bb = jnp.where(live, a * b_sh + bb, bb)
        a = jnp.where(live, a * a_sh, a)
        d *= 2
    carry = h_ref[...]
    hs = []
    for g in range(n // 8):
        h = bb[8 * g:8 * g + 8, :] + a[8 * g:8 * g + 8, :] * carry
        carry = h[7:8, :]
        hs.append(h)
    h_ref[...] = carry
    o_ref[...] = (jnp.concatenate(hs, axis=0) * _gelu(gate_ref[...].astype(F32))).astype(o_ref.dtype)


def _block_diag(w):
    nb, bs, _ = w.shape
    eye = jnp.eye(nb, dtype=w.dtype)
    return (w[:, :, None, :] * eye[:, None, :, None]).reshape(nb * bs, nb * bs)


def _rglru(l_x, l_g, positions, conv_w, conv_b, w_r, b_r, w_i, b_i, lam, b, s, ts=256):
    d = l_x.shape[1]
    nt = s // ts
    row = lambda bi, i: (bi * nt + i, 0)
    c2 = lambda bi, i: (0, 0)
    vec = lambda v: v.reshape(1, d).astype(F32)
    return pl.pallas_call(
        _rglru_kernel, grid=(b, nt),
        in_specs=[pl.BlockSpec((ts, d), row), pl.BlockSpec((ts, d), row), pl.BlockSpec((ts, 1), row),
                  pl.BlockSpec((CONV_WIDTH, d), c2), pl.BlockSpec((1, d), c2),
                  pl.BlockSpec((d, d), c2), pl.BlockSpec((1, d), c2),
                  pl.BlockSpec((d, d), c2), pl.BlockSpec((1, d), c2), pl.BlockSpec((1, d), c2)],
        out_specs=pl.BlockSpec((ts, d), row),
        out_shape=jax.ShapeDtypeStruct((b * s, d), BF16),
        scratch_shapes=[pltpu.VMEM((8 + ts, d), F32), pltpu.VMEM((1, d), F32)],
        compiler_params=_params(("parallel", "arbitrary")), name="rglru",
    )(l_x, l_g, positions.reshape(b * s, 1).astype(I32), conv_w.astype(F32), vec(conv_b),
      _block_diag(w_r).astype(BF16), vec(b_r), _block_diag(w_i).astype(BF16), vec(b_i), vec(lam))


def _ssd_kernel(z_ref, xbc_ref, dt_ref, cw_ref, cb_ref, dtb_ref, alog_ref, dfull_ref, ng_ref,
                o_ref, tail_ref, state_ref, y_ref):
    @pl.when(pl.program_id(1) == 0)
    def _():
        tail_ref[0:8, :] = jnp.zeros((8, tail_ref.shape[1]), F32)
        state_ref[...] = jnp.zeros(state_ref.shape, F32)

    n = xbc_ref.shape[0]
    hg = SSM_HEADS // SSM_GROUPS
    x = xbc_ref[...].astype(F32)
    xc = _causal_conv(x, tail_ref, cw_ref, cb_ref)
    xc = xc * _sigmoid(xc)
    xs = xc[:, :SSM_INNER]
    dt = _softplus(dt_ref[...] + dtb_ref[...])
    adt = dt * (-jnp.exp(alog_ref[...]))
    row = lax.broadcasted_iota(I32, adt.shape, 0)
    acs = adt
    d = 1
    while d < n:
        acs = acs + jnp.where(row >= d, pltpu.roll(acs, d, 0), 0.0)
        d *= 2
    acs_t = acs.T
    li = lax.broadcasted_iota(I32, (n, n), 0)
    si = lax.broadcasted_iota(I32, (n, n), 1)
    tri = li >= si
    for g in range(SSM_GROUPS):
        bm = xc[:, SSM_INNER + g * SSM_STATE:SSM_INNER + (g + 1) * SSM_STATE].astype(BF16)
        cm = xc[:, SSM_INNER + (SSM_GROUPS + g) * SSM_STATE:SSM_INNER + (SSM_GROUPS + g + 1) * SSM_STATE].astype(BF16)
        cb = lax.dot_general(cm, bm, (((1,), (1,)), ((), ())), preferred_element_type=F32)
        bm_t = bm.T
        for hh in range(hg):
            h = g * hg + hh
            acol = jnp.broadcast_to(acs[:, h:h + 1], (n, n))
            arow = acs_t[h:h + 1, :]
            decay = jnp.exp(jnp.where(tri, acol - arow, NEG))
            acol_p = acol[:, :SSM_HEAD_DIM]
            xh = xs[:, h * SSM_HEAD_DIM:(h + 1) * SSM_HEAD_DIM] * dt[:, h:h + 1]
            a_last = acol_p[n - 1:n, :]
            y = jnp.dot((cb * decay).astype(BF16), xh.astype(BF16), preferred_element_type=F32)
            st = state_ref[h]
            y = y + jnp.dot(cm, st.astype(BF16), preferred_element_type=F32) * jnp.exp(acol_p)
            upd = jnp.dot(bm_t, (xh * jnp.exp(a_last - acol_p)).astype(BF16), preferred_element_type=F32)
            state_ref[h] = jnp.exp(a_last) * st + upd
            y_ref[:, h * SSM_HEAD_DIM:(h + 1) * SSM_HEAD_DIM] = y
    z = z_ref[...].astype(F32)
    y = (y_ref[...] + dfull_ref[...] * xs) * (z * _sigmoid(z))
    gw = SSM_INNER // SSM_GROUPS
    for g in range(SSM_GROUPS):
        yg = y[:, g * gw:(g + 1) * gw]
        ms = jnp.mean(yg * yg, axis=-1, keepdims=True)
        o_ref[:, g * gw:(g + 1) * gw] = (yg * lax.rsqrt(ms + EPS) * ng_ref[:, g * gw:(g + 1) * gw]).astype(o_ref.dtype)


def _ssd(s_z, s_xbc, s_dt, conv_w, conv_b, dt_bias, a_log, d_skip, norm_g, b, s):
    n = SSM_CHUNK
    nt = s // n
    c = s_xbc.shape[1]
    row = lambda bi, i: (bi * nt + i, 0)
    c2 = lambda bi, i: (0, 0)
    pad_h = lambda v: jnp.zeros((1, LANES), F32).at[0, :SSM_HEADS].set(v.astype(F32))
    dfull = jnp.repeat(d_skip.astype(F32), SSM_HEAD_DIM).reshape(1, SSM_INNER)
    return pl.pallas_call(
        _ssd_kernel, grid=(b, nt),
        in_specs=[pl.BlockSpec((n, SSM_INNER), row), pl.BlockSpec((n, c), row), pl.BlockSpec((n, LANES), row),
                  pl.BlockSpec((CONV_WIDTH, c), c2), pl.BlockSpec((1, c), c2),
                  pl.BlockSpec((1, LANES), c2), pl.BlockSpec((1, LANES), c2),
                  pl.BlockSpec((1, SSM_INNER), c2), pl.BlockSpec((1, SSM_INNER), c2)],
        out_specs=pl.BlockSpec((n, SSM_INNER), row),
        out_shape=jax.ShapeDtypeStruct((b * s, SSM_INNER), BF16),
        scratch_shapes=[pltpu.VMEM((8 + n, c), F32), pltpu.VMEM((SSM_HEADS, SSM_STATE, SSM_HEAD_DIM), F32),
                        pltpu.VMEM((n, SSM_INNER), F32)],
        compiler_params=_params(("parallel", "arbitrary")), name="ssd",
    )(s_z, s_xbc, s_dt, conv_w.astype(F32), conv_b.reshape(1, c).astype(F32), pad_h(dt_bias), pad_h(a_log),
      dfull, norm_g.reshape(1, SSM_INNER).astype(F32))


def _merge_kernel(oc_ref, os_ref, ow_ref, ag_ref, ex_ref, yb_ref, yc_ref, mg_ref, x_ref, wb_ref, wo_ref, o_ref):
    gates = _sigmoid(ag_ref[...])
    hi = gates.astype(BF16)
    lo = (gates - hi.astype(F32)).astype(BF16)
    spread = lambda k: (jnp.dot(hi, ex_ref[k], preferred_element_type=F32)
                        + jnp.dot(lo, ex_ref[k], preferred_element_type=F32))
    ya = (spread(0) * oc_ref[...].astype(F32) + spread(1) * os_ref[...].astype(F32)
          + spread(2) * ow_ref[...].astype(F32))
    d = D_MODEL
    merged = _sigmoid(mg_ref[:, 0:d].astype(F32)) * jnp.dot(ya.astype(BF16), wb_ref[0], preferred_element_type=F32)
    merged += _sigmoid(mg_ref[:, d:2 * d].astype(F32)) * jnp.dot(yb_ref[...], wb_ref[1], preferred_element_type=F32)
    merged += _sigmoid(mg_ref[:, 2 * d:3 * d].astype(F32)) * jnp.dot(yc_ref[...], wb_ref[2], preferred_element_type=F32)
    o_ref[...] = x_ref[...] + jnp.dot(merged.astype(BF16), wo_ref[...], preferred_element_type=F32)


def _merge(o_c, o_s, o_w, a_g, y_b, y_c, m_g, x, w_branch, w_out, tm=512):
    t, d = x.shape
    row = lambda i: (i, 0)
    lane = jnp.arange(LANES)[:, None]
    col = jnp.arange(d)[None, :]
    expand = jnp.stack([(lane == 3 * (col // HEAD_DIM) + k) for k in range(3)]).astype(BF16)
    return pl.pallas_call(
        _merge_kernel, grid=(t // tm,),
        in_specs=[pl.BlockSpec((tm, d), row), pl.BlockSpec((tm, d), row), pl.BlockSpec((tm, d), row),
                  pl.BlockSpec((tm, LANES), row), pl.BlockSpec((3, LANES, d), lambda i: (0, 0, 0)),
                  pl.BlockSpec((tm, d), row), pl.BlockSpec((tm, d), row), pl.BlockSpec((tm, 3 * d), row),
                  pl.BlockSpec((tm, d), row),
                  pl.BlockSpec((3, d, d), lambda i: (0, 0, 0)), pl.BlockSpec((d, d), lambda i: (0, 0))],
        out_specs=pl.BlockSpec((tm, d), row),
        out_shape=jax.ShapeDtypeStruct((t, d), F32),
        compiler_params=_params(("parallel",)), name="merge",
    )(o_c.reshape(t, d), o_s.reshape(t, d), o_w.reshape(t, d), a_g, expand, y_b, y_c, m_g, x,
      w_branch.astype(BF16), w_out.astype(BF16))


class _Interleaver:
    def __init__(self, pieces, every):
        self.pieces, self.every, self.count = list(pieces), every, 0

    def tick(self):
        self.count += 1
        if self.pieces and self.count % self.every == 0:
            self.pieces.pop(0)()

    def drain(self):
        while self.pieces:
            self.pieces.pop(0)()


def _sorting_network(n):
    pairs = []

    def merge(lo, m, r):
        step = r * 2
        if step < m:
            merge(lo, m, step)
            merge(lo + r, m, step)
            pairs.extend((i, i + r) for i in range(lo + r, lo + m - r, step))
        else:
            pairs.append((lo, lo + r))

    def sort(lo, m):
        if m > 1:
            sort(lo, m // 2)
            sort(lo + m // 2, m // 2)
            merge(lo, m, 1)

    sort(0, n)
    return pairs


def _topk_rows(s, k, tick):
    n, lanes = s.shape
    assert n == 8 * k
    sub = lax.broadcasted_iota(I32, (8, lanes), 0)
    vals = [s[8 * j:8 * j + 8, :] for j in range(k)]
    ids = [sub + 8 * j for j in range(k)]
    for count, (a, b) in enumerate(_sorting_network(k)):
        first = (vals[a] > vals[b]) | ((vals[a] == vals[b]) & (ids[a] < ids[b]))
        vals[a], vals[b] = jnp.where(first, vals[a], vals[b]), jnp.where(first, vals[b], vals[a])
        ids[a], ids[b] = jnp.where(first, ids[a], ids[b]), jnp.where(first, ids[b], ids[a])
        if count % 16 == 15:
            tick()
    top_v, top_i = [], []
    for r in range(k):
        best = jnp.max(vals[0], axis=0, keepdims=True)
        row = jnp.min(jnp.where(vals[0] == best, ids[0], n), axis=0, keepdims=True)
        top_v.append(best)
        top_i.append(row)
        won = ids[0] == row
        for d in range(k - 1 - r):
            vals[d] = jnp.where(won, vals[d + 1], vals[d])
            ids[d] = jnp.where(won, ids[d + 1], ids[d])
        tick()
    return jnp.concatenate(top_v, axis=0), jnp.concatenate(top_i, axis=0)


def _top_pairs(s1, s2, k, tick):
    lanes = s1.shape[1]
    sub = lax.broadcasted_iota(I32, (8, lanes), 0)
    lists = [jnp.where(sub <= k // (d + 1) - 1, s1[0:8, :] + s2[d:d + 1, :], -jnp.inf) for d in range(k)]
    tail = s1[8:16, :] + s2[0:1, :]
    tail_pos = (sub + 8) * k
    taken = jnp.zeros((8, lanes), I32)
    vs, aa, bb = [], [], []
    for r in range(k):
        best = jnp.maximum(jnp.max(lists[0], axis=0, keepdims=True), jnp.max(tail, axis=0, keepdims=True))
        head_pos = sub * k + taken
        pos = jnp.minimum(jnp.min(jnp.where(lists[0] == best, head_pos, k * k), axis=0, keepdims=True),
                          jnp.min(jnp.where(tail == best, tail_pos, k * k), axis=0, keepdims=True))
        vs.append(best)
        aa.append(pos // k)
        bb.append(pos % k)
        won = head_pos == pos
        tail = jnp.where(tail_pos == pos, -jnp.inf, tail)
        for d in range(k - 1 - r):
            lists[d] = jnp.where(won, lists[d + 1], lists[d])
        taken = taken + won.astype(I32)
        tick()
    return jnp.concatenate(vs, axis=0), jnp.concatenate(aa, axis=0), jnp.concatenate(bb, axis=0)


def _pick_rows(table, sel, k):
    out = jnp.zeros(sel.shape, table.dtype)
    for a in range(k):
        out = jnp.where(sel == a, table[a:a + 1, :], out)
    return out


def _route_head(qt, keys_ref, tick):
    k = PEER_TOPK
    assert k == 16 and PEER_KEYS == 8 * k
    tops = []
    for half in range(2):
        sc = jnp.dot(keys_ref[half], qt[half * PEER_HALF:(half + 1) * PEER_HALF, :], preferred_element_type=F32)
        tops.append(_topk_rows(sc, k, tick))
    (s1, i1), (s2, i2) = tops
    sc, a_sel, b_sel = _top_pairs(s1, s2, k, tick)
    e = jnp.exp(sc - sc[0:1, :])
    g = e / jnp.sum(e, axis=0, keepdims=True)
    return _pick_rows(i1, a_sel, k), _pick_rows(i2, b_sel, k), g


def _peer_route_u_kernel(hn_ref, hc_ref, wq_ref, keys_ref, u_ref, i1_ref, i2_ref, g_ref, act_ref,
                         i1t_ref, i2t_ref, gt_ref, i1c_ref, i2c_ref, *, blocks):
    o = pl.program_id(0)
    c = pl.program_id(1)
    k = PEER_TOPK

    @pl.when((o == 0) & (c == 0))
    def _():
        i1c_ref[...] = jnp.zeros(i1c_ref.shape, I32)
        i2c_ref[...] = jnp.zeros(i2c_ref.shape, I32)

    @pl.when(c == 0)
    def _():
        act_ref[...] = jnp.zeros(act_ref.shape, F32)

    i1c = i1c_ref[...]
    i2c = i2c_ref[...]
    hc = hc_ref[...]
    acc = [act_ref[...]]

    def piece(pc):
        def run():
            a = lax.dot_general(hc, u_ref[pl.ds(pc * 2 * PEER_KEYS, 2 * PEER_KEYS), :], (((1,), (1,)), ((), ())),
                                preferred_element_type=F32)
            for sub in range(2):
                got = jnp.take_along_axis(a[:, sub * PEER_KEYS:(sub + 1) * PEER_KEYS], i2c, axis=1,
                                          mode="promise_in_bounds")
                acc[0] = jnp.where(i1c == c * blocks + 2 * pc + sub, got, acc[0])
        return run

    tm = hn_ref.shape[0]
    groups = tm // ROUTE_LANES
    hps = keys_ref.shape[0] // 2
    hq = wq_ref.shape[0] // hps
    ticks = hps * groups * (2 * (len(_sorting_network(k)) // 16 + k) + k)
    pieces = _Interleaver([piece(pc) for pc in range(blocks // 2)], every=ticks // (blocks // 2 + 3))
    qt = lax.dot_general(wq_ref[...], hn_ref[...], (((1,), (1,)), ((), ())), preferred_element_type=F32)
    qt = qt.astype(BF16)
    for hd in range(hps):
        rows = pl.ds(pl.multiple_of((c * hps + hd) * k, k), k)
        for gi in range(groups):
            cols = slice(gi * ROUTE_LANES, (gi + 1) * ROUTE_LANES)
            i1, i2, g = _route_head(qt[hd * hq:(hd + 1) * hq, cols], keys_ref.at[pl.ds(2 * hd, 2)], pieces.tick)
            i1t_ref[rows, cols] = i1
            i2t_ref[rows, cols] = i2
            gt_ref[rows, cols] = g
    pieces.drain()
    act_ref[...] = acc[0]

    @pl.when(c == pl.num_programs(1) - 1)
    def _():
        i1n = i1t_ref[...].T
        i2n = i2t_ref[...].T
        i1_ref[...] = i1n
        i2_ref[...] = i2n
        g_ref[...] = gt_ref[...].T
        i1c_ref[...] = i1n
        i2c_ref[...] = i2n


def _peer_route_u(h, w_q, sub_keys, u, tm=512, hps=2):
    t, d = h.shape
    nt = t // tm
    ne = u.shape[0]
    nchunk = PEER_HEADS // hps
    ec = ne // nchunk
    blocks = ec // PEER_KEYS
    hq = hps * (w_q.shape[1] // PEER_HEADS)
    wq_t = w_q.T.astype(BF16)
    keys = sub_keys.reshape(PEER_HEADS * 2, PEER_KEYS, PEER_HALF).astype(BF16)
    slots = PEER_HEADS * PEER_TOPK
    nxt = lambda o, c: (jnp.minimum(o, nt - 1), 0)
    cur = lambda o, c: (jnp.maximum(o - 1, 0), 0)
    kern = functools.partial(_peer_route_u_kernel, blocks=blocks)
    return pl.pallas_call(
        kern, grid=(nt + 1, nchunk),
        in_specs=[pl.BlockSpec((tm, d), nxt), pl.BlockSpec((tm, d), cur),
                  pl.BlockSpec((hq, d), lambda o, c: (c, 0)),
                  pl.BlockSpec((2 * hps, PEER_KEYS, PEER_HALF), lambda o, c: (c, 0, 0)),
                  pl.BlockSpec((ec, d), lambda o, c: (c, 0))],
        out_specs=[pl.BlockSpec((tm, slots), nxt)] * 3 + [pl.BlockSpec((tm, slots), cur)],
        out_shape=[jax.ShapeDtypeStruct((t, slots), I32), jax.ShapeDtypeStruct((t, slots), I32),
                   jax.ShapeDtypeStruct((t, slots), F32), jax.ShapeDtypeStruct((t, slots), F32)],
        scratch_shapes=[pltpu.VMEM((slots, tm), I32), pltpu.VMEM((slots, tm), I32), pltpu.VMEM((slots, tm), F32),
                        pltpu.VMEM((tm, slots), I32), pltpu.VMEM((tm, slots), I32)],
        compiler_params=_params(("arbitrary", "arbitrary")), name="peer_route_u",
    )(h, h, wq_t, keys, u)


def _peer_v_kernel(act_ref, g_ref, i1_ref, i2_ref, v_ref, x_ref, o_ref, w_ref, wg_ref, *, tm, blocks):
    c = pl.program_id(1)
    nk = PEER_KEYS

    @pl.when(c == 0)
    def _():
        w_ref[...] = g_ref[...] * _gelu(act_ref[...])
        o_ref[...] = x_ref[...]
        sub = lax.broadcasted_iota(I32, (nk, w_ref.shape[1]), 0)

        def per_token(t, carry):
            wrow = w_ref[pl.ds(t, 1), :]
            lhs = jnp.where(i1_ref[pl.ds(t, 1), :] == sub, wrow, 0.0).astype(BF16)
            rhs = jnp.where(i2_ref[pl.ds(t, 1), :] == sub, 1.0, 0.0).astype(BF16)
            grid = lax.dot_general(lhs, rhs, (((1,), (1,)), ((), ())), preferred_element_type=F32)
            wg_ref[pl.ds(pl.multiple_of(t * WG_PITCH, 8), nk), :] = grid
            return carry

        lax.fori_loop(0, tm, per_token, 0, unroll=64)

    acc = jnp.zeros(o_ref.shape, F32)
    for bk in range(0, blocks, 2):
        i1 = c * blocks + bk
        lhs = jnp.concatenate([wg_ref[pl.ds(i1, tm, stride=WG_PITCH), :],
                               wg_ref[pl.ds(i1 + 1, tm, stride=WG_PITCH), :]], axis=1).astype(BF16)
        rhs = v_ref[bk:bk + 2].reshape(2 * nk, v_ref.shape[2])
        acc += jnp.dot(lhs, rhs, preferred_element_type=F32)
    o_ref[...] += acc


def _peer_v(act, g, i1, i2, v, x, tm=256, blocks=32):
    t, d = x.shape
    slots = g.shape[1]
    nk = PEER_KEYS
    v3 = v.reshape(nk, nk, d)
    kern = functools.partial(_peer_v_kernel, tm=tm, blocks=blocks)
    row = lambda i, c: (i, 0)
    return pl.pallas_call(
        kern, grid=(t // tm, nk // blocks),
        in_specs=[pl.BlockSpec((tm, slots), row),
                  pl.BlockSpec((tm, slots), row), pl.BlockSpec((tm, slots), row), pl.BlockSpec((tm, slots), row),
                  pl.BlockSpec((blocks, nk, d), lambda i, c: (c, 0, 0)), pl.BlockSpec((tm, d), row)],
        out_specs=pl.BlockSpec((tm, d), row),
        out_shape=jax.ShapeDtypeStruct((t, d), F32),
        scratch_shapes=[pltpu.VMEM((tm, slots), F32), pltpu.VMEM((tm * WG_PITCH, nk), F32)],
        compiler_params=_params(("parallel", "arbitrary")), name="peer_v",
    )(act, g, i1, i2, v3, x)


def _pad_cols(w, n):
    return jnp.pad(w, ((0, 0), (0, n - w.shape[1])))


def _in_proj_kernel(x_ref, g_ref, w_ref, *rest, bounds):
    outs, xn_ref = rest[:-1], rest[-1]
    j = pl.program_id(1)

    @pl.when(j == 0)
    def _():
        x = x_ref[...]
        ms = jnp.mean(x * x, axis=-1, keepdims=True)
        xn_ref[...] = (x * lax.rsqrt(ms + EPS) * g_ref[...]).astype(xn_ref.dtype)

    for (lo, hi), o_ref in zip(bounds, outs):
        @pl.when((j >= lo) & (j < hi))
        def _(o_ref=o_ref):
            o_ref[...] = jnp.dot(xn_ref[...], w_ref[...], preferred_element_type=F32).astype(o_ref.dtype)


PROJ_TILE = 512
IN_GROUPS = ((1024, BF16), (768, BF16), (48, F32), (1024, BF16), (1024, BF16), (1024, BF16), (1536, BF16),
             (16, F32), (3072, BF16))


def _in_proj(x, g, w_in, tm=1024):
    t, d = x.shape
    tn = PROJ_TILE
    ws, bounds, shapes, off, tile = [], [], [], 0, 0
    for n, dtype in IN_GROUPS:
        width = -(-n // tn) * tn
        ws.append(_pad_cols(w_in[:, off:off + n], width))
        bounds.append((tile, tile + width // tn))
        shapes.append(jax.ShapeDtypeStruct((t, width), dtype))
        off += n
        tile += width // tn
    w_all = jnp.concatenate(ws, axis=1).astype(BF16)
    out_specs = [pl.BlockSpec((tm, tn), functools.partial(lambda i, j, lo, hi: (i, jnp.clip(j - lo, 0, hi - lo - 1)),
                                                          lo=lo, hi=hi)) for lo, hi in bounds]
    kern = functools.partial(_in_proj_kernel, bounds=tuple(bounds))
    return pl.pallas_call(
        kern, grid=(t // tm, tile),
        in_specs=[pl.BlockSpec((tm, d), lambda i, j: (i, 0)), pl.BlockSpec((1, d), lambda i, j: (0, 0)),
                  pl.BlockSpec((d, tn), lambda i, j: (0, j))],
        out_specs=out_specs, out_shape=shapes,
        scratch_shapes=[pltpu.VMEM((tm, d), BF16)],
        compiler_params=_params(("parallel", "arbitrary")), name="in_proj",
    )(x, g.reshape(1, d).astype(F32), w_all)


def _mixer(x, positions, b, s, p):
    a_q, a_kv, a_g, l_x, l_g, s_z, s_xbc, s_dt, m_g = _in_proj(x, p["mix_norm_g"], p["w_in"])

    q, kc, vc, ks, vs, kw, vw = _nsa_prep(a_q, a_kv, positions, p["q_norm_g"], p["k_norm_g"], b, s)
    k_cmp, v_cmp = _compress(kc, vc, p["cmp_pe_k"], p["cmp_pe_v"], p["cmp_k_w1"], p["cmp_k_w2"],
                             p["cmp_v_w1"], p["cmp_v_w2"], p["k_norm_g"])
    o_c, bias = _cmp_select(q, k_cmp, v_cmp, s)
    o_s = _sel_attn(q, bias, ks, vs, s)
    o_w = _win_attn(q, kw, vw, s)
    y_b = _rglru(l_x, l_g, positions, p["lru_conv_w"], p["lru_conv_b"], p["lru_w_r"], p["lru_b_r"],
                 p["lru_w_i"], p["lru_b_i"], p["lru_lambda"], b, s)
    y_c = _ssd(s_z, s_xbc, s_dt, p["ssm_conv_w"], p["ssm_conv_b"], p["ssm_dt_bias"], p["ssm_a_log"],
               p["ssm_d"], p["ssm_norm_g"], b, s)
    return _merge(o_c, o_s, o_w, a_g, y_b, y_c, m_g, x, p["w_branch"], p["w_out"])


def _peer(x, p):
    h = _rmsnorm(x, p["ffn_norm_g"])
    i1, i2, g, act = _peer_route_u(h, p["peer_w_q"], p["peer_sub_keys"], p["peer_u"].astype(BF16))
    return _peer_v(act, g, i1, i2, p["peer_v"].astype(BF16), x)


_LAYER_PARAMS = ("mix_norm_g", "w_in", "q_norm_g", "k_norm_g", "cmp_pe_k", "cmp_pe_v", "cmp_k_w1", "cmp_k_w2",
                 "cmp_v_w1", "cmp_v_w2", "lru_conv_w", "lru_conv_b", "lru_w_r", "lru_b_r", "lru_w_i", "lru_b_i",
                 "lru_lambda", "ssm_conv_w", "ssm_conv_b", "ssm_dt_bias", "ssm_a_log", "ssm_d", "ssm_norm_g",
                 "w_branch", "w_out", "ffn_norm_g", "peer_w_q", "peer_sub_keys", "peer_u", "peer_v")


def kernel(x, positions, mix_norm_g, w_in, q_norm_g, k_norm_g, cmp_pe_k, cmp_pe_v, cmp_k_w1, cmp_k_w2, cmp_v_w1, cmp_v_w2, lru_conv_w, lru_conv_b, lru_w_r, lru_b_r, lru_w_i, lru_b_i, lru_lambda, ssm_conv_w, ssm_conv_b, ssm_dt_bias, ssm_a_log, ssm_d, ssm_norm_g, w_branch, w_out, ffn_norm_g, peer_w_q, peer_sub_keys, peer_u, peer_v):
    stacked = dict(zip(_LAYER_PARAMS, (mix_norm_g, w_in, q_norm_g, k_norm_g, cmp_pe_k, cmp_pe_v, cmp_k_w1,
                                       cmp_k_w2, cmp_v_w1, cmp_v_w2, lru_conv_w, lru_conv_b, lru_w_r, lru_b_r,
                                       lru_w_i, lru_b_i, lru_lambda, ssm_conv_w, ssm_conv_b, ssm_dt_bias,
                                       ssm_a_log, ssm_d, ssm_norm_g, w_branch, w_out, ffn_norm_g, peer_w_q,
                                       peer_sub_keys, peer_u, peer_v)))
    b, s, d = x.shape
    xf = x.reshape(b * s, d).astype(F32)
    for layer in range(mix_norm_g.shape[0]):
        p = {name: arr[layer] for name, arr in stacked.items()}
        xf = _mixer(xf, positions, b, s, p)
        xf = _peer(xf, p)
    return xf.reshape(b, s, d).astype(x.dtype)
```

```python
import functools
import math

import jax
import jax.numpy as jnp
from jax import lax
from jax.experimental import pallas as pl
from jax.experimental.pallas import tpu as pltpu

F32 = jnp.float32
BF16 = jnp.bfloat16
I32 = jnp.int32

D_MODEL = 1024
HEAD_DIM = 64
N_HEADS = 16
N_KV_GROUPS = 2
HEADS_PER_GROUP = 8
CMP_BLOCK = 32
CMP_STRIDE = 16
CMP_HIDDEN = 256
SEL_BLOCK = 64
N_SELECT = 16
WINDOW = 512
SEL_FORCE = 100.0
ROPE_THETA = 10000.0
SCALE = HEAD_DIM ** -0.5
LOG2E = math.log2(math.e)
LRU_C = 8.0
CONV_WIDTH = 4
SSM_HEADS = 16
SSM_HEAD_DIM = 64
SSM_GROUPS = 2
SSM_STATE = 128
SSM_CHUNK = 128
SSM_INNER = 1024
PEER_HEADS = 8
PEER_KEYS = 128
PEER_HALF = 128
PEER_TOPK = 16
EPS = 1e-6
NEG = -1e30
LANES = 128

VMEM_LIMIT = 56 * 1024 * 1024
SEL_LOOKAHEAD = 5
WIN_SUBTILE = 256
WIN_LOOKAHEAD = 3
SEL_SUBTILE = 256
SEL_ACC_ROWS = HEAD_DIM + 16
WG_PITCH = PEER_KEYS + 8
ROUTE_SLACK = 3
ROUTE_LANES = 128


def _params(sem):
    return pltpu.CompilerParams(dimension_semantics=sem, vmem_limit_bytes=VMEM_LIMIT)


def _gelu(x):
    return 0.5 * x * (1.0 + jnp.tanh(math.sqrt(2.0 / math.pi) * (x + 0.044715 * x * x * x)))


def _sigmoid(x):
    return 0.5 * jnp.tanh(0.5 * x) + 0.5


def _softplus(x):
    return jnp.maximum(x, 0.0) + jnp.log(1.0 + jnp.exp(-jnp.abs(x)))


def _rmsnorm_kernel(x_ref, g_ref, o_ref):
    x = x_ref[...].astype(F32)
    ms = jnp.mean(x * x, axis=-1, keepdims=True)
    o_ref[...] = (x * lax.rsqrt(ms + EPS) * g_ref[...]).astype(o_ref.dtype)


def _rmsnorm(x, g, tm=512):
    t, d = x.shape
    return pl.pallas_call(
        _rmsnorm_kernel, grid=(t // tm,),
        in_specs=[pl.BlockSpec((tm, d), lambda i: (i, 0)), pl.BlockSpec((1, d), lambda i: (0, 0))],
        out_specs=pl.BlockSpec((tm, d), lambda i: (i, 0)),
        out_shape=jax.ShapeDtypeStruct((t, d), BF16),
        compiler_params=_params(("parallel",)), name="rmsnorm")(x, g.reshape(1, d).astype(F32))


def _norm_rope(x, g, cos, sin_signed, bd):
    ms = jnp.dot((x * x).astype(BF16), bd, preferred_element_type=F32)
    y = x * lax.rsqrt(ms + EPS) * g
    lane = lax.broadcasted_iota(I32, y.shape, 1)
    first_half = (lane % HEAD_DIM) < (HEAD_DIM // 2)
    partner = jnp.where(first_half, pltpu.roll(y, LANES - HEAD_DIM // 2, 1), pltpu.roll(y, HEAD_DIM // 2, 1))
    return y * cos + partner * sin_signed


def _nsa_prep_kernel(aq_ref, akv_ref, pos_ref, inv_ref, sgn_ref, gq_ref, gk_ref, bd_ref,
                     q_ref, kc_ref, vc_ref, ks_ref, vs_ref, kw_ref, vw_ref):
    ang = pos_ref[...].astype(F32) * inv_ref[...]
    cos = jnp.cos(ang)
    sin_signed = jnp.sin(ang) * sgn_ref[...]
    bd = bd_ref[...]
    for c in range(D_MODEL // LANES):
        x = aq_ref[:, c * LANES:(c + 1) * LANES].astype(F32)
        y = (_norm_rope(x, gq_ref[...], cos, sin_signed, bd) * (SCALE * LOG2E)).astype(q_ref.dtype)
        q_ref[0, 2 * c] = y[:, :HEAD_DIM]
        q_ref[0, 2 * c + 1] = y[:, HEAD_DIM:]
    outs = (kc_ref, vc_ref, ks_ref, vs_ref, kw_ref, vw_ref)
    for c, o_ref in enumerate(outs):
        x = akv_ref[:, c * LANES:(c + 1) * LANES]
        if c % 2 == 0:
            y = _norm_rope(x.astype(F32), gk_ref[...], cos, sin_signed, bd).astype(o_ref.dtype)
        else:
            y = x.astype(o_ref.dtype)
        o_ref[0, 0] = y[:, :HEAD_DIM]
        o_ref[0, 1] = y[:, HEAD_DIM:]


def _nsa_prep(a_q, a_kv, positions, q_norm_g, k_norm_g, b, s, tm=512):
    half = HEAD_DIM // 2
    lane = jnp.arange(LANES)
    inv = (ROPE_THETA ** (-((lane % half).astype(F32)) / half)).reshape(1, LANES)
    sgn = jnp.where((lane % HEAD_DIM) < half, -1.0, 1.0).astype(F32).reshape(1, LANES)
    bd = jnp.where((lane[:, None] // HEAD_DIM) == (lane[None, :] // HEAD_DIM), 1.0 / HEAD_DIM, 0.0).astype(BF16)
    gq = jnp.tile(q_norm_g.astype(F32), 2).reshape(1, LANES)
    gk = jnp.tile(k_norm_g.astype(F32), 2).reshape(1, LANES)
    nt = s // tm
    row = lambda i: (i, 0)
    const = lambda i: (0, 0)
    kv_shape = jax.ShapeDtypeStruct((b, N_KV_GROUPS, s, HEAD_DIM), BF16)
    kv_spec = pl.BlockSpec((1, N_KV_GROUPS, tm, HEAD_DIM), lambda i: (i // nt, 0, i % nt, 0))
    return pl.pallas_call(
        _nsa_prep_kernel, grid=(b * nt,),
        in_specs=[pl.BlockSpec((tm, D_MODEL), row), pl.BlockSpec((tm, 6 * LANES), row),
                  pl.BlockSpec((tm, 1), row), pl.BlockSpec((1, LANES), const), pl.BlockSpec((1, LANES), const),
                  pl.BlockSpec((1, LANES), const), pl.BlockSpec((1, LANES), const),
                  pl.BlockSpec((LANES, LANES), const)],
        out_specs=[pl.BlockSpec((1, N_HEADS, tm, HEAD_DIM), lambda i: (i // nt, 0, i % nt, 0))] + [kv_spec] * 6,
        out_shape=[jax.ShapeDtypeStruct((b, N_HEADS, s, HEAD_DIM), BF16)] + [kv_shape] * 6,
        compiler_params=_params(("parallel",)), name="nsa_prep",
    )(a_q, a_kv, positions.reshape(b * s, 1).astype(I32), inv, sgn, gq, gk, bd)


def _compress_kernel(uk_ref, uv_ref, pek_ref, pev_ref, kw1_ref, kw2_ref, vw1_ref, vw2_ref, gk_ref,
                     kc_ref, vc_ref):
    half = CMP_STRIDE * HEAD_DIM

    def mlp(u, pe, w1_ref, w2_ref):
        n = u.shape[0]
        ha = jnp.dot(u, w1_ref[:half, :], preferred_element_type=F32)
        hb = jnp.dot(u, w1_ref[half:, :], preferred_element_type=F32)
        bias = jnp.dot(pe, w1_ref[...], preferred_element_type=F32)[0:1, :]
        pre = ha + pltpu.roll(hb, n - 1, 0) + bias
        return jnp.dot(_gelu(pre).astype(BF16), w2_ref[...], preferred_element_type=F32)

    k = mlp(uk_ref[0, 0], pek_ref[...], kw1_ref, kw2_ref)
    ms = jnp.mean(k * k, axis=-1, keepdims=True)
    kc_ref[0, 0] = (k * lax.rsqrt(ms + EPS) * gk_ref[...]).astype(kc_ref.dtype)
    vc_ref[0, 0] = mlp(uv_ref[0, 0], pev_ref[...], vw1_ref, vw2_ref).astype(vc_ref.dtype)


def _compress(kc, vc, pe_k, pe_v, kw1, kw2, vw1, vw2, k_norm_g):
    b, g, s, dh = kc.shape
    ng = s // CMP_STRIDE
    wide = CMP_BLOCK * dh
    uk = kc.reshape(b, g, ng, CMP_STRIDE * dh)
    uv = vc.reshape(b, g, ng, CMP_STRIDE * dh)
    pek = jnp.zeros((8, wide), BF16).at[0].set(pe_k.reshape(wide).astype(BF16))
    pev = jnp.zeros((8, wide), BF16).at[0].set(pe_v.reshape(wide).astype(BF16))
    u_spec = pl.BlockSpec((1, 1, ng, CMP_STRIDE * dh), lambda i, j: (i, j, 0, 0))
    c2 = lambda i, j: (0, 0)
    o_spec = pl.BlockSpec((1, 1, ng, dh), lambda i, j: (i, j, 0, 0))
    o_shape = jax.ShapeDtypeStruct((b, g, ng, dh), BF16)
    return pl.pallas_call(
        _compress_kernel, grid=(b, g),
        in_specs=[u_spec, u_spec, pl.BlockSpec((8, wide), c2), pl.BlockSpec((8, wide), c2),
                  pl.BlockSpec((wide, CMP_HIDDEN), c2), pl.BlockSpec((CMP_HIDDEN, dh), c2),
                  pl.BlockSpec((wide, CMP_HIDDEN), c2), pl.BlockSpec((CMP_HIDDEN, dh), c2),
                  pl.BlockSpec((1, dh), c2)],
        out_specs=[o_spec, o_spec], out_shape=[o_shape, o_shape],
        compiler_params=_params(("parallel", "parallel")), name="nsa_compress",
    )(uk, uv, pek, pev, kw1.astype(BF16), kw2.astype(BF16), vw1.astype(BF16), vw2.astype(BF16),
      k_norm_g.reshape(1, dh).astype(F32))


def _store_head_pair(o_ref, h, o_t, held):
    if h % 2 == 0:
        return o_t
    pair = jnp.concatenate([held, o_t], axis=0).T
    o_ref[0, :, (h - 1) * HEAD_DIM:(h + 1) * HEAD_DIM] = pair.astype(o_ref.dtype)
    return None


def _cmp_select_kernel(q_ref, kc_ref, vc_ref, ovt_ref, o_ref, bias_ref, *, tq, n_cmp, n_pick):
    i = pl.program_id(2)
    chunk = min(LANES, kc_ref.shape[2])
    visible = ((i + 1) * tq - CMP_BLOCK) // CMP_STRIDE + 1
    nchunks = jnp.clip((visible + chunk - 1) // chunk, 1, kc_ref.shape[2] // chunk)
    for n in range(1, kc_ref.shape[2] // chunk + 1):
        pl.when(nchunks == n)(functools.partial(
            _cmp_select_keys, q_ref, kc_ref, vc_ref, ovt_ref, o_ref, bias_ref, nc=n * chunk, i=i, tq=tq,
            n_cmp=n_cmp, n_pick=n_pick))


def _cmp_select_keys(q_ref, kc_ref, vc_ref, ovt_ref, o_ref, bias_ref, *, nc, i, tq, n_cmp, n_pick):
    hg = q_ref.shape[1]
    nsel = ovt_ref.shape[0]
    kc = kc_ref[0, 0, :nc, :]
    vct = vc_ref[0, 0, :nc, :].astype(F32).T.astype(BF16)
    ovt = ovt_ref[:, :nc]
    c = lax.broadcasted_iota(I32, (nc, tq), 0)
    t = i * tq + lax.broadcasted_iota(I32, (nc, tq), 1)
    mask = ((CMP_STRIDE * c + CMP_BLOCK - 1) <= t) & (c < n_cmp)
    score = lambda h: lax.dot_general(kc, q_ref[0, h], (((1,), (1,)), ((), ())), preferred_element_type=F32)
    ahead = [score(h) for h in range(WIN_LOOKAHEAD)]
    psum = jnp.zeros((nc, tq), F32)
    held = None
    for h in range(hg):
        st = ahead.pop(0)
        if h + WIN_LOOKAHEAD < hg:
            ahead.append(score(h + WIN_LOOKAHEAD))
        st = jnp.where(mask, st, NEG)
        m = jnp.max(st, axis=0, keepdims=True)
        p = jnp.where(mask, jnp.exp2(st - m), 0.0)
        l = jnp.sum(p, axis=0, keepdims=True)
        p = p * jnp.where(l > 0.0, 1.0 / l, 0.0)
        o = jnp.dot(vct, p.astype(BF16), preferred_element_type=F32)
        held = _store_head_pair(o_ref, h, o, held)
        psum = psum + p

    hi = psum.astype(BF16)
    lo = (psum - hi.astype(F32)).astype(BF16)
    imp = jnp.dot(ovt, hi, preferred_element_type=F32) + jnp.dot(ovt, lo, preferred_element_type=F32)
    j = lax.broadcasted_iota(I32, (nsel, tq), 0)
    tt = i * tq + lax.broadcasted_iota(I32, (nsel, tq), 1)
    cur = tt // SEL_BLOCK
    forced = (j == 0) | (j == cur) | (j == cur - 1)
    sc = jnp.where(forced, SEL_FORCE, jnp.where(j * SEL_BLOCK <= tt, imp, -1.0))
    picked = jnp.zeros((nsel, tq), jnp.bool_)
    for _ in range(n_pick):
        mx = jnp.max(sc, axis=0, keepdims=True)
        idx = jnp.min(jnp.where(sc == mx, j, nsel), axis=0, keepdims=True)
        hit = j == idx
        picked = picked | hit
        sc = jnp.where(hit, -jnp.inf, sc)
    bias_ref[0, 0] = jnp.where(picked & (j <= cur), 0.0, NEG).T.astype(bias_ref.dtype)


def _cmp_select(q, k_cmp, v_cmp, s, tq=512):
    b, nh, _, dh = q.shape
    g, hg = N_KV_GROUPS, HEADS_PER_GROUP
    nc = k_cmp.shape[2]
    n_cmp = (s - CMP_BLOCK) // CMP_STRIDE + 1
    nsel = s // SEL_BLOCK
    n_pick = min(N_SELECT, nsel)
    cs = CMP_STRIDE * jnp.arange(nc)
    ss = SEL_BLOCK * jnp.arange(nsel)
    ovt = jnp.clip(jnp.minimum(cs[None, :] + CMP_BLOCK, ss[:, None] + SEL_BLOCK)
                   - jnp.maximum(cs[None, :], ss[:, None]), 0).astype(F32) / CMP_BLOCK
    ovt = jnp.where(jnp.arange(nc)[None, :] < n_cmp, ovt, 0.0).astype(BF16)
    kern = functools.partial(_cmp_select_kernel, tq=tq, n_cmp=n_cmp, n_pick=n_pick)
    return pl.pallas_call(
        kern, grid=(b, g, s // tq),
        in_specs=[pl.BlockSpec((1, hg, tq, dh), lambda bi, gi, i: (bi, gi, i, 0)),
                  pl.BlockSpec((1, 1, nc, dh), lambda bi, gi, i: (bi, gi, 0, 0)),
                  pl.BlockSpec((1, 1, nc, dh), lambda bi, gi, i: (bi, gi, 0, 0)),
                  pl.BlockSpec((nsel, nc), lambda bi, gi, i: (0, 0))],
        out_specs=[pl.BlockSpec((1, tq, hg * dh), lambda bi, gi, i: (bi, i, gi)),
                   pl.BlockSpec((1, 1, tq, nsel), lambda bi, gi, i: (bi, gi, i, 0))],
        out_shape=[jax.ShapeDtypeStruct((b, s, nh * dh), BF16),
                   jax.ShapeDtypeStruct((b, g, s, nsel), BF16)],
        compiler_params=_params(("parallel", "parallel", "parallel")), name="nsa_cmp_select",
    )(q, k_cmp, v_cmp, ovt)


def _sel_attn_kernel(it_ref, jt_ref, q_ref, bias_ref, k_ref, v_ref, o_ref, qa_ref, m_ref, acc_ref, *, tq, tk):
    pid = pl.program_id(2)
    i = it_ref[pid]
    j = jt_ref[pid]
    hg = q_ref.shape[1]
    nsel = bias_ref.shape[3]
    last_j = ((i + 1) * tq - 1) // tk

    @pl.when(j == 0)
    def _():
        for h in range(hg):
            qa_ref[h, :, :nsel] = bias_ref[0, 0]
            qa_ref[h, :, nsel:] = q_ref[0, h]
        m_ref[...] = jnp.full(m_ref.shape, -jnp.inf, F32)
        acc_ref[...] = jnp.zeros(acc_ref.shape, F32)

    def accumulate(diagonal):
        key = j * tk + lax.broadcasted_iota(I32, (tk, nsel), 0)
        blk = lax.broadcasted_iota(I32, (tk, nsel), 1)
        onehot = jnp.where(key // SEL_BLOCK == blk, 1.0, 0.0).astype(BF16)
        ka = jnp.concatenate([onehot, k_ref[0, 0]], axis=1)
        va = jnp.concatenate([v_ref[0, 0].astype(F32), jnp.ones((tk, LANES - HEAD_DIM), F32)], axis=1)
        vat = va.T[:SEL_ACC_ROWS, :].astype(BF16)
        qw = min(SEL_SUBTILE, tq)
        items = [(h, b) for h in range(hg) for b in range(tq // qw)]
        rows = lambda b: (b + 1) * qw if diagonal else tk

        def score(item):
            h, b = item
            return lax.dot_general(ka[:rows(b), :], qa_ref[h, b * qw:(b + 1) * qw, :], (((1,), (1,)), ((), ())),
                                   preferred_element_type=F32)

        ahead = [score(item) for item in items[:SEL_LOOKAHEAD]]
        for n, (h, b) in enumerate(items):
            st = ahead.pop(0)
            if n + SEL_LOOKAHEAD < len(items):
                ahead.append(score(items[n + SEL_LOOKAHEAD]))
            if diagonal:
                kp = lax.broadcasted_iota(I32, st.shape, 0)
                t = b * qw + lax.broadcasted_iota(I32, st.shape, 1)
                st = jnp.where(kp <= t, st, NEG)
            lanes = slice(b * qw, (b + 1) * qw)
            m_old = m_ref[h, :, lanes]
            m_new = jnp.maximum(m_old, jnp.max(st, axis=0, keepdims=True))
            alpha = jnp.exp2(m_old - m_new)
            p = jnp.exp2(st - m_new)
            acc_ref[h, :, lanes] = alpha * acc_ref[h, :, lanes] + jnp.dot(
                vat[:, :rows(b)], p.astype(BF16), preferred_element_type=F32)
            m_ref[h, :, lanes] = m_new

    @pl.when(j < last_j)
    def _():
        accumulate(False)

    @pl.when(j == last_j)
    def _():
        accumulate(True)
        held = None
        for h in range(hg):
            acc = acc_ref[h]
            o = acc[:HEAD_DIM, :] / acc[HEAD_DIM:HEAD_DIM + 1, :]
            held = _store_head_pair(o_ref, h, o, held)


def _sel_attn(q, bias, k_s, v_s, s, tq=1024):
    tk = tq
    b, nh, _, dh = q.shape
    g, hg = N_KV_GROUPS, HEADS_PER_GROUP
    nsel = bias.shape[3]
    pairs = [(i, j) for i in range(s // tq) for j in range(((i + 1) * tq - 1) // tk + 1)]
    it = jnp.asarray([pr[0] for pr in pairs], I32)
    jt = jnp.asarray([pr[1] for pr in pairs], I32)
    q_map = lambda bi, gi, pid, it_ref, jt_ref: (bi, gi, it_ref[pid], 0)
    kv_map = lambda bi, gi, pid, it_ref, jt_ref: (bi, gi, jt_ref[pid], 0)
    kern = functools.partial(_sel_attn_kernel, tq=tq, tk=tk)
    grid_spec = pltpu.PrefetchScalarGridSpec(
        num_scalar_prefetch=2, grid=(b, g, len(pairs)),
        in_specs=[pl.BlockSpec((1, hg, tq, dh), q_map), pl.BlockSpec((1, 1, tq, nsel), q_map),
                  pl.BlockSpec((1, 1, tk, dh), kv_map), pl.BlockSpec((1, 1, tk, dh), kv_map)],
        out_specs=pl.BlockSpec((1, tq, hg * dh), lambda bi, gi, pid, it_ref, jt_ref: (bi, it_ref[pid], gi)),
        scratch_shapes=[pltpu.VMEM((hg, tq, nsel + dh), BF16), pltpu.VMEM((hg, 1, tq), F32),
                        pltpu.VMEM((hg, SEL_ACC_ROWS, tq), F32)])
    return pl.pallas_call(
        kern, grid_spec=grid_spec, out_shape=jax.ShapeDtypeStruct((b, s, nh * dh), BF16),
        compiler_params=_params(("parallel", "parallel", "arbitrary")), name="nsa_sel_attn",
    )(it, jt, q, bias, k_s, v_s)


def _win_attn_kernel(q_ref, *refs, tq, nwin):
    k_refs, v_refs, o_ref = refs[:nwin], refs[nwin:2 * nwin], refs[2 * nwin]
    i = pl.program_id(2)
    hg = q_ref.shape[1]
    k = jnp.concatenate([r[0, 0] for r in k_refs], axis=0)
    vt = jnp.concatenate([r[0, 0] for r in v_refs], axis=0).astype(F32).T.astype(BF16)
    qw = WIN_SUBTILE
    nk = WINDOW + qw
    r = lax.broadcasted_iota(I32, (nk, qw), 0)
    c = lax.broadcasted_iota(I32, (nk, qw), 1)
    band = (r > c) & (r <= c + WINDOW)
    items = [(h, b) for h in range(hg) for b in range(tq // qw)]

    def score(item):
        h, b = item
        return lax.dot_general(k[b * qw:b * qw + nk, :], q_ref[0, h, b * qw:(b + 1) * qw, :],
                               (((1,), (1,)), ((), ())), preferred_element_type=F32)

    ahead = [score(item) for item in items[:WIN_LOOKAHEAD]]
    held, parts = None, []
    for n, (h, b) in enumerate(items):
        st = ahead.pop(0)
        if n + WIN_LOOKAHEAD < len(items):
            ahead.append(score(items[n + WIN_LOOKAHEAD]))
        mask = band & (i * tq - WINDOW + b * qw + r >= 0)
        st = jnp.where(mask, st, NEG)
        m = jnp.max(st, axis=0, keepdims=True)
        p = jnp.where(mask, jnp.exp2(st - m), 0.0)
        l = jnp.sum(p, axis=0, keepdims=True)
        parts.append(jnp.dot(vt[:, b * qw:b * qw + nk], p.astype(BF16), preferred_element_type=F32) / l)
        if len(parts) == tq // qw:
            held = _store_head_pair(o_ref, h, jnp.concatenate(parts, axis=1), held)
            parts = []


def _win_attn(q, k_w, v_w, s, tq=512):
    b, nh, _, dh = q.shape
    g, hg = N_KV_GROUPS, HEADS_PER_GROUP
    nwin = WINDOW // tq + 1
    pad = ((0, 0), (0, 0), (WINDOW, 0), (0, 0))
    kp, vp = jnp.pad(k_w, pad), jnp.pad(v_w, pad)
    kv_specs = [pl.BlockSpec((1, 1, tq, dh), functools.partial(lambda bi, gi, i, w: (bi, gi, i + w, 0), w=w))
                for w in range(nwin)]
    kern = functools.partial(_win_attn_kernel, tq=tq, nwin=nwin)
    return pl.pallas_call(
        kern, grid=(b, g, s // tq),
        in_specs=[pl.BlockSpec((1, hg, tq, dh), lambda bi, gi, i: (bi, gi, i, 0))] + kv_specs + kv_specs,
        out_specs=pl.BlockSpec((1, tq, hg * dh), lambda bi, gi, i: (bi, i, gi)),
        out_shape=jax.ShapeDtypeStruct((b, s, nh * dh), BF16),
        compiler_params=_params(("parallel", "parallel", "parallel")), name="nsa_win_attn",
    )(q, *([kp] * nwin), *([vp] * nwin))


def _causal_conv(x, win_ref, w_ref, b_ref):
    n = x.shape[0]
    win_ref[8:, :] = x
    out = b_ref[...] + w_ref[CONV_WIDTH - 1:CONV_WIDTH, :] * x
    for k in range(CONV_WIDTH - 1):
        off = 8 - (CONV_WIDTH - 1) + k
        out = out + w_ref[k:k + 1, :] * win_ref[off:off + n, :]
    win_ref[0:8, :] = x[n - 8:, :]
    return out


def _rglru_kernel(x_ref, gate_ref, pos_ref, cw_ref, cb_ref, wr_ref, br_ref, wi_ref, bi_ref, lam_ref,
                  o_ref, tail_ref, h_ref):
    @pl.when(pl.program_id(1) == 0)
    def _():
        tail_ref[0:8, :] = jnp.zeros((8, tail_ref.shape[1]), F32)
        h_ref[...] = jnp.zeros(h_ref.shape, F32)

    n = x_ref.shape[0]
    x = x_ref[...].astype(F32)
    xc = _causal_conv(x, tail_ref, cw_ref, cb_ref)
    xcb = xc.astype(BF16)
    r = _sigmoid(jnp.dot(xcb, wr_ref[...], preferred_element_type=F32) + br_ref[...])
    gi = _sigmoid(jnp.dot(xcb, wi_ref[...], preferred_element_type=F32) + bi_ref[...])
    log_a = -LRU_C * r * _softplus(-lam_ref[...])
    reset = pos_ref[...] == 0
    a = jnp.where(reset, 0.0, jnp.exp(log_a))
    mult = jnp.where(reset, 1.0, jnp.sqrt(jnp.maximum(1.0 - jnp.exp(2.0 * log_a), 0.0)))
    bb = mult * (gi * xc)
    sub = lax.broadcasted_iota(I32, a.shape, 0) % 8
    d = 1
    while d < 8:
        a_sh = pltpu.roll(a, d, 0)
        b_sh = pltpu.roll(bb, d, 0)
        live = sub >= d
        bb = jnp.where(live, a * b_sh + bb, bb)
        a = jnp.where(live, a * a_sh, a)
        d *= 2
    carry = h_ref[...]
    hs = []
    for g in range(n // 8):
        h = bb[8 * g:8 * g + 8, :] + a[8 * g:8 * g + 8, :] * carry
        carry = h[7:8, :]
        hs.append(h)
    h_ref[...] = carry
    o_ref[...] = (jnp.concatenate(hs, axis=0) * _gelu(gate_ref[...].astype(F32))).astype(o_ref.dtype)


def _block_diag(w):
    nb, bs, _ = w.shape
    eye = jnp.eye(nb, dtype=w.dtype)
    return (w[:, :, None, :] * eye[:, None, :, None]).reshape(nb * bs, nb * bs)


def _rglru(l_x, l_g, positions, conv_w, conv_b, w_r, b_r, w_i, b_i, lam, b, s, ts=256):
    d = l_x.shape[1]
    nt = s // ts
    row = lambda bi, i: (bi * nt + i, 0)
    c2 = lambda bi, i: (0, 0)
    vec = lambda v: v.reshape(1, d).astype(F32)
    return pl.pallas_call(
        _rglru_kernel, grid=(b, nt),
        in_specs=[pl.BlockSpec((ts, d), row), pl.BlockSpec((ts, d), row), pl.BlockSpec((ts, 1), row),
                  pl.BlockSpec((CONV_WIDTH, d), c2), pl.BlockSpec((1, d), c2),
                  pl.BlockSpec((d, d), c2), pl.BlockSpec((1, d), c2),
                  pl.BlockSpec((d, d), c2), pl.BlockSpec((1, d), c2), pl.BlockSpec((1, d), c2)],
        out_specs=pl.BlockSpec((ts, d), row),
        out_shape=jax.ShapeDtypeStruct((b * s, d), BF16),
        scratch_shapes=[pltpu.VMEM((8 + ts, d), F32), pltpu.VMEM((1, d), F32)],
        compiler_params=_params(("parallel", "arbitrary")), name="rglru",
    )(l_x, l_g, positions.reshape(b * s, 1).astype(I32), conv_w.astype(F32), vec(conv_b),
      _block_diag(w_r).astype(BF16), vec(b_r), _block_diag(w_i).astype(BF16), vec(b_i), vec(lam))


def _ssd_kernel(z_ref, xbc_ref, dt_ref, cw_ref, cb_ref, dtb_ref, alog_ref, dfull_ref, ng_ref,
                o_ref, tail_ref, state_ref, y_ref):
    @pl.when(pl.program_id(1) == 0)
    def _():
        tail_ref[0:8, :] = jnp.zeros((8, tail_ref.shape[1]), F32)
        state_ref[...] = jnp.zeros(state_ref.shape, F32)

    n = xbc_ref.shape[0]
    hg = SSM_HEADS // SSM_GROUPS
    x = xbc_ref[...].astype(F32)
    xc = _causal_conv(x, tail_ref, cw_ref, cb_ref)
    xc = xc * _sigmoid(xc)
    xs = xc[:, :SSM_INNER]
    dt = _softplus(dt_ref[...] + dtb_ref[...])
    adt = dt * (-jnp.exp(alog_ref[...]))
    row = lax.broadcasted_iota(I32, adt.shape, 0)
    acs = adt
    d = 1
    while d < n:
        acs = acs + jnp.where(row >= d, pltpu.roll(acs, d, 0), 0.0)
        d *= 2
    acs_t = acs.T
    li = lax.broadcasted_iota(I32, (n, n), 0)
    si = lax.broadcasted_iota(I32, (n, n), 1)
    tri = li >= si
    for g in range(SSM_GROUPS):
        bm = xc[:, SSM_INNER + g * SSM_STATE:SSM_INNER + (g + 1) * SSM_STATE].astype(BF16)
        cm = xc[:, SSM_INNER + (SSM_GROUPS + g) * SSM_STATE:SSM_INNER + (SSM_GROUPS + g + 1) * SSM_STATE].astype(BF16)
        cb = lax.dot_general(cm, bm, (((1,), (1,)), ((), ())), preferred_element_type=F32)
        bm_t = bm.T
        for hh in range(hg):
            h = g * hg + hh
            acol = jnp.broadcast_to(acs[:, h:h + 1], (n, n))
            arow = acs_t[h:h + 1, :]
            decay = jnp.exp(jnp.where(tri, acol - arow, NEG))
            acol_p = acol[:, :SSM_HEAD_DIM]
            xh = xs[:, h * SSM_HEAD_DIM:(h + 1) * SSM_HEAD_DIM] * dt[:, h:h + 1]
            a_last = acol_p[n - 1:n, :]
            y = jnp.dot((cb * decay).astype(BF16), xh.astype(BF16), preferred_element_type=F32)
            st = state_ref[h]
            y = y + jnp.dot(cm, st.astype(BF16), preferred_element_type=F32) * jnp.exp(acol_p)
            upd = jnp.dot(bm_t, (xh * jnp.exp(a_last - acol_p)).astype(BF16), preferred_element_type=F32)
            state_ref[h] = jnp.exp(a_last) * st + upd
            y_ref[:, h * SSM_HEAD_DIM:(h + 1) * SSM_HEAD_DIM] = y
    z = z_ref[...].astype(F32)
    y = (y_ref[...] + dfull_ref[...] * xs) * (z * _sigmoid(z))
    gw = SSM_INNER // SSM_GROUPS
    for g in range(SSM_GROUPS):
        yg = y[:, g * gw:(g + 1) * gw]
        ms = jnp.mean(yg * yg, axis=-1, keepdims=True)
        o_ref[:, g * gw:(g + 1) * gw] = (yg * lax.rsqrt(ms + EPS) * ng_ref[:, g * gw:(g + 1) * gw]).astype(o_ref.dtype)


def _ssd(s_z, s_xbc, s_dt, conv_w, conv_b, dt_bias, a_log, d_skip, norm_g, b, s):
    n = SSM_CHUNK
    nt = s // n
    c = s_xbc.shape[1]
    row = lambda bi, i: (bi * nt + i, 0)
    c2 = lambda bi, i: (0, 0)
    pad_h = lambda v: jnp.zeros((1, LANES), F32).at[0, :SSM_HEADS].set(v.astype(F32))
    dfull = jnp.repeat(d_skip.astype(F32), SSM_HEAD_DIM).reshape(1, SSM_INNER)
    return pl.pallas_call(
        _ssd_kernel, grid=(b, nt),
        in_specs=[pl.BlockSpec((n, SSM_INNER), row), pl.BlockSpec((n, c), row), pl.BlockSpec((n, LANES), row),
                  pl.BlockSpec((CONV_WIDTH, c), c2), pl.BlockSpec((1, c), c2),
                  pl.BlockSpec((1, LANES), c2), pl.BlockSpec((1, LANES), c2),
                  pl.BlockSpec((1, SSM_INNER), c2), pl.BlockSpec((1, SSM_INNER), c2)],
        out_specs=pl.BlockSpec((n, SSM_INNER), row),
        out_shape=jax.ShapeDtypeStruct((b * s, SSM_INNER), BF16),
        scratch_shapes=[pltpu.VMEM((8 + n, c), F32), pltpu.VMEM((SSM_HEADS, SSM_STATE, SSM_HEAD_DIM), F32),
                        pltpu.VMEM((n, SSM_INNER), F32)],
        compiler_params=_params(("parallel", "arbitrary")), name="ssd",
    )(s_z, s_xbc, s_dt, conv_w.astype(F32), conv_b.reshape(1, c).astype(F32), pad_h(dt_bias), pad_h(a_log),
      dfull, norm_g.reshape(1, SSM_INNER).astype(F32))


def _merge_kernel(oc_ref, os_ref, ow_ref, ag_ref, ex_ref, yb_ref, yc_ref, mg_ref, x_ref, wb_ref, wo_ref, o_ref):
    gates = _sigmoid(ag_ref[...])
    hi = gates.astype(BF16)
    lo = (gates - hi.astype(F32)).astype(BF16)
    spread = lambda k: (jnp.dot(hi, ex_ref[k], preferred_element_type=F32)
                        + jnp.dot(lo, ex_ref[k], preferred_element_type=F32))
    ya = (spread(0) * oc_ref[...].astype(F32) + spread(1) * os_ref[...].astype(F32)
          + spread(2) * ow_ref[...].astype(F32))
    d = D_MODEL
    merged = _sigmoid(mg_ref[:, 0:d].astype(F32)) * jnp.dot(ya.astype(BF16), wb_ref[0], preferred_element_type=F32)
    merged += _sigmoid(mg_ref[:, d:2 * d].astype(F32)) * jnp.dot(yb_ref[...], wb_ref[1], preferred_element_type=F32)
    merged += _sigmoid(mg_ref[:, 2 * d:3 * d].astype(F32)) * jnp.dot(yc_ref[...], wb_ref[2], preferred_element_type=F32)
    o_ref[...] = x_ref[...] + jnp.dot(merged.astype(BF16), wo_ref[...], preferred_element_type=F32)


def _merge(o_c, o_s, o_w, a_g, y_b, y_c, m_g, x, w_branch, w_out, tm=512):
    t, d = x.shape
    row = lambda i: (i, 0)
    lane = jnp.arange(LANES)[:, None]
    col = jnp.arange(d)[None, :]
    expand = jnp.stack([(lane == 3 * (col // HEAD_DIM) + k) for k in range(3)]).astype(BF16)
    return pl.pallas_call(
        _merge_kernel, grid=(t // tm,),
        in_specs=[pl.BlockSpec((tm, d), row), pl.BlockSpec((tm, d), row), pl.BlockSpec((tm, d), row),
                  pl.BlockSpec((tm, LANES), row), pl.BlockSpec((3, LANES, d), lambda i: (0, 0, 0)),
                  pl.BlockSpec((tm, d), row), pl.BlockSpec((tm, d), row), pl.BlockSpec((tm, 3 * d), row),
                  pl.BlockSpec((tm, d), row),
                  pl.BlockSpec((3, d, d), lambda i: (0, 0, 0)), pl.BlockSpec((d, d), lambda i: (0, 0))],
        out_specs=pl.BlockSpec((tm, d), row),
        out_shape=jax.ShapeDtypeStruct((t, d), F32),
        compiler_params=_params(("parallel",)), name="merge",
    )(o_c.reshape(t, d), o_s.reshape(t, d), o_w.reshape(t, d), a_g, expand, y_b, y_c, m_g, x,
      w_branch.astype(BF16), w_out.astype(BF16))


class _Interleaver:
    def __init__(self, pieces, every):
        self.pieces, self.every, self.count = list(pieces), every, 0

    def tick(self):
        self.count += 1
        if self.pieces and self.count % self.every == 0:
            self.pieces.pop(0)()

    def drain(self):
        while self.pieces:
            self.pieces.pop(0)()


def _sorting_network(n):
    pairs = []

    def merge(lo, m, r):
        step = r * 2
        if step < m:
            merge(lo, m, step)
            merge(lo + r, m, step)
            pairs.extend((i, i + r) for i in range(lo + r, lo + m - r, step))
        else:
            pairs.append((lo, lo + r))

    def sort(lo, m):
        if m > 1:
            sort(lo, m // 2)
            sort(lo + m // 2, m // 2)
            merge(lo, m, 1)

    sort(0, n)
    return pairs


def _topk_rows(s, k, tick):
    n, lanes = s.shape
    assert n == 8 * k
    sub = lax.broadcasted_iota(I32, (8, lanes), 0)
    vals = [s[8 * j:8 * j + 8, :] for j in range(k)]
    ids = [sub + 8 * j for j in range(k)]
    for count, (a, b) in enumerate(_sorting_network(k)):
        first = (vals[a] > vals[b]) | ((vals[a] == vals[b]) & (ids[a] < ids[b]))
        vals[a], vals[b] = jnp.where(first, vals[a], vals[b]), jnp.where(first, vals[b], vals[a])
        ids[a], ids[b] = jnp.where(first, ids[a], ids[b]), jnp.where(first, ids[b], ids[a])
        if count % 16 == 15:
            tick()
    top_v, top_i = [], []
    for r in range(k):
        best = jnp.max(vals[0], axis=0, keepdims=True)
        row = jnp.min(jnp.where(vals[0] == best, ids[0], n), axis=0, keepdims=True)
        top_v.append(best)
        top_i.append(row)
        won = ids[0] == row
        for d in range(k - 1 - r):
            vals[d] = jnp.where(won, vals[d + 1], vals[d])
            ids[d] = jnp.where(won, ids[d + 1], ids[d])
        tick()
    return jnp.concatenate(top_v, axis=0), jnp.concatenate(top_i, axis=0)


def _top_pairs(s1, s2, k, tick):
    lanes = s1.shape[1]
    sub = lax.broadcasted_iota(I32, (8, lanes), 0)
    lists = [jnp.where(sub <= k // (d + 1) - 1, s1[0:8, :] + s2[d:d + 1, :], -jnp.inf) for d in range(k)]
    tail = s1[8:16, :] + s2[0:1, :]
    tail_pos = (sub + 8) * k
    taken = jnp.zeros((8, lanes), I32)
    vs, aa, bb = [], [], []
    for r in range(k):
        best = jnp.maximum(jnp.max(lists[0], axis=0, keepdims=True), jnp.max(tail, axis=0, keepdims=True))
        head_pos = sub * k + taken
        pos = jnp.minimum(jnp.min(jnp.where(lists[0] == best, head_pos, k * k), axis=0, keepdims=True),
                          jnp.min(jnp.where(tail == best, tail_pos, k * k), axis=0, keepdims=True))
        vs.append(best)
        aa.append(pos // k)
        bb.append(pos % k)
        won = head_pos == pos
        tail = jnp.where(tail_pos == pos, -jnp.inf, tail)
        for d in range(k - 1 - r):
            lists[d] = jnp.where(won, lists[d + 1], lists[d])
        taken = taken + won.astype(I32)
        tick()
    return jnp.concatenate(vs, axis=0), jnp.concatenate(aa, axis=0), jnp.concatenate(bb, axis=0)


def _pick_rows(table, sel, k):
    out = jnp.zeros(sel.shape, table.dtype)
    for a in range(k):
        out = jnp.where(sel == a, table[a:a + 1, :], out)
    return out


def _route_head(qt, keys_ref, tick):
    k = PEER_TOPK
    assert k == 16 and PEER_KEYS == 8 * k
    tops = []
    for half in range(2):
        sc = jnp.dot(keys_ref[half], qt[half * PEER_HALF:(half + 1) * PEER_HALF, :], preferred_element_type=F32)
        tops.append(_topk_rows(sc, k, tick))
    (s1, i1), (s2, i2) = tops
    sc, a_sel, b_sel = _top_pairs(s1, s2, k, tick)
    e = jnp.exp(sc - sc[0:1, :])
    g = e / jnp.sum(e, axis=0, keepdims=True)
    return _pick_rows(i1, a_sel, k), _pick_rows(i2, b_sel, k), g


def _peer_route_u_kernel(hn_ref, hc_ref, wq_ref, keys_ref, u_ref, i1_ref, i2_ref, g_ref, act_ref,
                         i1t_ref, i2t_ref, gt_ref, i1c_ref, i2c_ref, *, blocks):
    o = pl.program_id(0)
    c = pl.program_id(1)
    k = PEER_TOPK

    @pl.when((o == 0) & (c == 0))
    def _():
        i1c_ref[...] = jnp.zeros(i1c_ref.shape, I32)
        i2c_ref[...] = jnp.zeros(i2c_ref.shape, I32)

    @pl.when(c == 0)
    def _():
        act_ref[...] = jnp.zeros(act_ref.shape, F32)

    i1c = i1c_ref[...]
    i2c = i2c_ref[...]
    hc = hc_ref[...]
    acc = [act_ref[...]]

    def piece(pc):
        def run():
            a = lax.dot_general(hc, u_ref[pl.ds(pc * 2 * PEER_KEYS, 2 * PEER_KEYS), :], (((1,), (1,)), ((), ())),
                                preferred_element_type=F32)
            for sub in range(2):
                got = jnp.take_along_axis(a[:, sub * PEER_KEYS:(sub + 1) * PEER_KEYS], i2c, axis=1,
                                          mode="promise_in_bounds")
                acc[0] = jnp.where(i1c == c * blocks + 2 * pc + sub, got, acc[0])
        return run

    tm = hn_ref.shape[0]
    groups = tm // ROUTE_LANES
    hps = keys_ref.shape[0] // 2
    hq = wq_ref.shape[0] // hps
    ticks = hps * groups * (2 * (len(_sorting_network(k)) // 16 + k) + k)
    pieces = _Interleaver([piece(pc) for pc in range(blocks // 2)], every=ticks // (blocks // 2 + ROUTE_SLACK))
    qt = lax.dot_general(wq_ref[...], hn_ref[...], (((1,), (1,)), ((), ())), preferred_element_type=F32)
    qt = qt.astype(BF16)
    for hd in range(hps):
        rows = pl.ds(pl.multiple_of((c * hps + hd) * k, k), k)
        for gi in range(groups):
            cols = slice(gi * ROUTE_LANES, (gi + 1) * ROUTE_LANES)
            i1, i2, g = _route_head(qt[hd * hq:(hd + 1) * hq, cols], keys_ref.at[pl.ds(2 * hd, 2)], pieces.tick)
            i1t_ref[rows, cols] = i1
            i2t_ref[rows, cols] = i2
            gt_ref[rows, cols] = g
    pieces.drain()
    act_ref[...] = acc[0]

    @pl.when(c == pl.num_programs(1) - 1)
    def _():
        i1n = i1t_ref[...].T
        i2n = i2t_ref[...].T
        i1_ref[...] = i1n
        i2_ref[...] = i2n
        g_ref[...] = gt_ref[...].T
        i1c_ref[...] = i1n
        i2c_ref[...] = i2n


def _peer_route_u(h, w_q, sub_keys, u, tm=512, hps=2):
    t, d = h.shape
    nt = t // tm
    ne = u.shape[0]
    nchunk = PEER_HEADS // hps
    ec = ne // nchunk
    blocks = ec // PEER_KEYS
    hq = hps * (w_q.shape[1] // PEER_HEADS)
    wq_t = w_q.T.astype(BF16)
    keys = sub_keys.reshape(PEER_HEADS * 2, PEER_KEYS, PEER_HALF).astype(BF16)
    slots = PEER_HEADS * PEER_TOPK
    nxt = lambda o, c: (jnp.minimum(o, nt - 1), 0)
    cur = lambda o, c: (jnp.maximum(o - 1, 0), 0)
    kern = functools.partial(_peer_route_u_kernel, blocks=blocks)
    return pl.pallas_call(
        kern, grid=(nt + 1, nchunk),
        in_specs=[pl.BlockSpec((tm, d), nxt), pl.BlockSpec((tm, d), cur),
                  pl.BlockSpec((hq, d), lambda o, c: (c, 0)),
                  pl.BlockSpec((2 * hps, PEER_KEYS, PEER_HALF), lambda o, c: (c, 0, 0)),
                  pl.BlockSpec((ec, d), lambda o, c: (c, 0))],
        out_specs=[pl.BlockSpec((tm, slots), nxt)] * 3 + [pl.BlockSpec((tm, slots), cur)],
        out_shape=[jax.ShapeDtypeStruct((t, slots), I32), jax.ShapeDtypeStruct((t, slots), I32),
                   jax.ShapeDtypeStruct((t, slots), F32), jax.ShapeDtypeStruct((t, slots), F32)],
        scratch_shapes=[pltpu.VMEM((slots, tm), I32), pltpu.VMEM((slots, tm), I32), pltpu.VMEM((slots, tm), F32),
                        pltpu.VMEM((tm, slots), I32), pltpu.VMEM((tm, slots), I32)],
        compiler_params=_params(("arbitrary", "arbitrary")), name="peer_route_u",
    )(h, h, wq_t, keys, u)


def _peer_v_kernel(act_ref, g_ref, i1_ref, i2_ref, v_ref, x_ref, o_ref, w_ref, wg_ref, *, tm, blocks):
    c = pl.program_id(1)
    nk = PEER_KEYS

    @pl.when(c == 0)
    def _():
        w_ref[...] = g_ref[...] * _gelu(act_ref[...])
        o_ref[...] = x_ref[...]
        sub = lax.broadcasted_iota(I32, (nk, w_ref.shape[1]), 0)

        def per_token(t, carry):
            wrow = w_ref[pl.ds(t, 1), :]
            lhs = jnp.where(i1_ref[pl.ds(t, 1), :] == sub, wrow, 0.0).astype(BF16)
            rhs = jnp.where(i2_ref[pl.ds(t, 1), :] == sub, 1.0, 0.0).astype(BF16)
            grid = lax.dot_general(lhs, rhs, (((1,), (1,)), ((), ())), preferred_element_type=F32)
            wg_ref[pl.ds(pl.multiple_of(t * WG_PITCH, 8), nk), :] = grid
            return carry

        lax.fori_loop(0, tm, per_token, 0, unroll=64)

    acc = jnp.zeros(o_ref.shape, F32)
    for bk in range(0, blocks, 2):
        i1 = c * blocks + bk
        lhs = jnp.concatenate([wg_ref[pl.ds(i1, tm, stride=WG_PITCH), :],
                               wg_ref[pl.ds(i1 + 1, tm, stride=WG_PITCH), :]], axis=1).astype(BF16)
        rhs = v_ref[bk:bk + 2].reshape(2 * nk, v_ref.shape[2])
        acc += jnp.dot(lhs, rhs, preferred_element_type=F32)
    o_ref[...] += acc


def _peer_v(act, g, i1, i2, v, x, tm=256, blocks=32):
    t, d = x.shape
    slots = g.shape[1]
    nk = PEER_KEYS
    v3 = v.reshape(nk, nk, d)
    kern = functools.partial(_peer_v_kernel, tm=tm, blocks=blocks)
    row = lambda i, c: (i, 0)
    return pl.pallas_call(
        kern, grid=(t // tm, nk // blocks),
        in_specs=[pl.BlockSpec((tm, slots), row),
                  pl.BlockSpec((tm, slots), row), pl.BlockSpec((tm, slots), row), pl.BlockSpec((tm, slots), row),
                  pl.BlockSpec((blocks, nk, d), lambda i, c: (c, 0, 0)), pl.BlockSpec((tm, d), row)],
        out_specs=pl.BlockSpec((tm, d), row),
        out_shape=jax.ShapeDtypeStruct((t, d), F32),
        scratch_shapes=[pltpu.VMEM((tm, slots), F32), pltpu.VMEM((tm * WG_PITCH, nk), F32)],
        compiler_params=_params(("parallel", "arbitrary")), name="peer_v",
    )(act, g, i1, i2, v3, x)


def _pad_cols(w, n):
    return jnp.pad(w, ((0, 0), (0, n - w.shape[1])))


def _in_proj_kernel(x_ref, g_ref, w_ref, *rest, bounds):
    outs, xn_ref = rest[:-1], rest[-1]
    j = pl.program_id(1)

    @pl.when(j == 0)
    def _():
        x = x_ref[...]
        ms = jnp.mean(x * x, axis=-1, keepdims=True)
        xn_ref[...] = (x * lax.rsqrt(ms + EPS) * g_ref[...]).astype(xn_ref.dtype)

    for (lo, hi), o_ref in zip(bounds, outs):
        @pl.when((j >= lo) & (j < hi))
        def _(o_ref=o_ref):
            o_ref[...] = jnp.dot(xn_ref[...], w_ref[...], preferred_element_type=F32).astype(o_ref.dtype)


PROJ_TILE = 512
IN_GROUPS = ((1024, BF16), (768, BF16), (48, F32), (1024, BF16), (1024, BF16), (1024, BF16), (1536, BF16),
             (16, F32), (3072, BF16))


def _in_proj(x, g, w_in, tm=1024):
    t, d = x.shape
    tn = PROJ_TILE
    ws, bounds, shapes, off, tile = [], [], [], 0, 0
    for n, dtype in IN_GROUPS:
        width = -(-n // tn) * tn
        ws.append(_pad_cols(w_in[:, off:off + n], width))
        bounds.append((tile, tile + width // tn))
        shapes.append(jax.ShapeDtypeStruct((t, width), dtype))
        off += n
        tile += width // tn
    w_all = jnp.concatenate(ws, axis=1).astype(BF16)
    out_specs = [pl.BlockSpec((tm, tn), functools.partial(lambda i, j, lo, hi: (i, jnp.clip(j - lo, 0, hi - lo - 1)),
                                                          lo=lo, hi=hi)) for lo, hi in bounds]
    kern = functools.partial(_in_proj_kernel, bounds=tuple(bounds))
    return pl.pallas_call(
        kern, grid=(t // tm, tile),
        in_specs=[pl.BlockSpec((tm, d), lambda i, j: (i, 0)), pl.BlockSpec((1, d), lambda i, j: (0, 0)),
                  pl.BlockSpec((d, tn), lambda i, j: (0, j))],
        out_specs=out_specs, out_shape=shapes,
        scratch_shapes=[pltpu.VMEM((tm, d), BF16)],
        compiler_params=_params(("parallel", "arbitrary")), name="in_proj",
    )(x, g.reshape(1, d).astype(F32), w_all)


def _mixer(x, positions, b, s, p):
    a_q, a_kv, a_g, l_x, l_g, s_z, s_xbc, s_dt, m_g = _in_proj(x, p["mix_norm_g"], p["w_in"])

    q, kc, vc, ks, vs, kw, vw = _nsa_prep(a_q, a_kv, positions, p["q_norm_g"], p["k_norm_g"], b, s)
    k_cmp, v_cmp = _compress(kc, vc, p["cmp_pe_k"], p["cmp_pe_v"], p["cmp_k_w1"], p["cmp_k_w2"],
                             p["cmp_v_w1"], p["cmp_v_w2"], p["k_norm_g"])
    o_c, bias = _cmp_select(q, k_cmp, v_cmp, s)
    o_s = _sel_attn(q, bias, ks, vs, s)
    o_w = _win_attn(q, kw, vw, s)
    y_b = _rglru(l_x, l_g, positions, p["lru_conv_w"], p["lru_conv_b"], p["lru_w_r"], p["lru_b_r"],
                 p["lru_w_i"], p["lru_b_i"], p["lru_lambda"], b, s)
    y_c = _ssd(s_z, s_xbc, s_dt, p["ssm_conv_w"], p["ssm_conv_b"], p["ssm_dt_bias"], p["ssm_a_log"],
               p["ssm_d"], p["ssm_norm_g"], b, s)
    return _merge(o_c, o_s, o_w, a_g, y_b, y_c, m_g, x, p["w_branch"], p["w_out"])


def _peer(x, p):
    h = _rmsnorm(x, p["ffn_norm_g"])
    i1, i2, g, act = _peer_route_u(h, p["peer_w_q"], p["peer_sub_keys"], p["peer_u"].astype(BF16))
    return _peer_v(act, g, i1, i2, p["peer_v"].astype(BF16), x)


_LAYER_PARAMS = ("mix_norm_g", "w_in", "q_norm_g", "k_norm_g", "cmp_pe_k", "cmp_pe_v", "cmp_k_w1", "cmp_k_w2",
                 "cmp_v_w1", "cmp_v_w2", "lru_conv_w", "lru_conv_b", "lru_w_r", "lru_b_r", "lru_w_i", "lru_b_i",
                 "lru_lambda", "ssm_conv_w", "ssm_conv_b", "ssm_dt_bias", "ssm_a_log", "ssm_d", "ssm_norm_g",
                 "w_branch", "w_out", "ffn_norm_g", "peer_w_q", "peer_sub_keys", "peer_u", "peer_v")


def kernel(x, positions, mix_norm_g, w_in, q_norm_g, k_norm_g, cmp_pe_k, cmp_pe_v, cmp_k_w1, cmp_k_w2, cmp_v_w1, cmp_v_w2, lru_conv_w, lru_conv_b, lru_w_r, lru_b_r, lru_w_i, lru_b_i, lru_lambda, ssm_conv_w, ssm_conv_b, ssm_dt_bias, ssm_a_log, ssm_d, ssm_norm_g, w_branch, w_out, ffn_norm_g, peer_w_q, peer_sub_keys, peer_u, peer_v):
    stacked = dict(zip(_LAYER_PARAMS, (mix_norm_g, w_in, q_norm_g, k_norm_g, cmp_pe_k, cmp_pe_v, cmp_k_w1,
                                       cmp_k_w2, cmp_v_w1, cmp_v_w2, lru_conv_w, lru_conv_b, lru_w_r, lru_b_r,
                                       lru_w_i, lru_b_i, lru_lambda, ssm_conv_w, ssm_conv_b, ssm_dt_bias,
                                       ssm_a_log, ssm_d, ssm_norm_g, w_branch, w_out, ffn_norm_g, peer_w_q,
                                       peer_sub_keys, peer_u, peer_v)))
    b, s, d = x.shape
    xf = x.reshape(b * s, d).astype(F32)
    for layer in range(mix_norm_g.shape[0]):
        p = {name: arr[layer] for name, arr in stacked.items()}
        xf = _mixer(xf, positions, b, s, p)
        xf = _peer(xf, p)
    return xf.reshape(b, s, d).astype(x.dtype)
```

```python
import functools
import math

import jax
import jax.numpy as jnp
from jax import lax
from jax.experimental import pallas as pl
from jax.experimental.pallas import tpu as pltpu

F32 = jnp.float32
BF16 = jnp.bfloat16
I32 = jnp.int32

D_MODEL = 1024
HEAD_DIM = 64
N_HEADS = 16
N_KV_GROUPS = 2
HEADS_PER_GROUP = 8
CMP_BLOCK = 32
CMP_STRIDE = 16
CMP_HIDDEN = 256
SEL_BLOCK = 64
N_SELECT = 16
WINDOW = 512
SEL_FORCE = 100.0
ROPE_THETA = 10000.0
SCALE = HEAD_DIM ** -0.5
LOG2E = math.log2(math.e)
LRU_C = 8.0
CONV_WIDTH = 4
SSM_HEADS = 16
SSM_HEAD_DIM = 64
SSM_GROUPS = 2
SSM_STATE = 128
SSM_CHUNK = 128
SSM_INNER = 1024
PEER_HEADS = 8
PEER_KEYS = 128
PEER_HALF = 128
PEER_TOPK = 16
EPS = 1e-6
NEG = -1e30
LANES = 128

VMEM_LIMIT = 56 * 1024 * 1024
SEL_LOOKAHEAD = 5
WIN_SUBTILE = 256
WIN_LOOKAHEAD = 3
SEL_SUBTILE = 256
SEL_ACC_ROWS = HEAD_DIM + 16
WG_PITCH = PEER_KEYS + 8
ROUTE_SLACK = 3
ROUTE_LANES = 128


def _params(sem):
    return pltpu.CompilerParams(dimension_semantics=sem, vmem_limit_bytes=VMEM_LIMIT)


def _gelu(x):
    return 0.5 * x * (1.0 + jnp.tanh(math.sqrt(2.0 / math.pi) * (x + 0.044715 * x * x * x)))


def _sigmoid(x):
    return 0.5 * jnp.tanh(0.5 * x) + 0.5


def _softplus(x):
    return jnp.maximum(x, 0.0) + jnp.log(1.0 + jnp.exp(-jnp.abs(x)))


def _rmsnorm_kernel(x_ref, g_ref, o_ref):
    x = x_ref[...].astype(F32)
    ms = jnp.mean(x * x, axis=-1, keepdims=True)
    o_ref[...] = (x * lax.rsqrt(ms + EPS) * g_ref[...]).astype(o_ref.dtype)


def _rmsnorm(x, g, tm=512):
    t, d = x.shape
    return pl.pallas_call(
        _rmsnorm_kernel, grid=(t // tm,),
        in_specs=[pl.BlockSpec((tm, d), lambda i: (i, 0)), pl.BlockSpec((1, d), lambda i: (0, 0))],
        out_specs=pl.BlockSpec((tm, d), lambda i: (i, 0)),
        out_shape=jax.ShapeDtypeStruct((t, d), BF16),
        compiler_params=_params(("parallel",)), name="rmsnorm")(x, g.reshape(1, d).astype(F32))


def _norm_rope(x, g, cos, sin_signed, bd):
    ms = jnp.dot((x * x).astype(BF16), bd, preferred_element_type=F32)
    y = x * lax.rsqrt(ms + EPS) * g
    lane = lax.broadcasted_iota(I32, y.shape, 1)
    first_half = (lane % HEAD_DIM) < (HEAD_DIM // 2)
    partner = jnp.where(first_half, pltpu.roll(y, LANES - HEAD_DIM // 2, 1), pltpu.roll(y, HEAD_DIM // 2, 1))
    return y * cos + partner * sin_signed


def _nsa_prep_kernel(aq_ref, akv_ref, pos_ref, inv_ref, sgn_ref, gq_ref, gk_ref, bd_ref,
                     q_ref, kc_ref, vc_ref, ks_ref, vs_ref, kw_ref, vw_ref):
    ang = pos_ref[...].astype(F32) * inv_ref[...]
    cos = jnp.cos(ang)
    sin_signed = jnp.sin(ang) * sgn_ref[...]
    bd = bd_ref[...]
    for c in range(D_MODEL // LANES):
        x = aq_ref[:, c * LANES:(c + 1) * LANES].astype(F32)
        y = (_norm_rope(x, gq_ref[...], cos, sin_signed, bd) * (SCALE * LOG2E)).astype(q_ref.dtype)
        q_ref[0, 2 * c] = y[:, :HEAD_DIM]
        q_ref[0, 2 * c + 1] = y[:, HEAD_DIM:]
    outs = (kc_ref, vc_ref, ks_ref, vs_ref, kw_ref, vw_ref)
    for c, o_ref in enumerate(outs):
        x = akv_ref[:, c * LANES:(c + 1) * LANES]
        if c % 2 == 0:
            y = _norm_rope(x.astype(F32), gk_ref[...], cos, sin_signed, bd).astype(o_ref.dtype)
        else:
            y = x.astype(o_ref.dtype)
        o_ref[0, 0] = y[:, :HEAD_DIM]
        o_ref[0, 1] = y[:, HEAD_DIM:]


def _nsa_prep(a_q, a_kv, positions, q_norm_g, k_norm_g, b, s, tm=512):
    half = HEAD_DIM // 2
    lane = jnp.arange(LANES)
    inv = (ROPE_THETA ** (-((lane % half).astype(F32)) / half)).reshape(1, LANES)
    sgn = jnp.where((lane % HEAD_DIM) < half, -1.0, 1.0).astype(F32).reshape(1, LANES)
    bd = jnp.where((lane[:, None] // HEAD_DIM) == (lane[None, :] // HEAD_DIM), 1.0 / HEAD_DIM, 0.0).astype(BF16)
    gq = jnp.tile(q_norm_g.astype(F32), 2).reshape(1, LANES)
    gk = jnp.tile(k_norm_g.astype(F32), 2).reshape(1, LANES)
    nt = s // tm
    row = lambda i: (i, 0)
    const = lambda i: (0, 0)
    kv_shape = jax.ShapeDtypeStruct((b, N_KV_GROUPS, s, HEAD_DIM), BF16)
    kv_spec = pl.BlockSpec((1, N_KV_GROUPS, tm, HEAD_DIM), lambda i: (i // nt, 0, i % nt, 0))
    return pl.pallas_call(
        _nsa_prep_kernel, grid=(b * nt,),
        in_specs=[pl.BlockSpec((tm, D_MODEL), row), pl.BlockSpec((tm, 6 * LANES), row),
                  pl.BlockSpec((tm, 1), row), pl.BlockSpec((1, LANES), const), pl.BlockSpec((1, LANES), const),
                  pl.BlockSpec((1, LANES), const), pl.BlockSpec((1, LANES), const),
                  pl.BlockSpec((LANES, LANES), const)],
        out_specs=[pl.BlockSpec((1, N_HEADS, tm, HEAD_DIM), lambda i: (i // nt, 0, i % nt, 0))] + [kv_spec] * 6,
        out_shape=[jax.ShapeDtypeStruct((b, N_HEADS, s, HEAD_DIM), BF16)] + [kv_shape] * 6,
        compiler_params=_params(("parallel",)), name="nsa_prep",
    )(a_q, a_kv, positions.reshape(b * s, 1).astype(I32), inv, sgn, gq, gk, bd)


def _compress_kernel(uk_ref, uv_ref, pek_ref, pev_ref, kw1_ref, kw2_ref, vw1_ref, vw2_ref, gk_ref,
                     kc_ref, vc_ref):
    half = CMP_STRIDE * HEAD_DIM

    def mlp(u, pe, w1_ref, w2_ref):
        n = u.shape[0]
        ha = jnp.dot(u, w1_ref[:half, :], preferred_element_type=F32)
        hb = jnp.dot(u, w1_ref[half:, :], preferred_element_type=F32)
        bias = jnp.dot(pe, w1_ref[...], preferred_element_type=F32)[0:1, :]
        pre = ha + pltpu.roll(hb, n - 1, 0) + bias
        return jnp.dot(_gelu(pre).astype(BF16), w2_ref[...], preferred_element_type=F32)

    k = mlp(uk_ref[0, 0], pek_ref[...], kw1_ref, kw2_ref)
    ms = jnp.mean(k * k, axis=-1, keepdims=True)
    kc_ref[0, 0] = (k * lax.rsqrt(ms + EPS) * gk_ref[...]).astype(kc_ref.dtype)
    vc_ref[0, 0] = mlp(uv_ref[0, 0], pev_ref[...], vw1_ref, vw2_ref).astype(vc_ref.dtype)


def _compress(kc, vc, pe_k, pe_v, kw1, kw2, vw1, vw2, k_norm_g):
    b, g, s, dh = kc.shape
    ng = s // CMP_STRIDE
    wide = CMP_BLOCK * dh
    uk = kc.reshape(b, g, ng, CMP_STRIDE * dh)
    uv = vc.reshape(b, g, ng, CMP_STRIDE * dh)
    pek = jnp.zeros((8, wide), BF16).at[0].set(pe_k.reshape(wide).astype(BF16))
    pev = jnp.zeros((8, wide), BF16).at[0].set(pe_v.reshape(wide).astype(BF16))
    u_spec = pl.BlockSpec((1, 1, ng, CMP_STRIDE * dh), lambda i, j: (i, j, 0, 0))
    c2 = lambda i, j: (0, 0)
    o_spec = pl.BlockSpec((1, 1, ng, dh), lambda i, j: (i, j, 0, 0))
    o_shape = jax.ShapeDtypeStruct((b, g, ng, dh), BF16)
    return pl.pallas_call(
        _compress_kernel, grid=(b, g),
        in_specs=[u_spec, u_spec, pl.BlockSpec((8, wide), c2), pl.BlockSpec((8, wide), c2),
                  pl.BlockSpec((wide, CMP_HIDDEN), c2), pl.BlockSpec((CMP_HIDDEN, dh), c2),
                  pl.BlockSpec((wide, CMP_HIDDEN), c2), pl.BlockSpec((CMP_HIDDEN, dh), c2),
                  pl.BlockSpec((1, dh), c2)],
        out_specs=[o_spec, o_spec], out_shape=[o_shape, o_shape],
        compiler_params=_params(("parallel", "parallel")), name="nsa_compress",
    )(uk, uv, pek, pev, kw1.astype(BF16), kw2.astype(BF16), vw1.astype(BF16), vw2.astype(BF16),
      k_norm_g.reshape(1, dh).astype(F32))


def _store_head_pair(o_ref, h, o_t, held):
    if h % 2 == 0:
        return o_t
    pair = jnp.concatenate([held, o_t], axis=0).T
    o_ref[0, :, (h - 1) * HEAD_DIM:(h + 1) * HEAD_DIM] = pair.astype(o_ref.dtype)
    return None


def _cmp_select_kernel(q_ref, kc_ref, vc_ref, ovt_ref, o_ref, bias_ref, *, tq, n_cmp, n_pick):
    i = pl.program_id(2)
    chunk = min(LANES, kc_ref.shape[2])
    visible = ((i + 1) * tq - CMP_BLOCK) // CMP_STRIDE + 1
    nchunks = jnp.clip((visible + chunk - 1) // chunk, 1, kc_ref.shape[2] // chunk)
    for n in range(1, kc_ref.shape[2] // chunk + 1):
        pl.when(nchunks == n)(functools.partial(
            _cmp_select_keys, q_ref, kc_ref, vc_ref, ovt_ref, o_ref, bias_ref, nc=n * chunk, i=i, tq=tq,
            n_cmp=n_cmp, n_pick=n_pick))


def _cmp_select_keys(q_ref, kc_ref, vc_ref, ovt_ref, o_ref, bias_ref, *, nc, i, tq, n_cmp, n_pick):
    hg = q_ref.shape[1]
    nsel = ovt_ref.shape[0]
    kc = kc_ref[0, 0, :nc, :]
    vct = vc_ref[0, 0, :nc, :].astype(F32).T.astype(BF16)
    ovt = ovt_ref[:, :nc]
    c = lax.broadcasted_iota(I32, (nc, tq), 0)
    t = i * tq + lax.broadcasted_iota(I32, (nc, tq), 1)
    mask = ((CMP_STRIDE * c + CMP_BLOCK - 1) <= t) & (c < n_cmp)
    score = lambda h: lax.dot_general(kc, q_ref[0, h], (((1,), (1,)), ((), ())), preferred_element_type=F32)
    ahead = [score(h) for h in range(WIN_LOOKAHEAD)]
    psum = jnp.zeros((nc, tq), F32)
    held = None
    for h in range(hg):
        st = ahead.pop(0)
        if h + WIN_LOOKAHEAD < hg:
            ahead.append(score(h + WIN_LOOKAHEAD))
        st = jnp.where(mask, st, NEG)
        m = jnp.max(st, axis=0, keepdims=True)
        p = jnp.where(mask, jnp.exp2(st - m), 0.0)
        l = jnp.sum(p, axis=0, keepdims=True)
        p = p * jnp.where(l > 0.0, 1.0 / l, 0.0)
        o = jnp.dot(vct, p.astype(BF16), preferred_element_type=F32)
        held = _store_head_pair(o_ref, h, o, held)
        psum = psum + p

    hi = psum.astype(BF16)
    lo = (psum - hi.astype(F32)).astype(BF16)
    imp = jnp.dot(ovt, hi, preferred_element_type=F32) + jnp.dot(ovt, lo, preferred_element_type=F32)
    j = lax.broadcasted_iota(I32, (nsel, tq), 0)
    tt = i * tq + lax.broadcasted_iota(I32, (nsel, tq), 1)
    cur = tt // SEL_BLOCK
    forced = (j == 0) | (j == cur) | (j == cur - 1)
    sc = jnp.where(forced, SEL_FORCE, jnp.where(j * SEL_BLOCK <= tt, imp, -1.0))
    picked = jnp.zeros((nsel, tq), jnp.bool_)
    for _ in range(n_pick):
        mx = jnp.max(sc, axis=0, keepdims=True)
        idx = jnp.min(jnp.where(sc == mx, j, nsel), axis=0, keepdims=True)
        hit = j == idx
        picked = picked | hit
        sc = jnp.where(hit, -jnp.inf, sc)
    bias_ref[0, 0] = jnp.where(picked & (j <= cur), 0.0, NEG).T.astype(bias_ref.dtype)


def _cmp_select(q, k_cmp, v_cmp, s, tq=512):
    b, nh, _, dh = q.shape
    g, hg = N_KV_GROUPS, HEADS_PER_GROUP
    nc = k_cmp.shape[2]
    n_cmp = (s - CMP_BLOCK) // CMP_STRIDE + 1
    nsel = s // SEL_BLOCK
    n_pick = min(N_SELECT, nsel)
    cs = CMP_STRIDE * jnp.arange(nc)
    ss = SEL_BLOCK * jnp.arange(nsel)
    ovt = jnp.clip(jnp.minimum(cs[None, :] + CMP_BLOCK, ss[:, None] + SEL_BLOCK)
                   - jnp.maximum(cs[None, :], ss[:, None]), 0).astype(F32) / CMP_BLOCK
    ovt = jnp.where(jnp.arange(nc)[None, :] < n_cmp, ovt, 0.0).astype(BF16)
    kern = functools.partial(_cmp_select_kernel, tq=tq, n_cmp=n_cmp, n_pick=n_pick)
    return pl.pallas_call(
        kern, grid=(b, g, s // tq),
        in_specs=[pl.BlockSpec((1, hg, tq, dh), lambda bi, gi, i: (bi, gi, i, 0)),
                  pl.BlockSpec((1, 1, nc, dh), lambda bi, gi, i: (bi, gi, 0, 0)),
                  pl.BlockSpec((1, 1, nc, dh), lambda bi, gi, i: (bi, gi, 0, 0)),
                  pl.BlockSpec((nsel, nc), lambda bi, gi, i: (0, 0))],
        out_specs=[pl.BlockSpec((1, tq, hg * dh), lambda bi, gi, i: (bi, i, gi)),
                   pl.BlockSpec((1, 1, tq, nsel), lambda bi, gi, i: (bi, gi, i, 0))],
        out_shape=[jax.ShapeDtypeStruct((b, s, nh * dh), BF16),
                   jax.ShapeDtypeStruct((b, g, s, nsel), BF16)],
        compiler_params=_params(("parallel", "parallel", "parallel")), name="nsa_cmp_select",
    )(q, k_cmp, v_cmp, ovt)


def _sel_attn_kernel(it_ref, jt_ref, q_ref, bias_ref, k_ref, v_ref, o_ref, qa_ref, m_ref, acc_ref, *, tq, tk):
    pid = pl.program_id(2)
    i = it_ref[pid]
    j = jt_ref[pid]
    hg = q_ref.shape[1]
    nsel = bias_ref.shape[3]
    last_j = ((i + 1) * tq - 1) // tk

    @pl.when(j == 0)
    def _():
        for h in range(hg):
            qa_ref[h, :, :nsel] = bias_ref[0, 0]
            qa_ref[h, :, nsel:] = q_ref[0, h]
        m_ref[...] = jnp.full(m_ref.shape, -jnp.inf, F32)
        acc_ref[...] = jnp.zeros(acc_ref.shape, F32)

    def accumulate(diagonal):
        key = j * tk + lax.broadcasted_iota(I32, (tk, nsel), 0)
        blk = lax.broadcasted_iota(I32, (tk, nsel), 1)
        onehot = jnp.where(key // SEL_BLOCK == blk, 1.0, 0.0).astype(BF16)
        ka = jnp.concatenate([onehot, k_ref[0, 0]], axis=1)
        va = jnp.concatenate([v_ref[0, 0].astype(F32), jnp.ones((tk, LANES - HEAD_DIM), F32)], axis=1)
        vat = va.T[:SEL_ACC_ROWS, :].astype(BF16)
        qw = min(SEL_SUBTILE, tq)
        items = [(h, b) for h in range(hg) for b in range(tq // qw)]
        rows = lambda b: (b + 1) * qw if diagonal else tk

        def score(item):
            h, b = item
            return lax.dot_general(ka[:rows(b), :], qa_ref[h, b * qw:(b + 1) * qw, :], (((1,), (1,)), ((), ())),
                                   preferred_element_type=F32)

        ahead = [score(item) for item in items[:SEL_LOOKAHEAD]]
        for n, (h, b) in enumerate(items):
            st = ahead.pop(0)
            if n + SEL_LOOKAHEAD < len(items):
                ahead.append(score(items[n + SEL_LOOKAHEAD]))
            if diagonal:
                kp = lax.broadcasted_iota(I32, st.shape, 0)
                t = b * qw + lax.broadcasted_iota(I32, st.shape, 1)
                st = jnp.where(kp <= t, st, NEG)
            lanes = slice(b * qw, (b + 1) * qw)
            m_old = m_ref[h, :, lanes]
            m_new = jnp.maximum(m_old, jnp.max(st, axis=0, keepdims=True))
            alpha = jnp.exp2(m_old - m_new)
            p = jnp.exp2(st - m_new)
            acc_ref[h, :, lanes] = alpha * acc_ref[h, :, lanes] + jnp.dot(
                vat[:, :rows(b)], p.astype(BF16), preferred_element_type=F32)
            m_ref[h, :, lanes] = m_new

    @pl.when(j < last_j)
    def _():
        accumulate(False)

    @pl.when(j == last_j)
    def _():
        accumulate(True)
        held = None
        for h in range(hg):
            acc = acc_ref[h]
            o = acc[:HEAD_DIM, :] / acc[HEAD_DIM:HEAD_DIM + 1, :]
            held = _store_head_pair(o_ref, h, o, held)


def _sel_attn(q, bias, k_s, v_s, s, tq=1024):
    tk = tq
    b, nh, _, dh = q.shape
    g, hg = N_KV_GROUPS, HEADS_PER_GROUP
    nsel = bias.shape[3]
    pairs = [(i, j) for i in range(s // tq) for j in range(((i + 1) * tq - 1) // tk + 1)]
    it = jnp.asarray([pr[0] for pr in pairs], I32)
    jt = jnp.asarray([pr[1] for pr in pairs], I32)
    q_map = lambda bi, gi, pid, it_ref, jt_ref: (bi, gi, it_ref[pid], 0)
    kv_map = lambda bi, gi, pid, it_ref, jt_ref: (bi, gi, jt_ref[pid], 0)
    kern = functools.partial(_sel_attn_kernel, tq=tq, tk=tk)
    grid_spec = pltpu.PrefetchScalarGridSpec(
        num_scalar_prefetch=2, grid=(b, g, len(pairs)),
        in_specs=[pl.BlockSpec((1, hg, tq, dh), q_map), pl.BlockSpec((1, 1, tq, nsel), q_map),
                  pl.BlockSpec((1, 1, tk, dh), kv_map), pl.BlockSpec((1, 1, tk, dh), kv_map)],
        out_specs=pl.BlockSpec((1, tq, hg * dh), lambda bi, gi, pid, it_ref, jt_ref: (bi, it_ref[pid], gi)),
        scratch_shapes=[pltpu.VMEM((hg, tq, nsel + dh), BF16), pltpu.VMEM((hg, 1, tq), F32),
                        pltpu.VMEM((hg, SEL_ACC_ROWS, tq), F32)])
    return pl.pallas_call(
        kern, grid_spec=grid_spec, out_shape=jax.ShapeDtypeStruct((b, s, nh * dh), BF16),
        compiler_params=_params(("parallel", "parallel", "arbitrary")), name="nsa_sel_attn",
    )(it, jt, q, bias, k_s, v_s)


def _win_attn_kernel(q_ref, *refs, tq, nwin):
    k_refs, v_refs, o_ref = refs[:nwin], refs[nwin:2 * nwin], refs[2 * nwin]
    i = pl.program_id(2)
    hg = q_ref.shape[1]
    k = jnp.concatenate([r[0, 0] for r in k_refs], axis=0)
    vt = jnp.concatenate([r[0, 0] for r in v_refs], axis=0).astype(F32).T.astype(BF16)
    qw = WIN_SUBTILE
    nk = WINDOW + qw
    r = lax.broadcasted_iota(I32, (nk, qw), 0)
    c = lax.broadcasted_iota(I32, (nk, qw), 1)
    band = (r > c) & (r <= c + WINDOW)
    items = [(h, b) for h in range(hg) for b in range(tq // qw)]

    def score(item):
        h, b = item
        return lax.dot_general(k[b * qw:b * qw + nk, :], q_ref[0, h, b * qw:(b + 1) * qw, :],
                               (((1,), (1,)), ((), ())), preferred_element_type=F32)

    ahead = [score(item) for item in items[:WIN_LOOKAHEAD]]
    held, parts = None, []
    for n, (h, b) in enumerate(items):
        st = ahead.pop(0)
        if n + WIN_LOOKAHEAD < len(items):
            ahead.append(score(items[n + WIN_LOOKAHEAD]))
        mask = band & (i * tq - WINDOW + b * qw + r >= 0)
        st = jnp.where(mask, st, NEG)
        m = jnp.max(st, axis=0, keepdims=True)
        p = jnp.where(mask, jnp.exp2(st - m), 0.0)
        l = jnp.sum(p, axis=0, keepdims=True)
        parts.append(jnp.dot(vt[:, b * qw:b * qw + nk], p.astype(BF16), preferred_element_type=F32) / l)
        if len(parts) == tq // qw:
            held = _store_head_pair(o_ref, h, jnp.concatenate(parts, axis=1), held)
            parts = []


def _win_attn(q, k_w, v_w, s, tq=512):
    b, nh, _, dh = q.shape
    g, hg = N_KV_GROUPS, HEADS_PER_GROUP
    nwin = WINDOW // tq + 1
    pad = ((0, 0), (0, 0), (WINDOW, 0), (0, 0))
    kp, vp = jnp.pad(k_w, pad), jnp.pad(v_w, pad)
    kv_specs = [pl.BlockSpec((1, 1, tq, dh), functools.partial(lambda bi, gi, i, w: (bi, gi, i + w, 0), w=w))
                for w in range(nwin)]
    kern = functools.partial(_win_attn_kernel, tq=tq, nwin=nwin)
    return pl.pallas_call(
        kern, grid=(b, g, s // tq),
        in_specs=[pl.BlockSpec((1, hg, tq, dh), lambda bi, gi, i: (bi, gi, i, 0))] + kv_specs + kv_specs,
        out_specs=pl.BlockSpec((1, tq, hg * dh), lambda bi, gi, i: (bi, i, gi)),
        out_shape=jax.ShapeDtypeStruct((b, s, nh * dh), BF16),
        compiler_params=_params(("parallel", "parallel", "parallel")), name="nsa_win_attn",
    )(q, *([kp] * nwin), *([vp] * nwin))


def _causal_conv(x, win_ref, w_ref, b_ref):
    n = x.shape[0]
    win_ref[8:, :] = x
    out = b_ref[...] + w_ref[CONV_WIDTH - 1:CONV_WIDTH, :] * x
    for k in range(CONV_WIDTH - 1):
        off = 8 - (CONV_WIDTH - 1) + k
        out = out + w_ref[k:k + 1, :] * win_ref[off:off + n, :]
    win_ref[0:8, :] = x[n - 8:, :]
    return out


def _rglru_kernel(x_ref, gate_ref, pos_ref, cw_ref, cb_ref, wr_ref, br_ref, wi_ref, bi_ref, lam_ref,
                  o_ref, tail_ref, h_ref):
    @pl.when(pl.program_id(1) == 0)
    def _():
        tail_ref[0:8, :] = jnp.zeros((8, tail_ref.shape[1]), F32)
        h_ref[...] = jnp.zeros(h_ref.shape, F32)

    n = x_ref.shape[0]
    x = x_ref[...].astype(F32)
    xc = _causal_conv(x, tail_ref, cw_ref, cb_ref)
    xcb = xc.astype(BF16)
    r = _sigmoid(jnp.dot(xcb, wr_ref[...], preferred_element_type=F32) + br_ref[...])
    gi = _sigmoid(jnp.dot(xcb, wi_ref[...], preferred_element_type=F32) + bi_ref[...])
    log_a = -LRU_C * r * _softplus(-lam_ref[...])
    reset = pos_ref[...] == 0
    a = jnp.where(reset, 0.0, jnp.exp(log_a))
    mult = jnp.where(reset, 1.0, jnp.sqrt(jnp.maximum(1.0 - jnp.exp(2.0 * log_a), 0.0)))
    bb = mult * (gi * xc)
    sub = lax.broadcasted_iota(I32, a.shape, 0) % 8
    d = 1
    while d < 8:
        a_sh = pltpu.roll(a, d, 0)
        b_sh = pltpu.roll(bb, d, 0)
        live = sub >= d
        bb = jnp.where(live, a * b_sh + bb, bb)
        a = jnp.where(live, a * a_sh, a)
        d *= 2
    carry = h_ref[...]
    hs = []
    for g in range(n // 8):
        h = bb[8 * g:8 * g + 8, :] + a[8 * g:8 * g + 8, :] * carry
        carry = h[7:8, :]
        hs.append(h)
    h_ref[...] = carry
    o_ref[...] = (jnp.concatenate(hs, axis=0) * _gelu(gate_ref[...].astype(F32))).astype(o_ref.dtype)


def _block_diag(w):
    nb, bs, _ = w.shape
    eye = jnp.eye(nb, dtype=w.dtype)
    return (w[:, :, None, :] * eye[:, None, :, None]).reshape(nb * bs, nb * bs)


def _rglru(l_x, l_g, positions, conv_w, conv_b, w_r, b_r, w_i, b_i, lam, b, s, ts=256):
    d = l_x.shape[1]
    nt = s // ts
    row = lambda bi, i: (bi * nt + i, 0)
    c2 = lambda bi, i: (0, 0)
    vec = lambda v: v.reshape(1, d).astype(F32)
    return pl.pallas_call(
        _rglru_kernel, grid=(b, nt),
        in_specs=[pl.BlockSpec((ts, d), row), pl.BlockSpec((ts, d), row), pl.BlockSpec((ts, 1), row),
                  pl.BlockSpec((CONV_WIDTH, d), c2), pl.BlockSpec((1, d), c2),
                  pl.BlockSpec((d, d), c2), pl.BlockSpec((1, d), c2),
                  pl.BlockSpec((d, d), c2), pl.BlockSpec((1, d), c2), pl.BlockSpec((1, d), c2)],
        out_specs=pl.BlockSpec((ts, d), row),
        out_shape=jax.ShapeDtypeStruct((b * s, d), BF16),
        scratch_shapes=[pltpu.VMEM((8 + ts, d), F32), pltpu.VMEM((1, d), F32)],
        compiler_params=_params(("parallel", "arbitrary")), name="rglru",
    )(l_x, l_g, positions.reshape(b * s, 1).astype(I32), conv_w.astype(F32), vec(conv_b),
      _block_diag(w_r).astype(BF16), vec(b_r), _block_diag(w_i).astype(BF16), vec(b_i), vec(lam))


def _ssd_kernel(z_ref, xbc_ref, dt_ref, cw_ref, cb_ref, dtb_ref, alog_ref, dfull_ref, ng_ref,
                o_ref, tail_ref, state_ref, y_ref):
    @pl.when(pl.program_id(1) == 0)
    def _():
        tail_ref[0:8, :] = jnp.zeros((8, tail_ref.shape[1]), F32)
        state_ref[...] = jnp.zeros(state_ref.shape, F32)

    n = xbc_ref.shape[0]
    hg = SSM_HEADS // SSM_GROUPS
    x = xbc_ref[...].astype(F32)
    xc = _causal_conv(x, tail_ref, cw_ref, cb_ref)
    xc = xc * _sigmoid(xc)
    xs = xc[:, :SSM_INNER]
    dt = _softplus(dt_ref[...] + dtb_ref[...])
    adt = dt * (-jnp.exp(alog_ref[...]))
    row = lax.broadcasted_iota(I32, adt.shape, 0)
    acs = adt
    d = 1
    while d < n:
        acs = acs + jnp.where(row >= d, pltpu.roll(acs, d, 0), 0.0)
        d *= 2
    acs_t = acs.T
    li = lax.broadcasted_iota(I32, (n, n), 0)
    si = lax.broadcasted_iota(I32, (n, n), 1)
    tri = li >= si
    for g in range(SSM_GROUPS):
        bm = xc[:, SSM_INNER + g * SSM_STATE:SSM_INNER + (g + 1) * SSM_STATE].astype(BF16)
        cm = xc[:, SSM_INNER + (SSM_GROUPS + g) * SSM_STATE:SSM_INNER + (SSM_GROUPS + g + 1) * SSM_STATE].astype(BF16)
        cb = lax.dot_general(cm, bm, (((1,), (1,)), ((), ())), preferred_element_type=F32)
        bm_t = bm.T
        for hh in range(hg):
            h = g * hg + hh
            acol = jnp.broadcast_to(acs[:, h:h + 1], (n, n))
            arow = acs_t[h:h + 1, :]
            decay = jnp.exp(jnp.where(tri, acol - arow, NEG))
            acol_p = acol[:, :SSM_HEAD_DIM]
            xh = xs[:, h * SSM_HEAD_DIM:(h + 1) * SSM_HEAD_DIM] * dt[:, h:h + 1]
            a_last = acol_p[n - 1:n, :]
            y = jnp.dot((cb * decay).astype(BF16), xh.astype(BF16), preferred_element_type=F32)
            st = state_ref[h]
            y = y + jnp.dot(cm, st.astype(BF16), preferred_element_type=F32) * jnp.exp(acol_p)
            upd = jnp.dot(bm_t, (xh * jnp.exp(a_last - acol_p)).astype(BF16), preferred_element_type=F32)
            state_ref[h] = jnp.exp(a_last) * st + upd
            y_ref[:, h * SSM_HEAD_DIM:(h + 1) * SSM_HEAD_DIM] = y
    z = z_ref[...].astype(F32)
    y = (y_ref[...] + dfull_ref[...] * xs) * (z * _sigmoid(z))
    gw = SSM_INNER // SSM_GROUPS
    for g in range(SSM_GROUPS):
        yg = y[:, g * gw:(g + 1) * gw]
        ms = jnp.mean(yg * yg, axis=-1, keepdims=True)
        o_ref[:, g * gw:(g + 1) * gw] = (yg * lax.rsqrt(ms + EPS) * ng_ref[:, g * gw:(g + 1) * gw]).astype(o_ref.dtype)


def _ssd(s_z, s_xbc, s_dt, conv_w, conv_b, dt_bias, a_log, d_skip, norm_g, b, s):
    n = SSM_CHUNK
    nt = s // n
    c = s_xbc.shape[1]
    row = lambda bi, i: (bi * nt + i, 0)
    c2 = lambda bi, i: (0, 0)
    pad_h = lambda v: jnp.zeros((1, LANES), F32).at[0, :SSM_HEADS].set(v.astype(F32))
    dfull = jnp.repeat(d_skip.astype(F32), SSM_HEAD_DIM).reshape(1, SSM_INNER)
    return pl.pallas_call(
        _ssd_kernel, grid=(b, nt),
        in_specs=[pl.BlockSpec((n, SSM_INNER), row), pl.BlockSpec((n, c), row), pl.BlockSpec((n, LANES), row),
                  pl.BlockSpec((CONV_WIDTH, c), c2), pl.BlockSpec((1, c), c2),
                  pl.BlockSpec((1, LANES), c2), pl.BlockSpec((1, LANES), c2),
                  pl.BlockSpec((1, SSM_INNER), c2), pl.BlockSpec((1, SSM_INNER), c2)],
        out_specs=pl.BlockSpec((n, SSM_INNER), row),
        out_shape=jax.ShapeDtypeStruct((b * s, SSM_INNER), BF16),
        scratch_shapes=[pltpu.VMEM((8 + n, c), F32), pltpu.VMEM((SSM_HEADS, SSM_STATE, SSM_HEAD_DIM), F32),
                        pltpu.VMEM((n, SSM_INNER), F32)],
        compiler_params=_params(("parallel", "arbitrary")), name="ssd",
    )(s_z, s_xbc, s_dt, conv_w.astype(F32), conv_b.reshape(1, c).astype(F32), pad_h(dt_bias), pad_h(a_log),
      dfull, norm_g.reshape(1, SSM_INNER).astype(F32))


def _merge_kernel(oc_ref, os_ref, ow_ref, ag_ref, ex_ref, yb_ref, yc_ref, mg_ref, x_ref, wb_ref, wo_ref, o_ref):
    gates = _sigmoid(ag_ref[...])
    hi = gates.astype(BF16)
    lo = (gates - hi.astype(F32)).astype(BF16)
    spread = lambda k: (jnp.dot(hi, ex_ref[k], preferred_element_type=F32)
                        + jnp.dot(lo, ex_ref[k], preferred_element_type=F32))
    ya = (spread(0) * oc_ref[...].astype(F32) + spread(1) * os_ref[...].astype(F32)
          + spread(2) * ow_ref[...].astype(F32))
    d = D_MODEL
    merged = _sigmoid(mg_ref[:, 0:d].astype(F32)) * jnp.dot(ya.astype(BF16), wb_ref[0], preferred_element_type=F32)
    merged += _sigmoid(mg_ref[:, d:2 * d].astype(F32)) * jnp.dot(yb_ref[...], wb_ref[1], preferred_element_type=F32)
    merged += _sigmoid(mg_ref[:, 2 * d:3 * d].astype(F32)) * jnp.dot(yc_ref[...], wb_ref[2], preferred_element_type=F32)
    o_ref[...] = x_ref[...] + jnp.dot(merged.astype(BF16), wo_ref[...], preferred_element_type=F32)


def _merge(o_c, o_s, o_w, a_g, y_b, y_c, m_g, x, w_branch, w_out, tm=512):
    t, d = x.shape
    row = lambda i: (i, 0)
    lane = jnp.arange(LANES)[:, None]
    col = jnp.arange(d)[None, :]
    expand = jnp.stack([(lane == 3 * (col // HEAD_DIM) + k) for k in range(3)]).astype(BF16)
    return pl.pallas_call(
        _merge_kernel, grid=(t // tm,),
        in_specs=[pl.BlockSpec((tm, d), row), pl.BlockSpec((tm, d), row), pl.BlockSpec((tm, d), row),
                  pl.BlockSpec((tm, LANES), row), pl.BlockSpec((3, LANES, d), lambda i: (0, 0, 0)),
                  pl.BlockSpec((tm, d), row), pl.BlockSpec((tm, d), row), pl.BlockSpec((tm, 3 * d), row),
                  pl.BlockSpec((tm, d), row),
                  pl.BlockSpec((3, d, d), lambda i: (0, 0, 0)), pl.BlockSpec((d, d), lambda i: (0, 0))],
        out_specs=pl.BlockSpec((tm, d), row),
        out_shape=jax.ShapeDtypeStruct((t, d), F32),
        compiler_params=_params(("parallel",)), name="merge",
    )(o_c.reshape(t, d), o_s.reshape(t, d), o_w.reshape(t, d), a_g, expand, y_b, y_c, m_g, x,
      w_branch.astype(BF16), w_out.astype(BF16))


class _Interleaver:
    def __init__(self, pieces, every):
        self.pieces, self.every, self.count = list(pieces), every, 0

    def tick(self):
        self.count += 1
        if self.pieces and self.count % self.every == 0:
            self.pieces.pop(0)()

    def drain(self):
        while self.pieces:
            self.pieces.pop(0)()


def _sorting_network(n):
    pairs = []

    def merge(lo, m, r):
        step = r * 2
        if step < m:
            merge(lo, m, step)
            merge(lo + r, m, step)
            pairs.extend((i, i + r) for i in range(lo + r, lo + m - r, step))
        else:
            pairs.append((lo, lo + r))

    def sort(lo, m):
        if m > 1:
            sort(lo, m // 2)
            sort(lo + m // 2, m // 2)
            merge(lo, m, 1)

    sort(0, n)
    return pairs


def _topk_rows(s, k, tick):
    n, lanes = s.shape
    assert n == 8 * k
    sub = lax.broadcasted_iota(I32, (8, lanes), 0)
    vals = [s[8 * j:8 * j + 8, :] for j in range(k)]
    ids = [sub + 8 * j for j in range(k)]
    for count, (a, b) in enumerate(_sorting_network(k)):
        first = (vals[a] > vals[b]) | ((vals[a] == vals[b]) & (ids[a] < ids[b]))
        vals[a], vals[b] = jnp.where(first, vals[a], vals[b]), jnp.where(first, vals[b], vals[a])
        ids[a], ids[b] = jnp.where(first, ids[a], ids[b]), jnp.where(first, ids[b], ids[a])
        if count % 16 == 15:
            tick()
    top_v, top_i = [], []
    for r in range(k):
        best = jnp.max(vals[0], axis=0, keepdims=True)
        row = jnp.min(jnp.where(vals[0] == best, ids[0], n), axis=0, keepdims=True)
        top_v.append(best)
        top_i.append(row)
        won = ids[0] == row
        for d in range(k - 1 - r):
            vals[d] = jnp.where(won, vals[d + 1], vals[d])
            ids[d] = jnp.where(won, ids[d + 1], ids[d])
        tick()
    return jnp.concatenate(top_v, axis=0), jnp.concatenate(top_i, axis=0)


def _top_pairs(s1, s2, k, tick):
    lanes = s1.shape[1]
    sub = lax.broadcasted_iota(I32, (8, lanes), 0)
    lists = [jnp.where(sub <= k // (d + 1) - 1, s1[0:8, :] + s2[d:d + 1, :], -jnp.inf) for d in range(k)]
    tail = s1[8:16, :] + s2[0:1, :]
    tail_pos = (sub + 8) * k
    taken = jnp.zeros((8, lanes), I32)
    vs, aa, bb = [], [], []
    for r in range(k):
        best = jnp.maximum(jnp.max(lists[0], axis=0, keepdims=True), jnp.max(tail, axis=0, keepdims=True))
        head_pos = sub * k + taken
        pos = jnp.minimum(jnp.min(jnp.where(lists[0] == best, head_pos, k * k), axis=0, keepdims=True),
                          jnp.min(jnp.where(tail == best, tail_pos, k * k), axis=0, keepdims=True))
        vs.append(best)
        aa.append(pos // k)
        bb.append(pos % k)
        won = head_pos == pos
        tail = jnp.where(tail_pos == pos, -jnp.inf, tail)
        for d in range(k - 1 - r):
            lists[d] = jnp.where(won, lists[d + 1], lists[d])
        taken = taken + won.astype(I32)
        tick()
    return jnp.concatenate(vs, axis=0), jnp.concatenate(aa, axis=0), jnp.concatenate(bb, axis=0)


def _pick_rows(table, sel, k):
    out = jnp.zeros(sel.shape, table.dtype)
    for a in range(k):
        out = jnp.where(sel == a, table[a:a + 1, :], out)
    return out


def _route_head(qt, keys_ref, tick):
    k = PEER_TOPK
    assert k == 16 and PEER_KEYS == 8 * k
    tops = []
    for half in range(2):
        sc = jnp.dot(keys_ref[half], qt[half * PEER_HALF:(half + 1) * PEER_HALF, :], preferred_element_type=F32)
        tops.append(_topk_rows(sc, k, tick))
    (s1, i1), (s2, i2) = tops
    sc, a_sel, b_sel = _top_pairs(s1, s2, k, tick)
    e = jnp.exp(sc - sc[0:1, :])
    g = e / jnp.sum(e, axis=0, keepdims=True)
    return _pick_rows(i1, a_sel, k), _pick_rows(i2, b_sel, k), g


def _peer_route_u_kernel(hn_ref, hc_ref, wq_ref, keys_ref, u_ref, i1_ref, i2_ref, g_ref, act_ref,
                         i1t_ref, i2t_ref, gt_ref, i1c_ref, i2c_ref, *, blocks):
    o = pl.program_id(0)
    c = pl.program_id(1)
    k = PEER_TOPK

    @pl.when((o == 0) & (c == 0))
    def _():
        i1c_ref[...] = jnp.zeros(i1c_ref.shape, I32)
        i2c_ref[...] = jnp.zeros(i2c_ref.shape, I32)

    @pl.when(c == 0)
    def _():
        act_ref[...] = jnp.zeros(act_ref.shape, F32)

    i1c = i1c_ref[...]
    i2c = i2c_ref[...]
    hc = hc_ref[...]
    acc = [act_ref[...]]

    def piece(pc):
        def run():
            a = lax.dot_general(hc, u_ref[pl.ds(pc * 2 * PEER_KEYS, 2 * PEER_KEYS), :], (((1,), (1,)), ((), ())),
                                preferred_element_type=F32)
            for sub in range(2):
                got = jnp.take_along_axis(a[:, sub * PEER_KEYS:(sub + 1) * PEER_KEYS], i2c, axis=1,
                                          mode="promise_in_bounds")
                acc[0] = jnp.where(i1c == c * blocks + 2 * pc + sub, got, acc[0])
        return run

    tm = hn_ref.shape[0]
    groups = tm // ROUTE_LANES
    hps = keys_ref.shape[0] // 2
    hq = wq_ref.shape[0] // hps
    ticks = hps * groups * (2 * (len(_sorting_network(k)) // 16 + k) + k)
    pieces = _Interleaver([piece(pc) for pc in range(blocks // 2)], every=ticks // (blocks // 2 + ROUTE_SLACK))
    qt = lax.dot_general(wq_ref[...], hn_ref[...], (((1,), (1,)), ((), ())), preferred_element_type=F32)
    qt = qt.astype(BF16)
    for hd in range(hps):
        rows = pl.ds(pl.multiple_of((c * hps + hd) * k, k), k)
        for gi in range(groups):
            cols = slice(gi * ROUTE_LANES, (gi + 1) * ROUTE_LANES)
            i1, i2, g = _route_head(qt[hd * hq:(hd + 1) * hq, cols], keys_ref.at[pl.ds(2 * hd, 2)], pieces.tick)
            i1t_ref[rows, cols] = i1
            i2t_ref[rows, cols] = i2
            gt_ref[rows, cols] = g
    pieces.drain()
    act_ref[...] = acc[0]

    @pl.when(c == pl.num_programs(1) - 1)
    def _():
        i1n = i1t_ref[...].T
        i2n = i2t_ref[...].T
        i1_ref[...] = i1n
        i2_ref[...] = i2n
        g_ref[...] = gt_ref[...].T
        i1c_ref[...] = i1n
        i2c_ref[...] = i2n


def _peer_route_u(h, w_q, sub_keys, u, tm=512, hps=4):
    t, d = h.shape
    nt = t // tm
    ne = u.shape[0]
    nchunk = PEER_HEADS // hps
    ec = ne // nchunk
    blocks = ec // PEER_KEYS
    hq = hps * (w_q.shape[1] // PEER_HEADS)
    wq_t = w_q.T.astype(BF16)
    keys = sub_keys.reshape(PEER_HEADS * 2, PEER_KEYS, PEER_HALF).astype(BF16)
    slots = PEER_HEADS * PEER_TOPK
    nxt = lambda o, c: (jnp.minimum(o, nt - 1), 0)
    cur = lambda o, c: (jnp.maximum(o - 1, 0), 0)
    kern = functools.partial(_peer_route_u_kernel, blocks=blocks)
    return pl.pallas_call(
        kern, grid=(nt + 1, nchunk),
        in_specs=[pl.BlockSpec((tm, d), nxt), pl.BlockSpec((tm, d), cur),
                  pl.BlockSpec((hq, d), lambda o, c: (c, 0)),
                  pl.BlockSpec((2 * hps, PEER_KEYS, PEER_HALF), lambda o, c: (c, 0, 0)),
                  pl.BlockSpec((ec, d), lambda o, c: (c, 0))],
        out_specs=[pl.BlockSpec((tm, slots), nxt)] * 3 + [pl.BlockSpec((tm, slots), cur)],
        out_shape=[jax.ShapeDtypeStruct((t, slots), I32), jax.ShapeDtypeStruct((t, slots), I32),
                   jax.ShapeDtypeStruct((t, slots), F32), jax.ShapeDtypeStruct((t, slots), F32)],
        scratch_shapes=[pltpu.VMEM((slots, tm), I32), pltpu.VMEM((slots, tm), I32), pltpu.VMEM((slots, tm), F32),
                        pltpu.VMEM((tm, slots), I32), pltpu.VMEM((tm, slots), I32)],
        compiler_params=_params(("arbitrary", "arbitrary")), name="peer_route_u",
    )(h, h, wq_t, keys, u)


def _peer_v_kernel(act_ref, g_ref, i1_ref, i2_ref, v_ref, x_ref, o_ref, w_ref, wg_ref, *, tm, blocks):
    c = pl.program_id(1)
    nk = PEER_KEYS

    @pl.when(c == 0)
    def _():
        w_ref[...] = g_ref[...] * _gelu(act_ref[...])
        o_ref[...] = x_ref[...]
        sub = lax.broadcasted_iota(I32, (nk, w_ref.shape[1]), 0)

        def per_token(t, carry):
            wrow = w_ref[pl.ds(t, 1), :]
            lhs = jnp.where(i1_ref[pl.ds(t, 1), :] == sub, wrow, 0.0).astype(BF16)
            rhs = jnp.where(i2_ref[pl.ds(t, 1), :] == sub, 1.0, 0.0).astype(BF16)
            grid = lax.dot_general(lhs, rhs, (((1,), (1,)), ((), ())), preferred_element_type=F32)
            wg_ref[pl.ds(pl.multiple_of(t * WG_PITCH, 8), nk), :] = grid
            return carry

        lax.fori_loop(0, tm, per_token, 0, unroll=64)

    acc = jnp.zeros(o_ref.shape, F32)
    for bk in range(0, blocks, 2):
        i1 = c * blocks + bk
        lhs = jnp.concatenate([wg_ref[pl.ds(i1, tm, stride=WG_PITCH), :],
                               wg_ref[pl.ds(i1 + 1, tm, stride=WG_PITCH), :]], axis=1).astype(BF16)
        rhs = v_ref[bk:bk + 2].reshape(2 * nk, v_ref.shape[2])
        acc += jnp.dot(lhs, rhs, preferred_element_type=F32)
    o_ref[...] += acc


def _peer_v(act, g, i1, i2, v, x, tm=256, blocks=64):
    t, d = x.shape
    slots = g.shape[1]
    nk = PEER_KEYS
    v3 = v.reshape(nk, nk, d)
    kern = functools.partial(_peer_v_kernel, tm=tm, blocks=blocks)
    row = lambda i, c: (i, 0)
    return pl.pallas_call(
        kern, grid=(t // tm, nk // blocks),
        in_specs=[pl.BlockSpec((tm, slots), row),
                  pl.BlockSpec((tm, slots), row), pl.BlockSpec((tm, slots), row), pl.BlockSpec((tm, slots), row),
                  pl.BlockSpec((blocks, nk, d), lambda i, c: (c, 0, 0)), pl.BlockSpec((tm, d), row)],
        out_specs=pl.BlockSpec((tm, d), row),
        out_shape=jax.ShapeDtypeStruct((t, d), F32),
        scratch_shapes=[pltpu.VMEM((tm, slots), F32), pltpu.VMEM((tm * WG_PITCH, nk), F32)],
        compiler_params=_params(("parallel", "arbitrary")), name="peer_v",
    )(act, g, i1, i2, v3, x)


def _pad_cols(w, n):
    return jnp.pad(w, ((0, 0), (0, n - w.shape[1])))


def _in_proj_kernel(x_ref, g_ref, w_ref, *rest, bounds):
    outs, xn_ref = rest[:-1], rest[-1]
    j = pl.program_id(1)

    @pl.when(j == 0)
    def _():
        x = x_ref[...]
        ms = jnp.mean(x * x, axis=-1, keepdims=True)
        xn_ref[...] = (x * lax.rsqrt(ms + EPS) * g_ref[...]).astype(xn_ref.dtype)

    for (lo, hi), o_ref in zip(bounds, outs):
        @pl.when((j >= lo) & (j < hi))
        def _(o_ref=o_ref):
            o_ref[...] = jnp.dot(xn_ref[...], w_ref[...], preferred_element_type=F32).astype(o_ref.dtype)


PROJ_TILE = 512
IN_GROUPS = ((1024, BF16), (768, BF16), (48, F32), (1024, BF16), (1024, BF16), (1024, BF16), (1536, BF16),
             (16, F32), (3072, BF16))


def _in_proj(x, g, w_in, tm=1024):
    t, d = x.shape
    tn = PROJ_TILE
    ws, bounds, shapes, off, tile = [], [], [], 0, 0
    for n, dtype in IN_GROUPS:
        width = -(-n // tn) * tn
        ws.append(_pad_cols(w_in[:, off:off + n], width))
        bounds.append((tile, tile + width // tn))
        shapes.append(jax.ShapeDtypeStruct((t, width), dtype))
        off += n
        tile += width // tn
    w_all = jnp.concatenate(ws, axis=1).astype(BF16)
    out_specs = [pl.BlockSpec((tm, tn), functools.partial(lambda i, j, lo, hi: (i, jnp.clip(j - lo, 0, hi - lo - 1)),
                                                          lo=lo, hi=hi)) for lo, hi in bounds]
    kern = functools.partial(_in_proj_kernel, bounds=tuple(bounds))
    return pl.pallas_call(
        kern, grid=(t // tm, tile),
        in_specs=[pl.BlockSpec((tm, d), lambda i, j: (i, 0)), pl.BlockSpec((1, d), lambda i, j: (0, 0)),
                  pl.BlockSpec((d, tn), lambda i, j: (0, j))],
        out_specs=out_specs, out_shape=shapes,
        scratch_shapes=[pltpu.VMEM((tm, d), BF16)],
        compiler_params=_params(("parallel", "arbitrary")), name="in_proj",
    )(x, g.reshape(1, d).astype(F32), w_all)


def _mixer(x, positions, b, s, p):
    a_q, a_kv, a_g, l_x, l_g, s_z, s_xbc, s_dt, m_g = _in_proj(x, p["mix_norm_g"], p["w_in"])

    q, kc, vc, ks, vs, kw, vw = _nsa_prep(a_q, a_kv, positions, p["q_norm_g"], p["k_norm_g"], b, s)
    k_cmp, v_cmp = _compress(kc, vc, p["cmp_pe_k"], p["cmp_pe_v"], p["cmp_k_w1"], p["cmp_k_w2"],
                             p["cmp_v_w1"], p["cmp_v_w2"], p["k_norm_g"])
    o_c, bias = _cmp_select(q, k_cmp, v_cmp, s)
    o_s = _sel_attn(q, bias, ks, vs, s)
    o_w = _win_attn(q, kw, vw, s)
    y_b = _rglru(l_x, l_g, positions, p["lru_conv_w"], p["lru_conv_b"], p["lru_w_r"], p["lru_b_r"],
                 p["lru_w_i"], p["lru_b_i"], p["lru_lambda"], b, s)
    y_c = _ssd(s_z, s_xbc, s_dt, p["ssm_conv_w"], p["ssm_conv_b"], p["ssm_dt_bias"], p["ssm_a_log"],
               p["ssm_d"], p["ssm_norm_g"], b, s)
    return _merge(o_c, o_s, o_w, a_g, y_b, y_c, m_g, x, p["w_branch"], p["w_out"])


def _peer(x, p):
    h = _rmsnorm(x, p["ffn_norm_g"])
    i1, i2, g, act = _peer_route_u(h, p["peer_w_q"], p["peer_sub_keys"], p["peer_u"].astype(BF16))
    return _peer_v(act, g, i1, i2, p["peer_v"].astype(BF16), x)


_LAYER_PARAMS = ("mix_norm_g", "w_in", "q_norm_g", "k_norm_g", "cmp_pe_k", "cmp_pe_v", "cmp_k_w1", "cmp_k_w2",
                 "cmp_v_w1", "cmp_v_w2", "lru_conv_w", "lru_conv_b", "lru_w_r", "lru_b_r", "lru_w_i", "lru_b_i",
                 "lru_lambda", "ssm_conv_w", "ssm_conv_b", "ssm_dt_bias", "ssm_a_log", "ssm_d", "ssm_norm_g",
                 "w_branch", "w_out", "ffn_norm_g", "peer_w_q", "peer_sub_keys", "peer_u", "peer_v")


def kernel(x, positions, mix_norm_g, w_in, q_norm_g, k_norm_g, cmp_pe_k, cmp_pe_v, cmp_k_w1, cmp_k_w2, cmp_v_w1, cmp_v_w2, lru_conv_w, lru_conv_b, lru_w_r, lru_b_r, lru_w_i, lru_b_i, lru_lambda, ssm_conv_w, ssm_conv_b, ssm_dt_bias, ssm_a_log, ssm_d, ssm_norm_g, w_branch, w_out, ffn_norm_g, peer_w_q, peer_sub_keys, peer_u, peer_v):
    stacked = dict(zip(_LAYER_PARAMS, (mix_norm_g, w_in, q_norm_g, k_norm_g, cmp_pe_k, cmp_pe_v, cmp_k_w1,
                                       cmp_k_w2, cmp_v_w1, cmp_v_w2, lru_conv_w, lru_conv_b, lru_w_r, lru_b_r,
                                       lru_w_i, lru_b_i, lru_lambda, ssm_conv_w, ssm_conv_b, ssm_dt_bias,
                                       ssm_a_log, ssm_d, ssm_norm_g, w_branch, w_out, ffn_norm_g, peer_w_q,
                                       peer_sub_keys, peer_u, peer_v)))
    b, s, d = x.shape
    xf = x.reshape(b * s, d).astype(F32)
    for layer in range(mix_norm_g.shape[0]):
        p = {name: arr[layer] for name, arr in stacked.items()}
        xf = _mixer(xf, positions, b, s, p)
        xf = _peer(xf, p)
    return xf.reshape(b, s, d).astype(x.dtype)
```

```python
import functools
import math

import jax
import jax.numpy as jnp
from jax import lax
from jax.experimental import pallas as pl
from jax.experimental.pallas import tpu as pltpu

F32 = jnp.float32
BF16 = jnp.bfloat16
I32 = jnp.int32

D_MODEL = 1024
HEAD_DIM = 64
N_HEADS = 16
N_KV_GROUPS = 2
HEADS_PER_GROUP = 8
CMP_BLOCK = 32
CMP_STRIDE = 16
CMP_HIDDEN = 256
SEL_BLOCK = 64
N_SELECT = 16
WINDOW = 512
SEL_FORCE = 100.0
ROPE_THETA = 10000.0
SCALE = HEAD_DIM ** -0.5
LOG2E = math.log2(math.e)
LRU_C = 8.0
CONV_WIDTH = 4
SSM_HEADS = 16
SSM_HEAD_DIM = 64
SSM_GROUPS = 2
SSM_STATE = 128
SSM_CHUNK = 128
SSM_INNER = 1024
PEER_HEADS = 8
PEER_KEYS = 128
PEER_HALF = 128
PEER_TOPK = 16
EPS = 1e-6
NEG = -1e30
LANES = 128

VMEM_LIMIT = 56 * 1024 * 1024
SEL_LOOKAHEAD = 5
WIN_SUBTILE = 256
WIN_LOOKAHEAD = 3
SEL_SUBTILE = 256
SEL_ACC_ROWS = HEAD_DIM + 16
WG_PITCH = PEER_KEYS + 8
ROUTE_SLACK = 3
ROUTE_LANES = 128


def _params(sem):
    return pltpu.CompilerParams(dimension_semantics=sem, vmem_limit_bytes=VMEM_LIMIT)


def _gelu(x):
    return 0.5 * x * (1.0 + jnp.tanh(math.sqrt(2.0 / math.pi) * (x + 0.044715 * x * x * x)))


def _sigmoid(x):
    return 0.5 * jnp.tanh(0.5 * x) + 0.5


def _softplus(x):
    return jnp.maximum(x, 0.0) + jnp.log(1.0 + jnp.exp(-jnp.abs(x)))


def _rmsnorm_kernel(x_ref, g_ref, o_ref):
    x = x_ref[...].astype(F32)
    ms = jnp.mean(x * x, axis=-1, keepdims=True)
    o_ref[...] = (x * lax.rsqrt(ms + EPS) * g_ref[...]).astype(o_ref.dtype)


def _rmsnorm(x, g, tm=512):
    t, d = x.shape
    return pl.pallas_call(
        _rmsnorm_kernel, grid=(t // tm,),
        in_specs=[pl.BlockSpec((tm, d), lambda i: (i, 0)), pl.BlockSpec((1, d), lambda i: (0, 0))],
        out_specs=pl.BlockSpec((tm, d), lambda i: (i, 0)),
        out_shape=jax.ShapeDtypeStruct((t, d), BF16),
        compiler_params=_params(("parallel",)), name="rmsnorm")(x, g.reshape(1, d).astype(F32))


def _norm_rope(x, g, cos, sin_signed, bd):
    ms = jnp.dot((x * x).astype(BF16), bd, preferred_element_type=F32)
    y = x * lax.rsqrt(ms + EPS) * g
    lane = lax.broadcasted_iota(I32, y.shape, 1)
    first_half = (lane % HEAD_DIM) < (HEAD_DIM // 2)
    partner = jnp.where(first_half, pltpu.roll(y, LANES - HEAD_DIM // 2, 1), pltpu.roll(y, HEAD_DIM // 2, 1))
    return y * cos + partner * sin_signed


def _nsa_prep_kernel(aq_ref, akv_ref, pos_ref, inv_ref, sgn_ref, gq_ref, gk_ref, bd_ref,
                     q_ref, kc_ref, vc_ref, ks_ref, vs_ref, kw_ref, vw_ref):
    ang = pos_ref[...].astype(F32) * inv_ref[...]
    cos = jnp.cos(ang)
    sin_signed = jnp.sin(ang) * sgn_ref[...]
    bd = bd_ref[...]
    for c in range(D_MODEL // LANES):
        x = aq_ref[:, c * LANES:(c + 1) * LANES].astype(F32)
        y = (_norm_rope(x, gq_ref[...], cos, sin_signed, bd) * (SCALE * LOG2E)).astype(q_ref.dtype)
        q_ref[0, 2 * c] = y[:, :HEAD_DIM]
        q_ref[0, 2 * c + 1] = y[:, HEAD_DIM:]
    outs = (kc_ref, vc_ref, ks_ref, vs_ref, kw_ref, vw_ref)
    for c, o_ref in enumerate(outs):
        x = akv_ref[:, c * LANES:(c + 1) * LANES]
        if c % 2 == 0:
            y = _norm_rope(x.astype(F32), gk_ref[...], cos, sin_signed, bd).astype(o_ref.dtype)
        else:
            y = x.astype(o_ref.dtype)
        o_ref[0, 0] = y[:, :HEAD_DIM]
        o_ref[0, 1] = y[:, HEAD_DIM:]


def _nsa_prep(a_q, a_kv, positions, q_norm_g, k_norm_g, b, s, tm=512):
    half = HEAD_DIM // 2
    lane = jnp.arange(LANES)
    inv = (ROPE_THETA ** (-((lane % half).astype(F32)) / half)).reshape(1, LANES)
    sgn = jnp.where((lane % HEAD_DIM) < half, -1.0, 1.0).astype(F32).reshape(1, LANES)
    bd = jnp.where((lane[:, None] // HEAD_DIM) == (lane[None, :] // HEAD_DIM), 1.0 / HEAD_DIM, 0.0).astype(BF16)
    gq = jnp.tile(q_norm_g.astype(F32), 2).reshape(1, LANES)
    gk = jnp.tile(k_norm_g.astype(F32), 2).reshape(1, LANES)
    nt = s // tm
    row = lambda i: (i, 0)
    const = lambda i: (0, 0)
    kv_shape = jax.ShapeDtypeStruct((b, N_KV_GROUPS, s, HEAD_DIM), BF16)
    kv_spec = pl.BlockSpec((1, N_KV_GROUPS, tm, HEAD_DIM), lambda i: (i // nt, 0, i % nt, 0))
    return pl.pallas_call(
        _nsa_prep_kernel, grid=(b * nt,),
        in_specs=[pl.BlockSpec((tm, D_MODEL), row), pl.BlockSpec((tm, 6 * LANES), row),
                  pl.BlockSpec((tm, 1), row), pl.BlockSpec((1, LANES), const), pl.BlockSpec((1, LANES), const),
                  pl.BlockSpec((1, LANES), const), pl.BlockSpec((1, LANES), const),
                  pl.BlockSpec((LANES, LANES), const)],
        out_specs=[pl.BlockSpec((1, N_HEADS, tm, HEAD_DIM), lambda i: (i // nt, 0, i % nt, 0))] + [kv_spec] * 6,
        out_shape=[jax.ShapeDtypeStruct((b, N_HEADS, s, HEAD_DIM), BF16)] + [kv_shape] * 6,
        compiler_params=_params(("parallel",)), name="nsa_prep",
    )(a_q, a_kv, positions.reshape(b * s, 1).astype(I32), inv, sgn, gq, gk, bd)


def _compress_kernel(uk_ref, uv_ref, pek_ref, pev_ref, kw1_ref, kw2_ref, vw1_ref, vw2_ref, gk_ref,
                     kc_ref, vc_ref):
    half = CMP_STRIDE * HEAD_DIM

    def mlp(u, pe, w1_ref, w2_ref):
        n = u.shape[0]
        ha = jnp.dot(u, w1_ref[:half, :], preferred_element_type=F32)
        hb = jnp.dot(u, w1_ref[half:, :], preferred_element_type=F32)
        bias = jnp.dot(pe, w1_ref[...], preferred_element_type=F32)[0:1, :]
        pre = ha + pltpu.roll(hb, n - 1, 0) + bias
        return jnp.dot(_gelu(pre).astype(BF16), w2_ref[...], preferred_element_type=F32)

    k = mlp(uk_ref[0, 0], pek_ref[...], kw1_ref, kw2_ref)
    ms = jnp.mean(k * k, axis=-1, keepdims=True)
    kc_ref[0, 0] = (k * lax.rsqrt(ms + EPS) * gk_ref[...]).astype(kc_ref.dtype)
    vc_ref[0, 0] = mlp(uv_ref[0, 0], pev_ref[...], vw1_ref, vw2_ref).astype(vc_ref.dtype)


def _compress(kc, vc, pe_k, pe_v, kw1, kw2, vw1, vw2, k_norm_g):
    b, g, s, dh = kc.shape
    ng = s // CMP_STRIDE
    wide = CMP_BLOCK * dh
    uk = kc.reshape(b, g, ng, CMP_STRIDE * dh)
    uv = vc.reshape(b, g, ng, CMP_STRIDE * dh)
    pek = jnp.zeros((8, wide), BF16).at[0].set(pe_k.reshape(wide).astype(BF16))
    pev = jnp.zeros((8, wide), BF16).at[0].set(pe_v.reshape(wide).astype(BF16))
    u_spec = pl.BlockSpec((1, 1, ng, CMP_STRIDE * dh), lambda i, j: (i, j, 0, 0))
    c2 = lambda i, j: (0, 0)
    o_spec = pl.BlockSpec((1, 1, ng, dh), lambda i, j: (i, j, 0, 0))
    o_shape = jax.ShapeDtypeStruct((b, g, ng, dh), BF16)
    return pl.pallas_call(
        _compress_kernel, grid=(b, g),
        in_specs=[u_spec, u_spec, pl.BlockSpec((8, wide), c2), pl.BlockSpec((8, wide), c2),
                  pl.BlockSpec((wide, CMP_HIDDEN), c2), pl.BlockSpec((CMP_HIDDEN, dh), c2),
                  pl.BlockSpec((wide, CMP_HIDDEN), c2), pl.BlockSpec((CMP_HIDDEN, dh), c2),
                  pl.BlockSpec((1, dh), c2)],
        out_specs=[o_spec, o_spec], out_shape=[o_shape, o_shape],
        compiler_params=_params(("parallel", "parallel")), name="nsa_compress",
    )(uk, uv, pek, pev, kw1.astype(BF16), kw2.astype(BF16), vw1.astype(BF16), vw2.astype(BF16),
      k_norm_g.reshape(1, dh).astype(F32))


def _store_head_pair(o_ref, h, o_t, held):
    if h % 2 == 0:
        return o_t
    pair = jnp.concatenate([held, o_t], axis=0).T
    o_ref[0, :, (h - 1) * HEAD_DIM:(h + 1) * HEAD_DIM] = pair.astype(o_ref.dtype)
    return None


def _cmp_select_kernel(q_ref, kc_ref, vc_ref, ovt_ref, o_ref, bias_ref, *, tq, n_cmp, n_pick):
    i = pl.program_id(2)
    chunk = min(LANES, kc_ref.shape[2])
    visible = ((i + 1) * tq - CMP_BLOCK) // CMP_STRIDE + 1
    nchunks = jnp.clip((visible + chunk - 1) // chunk, 1, kc_ref.shape[2] // chunk)
    for n in range(1, kc_ref.shape[2] // chunk + 1):
        pl.when(nchunks == n)(functools.partial(
            _cmp_select_keys, q_ref, kc_ref, vc_ref, ovt_ref, o_ref, bias_ref, nc=n * chunk, i=i, tq=tq,
            n_cmp=n_cmp, n_pick=n_pick))


def _cmp_select_keys(q_ref, kc_ref, vc_ref, ovt_ref, o_ref, bias_ref, *, nc, i, tq, n_cmp, n_pick):
    hg = q_ref.shape[1]
    nsel = ovt_ref.shape[0]
    kc = kc_ref[0, 0, :nc, :]
    vct = vc_ref[0, 0, :nc, :].astype(F32).T.astype(BF16)
    ovt = ovt_ref[:, :nc]
    c = lax.broadcasted_iota(I32, (nc, tq), 0)
    t = i * tq + lax.broadcasted_iota(I32, (nc, tq), 1)
    mask = ((CMP_STRIDE * c + CMP_BLOCK - 1) <= t) & (c < n_cmp)
    score = lambda h: lax.dot_general(kc, q_ref[0, h], (((1,), (1,)), ((), ())), preferred_element_type=F32)
    ahead = [score(h) for h in range(WIN_LOOKAHEAD)]
    psum = jnp.zeros((nc, tq), F32)
    held = None
    for h in range(hg):
        st = ahead.pop(0)
        if h + WIN_LOOKAHEAD < hg:
            ahead.append(score(h + WIN_LOOKAHEAD))
        st = jnp.where(mask, st, NEG)
        m = jnp.max(st, axis=0, keepdims=True)
        p = jnp.where(mask, jnp.exp2(st - m), 0.0)
        l = jnp.sum(p, axis=0, keepdims=True)
        p = p * jnp.where(l > 0.0, 1.0 / l, 0.0)
        o = jnp.dot(vct, p.astype(BF16), preferred_element_type=F32)
        held = _store_head_pair(o_ref, h, o, held)
        psum = psum + p

    hi = psum.astype(BF16)
    lo = (psum - hi.astype(F32)).astype(BF16)
    imp = jnp.dot(ovt, hi, preferred_element_type=F32) + jnp.dot(ovt, lo, preferred_element_type=F32)
    j = lax.broadcasted_iota(I32, (nsel, tq), 0)
    tt = i * tq + lax.broadcasted_iota(I32, (nsel, tq), 1)
    cur = tt // SEL_BLOCK
    forced = (j == 0) | (j == cur) | (j == cur - 1)
    sc = jnp.where(forced, SEL_FORCE, jnp.where(j * SEL_BLOCK <= tt, imp, -1.0))
    picked = jnp.zeros((nsel, tq), jnp.bool_)
    for _ in range(n_pick):
        mx = jnp.max(sc, axis=0, keepdims=True)
        idx = jnp.min(jnp.where(sc == mx, j, nsel), axis=0, keepdims=True)
        hit = j == idx
        picked = picked | hit
        sc = jnp.where(hit, -jnp.inf, sc)
    bias_ref[0, 0] = jnp.where(picked & (j <= cur), 0.0, NEG).T.astype(bias_ref.dtype)


def _cmp_select(q, k_cmp, v_cmp, s, tq=512):
    b, nh, _, dh = q.shape
    g, hg = N_KV_GROUPS, HEADS_PER_GROUP
    nc = k_cmp.shape[2]
    n_cmp = (s - CMP_BLOCK) // CMP_STRIDE + 1
    nsel = s // SEL_BLOCK
    n_pick = min(N_SELECT, nsel)
    cs = CMP_STRIDE * jnp.arange(nc)
    ss = SEL_BLOCK * jnp.arange(nsel)
    ovt = jnp.clip(jnp.minimum(cs[None, :] + CMP_BLOCK, ss[:, None] + SEL_BLOCK)
                   - jnp.maximum(cs[None, :], ss[:, None]), 0).astype(F32) / CMP_BLOCK
    ovt = jnp.where(jnp.arange(nc)[None, :] < n_cmp, ovt, 0.0).astype(BF16)
    kern = functools.partial(_cmp_select_kernel, tq=tq, n_cmp=n_cmp, n_pick=n_pick)
    return pl.pallas_call(
        kern, grid=(b, g, s // tq),
        in_specs=[pl.BlockSpec((1, hg, tq, dh), lambda bi, gi, i: (bi, gi, i, 0)),
                  pl.BlockSpec((1, 1, nc, dh), lambda bi, gi, i: (bi, gi, 0, 0)),
                  pl.BlockSpec((1, 1, nc, dh), lambda bi, gi, i: (bi, gi, 0, 0)),
                  pl.BlockSpec((nsel, nc), lambda bi, gi, i: (0, 0))],
        out_specs=[pl.BlockSpec((1, tq, hg * dh), lambda bi, gi, i: (bi, i, gi)),
                   pl.BlockSpec((1, 1, tq, nsel), lambda bi, gi, i: (bi, gi, i, 0))],
        out_shape=[jax.ShapeDtypeStruct((b, s, nh * dh), BF16),
                   jax.ShapeDtypeStruct((b, g, s, nsel), BF16)],
        compiler_params=_params(("parallel", "parallel", "parallel")), name="nsa_cmp_select",
    )(q, k_cmp, v_cmp, ovt)


def _sel_attn_kernel(it_ref, jt_ref, q_ref, bias_ref, k_ref, v_ref, o_ref, qa_ref, m_ref, acc_ref, *, tq, tk):
    pid = pl.program_id(2)
    i = it_ref[pid]
    j = jt_ref[pid]
    hg = q_ref.shape[1]
    nsel = bias_ref.shape[3]
    last_j = ((i + 1) * tq - 1) // tk

    @pl.when(j == 0)
    def _():
        for h in range(hg):
            qa_ref[h, :, :nsel] = bias_ref[0, 0]
            qa_ref[h, :, nsel:] = q_ref[0, h]
        m_ref[...] = jnp.full(m_ref.shape, -jnp.inf, F32)
        acc_ref[...] = jnp.zeros(acc_ref.shape, F32)

    def accumulate(diagonal):
        key = j * tk + lax.broadcasted_iota(I32, (tk, nsel), 0)
        blk = lax.broadcasted_iota(I32, (tk, nsel), 1)
        onehot = jnp.where(key // SEL_BLOCK == blk, 1.0, 0.0).astype(BF16)
        ka = jnp.concatenate([onehot, k_ref[0, 0]], axis=1)
        va = jnp.concatenate([v_ref[0, 0].astype(F32), jnp.ones((tk, LANES - HEAD_DIM), F32)], axis=1)
        vat = va.T[:SEL_ACC_ROWS, :].astype(BF16)
        qw = min(SEL_SUBTILE, tq)
        items = [(h, b) for h in range(hg) for b in range(tq // qw)]
        rows = lambda b: (b + 1) * qw if diagonal else tk

        def score(item):
            h, b = item
            return lax.dot_general(ka[:rows(b), :], qa_ref[h, b * qw:(b + 1) * qw, :], (((1,), (1,)), ((), ())),
                                   preferred_element_type=F32)

        ahead = [score(item) for item in items[:SEL_LOOKAHEAD]]
        for n, (h, b) in enumerate(items):
            st = ahead.pop(0)
            if n + SEL_LOOKAHEAD < len(items):
                ahead.append(score(items[n + SEL_LOOKAHEAD]))
            if diagonal:
                kp = lax.broadcasted_iota(I32, st.shape, 0)
                t = b * qw + lax.broadcasted_iota(I32, st.shape, 1)
                st = jnp.where(kp <= t, st, NEG)
            lanes = slice(b * qw, (b + 1) * qw)
            m_old = m_ref[h, :, lanes]
            m_new = jnp.maximum(m_old, jnp.max(st, axis=0, keepdims=True))
            alpha = jnp.exp2(m_old - m_new)
            p = jnp.exp2(st - m_new)
            acc_ref[h, :, lanes] = alpha * acc_ref[h, :, lanes] + jnp.dot(
                vat[:, :rows(b)], p.astype(BF16), preferred_element_type=F32)
            m_ref[h, :, lanes] = m_new

    @pl.when(j < last_j)
    def _():
        accumulate(False)

    @pl.when(j == last_j)
    def _():
        accumulate(True)
        held = None
        for h in range(hg):
            acc = acc_ref[h]
            o = acc[:HEAD_DIM, :] / acc[HEAD_DIM:HEAD_DIM + 1, :]
            held = _store_head_pair(o_ref, h, o, held)


def _sel_attn(q, bias, k_s, v_s, s, tq=1024):
    tk = tq
    b, nh, _, dh = q.shape
    g, hg = N_KV_GROUPS, HEADS_PER_GROUP
    nsel = bias.shape[3]
    pairs = [(i, j) for i in range(s // tq) for j in range(((i + 1) * tq - 1) // tk + 1)]
    it = jnp.asarray([pr[0] for pr in pairs], I32)
    jt = jnp.asarray([pr[1] for pr in pairs], I32)
    q_map = lambda bi, gi, pid, it_ref, jt_ref: (bi, gi, it_ref[pid], 0)
    kv_map = lambda bi, gi, pid, it_ref, jt_ref: (bi, gi, jt_ref[pid], 0)
    kern = functools.partial(_sel_attn_kernel, tq=tq, tk=tk)
    grid_spec = pltpu.PrefetchScalarGridSpec(
        num_scalar_prefetch=2, grid=(b, g, len(pairs)),
        in_specs=[pl.BlockSpec((1, hg, tq, dh), q_map), pl.BlockSpec((1, 1, tq, nsel), q_map),
                  pl.BlockSpec((1, 1, tk, dh), kv_map), pl.BlockSpec((1, 1, tk, dh), kv_map)],
        out_specs=pl.BlockSpec((1, tq, hg * dh), lambda bi, gi, pid, it_ref, jt_ref: (bi, it_ref[pid], gi)),
        scratch_shapes=[pltpu.VMEM((hg, tq, nsel + dh), BF16), pltpu.VMEM((hg, 1, tq), F32),
                        pltpu.VMEM((hg, SEL_ACC_ROWS, tq), F32)])
    return pl.pallas_call(
        kern, grid_spec=grid_spec, out_shape=jax.ShapeDtypeStruct((b, s, nh * dh), BF16),
        compiler_params=_params(("parallel", "parallel", "arbitrary")), name="nsa_sel_attn",
    )(it, jt, q, bias, k_s, v_s)


def _win_attn_kernel(q_ref, *refs, tq, nwin):
    k_refs, v_refs, o_ref = refs[:nwin], refs[nwin:2 * nwin], refs[2 * nwin]
    i = pl.program_id(2)
    hg = q_ref.shape[1]
    k = jnp.concatenate([r[0, 0] for r in k_refs], axis=0)
    vt = jnp.concatenate([r[0, 0] for r in v_refs], axis=0).astype(F32).T.astype(BF16)
    qw = WIN_SUBTILE
    nk = WINDOW + qw
    r = lax.broadcasted_iota(I32, (nk, qw), 0)
    c = lax.broadcasted_iota(I32, (nk, qw), 1)
    band = (r > c) & (r <= c + WINDOW)
    items = [(h, b) for h in range(hg) for b in range(tq // qw)]

    def score(item):
        h, b = item
        return lax.dot_general(k[b * qw:b * qw + nk, :], q_ref[0, h, b * qw:(b + 1) * qw, :],
                               (((1,), (1,)), ((), ())), preferred_element_type=F32)

    ahead = [score(item) for item in items[:WIN_LOOKAHEAD]]
    held, parts = None, []
    for n, (h, b) in enumerate(items):
        st = ahead.pop(0)
        if n + WIN_LOOKAHEAD < len(items):
            ahead.append(score(items[n + WIN_LOOKAHEAD]))
        mask = band & (i * tq - WINDOW + b * qw + r >= 0)
        st = jnp.where(mask, st, NEG)
        m = jnp.max(st, axis=0, keepdims=True)
        p = jnp.where(mask, jnp.exp2(st - m), 0.0)
        l = jnp.sum(p, axis=0, keepdims=True)
        parts.append(jnp.dot(vt[:, b * qw:b * qw + nk], p.astype(BF16), preferred_element_type=F32) / l)
        if len(parts) == tq // qw:
            held = _store_head_pair(o_ref, h, jnp.concatenate(parts, axis=1), held)
            parts = []


def _win_attn(q, k_w, v_w, s, tq=512):
    b, nh, _, dh = q.shape
    g, hg = N_KV_GROUPS, HEADS_PER_GROUP
    nwin = WINDOW // tq + 1
    pad = ((0, 0), (0, 0), (WINDOW, 0), (0, 0))
    kp, vp = jnp.pad(k_w, pad), jnp.pad(v_w, pad)
    kv_specs = [pl.BlockSpec((1, 1, tq, dh), functools.partial(lambda bi, gi, i, w: (bi, gi, i + w, 0), w=w))
                for w in range(nwin)]
    kern = functools.partial(_win_attn_kernel, tq=tq, nwin=nwin)
    return pl.pallas_call(
        kern, grid=(b, g, s // tq),
        in_specs=[pl.BlockSpec((1, hg, tq, dh), lambda bi, gi, i: (bi, gi, i, 0))] + kv_specs + kv_specs,
        out_specs=pl.BlockSpec((1, tq, hg * dh), lambda bi, gi, i: (bi, i, gi)),
        out_shape=jax.ShapeDtypeStruct((b, s, nh * dh), BF16),
        compiler_params=_params(("parallel", "parallel", "parallel")), name="nsa_win_attn",
    )(q, *([kp] * nwin), *([vp] * nwin))


def _causal_conv(x, win_ref, w_ref, b_ref):
    n = x.shape[0]
    win_ref[8:, :] = x
    out = b_ref[...] + w_ref[CONV_WIDTH - 1:CONV_WIDTH, :] * x
    for k in range(CONV_WIDTH - 1):
        off = 8 - (CONV_WIDTH - 1) + k
        out = out + w_ref[k:k + 1, :] * win_ref[off:off + n, :]
    win_ref[0:8, :] = x[n - 8:, :]
    return out


def _rglru_kernel(x_ref, gate_ref, pos_ref, cw_ref, cb_ref, wr_ref, br_ref, wi_ref, bi_ref, lam_ref,
                  o_ref, tail_ref, h_ref):
    @pl.when(pl.program_id(1) == 0)
    def _():
        tail_ref[0:8, :] = jnp.zeros((8, tail_ref.shape[1]), F32)
        h_ref[...] = jnp.zeros(h_ref.shape, F32)

    n = x_ref.shape[0]
    x = x_ref[...].astype(F32)
    xc = _causal_conv(x, tail_ref, cw_ref, cb_ref)
    xcb = xc.astype(BF16)
    r = _sigmoid(jnp.dot(xcb, wr_ref[...], preferred_element_type=F32) + br_ref[...])
    gi = _sigmoid(jnp.dot(xcb, wi_ref[...], preferred_element_type=F32) + bi_ref[...])
    log_a = -LRU_C * r * _softplus(-lam_ref[...])
    reset = pos_ref[...] == 0
    a = jnp.where(reset, 0.0, jnp.exp(log_a))
    mult = jnp.where(reset, 1.0, jnp.sqrt(jnp.maximum(1.0 - jnp.exp(2.0 * log_a), 0.0)))
    bb = mult * (gi * xc)
    sub = lax.broadcasted_iota(I32, a.shape, 0) % 8
    d = 1
    while d < 8:
        a_sh = pltpu.roll(a, d, 0)
        b_sh = pltpu.roll(bb, d, 0)
        live = sub >= d
        bb = jnp.where(live, a * b_sh + bb, bb)
        a = jnp.where(live, a * a_sh, a)
        d *= 2
    carry = h_ref[...]
    hs = []
    for g in range(n // 8):
        h = bb[8 * g:8 * g + 8, :] + a[8 * g:8 * g + 8, :] * carry
        carry = h[7:8, :]
        hs.append(h)
    h_ref[...] = carry
    o_ref[...] = (jnp.concatenate(hs, axis=0) * _gelu(gate_ref[...].astype(F32))).astype(o_ref.dtype)


def _block_diag(w):
    nb, bs, _ = w.shape
    eye = jnp.eye(nb, dtype=w.dtype)
    return (w[:, :, None, :] * eye[:, None, :, None]).reshape(nb * bs, nb * bs)


def _rglru(l_x, l_g, positions, conv_w, conv_b, w_r, b_r, w_i, b_i, lam, b, s, ts=256):
    d = l_x.shape[1]
    nt = s // ts
    row = lambda bi, i: (bi * nt + i, 0)
    c2 = lambda bi, i: (0, 0)
    vec = lambda v: v.reshape(1, d).astype(F32)
    return pl.pallas_call(
        _rglru_kernel, grid=(b, nt),
        in_specs=[pl.BlockSpec((ts, d), row), pl.BlockSpec((ts, d), row), pl.BlockSpec((ts, 1), row),
                  pl.BlockSpec((CONV_WIDTH, d), c2), pl.BlockSpec((1, d), c2),
                  pl.BlockSpec((d, d), c2), pl.BlockSpec((1, d), c2),
                  pl.BlockSpec((d, d), c2), pl.BlockSpec((1, d), c2), pl.BlockSpec((1, d), c2)],
        out_specs=pl.BlockSpec((ts, d), row),
        out_shape=jax.ShapeDtypeStruct((b * s, d), BF16),
        scratch_shapes=[pltpu.VMEM((8 + ts, d), F32), pltpu.VMEM((1, d), F32)],
        compiler_params=_params(("parallel", "arbitrary")), name="rglru",
    )(l_x, l_g, positions.reshape(b * s, 1).astype(I32), conv_w.astype(F32), vec(conv_b),
      _block_diag(w_r).astype(BF16), vec(b_r), _block_diag(w_i).astype(BF16), vec(b_i), vec(lam))


def _ssd_kernel(z_ref, xbc_ref, dt_ref, cw_ref, cb_ref, dtb_ref, alog_ref, dfull_ref, ng_ref,
                o_ref, tail_ref, state_ref, y_ref):
    @pl.when(pl.program_id(1) == 0)
    def _():
        tail_ref[0:8, :] = jnp.zeros((8, tail_ref.shape[1]), F32)
        state_ref[...] = jnp.zeros(state_ref.shape, F32)

    n = xbc_ref.shape[0]
    hg = SSM_HEADS // SSM_GROUPS
    x = xbc_ref[...].astype(F32)
    xc = _causal_conv(x, tail_ref, cw_ref, cb_ref)
    xc = xc * _sigmoid(xc)
    xs = xc[:, :SSM_INNER]
    dt = _softplus(dt_ref[...] + dtb_ref[...])
    adt = dt * (-jnp.exp(alog_ref[...]))
    row = lax.broadcasted_iota(I32, adt.shape, 0)
    acs = adt
    d = 1
    while d < n:
        acs = acs + jnp.where(row >= d, pltpu.roll(acs, d, 0), 0.0)
        d *= 2
    acs_t = acs.T
    li = lax.broadcasted_iota(I32, (n, n), 0)
    si = lax.broadcasted_iota(I32, (n, n), 1)
    tri = li >= si
    for g in range(SSM_GROUPS):
        bm = xc[:, SSM_INNER + g * SSM_STATE:SSM_INNER + (g + 1) * SSM_STATE].astype(BF16)
        cm = xc[:, SSM_INNER + (SSM_GROUPS + g) * SSM_STATE:SSM_INNER + (SSM_GROUPS + g + 1) * SSM_STATE].astype(BF16)
        cb = lax.dot_general(cm, bm, (((1,), (1,)), ((), ())), preferred_element_type=F32)
        bm_t = bm.T
        for hh in range(hg):
            h = g * hg + hh
            acol = jnp.broadcast_to(acs[:, h:h + 1], (n, n))
            arow = acs_t[h:h + 1, :]
            decay = jnp.exp(jnp.where(tri, acol - arow, NEG))
            acol_p = acol[:, :SSM_HEAD_DIM]
            xh = xs[:, h * SSM_HEAD_DIM:(h + 1) * SSM_HEAD_DIM] * dt[:, h:h + 1]
            a_last = acol_p[n - 1:n, :]
            y = jnp.dot((cb * decay).astype(BF16), xh.astype(BF16), preferred_element_type=F32)
            st = state_ref[h]
            y = y + jnp.dot(cm, st.astype(BF16), preferred_element_type=F32) * jnp.exp(acol_p)
            upd = jnp.dot(bm_t, (xh * jnp.exp(a_last - acol_p)).astype(BF16), preferred_element_type=F32)
            state_ref[h] = jnp.exp(a_last) * st + upd
            y_ref[:, h * SSM_HEAD_DIM:(h + 1) * SSM_HEAD_DIM] = y
    z = z_ref[...].astype(F32)
    y = (y_ref[...] + dfull_ref[...] * xs) * (z * _sigmoid(z))
    gw = SSM_INNER // SSM_GROUPS
    for g in range(SSM_GROUPS):
        yg = y[:, g * gw:(g + 1) * gw]
        ms = jnp.mean(yg * yg, axis=-1, keepdims=True)
        o_ref[:, g * gw:(g + 1) * gw] = (yg * lax.rsqrt(ms + EPS) * ng_ref[:, g * gw:(g + 1) * gw]).astype(o_ref.dtype)


def _ssd(s_z, s_xbc, s_dt, conv_w, conv_b, dt_bias, a_log, d_skip, norm_g, b, s):
    n = SSM_CHUNK
    nt = s // n
    c = s_xbc.shape[1]
    row = lambda bi, i: (bi * nt + i, 0)
    c2 = lambda bi, i: (0, 0)
    pad_h = lambda v: jnp.zeros((1, LANES), F32).at[0, :SSM_HEADS].set(v.astype(F32))
    dfull = jnp.repeat(d_skip.astype(F32), SSM_HEAD_DIM).reshape(1, SSM_INNER)
    return pl.pallas_call(
        _ssd_kernel, grid=(b, nt),
        in_specs=[pl.BlockSpec((n, SSM_INNER), row), pl.BlockSpec((n, c), row), pl.BlockSpec((n, LANES), row),
                  pl.BlockSpec((CONV_WIDTH, c), c2), pl.BlockSpec((1, c), c2),
                  pl.BlockSpec((1, LANES), c2), pl.BlockSpec((1, LANES), c2),
                  pl.BlockSpec((1, SSM_INNER), c2), pl.BlockSpec((1, SSM_INNER), c2)],
        out_specs=pl.BlockSpec((n, SSM_INNER), row),
        out_shape=jax.ShapeDtypeStruct((b * s, SSM_INNER), BF16),
        scratch_shapes=[pltpu.VMEM((8 + n, c), F32), pltpu.VMEM((SSM_HEADS, SSM_STATE, SSM_HEAD_DIM), F32),
                        pltpu.VMEM((n, SSM_INNER), F32)],
        compiler_params=_params(("parallel", "arbitrary")), name="ssd",
    )(s_z, s_xbc, s_dt, conv_w.astype(F32), conv_b.reshape(1, c).astype(F32), pad_h(dt_bias), pad_h(a_log),
      dfull, norm_g.reshape(1, SSM_INNER).astype(F32))


def _merge_kernel(oc_ref, os_ref, ow_ref, ag_ref, ex_ref, yb_ref, yc_ref, mg_ref, x_ref, wb_ref, wo_ref, o_ref):
    gates = _sigmoid(ag_ref[...])
    hi = gates.astype(BF16)
    lo = (gates - hi.astype(F32)).astype(BF16)
    spread = lambda k: (jnp.dot(hi, ex_ref[k], preferred_element_type=F32)
                        + jnp.dot(lo, ex_ref[k], preferred_element_type=F32))
    ya = (spread(0) * oc_ref[...].astype(F32) + spread(1) * os_ref[...].astype(F32)
          + spread(2) * ow_ref[...].astype(F32))
    d = D_MODEL
    merged = _sigmoid(mg_ref[:, 0:d].astype(F32)) * jnp.dot(ya.astype(BF16), wb_ref[0], preferred_element_type=F32)
    merged += _sigmoid(mg_ref[:, d:2 * d].astype(F32)) * jnp.dot(yb_ref[...], wb_ref[1], preferred_element_type=F32)
    merged += _sigmoid(mg_ref[:, 2 * d:3 * d].astype(F32)) * jnp.dot(yc_ref[...], wb_ref[2], preferred_element_type=F32)
    o_ref[...] = x_ref[...] + jnp.dot(merged.astype(BF16), wo_ref[...], preferred_element_type=F32)


def _merge(o_c, o_s, o_w, a_g, y_b, y_c, m_g, x, w_branch, w_out, tm=512):
    t, d = x.shape
    row = lambda i: (i, 0)
    lane = jnp.arange(LANES)[:, None]
    col = jnp.arange(d)[None, :]
    expand = jnp.stack([(lane == 3 * (col // HEAD_DIM) + k) for k in range(3)]).astype(BF16)
    return pl.pallas_call(
        _merge_kernel, grid=(t // tm,),
        in_specs=[pl.BlockSpec((tm, d), row), pl.BlockSpec((tm, d), row), pl.BlockSpec((tm, d), row),
                  pl.BlockSpec((tm, LANES), row), pl.BlockSpec((3, LANES, d), lambda i: (0, 0, 0)),
                  pl.BlockSpec((tm, d), row), pl.BlockSpec((tm, d), row), pl.BlockSpec((tm, 3 * d), row),
                  pl.BlockSpec((tm, d), row),
                  pl.BlockSpec((3, d, d), lambda i: (0, 0, 0)), pl.BlockSpec((d, d), lambda i: (0, 0))],
        out_specs=pl.BlockSpec((tm, d), row),
        out_shape=jax.ShapeDtypeStruct((t, d), F32),
        compiler_params=_params(("parallel",)), name="merge",
    )(o_c.reshape(t, d), o_s.reshape(t, d), o_w.reshape(t, d), a_g, expand, y_b, y_c, m_g, x,
      w_branch.astype(BF16), w_out.astype(BF16))


class _Interleaver:
    def __init__(self, pieces, every):
        self.pieces, self.every, self.count = list(pieces), every, 0

    def tick(self):
        self.count += 1
        if self.pieces and self.count % self.every == 0:
            self.pieces.pop(0)()

    def drain(self):
        while self.pieces:
            self.pieces.pop(0)()


def _sorting_network(n):
    pairs = []

    def merge(lo, m, r):
        step = r * 2
        if step < m:
            merge(lo, m, step)
            merge(lo + r, m, step)
            pairs.extend((i, i + r) for i in range(lo + r, lo + m - r, step))
        else:
            pairs.append((lo, lo + r))

    def sort(lo, m):
        if m > 1:
            sort(lo, m // 2)
            sort(lo + m // 2, m // 2)
            merge(lo, m, 1)

    sort(0, n)
    return pairs


def _topk_rows(s, k, tick):
    n, lanes = s.shape
    assert n == 8 * k
    sub = lax.broadcasted_iota(I32, (8, lanes), 0)
    vals = [s[8 * j:8 * j + 8, :] for j in range(k)]
    ids = [sub + 8 * j for j in range(k)]
    for count, (a, b) in enumerate(_sorting_network(k)):
        first = (vals[a] > vals[b]) | ((vals[a] == vals[b]) & (ids[a] < ids[b]))
        vals[a], vals[b] = jnp.where(first, vals[a], vals[b]), jnp.where(first, vals[b], vals[a])
        ids[a], ids[b] = jnp.where(first, ids[a], ids[b]), jnp.where(first, ids[b], ids[a])
        if count % 16 == 15:
            tick()
    top_v, top_i = [], []
    for r in range(k):
        best = jnp.max(vals[0], axis=0, keepdims=True)
        row = jnp.min(jnp.where(vals[0] == best, ids[0], n), axis=0, keepdims=True)
        top_v.append(best)
        top_i.append(row)
        won = ids[0] == row
        for d in range(k - 1 - r):
            vals[d] = jnp.where(won, vals[d + 1], vals[d])
            ids[d] = jnp.where(won, ids[d + 1], ids[d])
        tick()
    return jnp.concatenate(top_v, axis=0), jnp.concatenate(top_i, axis=0)


def _top_pairs(s1, s2, k, tick):
    lanes = s1.shape[1]
    sub = lax.broadcasted_iota(I32, (8, lanes), 0)
    lists = [jnp.where(sub <= k // (d + 1) - 1, s1[0:8, :] + s2[d:d + 1, :], -jnp.inf) for d in range(k)]
    tail = s1[8:16, :] + s2[0:1, :]
    tail_pos = (sub + 8) * k
    taken = jnp.zeros((8, lanes), I32)
    vs, aa, bb = [], [], []
    for r in range(k):
        best = jnp.maximum(jnp.max(lists[0], axis=0, keepdims=True), jnp.max(tail, axis=0, keepdims=True))
        head_pos = sub * k + taken
        pos = jnp.minimum(jnp.min(jnp.where(lists[0] == best, head_pos, k * k), axis=0, keepdims=True),
                          jnp.min(jnp.where(tail == best, tail_pos, k * k), axis=0, keepdims=True))
        vs.append(best)
        aa.append(pos // k)
        bb.append(pos % k)
        won = head_pos == pos
        tail = jnp.where(tail_pos == pos, -jnp.inf, tail)
        for d in range(k - 1 - r):
            lists[d] = jnp.where(won, lists[d + 1], lists[d])
        taken = taken + won.astype(I32)
        tick()
    return jnp.concatenate(vs, axis=0), jnp.concatenate(aa, axis=0), jnp.concatenate(bb, axis=0)


def _pick_rows(table, sel, k):
    out = jnp.zeros(sel.shape, table.dtype)
    for a in range(k):
        out = jnp.where(sel == a, table[a:a + 1, :], out)
    return out


def _route_head(qt, keys_ref, tick):
    k = PEER_TOPK
    assert k == 16 and PEER_KEYS == 8 * k
    tops = []
    for half in range(2):
        sc = jnp.dot(keys_ref[half], qt[half * PEER_HALF:(half + 1) * PEER_HALF, :], preferred_element_type=F32)
        tops.append(_topk_rows(sc, k, tick))
    (s1, i1), (s2, i2) = tops
    sc, a_sel, b_sel = _top_pairs(s1, s2, k, tick)
    e = jnp.exp(sc - sc[0:1, :])
    g = e / jnp.sum(e, axis=0, keepdims=True)
    return _pick_rows(i1, a_sel, k), _pick_rows(i2, b_sel, k), g


def _peer_route_u_kernel(hn_ref, hc_ref, wq_ref, keys_ref, u_ref, i1_ref, i2_ref, g_ref, act_ref,
                         i1t_ref, i2t_ref, gt_ref, i1c_ref, i2c_ref, *, blocks):
    o = pl.program_id(0)
    c = pl.program_id(1)
    k = PEER_TOPK

    @pl.when((o == 0) & (c == 0))
    def _():
        i1c_ref[...] = jnp.zeros(i1c_ref.shape, I32)
        i2c_ref[...] = jnp.zeros(i2c_ref.shape, I32)

    @pl.when(c == 0)
    def _():
        act_ref[...] = jnp.zeros(act_ref.shape, F32)

    i1c = i1c_ref[...]
    i2c = i2c_ref[...]
    hc = hc_ref[...]
    acc = [act_ref[...]]

    def piece(pc):
        def run():
            a = lax.dot_general(hc, u_ref[pl.ds(pc * 2 * PEER_KEYS, 2 * PEER_KEYS), :], (((1,), (1,)), ((), ())),
                                preferred_element_type=F32)
            for sub in range(2):
                got = jnp.take_along_axis(a[:, sub * PEER_KEYS:(sub + 1) * PEER_KEYS], i2c, axis=1,
                                          mode="promise_in_bounds")
                acc[0] = jnp.where(i1c == c * blocks + 2 * pc + sub, got, acc[0])
        return run

    tm = hn_ref.shape[0]
    groups = tm // ROUTE_LANES
    hps = keys_ref.shape[0] // 2
    hq = wq_ref.shape[0] // hps
    ticks = hps * groups * (2 * (len(_sorting_network(k)) // 16 + k) + k)
    pieces = _Interleaver([piece(pc) for pc in range(blocks // 2)], every=ticks // (blocks // 2 + ROUTE_SLACK))
    qt = lax.dot_general(wq_ref[...], hn_ref[...], (((1,), (1,)), ((), ())), preferred_element_type=F32)
    qt = qt.astype(BF16)
    for hd in range(hps):
        rows = pl.ds(pl.multiple_of((c * hps + hd) * k, k), k)
        for gi in range(groups):
            cols = slice(gi * ROUTE_LANES, (gi + 1) * ROUTE_LANES)
            i1, i2, g = _route_head(qt[hd * hq:(hd + 1) * hq, cols], keys_ref.at[pl.ds(2 * hd, 2)], pieces.tick)
            i1t_ref[rows, cols] = i1
            i2t_ref[rows, cols] = i2
            gt_ref[rows, cols] = g
    pieces.drain()
    act_ref[...] = acc[0]

    @pl.when(c == pl.num_programs(1) - 1)
    def _():
        i1n = i1t_ref[...].T
        i2n = i2t_ref[...].T
        i1_ref[...] = i1n
        i2_ref[...] = i2n
        g_ref[...] = gt_ref[...].T
        i1c_ref[...] = i1n
        i2c_ref[...] = i2n


def _peer_route_u(h, w_q, sub_keys, u, tm=512, hps=4):
    t, d = h.shape
    nt = t // tm
    ne = u.shape[0]
    nchunk = PEER_HEADS // hps
    ec = ne // nchunk
    blocks = ec // PEER_KEYS
    hq = hps * (w_q.shape[1] // PEER_HEADS)
    wq_t = w_q.T.astype(BF16)
    keys = sub_keys.reshape(PEER_HEADS * 2, PEER_KEYS, PEER_HALF).astype(BF16)
    slots = PEER_HEADS * PEER_TOPK
    nxt = lambda o, c: (jnp.minimum(o, nt - 1), 0)
    cur = lambda o, c: (jnp.maximum(o - 1, 0), 0)
    kern = functools.partial(_peer_route_u_kernel, blocks=blocks)
    return pl.pallas_call(
        kern, grid=(nt + 1, nchunk),
        in_specs=[pl.BlockSpec((tm, d), nxt), pl.BlockSpec((tm, d), cur),
                  pl.BlockSpec((hq, d), lambda o, c: (c, 0)),
                  pl.BlockSpec((2 * hps, PEER_KEYS, PEER_HALF), lambda o, c: (c, 0, 0)),
                  pl.BlockSpec((ec, d), lambda o, c: (c, 0))],
        out_specs=[pl.BlockSpec((tm, slots), nxt)] * 3 + [pl.BlockSpec((tm, slots), cur)],
        out_shape=[jax.ShapeDtypeStruct((t, slots), I32), jax.ShapeDtypeStruct((t, slots), I32),
                   jax.ShapeDtypeStruct((t, slots), F32), jax.ShapeDtypeStruct((t, slots), F32)],
        scratch_shapes=[pltpu.VMEM((slots, tm), I32), pltpu.VMEM((slots, tm), I32), pltpu.VMEM((slots, tm), F32),
                        pltpu.VMEM((tm, slots), I32), pltpu.VMEM((tm, slots), I32)],
        compiler_params=_params(("arbitrary", "arbitrary")), name="peer_route_u",
    )(h, h, wq_t, keys, u)


def _peer_v_kernel(act_ref, g_ref, i1_ref, i2_ref, v_ref, x_ref, o_ref, w_ref, wg_ref, *, tm, blocks):
    c = pl.program_id(1)
    nk = PEER_KEYS

    @pl.when(c == 0)
    def _():
        w_ref[...] = g_ref[...] * _gelu(act_ref[...])
        o_ref[...] = x_ref[...]
        sub = lax.broadcasted_iota(I32, (nk, w_ref.shape[1]), 0)

        def per_token(t, carry):
            wrow = w_ref[pl.ds(t, 1), :]
            lhs = jnp.where(i1_ref[pl.ds(t, 1), :] == sub, wrow, 0.0).astype(BF16)
            rhs = jnp.where(i2_ref[pl.ds(t, 1), :] == sub, 1.0, 0.0).astype(BF16)
            grid = lax.dot_general(lhs, rhs, (((1,), (1,)), ((), ())), preferred_element_type=F32)
            wg_ref[pl.ds(pl.multiple_of(t * WG_PITCH, 8), nk), :] = grid
            return carry

        lax.fori_loop(0, tm, per_token, 0, unroll=64)

    acc = jnp.zeros(o_ref.shape, F32)
    for bk in range(0, blocks, 2):
        i1 = c * blocks + bk
        lhs = jnp.concatenate([wg_ref[pl.ds(i1, tm, stride=WG_PITCH), :],
                               wg_ref[pl.ds(i1 + 1, tm, stride=WG_PITCH), :]], axis=1).astype(BF16)
        rhs = v_ref[bk:bk + 2].reshape(2 * nk, v_ref.shape[2])
        acc += jnp.dot(lhs, rhs, preferred_element_type=F32)
    o_ref[...] += acc


def _peer_v(act, g, i1, i2, v, x, tm=256, blocks=128):
    t, d = x.shape
    slots = g.shape[1]
    nk = PEER_KEYS
    v3 = v.reshape(nk, nk, d)
    kern = functools.partial(_peer_v_kernel, tm=tm, blocks=blocks)
    row = lambda i, c: (i, 0)
    return pl.pallas_call(
        kern, grid=(t // tm, nk // blocks),
        in_specs=[pl.BlockSpec((tm, slots), row),
                  pl.BlockSpec((tm, slots), row), pl.BlockSpec((tm, slots), row), pl.BlockSpec((tm, slots), row),
                  pl.BlockSpec((blocks, nk, d), lambda i, c: (c, 0, 0),
                               **({"pipeline_mode": pl.Buffered(1)} if blocks == nk else {})),
                  pl.BlockSpec((tm, d), row)],
        out_specs=pl.BlockSpec((tm, d), row),
        out_shape=jax.ShapeDtypeStruct((t, d), F32),
        scratch_shapes=[pltpu.VMEM((tm, slots), F32), pltpu.VMEM((tm * WG_PITCH, nk), F32)],
        compiler_params=_params(("parallel", "arbitrary")), name="peer_v",
    )(act, g, i1, i2, v3, x)


def _pad_cols(w, n):
    return jnp.pad(w, ((0, 0), (0, n - w.shape[1])))


def _in_proj_kernel(x_ref, g_ref, w_ref, *rest, bounds):
    outs, xn_ref = rest[:-1], rest[-1]
    j = pl.program_id(1)

    @pl.when(j == 0)
    def _():
        x = x_ref[...]
        ms = jnp.mean(x * x, axis=-1, keepdims=True)
        xn_ref[...] = (x * lax.rsqrt(ms + EPS) * g_ref[...]).astype(xn_ref.dtype)

    for (lo, hi), o_ref in zip(bounds, outs):
        @pl.when((j >= lo) & (j < hi))
        def _(o_ref=o_ref):
            o_ref[...] = jnp.dot(xn_ref[...], w_ref[...], preferred_element_type=F32).astype(o_ref.dtype)


PROJ_TILE = 512
IN_GROUPS = ((1024, BF16), (768, BF16), (48, F32), (1024, BF16), (1024, BF16), (1024, BF16), (1536, BF16),
             (16, F32), (3072, BF16))


def _in_proj(x, g, w_in, tm=1024):
    t, d = x.shape
    tn = PROJ_TILE
    ws, bounds, shapes, off, tile = [], [], [], 0, 0
    for n, dtype in IN_GROUPS:
        width = -(-n // tn) * tn
        ws.append(_pad_cols(w_in[:, off:off + n], width))
        bounds.append((tile, tile + width // tn))
        shapes.append(jax.ShapeDtypeStruct((t, width), dtype))
        off += n
        tile += width // tn
    w_all = jnp.concatenate(ws, axis=1).astype(BF16)
    out_specs = [pl.BlockSpec((tm, tn), functools.partial(lambda i, j, lo, hi: (i, jnp.clip(j - lo, 0, hi - lo - 1)),
                                                          lo=lo, hi=hi)) for lo, hi in bounds]
    kern = functools.partial(_in_proj_kernel, bounds=tuple(bounds))
    return pl.pallas_call(
        kern, grid=(t // tm, tile),
        in_specs=[pl.BlockSpec((tm, d), lambda i, j: (i, 0)), pl.BlockSpec((1, d), lambda i, j: (0, 0)),
                  pl.BlockSpec((d, tn), lambda i, j: (0, j))],
        out_specs=out_specs, out_shape=shapes,
        scratch_shapes=[pltpu.VMEM((tm, d), BF16)],
        compiler_params=_params(("parallel", "arbitrary")), name="in_proj",
    )(x, g.reshape(1, d).astype(F32), w_all)


def _mixer(x, positions, b, s, p):
    a_q, a_kv, a_g, l_x, l_g, s_z, s_xbc, s_dt, m_g = _in_proj(x, p["mix_norm_g"], p["w_in"])

    q, kc, vc, ks, vs, kw, vw = _nsa_prep(a_q, a_kv, positions, p["q_norm_g"], p["k_norm_g"], b, s)
    k_cmp, v_cmp = _compress(kc, vc, p["cmp_pe_k"], p["cmp_pe_v"], p["cmp_k_w1"], p["cmp_k_w2"],
                             p["cmp_v_w1"], p["cmp_v_w2"], p["k_norm_g"])
    o_c, bias = _cmp_select(q, k_cmp, v_cmp, s)
    o_s = _sel_attn(q, bias, ks, vs, s)
    o_w = _win_attn(q, kw, vw, s)
    y_b = _rglru(l_x, l_g, positions, p["lru_conv_w"], p["lru_conv_b"], p["lru_w_r"], p["lru_b_r"],
                 p["lru_w_i"], p["lru_b_i"], p["lru_lambda"], b, s)
    y_c = _ssd(s_z, s_xbc, s_dt, p["ssm_conv_w"], p["ssm_conv_b"], p["ssm_dt_bias"], p["ssm_a_log"],
               p["ssm_d"], p["ssm_norm_g"], b, s)
    return _merge(o_c, o_s, o_w, a_g, y_b, y_c, m_g, x, p["w_branch"], p["w_out"])


def _peer(x, p):
    h = _rmsnorm(x, p["ffn_norm_g"])
    i1, i2, g, act = _peer_route_u(h, p["peer_w_q"], p["peer_sub_keys"], p["peer_u"].astype(BF16))
    return _peer_v(act, g, i1, i2, p["peer_v"].astype(BF16), x)


_LAYER_PARAMS = ("mix_norm_g", "w_in", "q_norm_g", "k_norm_g", "cmp_pe_k", "cmp_pe_v", "cmp_k_w1", "cmp_k_w2",
                 "cmp_v_w1", "cmp_v_w2", "lru_conv_w", "lru_conv_b", "lru_w_r", "lru_b_r", "lru_w_i", "lru_b_i",
                 "lru_lambda", "ssm_conv_w", "ssm_conv_b", "ssm_dt_bias", "ssm_a_log", "ssm_d", "ssm_norm_g",
                 "w_branch", "w_out", "ffn_norm_g", "peer_w_q", "peer_sub_keys", "peer_u", "peer_v")


def kernel(x, positions, mix_norm_g, w_in, q_norm_g, k_norm_g, cmp_pe_k, cmp_pe_v, cmp_k_w1, cmp_k_w2, cmp_v_w1, cmp_v_w2, lru_conv_w, lru_conv_b, lru_w_r, lru_b_r, lru_w_i, lru_b_i, lru_lambda, ssm_conv_w, ssm_conv_b, ssm_dt_bias, ssm_a_log, ssm_d, ssm_norm_g, w_branch, w_out, ffn_norm_g, peer_w_q, peer_sub_keys, peer_u, peer_v):
    stacked = dict(zip(_LAYER_PARAMS, (mix_norm_g, w_in, q_norm_g, k_norm_g, cmp_pe_k, cmp_pe_v, cmp_k_w1,
                                       cmp_k_w2, cmp_v_w1, cmp_v_w2, lru_conv_w, lru_conv_b, lru_w_r, lru_b_r,
                                       lru_w_i, lru_b_i, lru_lambda, ssm_conv_w, ssm_conv_b, ssm_dt_bias,
                                       ssm_a_log, ssm_d, ssm_norm_g, w_branch, w_out, ffn_norm_g, peer_w_q,
                                       peer_sub_keys, peer_u, peer_v)))
    b, s, d = x.shape
    xf = x.reshape(b * s, d).astype(F32)
    for layer in range(mix_norm_g.shape[0]):
        p = {name: arr[layer] for name, arr in stacked.items()}
        xf = _mixer(xf, positions, b, s, p)
        xf = _peer(xf, p)
    return xf.reshape(b, s, d).astype(x.dtype)
```

```python
import functools
import math

import jax
import jax.numpy as jnp
from jax import lax
from jax.experimental import pallas as pl
from jax.experimental.pallas import tpu as pltpu

F32 = jnp.float32
BF16 = jnp.bfloat16
I32 = jnp.int32

D_MODEL = 1024
HEAD_DIM = 64
N_HEADS = 16
N_KV_GROUPS = 2
HEADS_PER_GROUP = 8
CMP_BLOCK = 32
CMP_STRIDE = 16
CMP_HIDDEN = 256
SEL_BLOCK = 64
N_SELECT = 16
WINDOW = 512
SEL_FORCE = 100.0
ROPE_THETA = 10000.0
SCALE = HEAD_DIM ** -0.5
LOG2E = math.log2(math.e)
LRU_C = 8.0
CONV_WIDTH = 4
SSM_HEADS = 16
SSM_HEAD_DIM = 64
SSM_GROUPS = 2
SSM_STATE = 128
SSM_CHUNK = 128
SSM_INNER = 1024
PEER_HEADS = 8
PEER_KEYS = 128
PEER_HALF = 128
PEER_TOPK = 16
EPS = 1e-6
NEG = -1e30
LANES = 128

VMEM_LIMIT = 56 * 1024 * 1024
SEL_LOOKAHEAD = 5
WIN_SUBTILE = 256
WIN_LOOKAHEAD = 3
SEL_SUBTILE = 256
SEL_ACC_ROWS = HEAD_DIM + 16
WG_PITCH = PEER_KEYS + 8
ROUTE_SLACK = 3
ROUTE_LANES = 128


def _params(sem):
    return pltpu.CompilerParams(dimension_semantics=sem, vmem_limit_bytes=VMEM_LIMIT)


def _gelu(x):
    return 0.5 * x * (1.0 + jnp.tanh(math.sqrt(2.0 / math.pi) * (x + 0.044715 * x * x * x)))


def _sigmoid(x):
    return 0.5 * jnp.tanh(0.5 * x) + 0.5


def _softplus(x):
    return jnp.maximum(x, 0.0) + jnp.log(1.0 + jnp.exp(-jnp.abs(x)))


def _rmsnorm_kernel(x_ref, g_ref, o_ref):
    x = x_ref[...].astype(F32)
    ms = jnp.mean(x * x, axis=-1, keepdims=True)
    o_ref[...] = (x * lax.rsqrt(ms + EPS) * g_ref[...]).astype(o_ref.dtype)


def _rmsnorm(x, g, tm=512):
    t, d = x.shape
    return pl.pallas_call(
        _rmsnorm_kernel, grid=(t // tm,),
        in_specs=[pl.BlockSpec((tm, d), lambda i: (i, 0)), pl.BlockSpec((1, d), lambda i: (0, 0))],
        out_specs=pl.BlockSpec((tm, d), lambda i: (i, 0)),
        out_shape=jax.ShapeDtypeStruct((t, d), BF16),
        compiler_params=_params(("parallel",)), name="rmsnorm")(x, g.reshape(1, d).astype(F32))


def _norm_rope(x, g, cos, sin_signed, bd):
    ms = jnp.dot((x * x).astype(BF16), bd, preferred_element_type=F32)
    y = x * lax.rsqrt(ms + EPS) * g
    lane = lax.broadcasted_iota(I32, y.shape, 1)
    first_half = (lane % HEAD_DIM) < (HEAD_DIM // 2)
    partner = jnp.where(first_half, pltpu.roll(y, LANES - HEAD_DIM // 2, 1), pltpu.roll(y, HEAD_DIM // 2, 1))
    return y * cos + partner * sin_signed


def _nsa_prep_kernel(aq_ref, akv_ref, pos_ref, inv_ref, sgn_ref, gq_ref, gk_ref, bd_ref,
                     q_ref, kc_ref, vc_ref, ks_ref, vs_ref, kw_ref, vw_ref):
    ang = pos_ref[...].astype(F32) * inv_ref[...]
    cos = jnp.cos(ang)
    sin_signed = jnp.sin(ang) * sgn_ref[...]
    bd = bd_ref[...]
    for c in range(D_MODEL // LANES):
        x = aq_ref[:, c * LANES:(c + 1) * LANES].astype(F32)
        y = (_norm_rope(x, gq_ref[...], cos, sin_signed, bd) * (SCALE * LOG2E)).astype(q_ref.dtype)
        q_ref[0, 2 * c] = y[:, :HEAD_DIM]
        q_ref[0, 2 * c + 1] = y[:, HEAD_DIM:]
    outs = (kc_ref, vc_ref, ks_ref, vs_ref, kw_ref, vw_ref)
    for c, o_ref in enumerate(outs):
        x = akv_ref[:, c * LANES:(c + 1) * LANES]
        if c % 2 == 0:
            y = _norm_rope(x.astype(F32), gk_ref[...], cos, sin_signed, bd).astype(o_ref.dtype)
        else:
            y = x.astype(o_ref.dtype)
        o_ref[0, 0] = y[:, :HEAD_DIM]
        o_ref[0, 1] = y[:, HEAD_DIM:]


def _nsa_prep(a_q, a_kv, positions, q_norm_g, k_norm_g, b, s, tm=512):
    half = HEAD_DIM // 2
    lane = jnp.arange(LANES)
    inv = (ROPE_THETA ** (-((lane % half).astype(F32)) / half)).reshape(1, LANES)
    sgn = jnp.where((lane % HEAD_DIM) < half, -1.0, 1.0).astype(F32).reshape(1, LANES)
    bd = jnp.where((lane[:, None] // HEAD_DIM) == (lane[None, :] // HEAD_DIM), 1.0 / HEAD_DIM, 0.0).astype(BF16)
    gq = jnp.tile(q_norm_g.astype(F32), 2).reshape(1, LANES)
    gk = jnp.tile(k_norm_g.astype(F32), 2).reshape(1, LANES)
    nt = s // tm
    row = lambda i: (i, 0)
    const = lambda i: (0, 0)
    kv_shape = jax.ShapeDtypeStruct((b, N_KV_GROUPS, s, HEAD_DIM), BF16)
    kv_spec = pl.BlockSpec((1, N_KV_GROUPS, tm, HEAD_DIM), lambda i: (i // nt, 0, i % nt, 0))
    return pl.pallas_call(
        _nsa_prep_kernel, grid=(b * nt,),
        in_specs=[pl.BlockSpec((tm, D_MODEL), row), pl.BlockSpec((tm, 6 * LANES), row),
                  pl.BlockSpec((tm, 1), row), pl.BlockSpec((1, LANES), const), pl.BlockSpec((1, LANES), const),
                  pl.BlockSpec((1, LANES), const), pl.BlockSpec((1, LANES), const),
                  pl.BlockSpec((LANES, LANES), const)],
        out_specs=[pl.BlockSpec((1, N_HEADS, tm, HEAD_DIM), lambda i: (i // nt, 0, i % nt, 0))] + [kv_spec] * 6,
        out_shape=[jax.ShapeDtypeStruct((b, N_HEADS, s, HEAD_DIM), BF16)] + [kv_shape] * 6,
        compiler_params=_params(("parallel",)), name="nsa_prep",
    )(a_q, a_kv, positions.reshape(b * s, 1).astype(I32), inv, sgn, gq, gk, bd)


def _compress_kernel(uk_ref, uv_ref, pek_ref, pev_ref, kw1_ref, kw2_ref, vw1_ref, vw2_ref, gk_ref,
                     kc_ref, vc_ref):
    half = CMP_STRIDE * HEAD_DIM

    def mlp(u, pe, w1_ref, w2_ref):
        n = u.shape[0]
        ha = jnp.dot(u, w1_ref[:half, :], preferred_element_type=F32)
        hb = jnp.dot(u, w1_ref[half:, :], preferred_element_type=F32)
        bias = jnp.dot(pe, w1_ref[...], preferred_element_type=F32)[0:1, :]
        pre = ha + pltpu.roll(hb, n - 1, 0) + bias
        return jnp.dot(_gelu(pre).astype(BF16), w2_ref[...], preferred_element_type=F32)

    k = mlp(uk_ref[0, 0], pek_ref[...], kw1_ref, kw2_ref)
    ms = jnp.mean(k * k, axis=-1, keepdims=True)
    kc_ref[0, 0] = (k * lax.rsqrt(ms + EPS) * gk_ref[...]).astype(kc_ref.dtype)
    vc_ref[0, 0] = mlp(uv_ref[0, 0], pev_ref[...], vw1_ref, vw2_ref).astype(vc_ref.dtype)


def _compress(kc, vc, pe_k, pe_v, kw1, kw2, vw1, vw2, k_norm_g):
    b, g, s, dh = kc.shape
    ng = s // CMP_STRIDE
    wide = CMP_BLOCK * dh
    uk = kc.reshape(b, g, ng, CMP_STRIDE * dh)
    uv = vc.reshape(b, g, ng, CMP_STRIDE * dh)
    pek = jnp.zeros((8, wide), BF16).at[0].set(pe_k.reshape(wide).astype(BF16))
    pev = jnp.zeros((8, wide), BF16).at[0].set(pe_v.reshape(wide).astype(BF16))
    u_spec = pl.BlockSpec((1, 1, ng, CMP_STRIDE * dh), lambda i, j: (i, j, 0, 0))
    c2 = lambda i, j: (0, 0)
    o_spec = pl.BlockSpec((1, 1, ng, dh), lambda i, j: (i, j, 0, 0))
    o_shape = jax.ShapeDtypeStruct((b, g, ng, dh), BF16)
    return pl.pallas_call(
        _compress_kernel, grid=(b, g),
        in_specs=[u_spec, u_spec, pl.BlockSpec((8, wide), c2), pl.BlockSpec((8, wide), c2),
                  pl.BlockSpec((wide, CMP_HIDDEN), c2), pl.BlockSpec((CMP_HIDDEN, dh), c2),
                  pl.BlockSpec((wide, CMP_HIDDEN), c2), pl.BlockSpec((CMP_HIDDEN, dh), c2),
                  pl.BlockSpec((1, dh), c2)],
        out_specs=[o_spec, o_spec], out_shape=[o_shape, o_shape],
        compiler_params=_params(("parallel", "parallel")), name="nsa_compress",
    )(uk, uv, pek, pev, kw1.astype(BF16), kw2.astype(BF16), vw1.astype(BF16), vw2.astype(BF16),
      k_norm_g.reshape(1, dh).astype(F32))


def _store_head_pair(o_ref, h, o_t, held):
    if h % 2 == 0:
        return o_t
    pair = jnp.concatenate([held, o_t], axis=0).T
    o_ref[0, :, (h - 1) * HEAD_DIM:(h + 1) * HEAD_DIM] = pair.astype(o_ref.dtype)
    return None


def _cmp_select_kernel(q_ref, kc_ref, vc_ref, ovt_ref, o_ref, bias_ref, *, tq, n_cmp, n_pick):
    i = pl.program_id(2)
    chunk = min(LANES, kc_ref.shape[2])
    visible = ((i + 1) * tq - CMP_BLOCK) // CMP_STRIDE + 1
    nchunks = jnp.clip((visible + chunk - 1) // chunk, 1, kc_ref.shape[2] // chunk)
    for n in range(1, kc_ref.shape[2] // chunk + 1):
        pl.when(nchunks == n)(functools.partial(
            _cmp_select_keys, q_ref, kc_ref, vc_ref, ovt_ref, o_ref, bias_ref, nc=n * chunk, i=i, tq=tq,
            n_cmp=n_cmp, n_pick=n_pick))


def _cmp_select_keys(q_ref, kc_ref, vc_ref, ovt_ref, o_ref, bias_ref, *, nc, i, tq, n_cmp, n_pick):
    hg = q_ref.shape[1]
    nsel = ovt_ref.shape[0]
    kc = kc_ref[0, 0, :nc, :]
    vct = vc_ref[0, 0, :nc, :].astype(F32).T.astype(BF16)
    ovt = ovt_ref[:, :nc]
    c = lax.broadcasted_iota(I32, (nc, tq), 0)
    t = i * tq + lax.broadcasted_iota(I32, (nc, tq), 1)
    mask = ((CMP_STRIDE * c + CMP_BLOCK - 1) <= t) & (c < n_cmp)
    score = lambda h: lax.dot_general(kc, q_ref[0, h], (((1,), (1,)), ((), ())), preferred_element_type=F32)
    ahead = [score(h) for h in range(WIN_LOOKAHEAD)]
    psum = jnp.zeros((nc, tq), F32)
    held = None
    for h in range(hg):
        st = ahead.pop(0)
        if h + WIN_LOOKAHEAD < hg:
            ahead.append(score(h + WIN_LOOKAHEAD))
        st = jnp.where(mask, st, NEG)
        m = jnp.max(st, axis=0, keepdims=True)
        p = jnp.where(mask, jnp.exp2(st - m), 0.0)
        l = jnp.sum(p, axis=0, keepdims=True)
        p = p * jnp.where(l > 0.0, 1.0 / l, 0.0)
        o = jnp.dot(vct, p.astype(BF16), preferred_element_type=F32)
        held = _store_head_pair(o_ref, h, o, held)
        psum = psum + p

    hi = psum.astype(BF16)
    lo = (psum - hi.astype(F32)).astype(BF16)
    imp = jnp.dot(ovt, hi, preferred_element_type=F32) + jnp.dot(ovt, lo, preferred_element_type=F32)
    j = lax.broadcasted_iota(I32, (nsel, tq), 0)
    tt = i * tq + lax.broadcasted_iota(I32, (nsel, tq), 1)
    cur = tt // SEL_BLOCK
    forced = (j == 0) | (j == cur) | (j == cur - 1)
    sc = jnp.where(forced, SEL_FORCE, jnp.where(j * SEL_BLOCK <= tt, imp, -1.0))
    picked = jnp.zeros((nsel, tq), jnp.bool_)
    for _ in range(n_pick):
        mx = jnp.max(sc, axis=0, keepdims=True)
        idx = jnp.min(jnp.where(sc == mx, j, nsel), axis=0, keepdims=True)
        hit = j == idx
        picked = picked | hit
        sc = jnp.where(hit, -jnp.inf, sc)
    bias_ref[0, 0] = jnp.where(picked & (j <= cur), 0.0, NEG).T.astype(bias_ref.dtype)


def _cmp_select(q, k_cmp, v_cmp, s, tq=512):
    b, nh, _, dh = q.shape
    g, hg = N_KV_GROUPS, HEADS_PER_GROUP
    nc = k_cmp.shape[2]
    n_cmp = (s - CMP_BLOCK) // CMP_STRIDE + 1
    nsel = s // SEL_BLOCK
    n_pick = min(N_SELECT, nsel)
    cs = CMP_STRIDE * jnp.arange(nc)
    ss = SEL_BLOCK * jnp.arange(nsel)
    ovt = jnp.clip(jnp.minimum(cs[None, :] + CMP_BLOCK, ss[:, None] + SEL_BLOCK)
                   - jnp.maximum(cs[None, :], ss[:, None]), 0).astype(F32) / CMP_BLOCK
    ovt = jnp.where(jnp.arange(nc)[None, :] < n_cmp, ovt, 0.0).astype(BF16)
    kern = functools.partial(_cmp_select_kernel, tq=tq, n_cmp=n_cmp, n_pick=n_pick)
    return pl.pallas_call(
        kern, grid=(b, g, s // tq),
        in_specs=[pl.BlockSpec((1, hg, tq, dh), lambda bi, gi, i: (bi, gi, i, 0)),
                  pl.BlockSpec((1, 1, nc, dh), lambda bi, gi, i: (bi, gi, 0, 0)),
                  pl.BlockSpec((1, 1, nc, dh), lambda bi, gi, i: (bi, gi, 0, 0)),
                  pl.BlockSpec((nsel, nc), lambda bi, gi, i: (0, 0))],
        out_specs=[pl.BlockSpec((1, tq, hg * dh), lambda bi, gi, i: (bi, i, gi)),
                   pl.BlockSpec((1, 1, tq, nsel), lambda bi, gi, i: (bi, gi, i, 0))],
        out_shape=[jax.ShapeDtypeStruct((b, s, nh * dh), BF16),
                   jax.ShapeDtypeStruct((b, g, s, nsel), BF16)],
        compiler_params=_params(("parallel", "parallel", "parallel")), name="nsa_cmp_select",
    )(q, k_cmp, v_cmp, ovt)


def _sel_attn_kernel(it_ref, jt_ref, q_ref, bias_ref, k_ref, v_ref, o_ref, qa_ref, m_ref, acc_ref, *, tq, tk):
    pid = pl.program_id(2)
    i = it_ref[pid]
    j = jt_ref[pid]
    hg = q_ref.shape[1]
    nsel = bias_ref.shape[3]
    last_j = ((i + 1) * tq - 1) // tk

    @pl.when(j == 0)
    def _():
        for h in range(hg):
            qa_ref[h, :, :nsel] = bias_ref[0, 0]
            qa_ref[h, :, nsel:] = q_ref[0, h]
        m_ref[...] = jnp.full(m_ref.shape, -jnp.inf, F32)
        acc_ref[...] = jnp.zeros(acc_ref.shape, F32)

    def accumulate(diagonal):
        key = j * tk + lax.broadcasted_iota(I32, (tk, nsel), 0)
        blk = lax.broadcasted_iota(I32, (tk, nsel), 1)
        onehot = jnp.where(key // SEL_BLOCK == blk, 1.0, 0.0).astype(BF16)
        ka = jnp.concatenate([onehot, k_ref[0, 0]], axis=1)
        va = jnp.concatenate([v_ref[0, 0].astype(F32), jnp.ones((tk, LANES - HEAD_DIM), F32)], axis=1)
        vat = va.T[:SEL_ACC_ROWS, :].astype(BF16)
        qw = min(SEL_SUBTILE, tq)
        items = [(h, b) for h in range(hg) for b in range(tq // qw)]
        rows = lambda b: (b + 1) * qw if diagonal else tk

        def score(item):
            h, b = item
            return lax.dot_general(ka[:rows(b), :], qa_ref[h, b * qw:(b + 1) * qw, :], (((1,), (1,)), ((), ())),
                                   preferred_element_type=F32)

        ahead = [score(item) for item in items[:SEL_LOOKAHEAD]]
        for n, (h, b) in enumerate(items):
            st = ahead.pop(0)
            if n + SEL_LOOKAHEAD < len(items):
                ahead.append(score(items[n + SEL_LOOKAHEAD]))
            if diagonal:
                kp = lax.broadcasted_iota(I32, st.shape, 0)
                t = b * qw + lax.broadcasted_iota(I32, st.shape, 1)
                st = jnp.where(kp <= t, st, NEG)
            lanes = slice(b * qw, (b + 1) * qw)
            m_old = m_ref[h, :, lanes]
            m_new = jnp.maximum(m_old, jnp.max(st, axis=0, keepdims=True))
            alpha = jnp.exp2(m_old - m_new)
            p = jnp.exp2(st - m_new)
            acc_ref[h, :, lanes] = alpha * acc_ref[h, :, lanes] + jnp.dot(
                vat[:, :rows(b)], p.astype(BF16), preferred_element_type=F32)
            m_ref[h, :, lanes] = m_new

    @pl.when(j < last_j)
    def _():
        accumulate(False)

    @pl.when(j == last_j)
    def _():
        accumulate(True)
        held = None
        for h in range(hg):
            acc = acc_ref[h]
            o = acc[:HEAD_DIM, :] / acc[HEAD_DIM:HEAD_DIM + 1, :]
            held = _store_head_pair(o_ref, h, o, held)


def _sel_attn(q, bias, k_s, v_s, s, tq=1024):
    tk = tq
    b, nh, _, dh = q.shape
    g, hg = N_KV_GROUPS, HEADS_PER_GROUP
    nsel = bias.shape[3]
    pairs = [(i, j) for i in range(s // tq) for j in range(((i + 1) * tq - 1) // tk + 1)]
    it = jnp.asarray([pr[0] for pr in pairs], I32)
    jt = jnp.asarray([pr[1] for pr in pairs], I32)
    q_map = lambda bi, gi, pid, it_ref, jt_ref: (bi, gi, it_ref[pid], 0)
    kv_map = lambda bi, gi, pid, it_ref, jt_ref: (bi, gi, jt_ref[pid], 0)
    kern = functools.partial(_sel_attn_kernel, tq=tq, tk=tk)
    grid_spec = pltpu.PrefetchScalarGridSpec(
        num_scalar_prefetch=2, grid=(b, g, len(pairs)),
        in_specs=[pl.BlockSpec((1, hg, tq, dh), q_map), pl.BlockSpec((1, 1, tq, nsel), q_map),
                  pl.BlockSpec((1, 1, tk, dh), kv_map), pl.BlockSpec((1, 1, tk, dh), kv_map)],
        out_specs=pl.BlockSpec((1, tq, hg * dh), lambda bi, gi, pid, it_ref, jt_ref: (bi, it_ref[pid], gi)),
        scratch_shapes=[pltpu.VMEM((hg, tq, nsel + dh), BF16), pltpu.VMEM((hg, 1, tq), F32),
                        pltpu.VMEM((hg, SEL_ACC_ROWS, tq), F32)])
    return pl.pallas_call(
        kern, grid_spec=grid_spec, out_shape=jax.ShapeDtypeStruct((b, s, nh * dh), BF16),
        compiler_params=_params(("parallel", "parallel", "arbitrary")), name="nsa_sel_attn",
    )(it, jt, q, bias, k_s, v_s)


def _win_attn_kernel(q_ref, *refs, tq, nwin):
    k_refs, v_refs, o_ref = refs[:nwin], refs[nwin:2 * nwin], refs[2 * nwin]
    i = pl.program_id(2)
    hg = q_ref.shape[1]
    k = jnp.concatenate([r[0, 0] for r in k_refs], axis=0)
    vt = jnp.concatenate([r[0, 0] for r in v_refs], axis=0).astype(F32).T.astype(BF16)
    qw = WIN_SUBTILE
    nk = WINDOW + qw
    r = lax.broadcasted_iota(I32, (nk, qw), 0)
    c = lax.broadcasted_iota(I32, (nk, qw), 1)
    band = (r > c) & (r <= c + WINDOW)
    items = [(h, b) for h in range(hg) for b in range(tq // qw)]

    def score(item):
        h, b = item
        return lax.dot_general(k[b * qw:b * qw + nk, :], q_ref[0, h, b * qw:(b + 1) * qw, :],
                               (((1,), (1,)), ((), ())), preferred_element_type=F32)

    ahead = [score(item) for item in items[:WIN_LOOKAHEAD]]
    held, parts = None, []
    for n, (h, b) in enumerate(items):
        st = ahead.pop(0)
        if n + WIN_LOOKAHEAD < len(items):
            ahead.append(score(items[n + WIN_LOOKAHEAD]))
        mask = band & (i * tq - WINDOW + b * qw + r >= 0)
        st = jnp.where(mask, st, NEG)
        m = jnp.max(st, axis=0, keepdims=True)
        p = jnp.where(mask, jnp.exp2(st - m), 0.0)
        l = jnp.sum(p, axis=0, keepdims=True)
        parts.append(jnp.dot(vt[:, b * qw:b * qw + nk], p.astype(BF16), preferred_element_type=F32) / l)
        if len(parts) == tq // qw:
            held = _store_head_pair(o_ref, h, jnp.concatenate(parts, axis=1), held)
            parts = []


def _win_attn(q, k_w, v_w, s, tq=512):
    b, nh, _, dh = q.shape
    g, hg = N_KV_GROUPS, HEADS_PER_GROUP
    nwin = WINDOW // tq + 1
    pad = ((0, 0), (0, 0), (WINDOW, 0), (0, 0))
    kp, vp = jnp.pad(k_w, pad), jnp.pad(v_w, pad)
    kv_specs = [pl.BlockSpec((1, 1, tq, dh), functools.partial(lambda bi, gi, i, w: (bi, gi, i + w, 0), w=w))
                for w in range(nwin)]
    kern = functools.partial(_win_attn_kernel, tq=tq, nwin=nwin)
    return pl.pallas_call(
        kern, grid=(b, g, s // tq),
        in_specs=[pl.BlockSpec((1, hg, tq, dh), lambda bi, gi, i: (bi, gi, i, 0))] + kv_specs + kv_specs,
        out_specs=pl.BlockSpec((1, tq, hg * dh), lambda bi, gi, i: (bi, i, gi)),
        out_shape=jax.ShapeDtypeStruct((b, s, nh * dh), BF16),
        compiler_params=_params(("parallel", "parallel", "parallel")), name="nsa_win_attn",
    )(q, *([kp] * nwin), *([vp] * nwin))


def _causal_conv(x, win_ref, w_ref, b_ref):
    n = x.shape[0]
    win_ref[8:, :] = x
    out = b_ref[...] + w_ref[CONV_WIDTH - 1:CONV_WIDTH, :] * x
    for k in range(CONV_WIDTH - 1):
        off = 8 - (CONV_WIDTH - 1) + k
        out = out + w_ref[k:k + 1, :] * win_ref[off:off + n, :]
    win_ref[0:8, :] = x[n - 8:, :]
    return out


def _rglru_kernel(x_ref, gate_ref, pos_ref, cw_ref, cb_ref, wr_ref, br_ref, wi_ref, bi_ref, lam_ref,
                  o_ref, tail_ref, h_ref):
    @pl.when(pl.program_id(1) == 0)
    def _():
        tail_ref[0:8, :] = jnp.zeros((8, tail_ref.shape[1]), F32)
        h_ref[...] = jnp.zeros(h_ref.shape, F32)

    n = x_ref.shape[0]
    x = x_ref[...].astype(F32)
    xc = _causal_conv(x, tail_ref, cw_ref, cb_ref)
    xcb = xc.astype(BF16)
    r = _sigmoid(jnp.dot(xcb, wr_ref[...], preferred_element_type=F32) + br_ref[...])
    gi = _sigmoid(jnp.dot(xcb, wi_ref[...], preferred_element_type=F32) + bi_ref[...])
    log_a = -LRU_C * r * _softplus(-lam_ref[...])
    reset = pos_ref[...] == 0
    a = jnp.where(reset, 0.0, jnp.exp(log_a))
    mult = jnp.where(reset, 1.0, jnp.sqrt(jnp.maximum(1.0 - jnp.exp(2.0 * log_a), 0.0)))
    bb = mult * (gi * xc)
    sub = lax.broadcasted_iota(I32, a.shape, 0) % 8
    d = 1
    while d < 8:
        a_sh = pltpu.roll(a, d, 0)
        b_sh = pltpu.roll(bb, d, 0)
        live = sub >= d
        bb = jnp.where(live, a * b_sh + bb, bb)
        a = jnp.where(live, a * a_sh, a)
        d *= 2
    carry = h_ref[...]
    hs = []
    for g in range(n // 8):
        h = bb[8 * g:8 * g + 8, :] + a[8 * g:8 * g + 8, :] * carry
        carry = h[7:8, :]
        hs.append(h)
    h_ref[...] = carry
    o_ref[...] = (jnp.concatenate(hs, axis=0) * _gelu(gate_ref[...].astype(F32))).astype(o_ref.dtype)


def _block_diag(w):
    nb, bs, _ = w.shape
    eye = jnp.eye(nb, dtype=w.dtype)
    return (w[:, :, None, :] * eye[:, None, :, None]).reshape(nb * bs, nb * bs)


def _rglru(l_x, l_g, positions, conv_w, conv_b, w_r, b_r, w_i, b_i, lam, b, s, ts=256):
    d = l_x.shape[1]
    nt = s // ts
    row = lambda bi, i: (bi * nt + i, 0)
    c2 = lambda bi, i: (0, 0)
    vec = lambda v: v.reshape(1, d).astype(F32)
    return pl.pallas_call(
        _rglru_kernel, grid=(b, nt),
        in_specs=[pl.BlockSpec((ts, d), row), pl.BlockSpec((ts, d), row), pl.BlockSpec((ts, 1), row),
                  pl.BlockSpec((CONV_WIDTH, d), c2), pl.BlockSpec((1, d), c2),
                  pl.BlockSpec((d, d), c2), pl.BlockSpec((1, d), c2),
                  pl.BlockSpec((d, d), c2), pl.BlockSpec((1, d), c2), pl.BlockSpec((1, d), c2)],
        out_specs=pl.BlockSpec((ts, d), row),
        out_shape=jax.ShapeDtypeStruct((b * s, d), BF16),
        scratch_shapes=[pltpu.VMEM((8 + ts, d), F32), pltpu.VMEM((1, d), F32)],
        compiler_params=_params(("parallel", "arbitrary")), name="rglru",
    )(l_x, l_g, positions.reshape(b * s, 1).astype(I32), conv_w.astype(F32), vec(conv_b),
      _block_diag(w_r).astype(BF16), vec(b_r), _block_diag(w_i).astype(BF16), vec(b_i), vec(lam))


def _ssd_kernel(z_ref, xbc_ref, dt_ref, cw_ref, cb_ref, dtb_ref, alog_ref, dfull_ref, ng_ref,
                o_ref, tail_ref, state_ref, y_ref):
    @pl.when(pl.program_id(1) == 0)
    def _():
        tail_ref[0:8, :] = jnp.zeros((8, tail_ref.shape[1]), F32)
        state_ref[...] = jnp.zeros(state_ref.shape, F32)

    n = xbc_ref.shape[0]
    hg = SSM_HEADS // SSM_GROUPS
    x = xbc_ref[...].astype(F32)
    xc = _causal_conv(x, tail_ref, cw_ref, cb_ref)
    xc = xc * _sigmoid(xc)
    xs = xc[:, :SSM_INNER]
    dt = _softplus(dt_ref[...] + dtb_ref[...])
    adt = dt * (-jnp.exp(alog_ref[...]))
    row = lax.broadcasted_iota(I32, adt.shape, 0)
    acs = adt
    d = 1
    while d < n:
        acs = acs + jnp.where(row >= d, pltpu.roll(acs, d, 0), 0.0)
        d *= 2
    acs_t = acs.T
    li = lax.broadcasted_iota(I32, (n, n), 0)
    si = lax.broadcasted_iota(I32, (n, n), 1)
    tri = li >= si
    for g in range(SSM_GROUPS):
        bm = xc[:, SSM_INNER + g * SSM_STATE:SSM_INNER + (g + 1) * SSM_STATE].astype(BF16)
        cm = xc[:, SSM_INNER + (SSM_GROUPS + g) * SSM_STATE:SSM_INNER + (SSM_GROUPS + g + 1) * SSM_STATE].astype(BF16)
        cb = lax.dot_general(cm, bm, (((1,), (1,)), ((), ())), preferred_element_type=F32)
        bm_t = bm.T
        for hh in range(hg):
            h = g * hg + hh
            acol = jnp.broadcast_to(acs[:, h:h + 1], (n, n))
            arow = acs_t[h:h + 1, :]
            decay = jnp.exp(jnp.where(tri, acol - arow, NEG))
            acol_p = acol[:, :SSM_HEAD_DIM]
            xh = xs[:, h * SSM_HEAD_DIM:(h + 1) * SSM_HEAD_DIM] * dt[:, h:h + 1]
            a_last = acol_p[n - 1:n, :]
            y = jnp.dot((cb * decay).astype(BF16), xh.astype(BF16), preferred_element_type=F32)
            st = state_ref[h]
            y = y + jnp.dot(cm, st.astype(BF16), preferred_element_type=F32) * jnp.exp(acol_p)
            upd = jnp.dot(bm_t, (xh * jnp.exp(a_last - acol_p)).astype(BF16), preferred_element_type=F32)
            state_ref[h] = jnp.exp(a_last) * st + upd
            y_ref[:, h * SSM_HEAD_DIM:(h + 1) * SSM_HEAD_DIM] = y
    z = z_ref[...].astype(F32)
    y = (y_ref[...] + dfull_ref[...] * xs) * (z * _sigmoid(z))
    gw = SSM_INNER // SSM_GROUPS
    for g in range(SSM_GROUPS):
        yg = y[:, g * gw:(g + 1) * gw]
        ms = jnp.mean(yg * yg, axis=-1, keepdims=True)
        o_ref[:, g * gw:(g + 1) * gw] = (yg * lax.rsqrt(ms + EPS) * ng_ref[:, g * gw:(g + 1) * gw]).astype(o_ref.dtype)


def _ssd(s_z, s_xbc, s_dt, conv_w, conv_b, dt_bias, a_log, d_skip, norm_g, b, s):
    n = SSM_CHUNK
    nt = s // n
    c = s_xbc.shape[1]
    row = lambda bi, i: (bi * nt + i, 0)
    c2 = lambda bi, i: (0, 0)
    pad_h = lambda v: jnp.zeros((1, LANES), F32).at[0, :SSM_HEADS].set(v.astype(F32))
    dfull = jnp.repeat(d_skip.astype(F32), SSM_HEAD_DIM).reshape(1, SSM_INNER)
    return pl.pallas_call(
        _ssd_kernel, grid=(b, nt),
        in_specs=[pl.BlockSpec((n, SSM_INNER), row), pl.BlockSpec((n, c), row), pl.BlockSpec((n, LANES), row),
                  pl.BlockSpec((CONV_WIDTH, c), c2), pl.BlockSpec((1, c), c2),
                  pl.BlockSpec((1, LANES), c2), pl.BlockSpec((1, LANES), c2),
                  pl.BlockSpec((1, SSM_INNER), c2), pl.BlockSpec((1, SSM_INNER), c2)],
        out_specs=pl.BlockSpec((n, SSM_INNER), row),
        out_shape=jax.ShapeDtypeStruct((b * s, SSM_INNER), BF16),
        scratch_shapes=[pltpu.VMEM((8 + n, c), F32), pltpu.VMEM((SSM_HEADS, SSM_STATE, SSM_HEAD_DIM), F32),
                        pltpu.VMEM((n, SSM_INNER), F32)],
        compiler_params=_params(("parallel", "arbitrary")), name="ssd",
    )(s_z, s_xbc, s_dt, conv_w.astype(F32), conv_b.reshape(1, c).astype(F32), pad_h(dt_bias), pad_h(a_log),
      dfull, norm_g.reshape(1, SSM_INNER).astype(F32))


def _merge_kernel(oc_ref, os_ref, ow_ref, ag_ref, ex_ref, yb_ref, yc_ref, mg_ref, x_ref, wb_ref, wo_ref, o_ref):
    gates = _sigmoid(ag_ref[...])
    hi = gates.astype(BF16)
    lo = (gates - hi.astype(F32)).astype(BF16)
    spread = lambda k: (jnp.dot(hi, ex_ref[k], preferred_element_type=F32)
                        + jnp.dot(lo, ex_ref[k], preferred_element_type=F32))
    ya = (spread(0) * oc_ref[...].astype(F32) + spread(1) * os_ref[...].astype(F32)
          + spread(2) * ow_ref[...].astype(F32))
    d = D_MODEL
    merged = _sigmoid(mg_ref[:, 0:d].astype(F32)) * jnp.dot(ya.astype(BF16), wb_ref[0], preferred_element_type=F32)
    merged += _sigmoid(mg_ref[:, d:2 * d].astype(F32)) * jnp.dot(yb_ref[...], wb_ref[1], preferred_element_type=F32)
    merged += _sigmoid(mg_ref[:, 2 * d:3 * d].astype(F32)) * jnp.dot(yc_ref[...], wb_ref[2], preferred_element_type=F32)
    o_ref[...] = x_ref[...] + jnp.dot(merged.astype(BF16), wo_ref[...], preferred_element_type=F32)


def _merge(o_c, o_s, o_w, a_g, y_b, y_c, m_g, x, w_branch, w_out, tm=512):
    t, d = x.shape
    row = lambda i: (i, 0)
    lane = jnp.arange(LANES)[:, None]
    col = jnp.arange(d)[None, :]
    expand = jnp.stack([(lane == 3 * (col // HEAD_DIM) + k) for k in range(3)]).astype(BF16)
    return pl.pallas_call(
        _merge_kernel, grid=(t // tm,),
        in_specs=[pl.BlockSpec((tm, d), row), pl.BlockSpec((tm, d), row), pl.BlockSpec((tm, d), row),
                  pl.BlockSpec((tm, LANES), row), pl.BlockSpec((3, LANES, d), lambda i: (0, 0, 0)),
                  pl.BlockSpec((tm, d), row), pl.BlockSpec((tm, d), row), pl.BlockSpec((tm, 3 * d), row),
                  pl.BlockSpec((tm, d), row),
                  pl.BlockSpec((3, d, d), lambda i: (0, 0, 0)), pl.BlockSpec((d, d), lambda i: (0, 0))],
        out_specs=pl.BlockSpec((tm, d), row),
        out_shape=jax.ShapeDtypeStruct((t, d), F32),
        compiler_params=_params(("parallel",)), name="merge",
    )(o_c.reshape(t, d), o_s.reshape(t, d), o_w.reshape(t, d), a_g, expand, y_b, y_c, m_g, x,
      w_branch.astype(BF16), w_out.astype(BF16))


class _Interleaver:
    def __init__(self, pieces, every):
        self.pieces, self.every, self.count = list(pieces), every, 0

    def tick(self):
        self.count += 1
        if self.pieces and self.count % self.every == 0:
            self.pieces.pop(0)()

    def drain(self):
        while self.pieces:
            self.pieces.pop(0)()


def _sorting_network(n):
    pairs = []

    def merge(lo, m, r):
        step = r * 2
        if step < m:
            merge(lo, m, step)
            merge(lo + r, m, step)
            pairs.extend((i, i + r) for i in range(lo + r, lo + m - r, step))
        else:
            pairs.append((lo, lo + r))

    def sort(lo, m):
        if m > 1:
            sort(lo, m // 2)
            sort(lo + m // 2, m // 2)
            merge(lo, m, 1)

    sort(0, n)
    return pairs


def _topk_rows(s, k, tick):
    n, lanes = s.shape
    assert n == 8 * k
    sub = lax.broadcasted_iota(I32, (8, lanes), 0)
    vals = [s[8 * j:8 * j + 8, :] for j in range(k)]
    ids = [sub + 8 * j for j in range(k)]
    for count, (a, b) in enumerate(_sorting_network(k)):
        first = (vals[a] > vals[b]) | ((vals[a] == vals[b]) & (ids[a] < ids[b]))
        vals[a], vals[b] = jnp.where(first, vals[a], vals[b]), jnp.where(first, vals[b], vals[a])
        ids[a], ids[b] = jnp.where(first, ids[a], ids[b]), jnp.where(first, ids[b], ids[a])
        if count % 16 == 15:
            tick()
    top_v, top_i = [], []
    for r in range(k):
        best = jnp.max(vals[0], axis=0, keepdims=True)
        row = jnp.min(jnp.where(vals[0] == best, ids[0], n), axis=0, keepdims=True)
        top_v.append(best)
        top_i.append(row)
        won = ids[0] == row
        for d in range(k - 1 - r):
            vals[d] = jnp.where(won, vals[d + 1], vals[d])
            ids[d] = jnp.where(won, ids[d + 1], ids[d])
        tick()
    return jnp.concatenate(top_v, axis=0), jnp.concatenate(top_i, axis=0)


def _top_pairs(s1, s2, k, tick):
    lanes = s1.shape[1]
    sub = lax.broadcasted_iota(I32, (8, lanes), 0)
    lists = [jnp.where(sub <= k // (d + 1) - 1, s1[0:8, :] + s2[d:d + 1, :], -jnp.inf) for d in range(k)]
    tail = s1[8:16, :] + s2[0:1, :]
    tail_pos = (sub + 8) * k
    taken = jnp.zeros((8, lanes), I32)
    vs, aa, bb = [], [], []
    for r in range(k):
        best = jnp.maximum(jnp.max(lists[0], axis=0, keepdims=True), jnp.max(tail, axis=0, keepdims=True))
        head_pos = sub * k + taken
        pos = jnp.minimum(jnp.min(jnp.where(lists[0] == best, head_pos, k * k), axis=0, keepdims=True),
                          jnp.min(jnp.where(tail == best, tail_pos, k * k), axis=0, keepdims=True))
        vs.append(best)
        aa.append(pos // k)
        bb.append(pos % k)
        won = head_pos == pos
        tail = jnp.where(tail_pos == pos, -jnp.inf, tail)
        for d in range(k - 1 - r):
            lists[d] = jnp.where(won, lists[d + 1], lists[d])
        taken = taken + won.astype(I32)
        tick()
    return jnp.concatenate(vs, axis=0), jnp.concatenate(aa, axis=0), jnp.concatenate(bb, axis=0)


def _pick_rows(table, sel, k):
    out = jnp.zeros(sel.shape, table.dtype)
    for a in range(k):
        out = jnp.where(sel == a, table[a:a + 1, :], out)
    return out


def _route_head(qt, keys_ref, tick):
    k = PEER_TOPK
    assert k == 16 and PEER_KEYS == 8 * k
    tops = []
    for half in range(2):
        sc = jnp.dot(keys_ref[half], qt[half * PEER_HALF:(half + 1) * PEER_HALF, :], preferred_element_type=F32)
        tops.append(_topk_rows(sc, k, tick))
    (s1, i1), (s2, i2) = tops
    sc, a_sel, b_sel = _top_pairs(s1, s2, k, tick)
    e = jnp.exp(sc - sc[0:1, :])
    g = e / jnp.sum(e, axis=0, keepdims=True)
    return _pick_rows(i1, a_sel, k), _pick_rows(i2, b_sel, k), g


def _peer_route_u_kernel(hn_ref, hc_ref, wq_ref, keys_ref, u_ref, i1_ref, i2_ref, g_ref, act_ref,
                         i1t_ref, i2t_ref, gt_ref, i1c_ref, i2c_ref, *, blocks):
    o = pl.program_id(0)
    c = pl.program_id(1)
    k = PEER_TOPK

    @pl.when((o == 0) & (c == 0))
    def _():
        i1c_ref[...] = jnp.zeros(i1c_ref.shape, I32)
        i2c_ref[...] = jnp.zeros(i2c_ref.shape, I32)

    @pl.when(c == 0)
    def _():
        act_ref[...] = jnp.zeros(act_ref.shape, F32)

    i1c = i1c_ref[...]
    i2c = i2c_ref[...]
    hc = hc_ref[...]
    acc = [act_ref[...]]

    def piece(pc):
        def run():
            start = pl.multiple_of((c * blocks + 2 * pc) * PEER_KEYS, 2 * PEER_KEYS)
            a = lax.dot_general(hc, u_ref[pl.ds(start, 2 * PEER_KEYS), :], (((1,), (1,)), ((), ())),
                                preferred_element_type=F32)
            for sub in range(2):
                got = jnp.take_along_axis(a[:, sub * PEER_KEYS:(sub + 1) * PEER_KEYS], i2c, axis=1,
                                          mode="promise_in_bounds")
                acc[0] = jnp.where(i1c == c * blocks + 2 * pc + sub, got, acc[0])
        return run

    tm = hn_ref.shape[0]
    groups = tm // ROUTE_LANES
    hps = keys_ref.shape[0] // 2
    hq = wq_ref.shape[0] // hps
    ticks = hps * groups * (2 * (len(_sorting_network(k)) // 16 + k) + k)
    pieces = _Interleaver([piece(pc) for pc in range(blocks // 2)], every=ticks // (blocks // 2 + ROUTE_SLACK))
    qt = lax.dot_general(wq_ref[...], hn_ref[...], (((1,), (1,)), ((), ())), preferred_element_type=F32)
    qt = qt.astype(BF16)
    for hd in range(hps):
        rows = pl.ds(pl.multiple_of((c * hps + hd) * k, k), k)
        for gi in range(groups):
            cols = slice(gi * ROUTE_LANES, (gi + 1) * ROUTE_LANES)
            i1, i2, g = _route_head(qt[hd * hq:(hd + 1) * hq, cols], keys_ref.at[pl.ds(2 * hd, 2)], pieces.tick)
            i1t_ref[rows, cols] = i1
            i2t_ref[rows, cols] = i2
            gt_ref[rows, cols] = g
    pieces.drain()
    act_ref[...] = acc[0]

    @pl.when(c == pl.num_programs(1) - 1)
    def _():
        i1n = i1t_ref[...].T
        i2n = i2t_ref[...].T
        i1_ref[...] = i1n
        i2_ref[...] = i2n
        g_ref[...] = gt_ref[...].T
        i1c_ref[...] = i1n
        i2c_ref[...] = i2n


def _peer_route_u(h, w_q, sub_keys, u, tm=512, hps=4):
    t, d = h.shape
    nt = t // tm
    ne = u.shape[0]
    nchunk = PEER_HEADS // hps
    ec = ne // nchunk
    blocks = ec // PEER_KEYS
    hq = hps * (w_q.shape[1] // PEER_HEADS)
    wq_t = w_q.T.astype(BF16)
    keys = sub_keys.reshape(PEER_HEADS * 2, PEER_KEYS, PEER_HALF).astype(BF16)
    slots = PEER_HEADS * PEER_TOPK
    nxt = lambda o, c: (jnp.minimum(o, nt - 1), 0)
    cur = lambda o, c: (jnp.maximum(o - 1, 0), 0)
    kern = functools.partial(_peer_route_u_kernel, blocks=blocks)
    return pl.pallas_call(
        kern, grid=(nt + 1, nchunk),
        in_specs=[pl.BlockSpec((tm, d), nxt), pl.BlockSpec((tm, d), cur),
                  pl.BlockSpec((hq, d), lambda o, c: (c, 0)),
                  pl.BlockSpec((2 * hps, PEER_KEYS, PEER_HALF), lambda o, c: (c, 0, 0)),
                  pl.BlockSpec((ne, d), lambda o, c: (0, 0), pipeline_mode=pl.Buffered(1))],
        out_specs=[pl.BlockSpec((tm, slots), nxt)] * 3 + [pl.BlockSpec((tm, slots), cur)],
        out_shape=[jax.ShapeDtypeStruct((t, slots), I32), jax.ShapeDtypeStruct((t, slots), I32),
                   jax.ShapeDtypeStruct((t, slots), F32), jax.ShapeDtypeStruct((t, slots), F32)],
        scratch_shapes=[pltpu.VMEM((slots, tm), I32), pltpu.VMEM((slots, tm), I32), pltpu.VMEM((slots, tm), F32),
                        pltpu.VMEM((tm, slots), I32), pltpu.VMEM((tm, slots), I32)],
        compiler_params=_params(("arbitrary", "arbitrary")), name="peer_route_u",
    )(h, h, wq_t, keys, u)


def _peer_v_kernel(act_ref, g_ref, i1_ref, i2_ref, v_ref, x_ref, o_ref, w_ref, wg_ref, *, tm, blocks):
    c = pl.program_id(1)
    nk = PEER_KEYS

    @pl.when(c == 0)
    def _():
        w_ref[...] = g_ref[...] * _gelu(act_ref[...])
        o_ref[...] = x_ref[...]
        sub = lax.broadcasted_iota(I32, (nk, w_ref.shape[1]), 0)

        def per_token(t, carry):
            wrow = w_ref[pl.ds(t, 1), :]
            lhs = jnp.where(i1_ref[pl.ds(t, 1), :] == sub, wrow, 0.0).astype(BF16)
            rhs = jnp.where(i2_ref[pl.ds(t, 1), :] == sub, 1.0, 0.0).astype(BF16)
            grid = lax.dot_general(lhs, rhs, (((1,), (1,)), ((), ())), preferred_element_type=F32)
            wg_ref[pl.ds(pl.multiple_of(t * WG_PITCH, 8), nk), :] = grid
            return carry

        lax.fori_loop(0, tm, per_token, 0, unroll=64)

    acc = jnp.zeros(o_ref.shape, F32)
    for bk in range(0, blocks, 2):
        i1 = c * blocks + bk
        lhs = jnp.concatenate([wg_ref[pl.ds(i1, tm, stride=WG_PITCH), :],
                               wg_ref[pl.ds(i1 + 1, tm, stride=WG_PITCH), :]], axis=1).astype(BF16)
        rhs = v_ref[bk:bk + 2].reshape(2 * nk, v_ref.shape[2])
        acc += jnp.dot(lhs, rhs, preferred_element_type=F32)
    o_ref[...] += acc


def _peer_v(act, g, i1, i2, v, x, tm=256, blocks=128):
    t, d = x.shape
    slots = g.shape[1]
    nk = PEER_KEYS
    v3 = v.reshape(nk, nk, d)
    kern = functools.partial(_peer_v_kernel, tm=tm, blocks=blocks)
    row = lambda i, c: (i, 0)
    return pl.pallas_call(
        kern, grid=(t // tm, nk // blocks),
        in_specs=[pl.BlockSpec((tm, slots), row),
                  pl.BlockSpec((tm, slots), row), pl.BlockSpec((tm, slots), row), pl.BlockSpec((tm, slots), row),
                  pl.BlockSpec((blocks, nk, d), lambda i, c: (c, 0, 0),
                               **({"pipeline_mode": pl.Buffered(1)} if blocks == nk else {})),
                  pl.BlockSpec((tm, d), row)],
        out_specs=pl.BlockSpec((tm, d), row),
        out_shape=jax.ShapeDtypeStruct((t, d), F32),
        scratch_shapes=[pltpu.VMEM((tm, slots), F32), pltpu.VMEM((tm * WG_PITCH, nk), F32)],
        compiler_params=_params(("parallel", "arbitrary")), name="peer_v",
    )(act, g, i1, i2, v3, x)


def _pad_cols(w, n):
    return jnp.pad(w, ((0, 0), (0, n - w.shape[1])))


def _in_proj_kernel(x_ref, g_ref, w_ref, *rest, bounds):
    outs, xn_ref = rest[:-1], rest[-1]
    j = pl.program_id(1)

    @pl.when(j == 0)
    def _():
        x = x_ref[...]
        ms = jnp.mean(x * x, axis=-1, keepdims=True)
        xn_ref[...] = (x * lax.rsqrt(ms + EPS) * g_ref[...]).astype(xn_ref.dtype)

    for (lo, hi), o_ref in zip(bounds, outs):
        @pl.when((j >= lo) & (j < hi))
        def _(o_ref=o_ref):
            o_ref[...] = jnp.dot(xn_ref[...], w_ref[...], preferred_element_type=F32).astype(o_ref.dtype)


PROJ_TILE = 512
IN_GROUPS = ((1024, BF16), (768, BF16), (48, F32), (1024, BF16), (1024, BF16), (1024, BF16), (1536, BF16),
             (16, F32), (3072, BF16))


def _in_proj(x, g, w_in, tm=1024):
    t, d = x.shape
    tn = PROJ_TILE
    ws, bounds, shapes, off, tile = [], [], [], 0, 0
    for n, dtype in IN_GROUPS:
        width = -(-n // tn) * tn
        ws.append(_pad_cols(w_in[:, off:off + n], width))
        bounds.append((tile, tile + width // tn))
        shapes.append(jax.ShapeDtypeStruct((t, width), dtype))
        off += n
        tile += width // tn
    w_all = jnp.concatenate(ws, axis=1).astype(BF16)
    out_specs = [pl.BlockSpec((tm, tn), functools.partial(lambda i, j, lo, hi: (i, jnp.clip(j - lo, 0, hi - lo - 1)),
                                                          lo=lo, hi=hi)) for lo, hi in bounds]
    kern = functools.partial(_in_proj_kernel, bounds=tuple(bounds))
    return pl.pallas_call(
        kern, grid=(t // tm, tile),
        in_specs=[pl.BlockSpec((tm, d), lambda i, j: (i, 0)), pl.BlockSpec((1, d), lambda i, j: (0, 0)),
                  pl.BlockSpec((d, tn), lambda i, j: (0, j))],
        out_specs=out_specs, out_shape=shapes,
        scratch_shapes=[pltpu.VMEM((tm, d), BF16)],
        compiler_params=_params(("parallel", "arbitrary")), name="in_proj",
    )(x, g.reshape(1, d).astype(F32), w_all)


def _mixer(x, positions, b, s, p):
    a_q, a_kv, a_g, l_x, l_g, s_z, s_xbc, s_dt, m_g = _in_proj(x, p["mix_norm_g"], p["w_in"])

    q, kc, vc, ks, vs, kw, vw = _nsa_prep(a_q, a_kv, positions, p["q_norm_g"], p["k_norm_g"], b, s)
    k_cmp, v_cmp = _compress(kc, vc, p["cmp_pe_k"], p["cmp_pe_v"], p["cmp_k_w1"], p["cmp_k_w2"],
                             p["cmp_v_w1"], p["cmp_v_w2"], p["k_norm_g"])
    o_c, bias = _cmp_select(q, k_cmp, v_cmp, s)
    o_s = _sel_attn(q, bias, ks, vs, s)
    o_w = _win_attn(q, kw, vw, s)
    y_b = _rglru(l_x, l_g, positions, p["lru_conv_w"], p["lru_conv_b"], p["lru_w_r"], p["lru_b_r"],
                 p["lru_w_i"], p["lru_b_i"], p["lru_lambda"], b, s)
    y_c = _ssd(s_z, s_xbc, s_dt, p["ssm_conv_w"], p["ssm_conv_b"], p["ssm_dt_bias"], p["ssm_a_log"],
               p["ssm_d"], p["ssm_norm_g"], b, s)
    return _merge(o_c, o_s, o_w, a_g, y_b, y_c, m_g, x, p["w_branch"], p["w_out"])


def _peer(x, p):
    h = _rmsnorm(x, p["ffn_norm_g"])
    i1, i2, g, act = _peer_route_u(h, p["peer_w_q"], p["peer_sub_keys"], p["peer_u"].astype(BF16))
    return _peer_v(act, g, i1, i2, p["peer_v"].astype(BF16), x)


_LAYER_PARAMS = ("mix_norm_g", "w_in", "q_norm_g", "k_norm_g", "cmp_pe_k", "cmp_pe_v", "cmp_k_w1", "cmp_k_w2",
                 "cmp_v_w1", "cmp_v_w2", "lru_conv_w", "lru_conv_b", "lru_w_r", "lru_b_r", "lru_w_i", "lru_b_i",
                 "lru_lambda", "ssm_conv_w", "ssm_conv_b", "ssm_dt_bias", "ssm_a_log", "ssm_d", "ssm_norm_g",
                 "w_branch", "w_out", "ffn_norm_g", "peer_w_q", "peer_sub_keys", "peer_u", "peer_v")


def kernel(x, positions, mix_norm_g, w_in, q_norm_g, k_norm_g, cmp_pe_k, cmp_pe_v, cmp_k_w1, cmp_k_w2, cmp_v_w1, cmp_v_w2, lru_conv_w, lru_conv_b, lru_w_r, lru_b_r, lru_w_i, lru_b_i, lru_lambda, ssm_conv_w, ssm_conv_b, ssm_dt_bias, ssm_a_log, ssm_d, ssm_norm_g, w_branch, w_out, ffn_norm_g, peer_w_q, peer_sub_keys, peer_u, peer_v):
    stacked = dict(zip(_LAYER_PARAMS, (mix_norm_g, w_in, q_norm_g, k_norm_g, cmp_pe_k, cmp_pe_v, cmp_k_w1,
                                       cmp_k_w2, cmp_v_w1, cmp_v_w2, lru_conv_w, lru_conv_b, lru_w_r, lru_b_r,
                                       lru_w_i, lru_b_i, lru_lambda, ssm_conv_w, ssm_conv_b, ssm_dt_bias,
                                       ssm_a_log, ssm_d, ssm_norm_g, w_branch, w_out, ffn_norm_g, peer_w_q,
                                       peer_sub_keys, peer_u, peer_v)))
    b, s, d = x.shape
    xf = x.reshape(b * s, d).astype(F32)
    for layer in range(mix_norm_g.shape[0]):
        p = {name: arr[layer] for name, arr in stacked.items()}
        xf = _mixer(xf, positions, b, s, p)
        xf = _peer(xf, p)
    return xf.reshape(b, s, d).astype(x.dtype)
```
